```python
import math
import jax, jax.numpy as jnp
from jax import lax
import numpy as np

D_MODEL = 1024
BATCH = 8
SEQ = 2048
DEPTH = 1

D_MIX = D_MODEL
GLA_WIDTH = D_MIX // 2
GLA_HEADS = 4
GLA_DV = GLA_WIDTH // GLA_HEADS
GLA_DK = GLA_DV // 2
GLA_KW = GLA_HEADS * GLA_DK
GLA_GATE_RANK = 16
GLA_GATE_NORM = 16.0
SSD_WIDTH = D_MIX - GLA_WIDTH
SSD_HEADDIM = 64
SSD_HEADS = SSD_WIDTH // SSD_HEADDIM
SSD_GROUPS = 2
SSD_HPG = SSD_HEADS // SSD_GROUPS
SSD_STATE = 128
SSD_CONV = 4
SSD_CONV_CH = SSD_WIDTH + 2 * SSD_GROUPS * SSD_STATE
CHUNK = 64
IN_SPLITS = [GLA_KW, GLA_KW, GLA_WIDTH, GLA_WIDTH, GLA_GATE_RANK,
             SSD_WIDTH, SSD_CONV_CH, SSD_HEADS]
D_IN = sum(IN_SPLITS)
N_EXPERTS = 32
TOP_K = 4
D_FF = D_MODEL
SWIGLU_LIMIT = 7.0
SWIGLU_ALPHA = 1.702
EPS = 1e-6
GROUP_EPS = 1e-5

kernel_name = "hymba_gla_ssd_moe_block"


def rmsnorm(x, w, eps=EPS):
    xf = x.astype(jnp.float32)
    y = xf * lax.rsqrt(jnp.mean(xf * xf, axis=-1, keepdims=True) + eps)
    return (y * w.astype(jnp.float32)).astype(x.dtype)


def gla_chunked(q, k, v, log_a):
    bsz, seqlen, heads, _ = q.shape
    dv = v.shape[-1]
    n = seqlen // CHUNK

    def to_chunks(t):
        return jnp.moveaxis(t.reshape(bsz, n, CHUNK, heads, t.shape[-1]), 1, 0).transpose(0, 1, 3, 2, 4)

    bcum = jnp.cumsum(to_chunks(log_a), axis=3)
    mask = jnp.tril(jnp.ones((CHUNK, CHUNK), bool))[:, :, None]

    def step(state, inp):
        qc, kc, vc, bc = inp
        inter = jnp.einsum('bhcd,bhdv->bhcv', qc * jnp.exp(bc), state)
        diff = bc[:, :, :, None, :] - bc[:, :, None, :, :]
        decay = jnp.where(mask, jnp.exp(jnp.where(mask, diff, 0.0)), 0.0)
        scores = jnp.einsum('bhid,bhjd,bhijd->bhij', qc, kc, decay)
        intra = jnp.einsum('bhij,bhjv->bhiv', scores, vc)
        b_last = bc[:, :, -1, :]
        state = state * jnp.exp(b_last)[..., None] + jnp.einsum(
            'bhcd,bhcv->bhdv', kc * jnp.exp(b_last[:, :, None, :] - bc), vc)
        return state, inter + intra

    s0 = jnp.zeros((bsz, heads, q.shape[-1], dv), jnp.float32)
    _, o = lax.scan(step, s0, (to_chunks(q), to_chunks(k), to_chunks(v), bcum))
    return o.transpose(1, 0, 3, 2, 4).reshape(bsz, seqlen, heads, dv)


def ssd_chunked(x, dt, a_neg, bm, cm):
    bsz, seqlen = x.shape[:2]
    n = seqlen // CHUNK

    def to_chunks(t):
        return jnp.moveaxis(t.reshape(bsz, n, CHUNK, *t.shape[2:]), 1, 0)

    acum = jnp.cumsum(to_chunks(dt * a_neg), axis=2)
    mask = jnp.tril(jnp.ones((CHUNK, CHUNK), bool))[None, :, :, None, None]

    def step(state, inp):
        xc, dtc, ac, bc, cc = inp
        seg = ac[:, :, None] - ac[:, None, :]
        lmat = jnp.where(mask, jnp.exp(jnp.where(mask, seg, 0.0)), 0.0)
        cb = jnp.einsum('bign,bjgn->bijg', cc, bc)
        intra = jnp.einsum('bijg,bijgh,bjghp->bighp', cb, lmat, xc * dtc[..., None])
        inter = jnp.einsum('bign,bghpn->bighp', cc, state) * jnp.exp(ac)[..., None]
        a_last = ac[:, -1]
        wgt = jnp.exp(a_last[:, None] - ac) * dtc
        state = state * jnp.exp(a_last)[..., None, None] + jnp.einsum(
            'bjgn,bjgh,bjghp->bghpn', bc, wgt, xc)
        return state, intra + inter

    s0 = jnp.zeros((bsz, SSD_GROUPS, SSD_HPG, SSD_HEADDIM, SSD_STATE), jnp.float32)
    _, y = lax.scan(step, s0, (to_chunks(x), to_chunks(dt), acum, to_chunks(bm), to_chunks(cm)))
    return jnp.moveaxis(y, 0, 1).reshape(x.shape)


def moe_ffn(h, router_w, router_b, w_gate, b_gate, w_up, b_up, w_down, b_down):
    bsz, seqlen, d = h.shape
    hf = h.reshape(-1, d)
    t = hf.shape[0]
    logits = hf.astype(jnp.float32) @ router_w.astype(jnp.float32) + router_b.astype(jnp.float32)
    top_vals, top_idx = lax.top_k(logits, TOP_K)
    gates = jax.nn.softmax(top_vals, axis=-1)
    flat_e = top_idx.reshape(-1)
    order = jnp.argsort(flat_e)
    tok = order // TOP_K
    sorted_e = flat_e[order]
    group_sizes = jnp.bincount(flat_e, length=N_EXPERTS).astype(jnp.int32)
    xs = hf[tok]
    gate = lax.ragged_dot(xs, w_gate, group_sizes) + b_gate[sorted_e]
    up = lax.ragged_dot(xs, w_up, group_sizes) + b_up[sorted_e]
    gate = jnp.minimum(gate, SWIGLU_LIMIT)
    up = jnp.clip(up, -SWIGLU_LIMIT, SWIGLU_LIMIT)
    act = (up + 1.0) * (gate * jax.nn.sigmoid(SWIGLU_ALPHA * gate))
    out = lax.ragged_dot(act, w_down, group_sizes) + b_down[sorted_e]
    wsel = gates.reshape(-1)[order].astype(out.dtype)
    y = jnp.zeros((t, d), out.dtype).at[tok].add(out * wsel[:, None])
    return y.reshape(bsz, seqlen, d).astype(h.dtype)


def setup_inputs(seed: int = 0) -> dict:
    key = jax.random.key(seed)
    ks = jax.random.split(key, 24)
    f32 = jnp.float32

    def nrm(k, shape, fan_in):
        return jax.random.normal(k, shape, f32) * (fan_in ** -0.5)

    def gain(k, shape):
        return 1.0 + 0.02 * jax.random.normal(k, shape, f32)

    dt = jnp.exp(jax.random.uniform(ks[9], (DEPTH, SSD_HEADS), f32, math.log(1e-3), math.log(1e-1)))
    return {
        "x": jax.random.normal(ks[0], (BATCH, SEQ, D_MODEL), f32),
        "norm_mix_w": gain(ks[1], (DEPTH, D_MODEL)),
        "w_in": nrm(ks[2], (DEPTH, D_MODEL, D_IN), D_MODEL),
        "gla_w_alpha_up": nrm(ks[3], (DEPTH, GLA_GATE_RANK, GLA_KW), GLA_GATE_RANK),
        "gla_b_alpha": 0.1 * jax.random.normal(ks[4], (DEPTH, GLA_KW), f32) + 2.0,
        "gla_norm_w": gain(ks[5], (DEPTH, GLA_WIDTH)),
        "ssd_conv_w": nrm(ks[6], (DEPTH, SSD_CONV, SSD_CONV_CH), SSD_CONV),
        "ssd_conv_b": 0.02 * jax.random.normal(ks[7], (DEPTH, SSD_CONV_CH), f32),
        "ssd_dt_bias": dt + jnp.log(-jnp.expm1(-dt)),
        "ssd_A_log": jnp.log(jax.random.uniform(ks[10], (DEPTH, SSD_HEADS), f32, 1.0, 16.0)),
        "ssd_D": gain(ks[11], (DEPTH, SSD_HEADS)),
        "ssd_norm_w": gain(ks[12], (DEPTH, SSD_WIDTH)),
        "w_out": nrm(ks[13], (DEPTH, D_MIX, D_MODEL), D_MIX),
        "norm_ffn_w": gain(ks[14], (DEPTH, D_MODEL)),
        "router_w": nrm(ks[15], (DEPTH, D_MODEL, N_EXPERTS), D_MODEL),
        "router_b": 0.01 * jax.random.normal(ks[16], (DEPTH, N_EXPERTS), f32),
        "moe_w_gate": nrm(ks[17], (DEPTH, N_EXPERTS, D_MODEL, D_FF), D_MODEL),
        "moe_b_gate": 0.02 * jax.random.normal(ks[18], (DEPTH, N_EXPERTS, D_FF), f32),
        "moe_w_up": nrm(ks[19], (DEPTH, N_EXPERTS, D_MODEL, D_FF), D_MODEL),
        "moe_b_up": 0.02 * jax.random.normal(ks[20], (DEPTH, N_EXPERTS, D_FF), f32),
        "moe_w_down": nrm(ks[21], (DEPTH, N_EXPERTS, D_FF, D_MODEL), D_FF),
        "moe_b_down": 0.02 * jax.random.normal(ks[22], (DEPTH, N_EXPERTS, D_MODEL), f32),
        "final_norm_w": gain(ks[23], (D_MODEL,)),
    }


def reference(x, norm_mix_w, w_in, gla_w_alpha_up, gla_b_alpha, gla_norm_w, ssd_conv_w, ssd_conv_b,
              ssd_dt_bias, ssd_A_log, ssd_D, ssd_norm_w, w_out, norm_ffn_w, router_w, router_b,
              moe_w_gate, moe_b_gate, moe_w_up, moe_b_up, moe_w_down, moe_b_down, final_norm_w):
    bsz, seqlen, _ = x.shape
    f32 = jnp.float32
    split_pts = list(np.cumsum(IN_SPLITS)[:-1])
    for l in range(DEPTH):
        h = rmsnorm(x, norm_mix_w[l])
        proj = h @ w_in[l]
        q, k, v, g, a_low, z, xbc, dt_raw = jnp.split(proj, split_pts, axis=-1)

        qh = q.astype(f32).reshape(bsz, seqlen, GLA_HEADS, GLA_DK) * (GLA_DK ** -0.5)
        kh = k.astype(f32).reshape(bsz, seqlen, GLA_HEADS, GLA_DK)
        vh = v.astype(f32).reshape(bsz, seqlen, GLA_HEADS, GLA_DV)
        log_a = jax.nn.log_sigmoid((a_low @ gla_w_alpha_up[l] + gla_b_alpha[l]).astype(f32)) / GLA_GATE_NORM
        o = gla_chunked(qh, kh, vh, log_a.reshape(bsz, seqlen, GLA_HEADS, GLA_DK))
        o = o * lax.rsqrt(jnp.mean(o * o, axis=-1, keepdims=True) + GROUP_EPS)
        gla_out = o.reshape(bsz, seqlen, GLA_WIDTH) * gla_norm_w[l].astype(f32) * jax.nn.silu(g.astype(f32))

        conv = lax.conv_general_dilated(
            xbc, ssd_conv_w[l][:, None, :], window_strides=(1,), padding=[(SSD_CONV - 1, 0)],
            dimension_numbers=('NWC', 'WIO', 'NWC'), feature_group_count=SSD_CONV_CH)
        xbc_act = jax.nn.silu((conv + ssd_conv_b[l]).astype(f32))
        xs, bm, cm = jnp.split(xbc_act, [SSD_WIDTH, SSD_WIDTH + SSD_GROUPS * SSD_STATE], axis=-1)
        xs = xs.reshape(bsz, seqlen, SSD_GROUPS, SSD_HPG, SSD_HEADDIM)
        bm = bm.reshape(bsz, seqlen, SSD_GROUPS, SSD_STATE)
        cm = cm.reshape(bsz, seqlen, SSD_GROUPS, SSD_STATE)
        dt = jax.nn.softplus(dt_raw.astype(f32) + ssd_dt_bias[l].astype(f32))
        dt = dt.reshape(bsz, seqlen, SSD_GROUPS, SSD_HPG)
        a_neg = -jnp.exp(ssd_A_log[l].astype(f32)).reshape(SSD_GROUPS, SSD_HPG)
        y = ssd_chunked(xs, dt, a_neg, bm, cm)
        y = y + ssd_D[l].astype(f32).reshape(SSD_GROUPS, SSD_HPG)[..., None] * xs
        y = y.reshape(bsz, seqlen, SSD_WIDTH) * jax.nn.silu(z.astype(f32))
        yg = y.reshape(bsz, seqlen, SSD_GROUPS, SSD_WIDTH // SSD_GROUPS)
        yg = yg * lax.rsqrt(jnp.mean(yg * yg, axis=-1, keepdims=True) + GROUP_EPS)
        ssd_out = yg.reshape(bsz, seqlen, SSD_WIDTH) * ssd_norm_w[l].astype(f32)

        mixed = jnp.concatenate([gla_out, ssd_out], axis=-1).astype(x.dtype)
        x = x + mixed @ w_out[l]

        h2 = rmsnorm(x, norm_ffn_w[l])
        x = x + moe_ffn(h2, router_w[l], router_b[l], moe_w_gate[l], moe_b_gate[l],
                        moe_w_up[l], moe_b_up[l], moe_w_down[l], moe_b_down[l])
    return rmsnorm(x, final_norm_w)
```

```python
import functools

import jax
import jax.numpy as jnp
from jax import lax
from jax.experimental import pallas as pl
from jax.experimental.pallas import tpu as pltpu

F32 = jnp.float32
BF16 = jnp.bfloat16

D_MODEL = 1024
GLA_WIDTH = 512
GLA_HEADS = 4
GLA_DV = 128
GLA_DK = 64
GLA_KW = 256
GLA_GATE_RANK = 16
GLA_GATE_NORM = 16.0
SSD_WIDTH = 512
SSD_HEADDIM = 64
SSD_HEADS = 8
SSD_GROUPS = 2
SSD_HPG = 4
SSD_STATE = 128
SSD_CONV = 4
SSD_CONV_CH = 1024
N_EXPERTS = 32
TOP_K = 4
SWIGLU_LIMIT = 7.0
SWIGLU_ALPHA = 1.702
EPS = 1e-6
GROUP_EPS = 1e-5

LANES = 128
N_MAIN = 3072
N_SMALL = LANES
DT_COL = GLA_GATE_RANK

GLA_CHUNK = 64
SSD_CHUNK = 128
TM_PROJ = 512
TL = 256
TR = 512
TT = 256
TM = 256
VMEM_LIMIT = 56 * 1024 * 1024


def _dot(a, b):
    return jnp.dot(a, b, preferred_element_type=F32)


def _dot_nt(a, b):
    return lax.dot_general(a, b, (((1,), (1,)), ((), ())), preferred_element_type=F32)


def _dot_tn(a, b):
    return lax.dot_general(a, b, (((0,), (0,)), ((), ())), preferred_element_type=F32)


def _split3(a):
    hi = a.astype(BF16)
    r1 = a - hi.astype(F32)
    mid = r1.astype(BF16)
    lo = (r1 - mid.astype(F32)).astype(BF16)
    return hi, mid, lo


def _dot_sel_lhs(sel, a):
    hi, mid, lo = _split3(a)
    return _dot(sel, hi) + _dot(sel, mid) + _dot(sel, lo)


def _dot_sel_rhs(a, sel):
    hi, mid, lo = _split3(a)
    return _dot(hi, sel) + _dot(mid, sel) + _dot(lo, sel)


def _dot_hi(a, b):
    a_hi = a.astype(BF16)
    a_lo = (a - a_hi.astype(F32)).astype(BF16)
    b_hi = b.astype(BF16)
    b_lo = (b - b_hi.astype(F32)).astype(BF16)
    return _dot(a_hi, b_hi) + _dot(a_lo, b_hi) + _dot(a_hi, b_lo)


def _dot_hi_nt(a, b):
    a_hi = a.astype(BF16)
    a_lo = (a - a_hi.astype(F32)).astype(BF16)
    b_hi = b.astype(BF16)
    b_lo = (b - b_hi.astype(F32)).astype(BF16)
    return _dot_nt(a_hi, b_hi) + _dot_nt(a_lo, b_hi) + _dot_nt(a_hi, b_lo)


def _softplus(x):
    return jnp.maximum(x, 0.0) + jnp.log1p(jnp.exp(-jnp.abs(x)))


def _silu(x):
    return x * jax.nn.sigmoid(x)


def _iota(shape, dim):
    return lax.broadcasted_iota(jnp.int32, shape, dim)


def _inproj_kernel(x_ref, nw_ref, w_ref, pm_ref, ps_ref):
    x = x_ref[...]
    ms = jnp.mean(x * x, axis=-1, keepdims=True)
    h = (x * lax.rsqrt(ms + EPS) * nw_ref[...]).astype(BF16)
    step = 512
    for n0 in range(0, N_MAIN, step):
        pm_ref[:, n0:n0 + step] = _dot(h, w_ref[:, n0:n0 + step]).astype(BF16)
    ps_ref[...] = _dot(h, w_ref[:, N_MAIN:N_MAIN + N_SMALL])


def _inproj(x2d, norm_w, w_all):
    t = x2d.shape[0]
    return pl.pallas_call(
        _inproj_kernel,
        grid=(t // TM_PROJ,),
        in_specs=[
            pl.BlockSpec((TM_PROJ, D_MODEL), lambda i: (i, 0)),
            pl.BlockSpec((1, D_MODEL), lambda i: (0, 0)),
            pl.BlockSpec((D_MODEL, N_MAIN + N_SMALL), lambda i: (0, 0)),
        ],
        out_specs=[
            pl.BlockSpec((TM_PROJ, N_MAIN), lambda i: (i, 0)),
            pl.BlockSpec((TM_PROJ, N_SMALL), lambda i: (i, 0)),
        ],
        out_shape=[
            jax.ShapeDtypeStruct((t, N_MAIN), BF16),
            jax.ShapeDtypeStruct((t, N_SMALL), F32),
        ],
        compiler_params=pltpu.CompilerParams(
            dimension_semantics=("arbitrary",), vmem_limit_bytes=VMEM_LIMIT),
        name="inproj",
    )(x2d, norm_w, w_all)


def _mixer_kernel(pm_ref, ps_ref, x_ref, wup_ref, balpha_ref, gnw_ref, convw_ref, convb_ref,
                  dtb_ref, aneg_ref, dexp_ref, snw_ref, wout_ref, nfw_ref, rw_ref, rb_ref,
                  x1_ref, h2_ref, lg_ref,
                  gla_state, ssd_state, conv_tail, mix_scr):
    @pl.when(pl.program_id(1) == 0)
    def _():
        gla_state[...] = jnp.zeros_like(gla_state)
        ssd_state[...] = jnp.zeros_like(ssd_state)
        conv_tail[...] = jnp.zeros_like(conv_tail)

    small = ps_ref[...]

    row = _iota((TL, TL), 0)
    col = _iota((TL, TL), 1)
    causal = col <= row
    cum64 = jnp.where(causal & ((row // GLA_CHUNK) == (col // GLA_CHUNK)), 1.0, 0.0).astype(BF16)
    cum128 = jnp.where(causal & ((row // SSD_CHUNK) == (col // SSD_CHUNK)), 1.0, 0.0).astype(BF16)

    xa = _dot_hi(small, wup_ref[...]) + balpha_ref[...]
    log_a = (jnp.minimum(xa, 0.0) - jnp.log1p(jnp.exp(-jnp.abs(xa)))) * (1.0 / GLA_GATE_NORM)
    bcum = _dot_sel_lhs(cum64, log_a)

    lane_kw = _iota((GLA_CHUNK, GLA_KW), 1)
    head_masks = [(lane_kw // GLA_DK) == h for h in range(GLA_HEADS)]
    lane_kw_s = _iota((GLA_DV, GLA_KW), 1)
    head_masks_s = [(lane_kw_s // GLA_DK) == h for h in range(GLA_HEADS)]
    tril64 = _iota((GLA_CHUNK, GLA_CHUNK), 1) <= _iota((GLA_CHUNK, GLA_CHUNK), 0)
    q_scale = GLA_DK ** -0.5

    for c in range(TL // GLA_CHUNK):
        rs = slice(c * GLA_CHUNK, (c + 1) * GLA_CHUNK)
        bc = bcum[rs]
        b_mid = bc[GLA_CHUNK // 2:GLA_CHUNK // 2 + 1]
        b_last = bc[GLA_CHUNK - 1:GLA_CHUNK]
        qc = pm_ref[rs, 0:GLA_KW].astype(F32) * q_scale
        kc = pm_ref[rs, GLA_KW:2 * GLA_KW].astype(F32)
        vc = pm_ref[rs, 2 * GLA_KW:2 * GLA_KW + GLA_WIDTH]
        q_in = (qc * jnp.exp(bc - b_mid)).astype(BF16)
        k_in = (kc * jnp.exp(b_mid - bc)).astype(BF16)
        q_st = (qc * jnp.exp(bc)).astype(BF16)
        k_st = (kc * jnp.exp(b_last - bc)).astype(BF16)
        st = gla_state[...]
        st_b = st.astype(BF16)
        zero_b = jnp.zeros_like(q_in)
        for h in range(GLA_HEADS):
            scores = _dot_nt(jnp.where(head_masks[h], q_in, zero_b), k_in)
            scores = jnp.where(tril64, scores, 0.0).astype(BF16)
            o_h = _dot(scores, vc[:, h * GLA_DV:(h + 1) * GLA_DV])
            o_h = o_h + _dot_nt(jnp.where(head_masks[h], q_st, zero_b), st_b)
            mix_scr[rs, h * GLA_DV:(h + 1) * GLA_DV] = o_h
        upd = _dot_tn(vc, k_st)
        new_st = st * jnp.exp(b_last)
        for h in range(GLA_HEADS):
            new_st = new_st + jnp.where(head_masks_s[h], upd[h * GLA_DV:(h + 1) * GLA_DV], 0.0)
        gla_state[...] = new_st

    xbc = pm_ref[:, 2048:3072].astype(F32)
    tail = conv_tail[...]
    conv_tail[...] = xbc[TL - 8:TL]
    row8 = _iota((8, SSD_CONV_CH), 0)
    conv = xbc * convw_ref[SSD_CONV - 1:SSD_CONV, :]
    for s in range(1, SSD_CONV):
        shifted = pltpu.roll(xbc, s, 0)
        head = jnp.where(row8 < s, pltpu.roll(tail, s, 0), shifted[0:8])
        shifted = jnp.concatenate([head, shifted[8:]], axis=0)
        conv = conv + shifted * convw_ref[SSD_CONV - 1 - s:SSD_CONV - s, :]
    act = _silu(conv + convb_ref[...])
    xs = act[:, 0:SSD_WIDTH]
    bm = act[:, SSD_WIDTH:SSD_WIDTH + SSD_GROUPS * SSD_STATE].astype(BF16)
    cm = act[:, SSD_WIDTH + SSD_GROUPS * SSD_STATE:].astype(BF16)

    dt_full = _softplus(small + dtb_ref[...])
    a_full = dt_full * aneg_ref[...]
    acum = _dot_sel_lhs(cum128, a_full)
    acum_t = acum.T

    e_row = _iota((N_SMALL, SSD_WIDTH), 0)
    e_col = _iota((N_SMALL, SSD_WIDTH), 1)
    spread64 = jnp.where(e_row == DT_COL + e_col // SSD_HEADDIM, 1.0, 0.0).astype(BF16)
    e_row2 = _iota((N_SMALL, SSD_HEADS * LANES), 0)
    e_col2 = _iota((N_SMALL, SSD_HEADS * LANES), 1)
    spread128 = jnp.where(e_row2 == DT_COL + e_col2 // LANES, 1.0, 0.0).astype(BF16)
    dt_e = _dot_sel_rhs(dt_full, spread64)
    ac_e = _dot_sel_rhs(acum, spread64)
    ac_w = _dot_sel_rhs(acum, spread128)

    tril128 = _iota((SSD_CHUNK, SSD_CHUNK), 1) <= _iota((SSD_CHUNK, SSD_CHUNK), 0)
    lane_g = _iota((SSD_CHUNK, SSD_HPG * SSD_HEADDIM), 1)
    for c in range(TL // SSD_CHUNK):
        rs = slice(c * SSD_CHUNK, (c + 1) * SSD_CHUNK)
        ac_c = ac_e[rs]
        a_last = ac_c[SSD_CHUNK - 1:SSD_CHUNK]
        dt_c = dt_e[rs]
        xs_c = xs[rs]
        x_dt = (xs_c * dt_c).astype(BF16)
        x_w = (xs_c * (jnp.exp(a_last - ac_c) * dt_c)).astype(BF16)
        e_ac = jnp.exp(ac_c)
        for g in range(SSD_GROUPS):
            gs = slice(g * SSD_STATE, (g + 1) * SSD_STATE)
            ws = slice(g * SSD_HPG * SSD_HEADDIM, (g + 1) * SSD_HPG * SSD_HEADDIM)
            c_g = cm[rs, gs]
            b_g = bm[rs, gs]
            cb = _dot_nt(c_g, b_g)
            x_dt_g = x_dt[:, ws]
            lhs_parts = []
            rhs_parts = []
            for hh in range(SSD_HPG):
                h = g * SSD_HPG + hh
                seg = ac_w[rs, h * LANES:(h + 1) * LANES] - acum_t[DT_COL + h:DT_COL + h + 1, rs]
                lmat = jnp.where(tril128, jnp.exp(jnp.where(tril128, seg, 0.0)), 0.0)
                lhs_parts.append((cb * lmat).astype(BF16))
                rhs_parts.append(jnp.where((lane_g // SSD_HEADDIM) == hh, x_dt_g,
                                           jnp.zeros_like(x_dt_g)))
            intra = _dot(jnp.concatenate(lhs_parts, axis=1), jnp.concatenate(rhs_parts, axis=0))
            st = ssd_state[g]
            inter = _dot(c_g, st.astype(BF16)) * e_ac[:, ws]
            mix_scr[rs, GLA_WIDTH + g * 256:GLA_WIDTH + (g + 1) * 256] = intra + inter
            ssd_state[g] = st * jnp.exp(a_last[:, ws]) + _dot_tn(b_g, x_w[:, ws])

    o = mix_scr[:, 0:GLA_WIDTH]
    g_gate = _silu(pm_ref[:, 1024:1536].astype(F32))
    gla_parts = []
    for h in range(GLA_HEADS):
        o_h = o[:, h * GLA_DV:(h + 1) * GLA_DV]
        ms = jnp.mean(o_h * o_h, axis=-1, keepdims=True)
        gla_parts.append(o_h * lax.rsqrt(ms + GROUP_EPS))
    gla_out = jnp.concatenate(gla_parts, axis=1) * gnw_ref[...] * g_gate

    y = mix_scr[:, GLA_WIDTH:] + dexp_ref[...] * xs
    y = y * _silu(pm_ref[:, 1536:2048].astype(F32))
    ssd_parts = []
    for g in range(SSD_GROUPS):
        y_g = y[:, g * 256:(g + 1) * 256]
        ms = jnp.mean(y_g * y_g, axis=-1, keepdims=True)
        ssd_parts.append(y_g * lax.rsqrt(ms + GROUP_EPS))
    ssd_out = jnp.concatenate(ssd_parts, axis=1) * snw_ref[...]

    mixed = jnp.concatenate([gla_out, ssd_out], axis=1).astype(BF16)
    x1 = x_ref[...] + _dot(mixed, wout_ref[...])
    x1_ref[...] = x1

    ms = jnp.mean(x1 * x1, axis=-1, keepdims=True)
    h2 = x1 * lax.rsqrt(ms + EPS) * nfw_ref[...]
    h2_ref[...] = h2
    lg_ref[...] = _dot_hi(h2, rw_ref[...]) + rb_ref[...]


def _mixer(pm, ps, x, wup, balpha, gnw, convw, convb, dtb, aneg, dexp, snw, wout, nfw, rw, rb):
    bsz, seqlen, _ = x.shape

    def full(a):
        return pl.BlockSpec(a.shape, lambda b, l: (0,) * a.ndim)

    def tile(width):
        return pl.BlockSpec((None, TL, width), lambda b, l: (b, l, 0))

    params = (wup, balpha, gnw, convw, convb, dtb, aneg, dexp, snw, wout, nfw, rw, rb)
    return pl.pallas_call(
        _mixer_kernel,
        grid=(bsz, seqlen // TL),
        in_specs=[tile(N_MAIN), tile(N_SMALL), tile(D_MODEL)] + [full(p) for p in params],
        out_specs=[tile(D_MODEL), tile(D_MODEL), tile(LANES)],
        out_shape=[
            jax.ShapeDtypeStruct((bsz, seqlen, D_MODEL), F32),
            jax.ShapeDtypeStruct((bsz, seqlen, D_MODEL), F32),
            jax.ShapeDtypeStruct((bsz, seqlen, LANES), F32),
        ],
        scratch_shapes=[
            pltpu.VMEM((GLA_DV, GLA_KW), F32),
            pltpu.VMEM((SSD_GROUPS, SSD_STATE, SSD_HPG * SSD_HEADDIM), F32),
            pltpu.VMEM((8, SSD_CONV_CH), F32),
            pltpu.VMEM((TL, D_MODEL), F32),
        ],
        compiler_params=pltpu.CompilerParams(
            dimension_semantics=("arbitrary", "arbitrary"), vmem_limit_bytes=VMEM_LIMIT),
        name="mixer",
    )(pm, ps, x, *params)


def _route_kernel(lg_ref, pos_ref, gate_ref, cnt_ref, cnt_scr, run_scr):
    phase = pl.program_id(0)
    step = pl.program_id(1)

    @pl.when((phase == 0) & (step == 0))
    def _():
        cnt_scr[...] = jnp.zeros_like(cnt_scr)
        run_scr[...] = jnp.zeros_like(run_scr)

    lg = lg_ref[...]
    lane = _iota((TR, LANES), 1)
    work = lg
    onehots = []
    vals = []
    for _ in range(TOP_K):
        m = jnp.max(work, axis=-1, keepdims=True)
        idx = jnp.min(jnp.where(work == m, lane, LANES), axis=-1, keepdims=True)
        oh = lane == idx
        onehots.append(oh)
        vals.append(m)
        work = jnp.where(oh, -jnp.inf, work)
    multi = jnp.where(onehots[0] | onehots[1] | onehots[2] | onehots[3], 1.0, 0.0)
    tile_cnt = jnp.sum(multi, axis=0, keepdims=True)

    @pl.when(phase == 0)
    def _():
        cnt_scr[...] = cnt_scr[...] + tile_cnt

    @pl.when(phase == 1)
    def _():
        counts = cnt_scr[...]
        padded = jnp.floor((counts + (TM - 1)) * (1.0 / TM)) * TM
        upper = jnp.where(_iota((LANES, LANES), 0) < _iota((LANES, LANES), 1), 1.0, 0.0).astype(BF16)
        offs = _dot_sel_rhs(jnp.broadcast_to(padded, (8, LANES)), upper)[0:1]
        strict = jnp.where(_iota((TR, TR), 1) < _iota((TR, TR), 0), 1.0, 0.0).astype(BF16)
        rank = _dot(strict, multi.astype(BF16)) + run_scr[...]
        run_scr[...] = run_scr[...] + tile_cnt
        base = offs + rank
        exps = [jnp.exp(v - vals[0]) for v in vals]
        den = exps[0] + exps[1] + exps[2] + exps[3]
        pos_mat = jnp.zeros((TR, LANES), F32)
        gate_mat = jnp.zeros((TR, LANES), F32)
        for k in range(TOP_K):
            pos_k = jnp.sum(jnp.where(onehots[k], base, 0.0), axis=-1, keepdims=True)
            pos_mat = jnp.where(lane == k, pos_k, pos_mat)
            gate_mat = jnp.where(lane == k, exps[k] / den, gate_mat)
        pos_ref[...] = pos_mat.T[0:8].astype(jnp.int32)
        gate_ref[...] = gate_mat
        cnt_ref[...] = jnp.broadcast_to(counts, (8, LANES))


def _route(logits):
    t = logits.shape[0]
    return pl.pallas_call(
        _route_kernel,
        grid=(2, t // TR),
        in_specs=[pl.BlockSpec((TR, LANES), lambda p, i: (i, 0))],
        out_specs=[
            pl.BlockSpec((8, TR), lambda p, i: (0, i * p)),
            pl.BlockSpec((TR, LANES), lambda p, i: (i * p, 0)),
            pl.BlockSpec((8, LANES), lambda p, i: (0, 0)),
        ],
        out_shape=[
            jax.ShapeDtypeStruct((8, t), jnp.int32),
            jax.ShapeDtypeStruct((t, LANES), F32),
            jax.ShapeDtypeStruct((8, LANES), F32),
        ],
        scratch_shapes=[pltpu.VMEM((1, LANES), F32), pltpu.VMEM((1, LANES), F32)],
        compiler_params=pltpu.CompilerParams(
            dimension_semantics=("arbitrary", "arbitrary"), vmem_limit_bytes=VMEM_LIMIT),
        name="route",
    )(logits)


def _row_copy(src_ref, src_row, dst_ref, dst_row, sem):
    return pltpu.make_async_copy(src_ref.at[pl.ds(src_row, 1), :], dst_ref.at[pl.ds(dst_row, 1), :], sem)


def _dispatch_kernel(pos_ref, h_ref, xs_in_ref, xs_ref, sem):
    del xs_in_ref

    def issue(t, carry):
        for k in range(TOP_K):
            _row_copy(h_ref, t, xs_ref, pos_ref[k, t], sem).start()
        return carry

    lax.fori_loop(0, TT, issue, 0)

    def drain(t, carry):
        for k in range(TOP_K):
            _row_copy(h_ref, t, xs_ref, pos_ref[k, t], sem).wait()
        return carry

    lax.fori_loop(0, TT, drain, 0)


def _dispatch(pos, h2, xs_init):
    t = h2.shape[0]
    return pl.pallas_call(
        _dispatch_kernel,
        grid=(t // TT,),
        in_specs=[
            pl.BlockSpec((8, TT), lambda i: (0, i), memory_space=pltpu.SMEM),
            pl.BlockSpec((TT, D_MODEL), lambda i: (i, 0)),
            pl.BlockSpec(memory_space=pl.ANY),
        ],
        out_specs=pl.BlockSpec(memory_space=pl.ANY),
        out_shape=jax.ShapeDtypeStruct(xs_init.shape, xs_init.dtype),
        scratch_shapes=[pltpu.SemaphoreType.DMA(())],
        input_output_aliases={2: 0},
        compiler_params=pltpu.CompilerParams(
            dimension_semantics=("arbitrary",), vmem_limit_bytes=VMEM_LIMIT),
        name="dispatch",
    )(pos, h2, xs_init)


def _experts_kernel(te_ref, nv_ref, x_ref, wg_ref, bg_ref, wu_ref, bu_ref, wd_ref, bd_ref,
                    o_ref, wg_b, wu_b, wd_b):
    i = pl.program_id(0)
    prev = jnp.maximum(i - 1, 0)
    new_expert = (i == 0) | (te_ref[i] != te_ref[prev])

    @pl.when(new_expert)
    def _():
        wg_b[...] = wg_ref[...].astype(BF16)
        wu_b[...] = wu_ref[...].astype(BF16)
        wd_b[...] = wd_ref[...].astype(BF16)

    @pl.when(i < nv_ref[0])
    def _():
        xb = x_ref[...].astype(BF16)
        gate = _dot(xb, wg_b[...]) + bg_ref[...]
        up = _dot(xb, wu_b[...]) + bu_ref[...]
        gate = jnp.minimum(gate, SWIGLU_LIMIT)
        up = jnp.clip(up, -SWIGLU_LIMIT, SWIGLU_LIMIT)
        act = (up + 1.0) * (gate * jax.nn.sigmoid(SWIGLU_ALPHA * gate))
        o_ref[...] = _dot(act.astype(BF16), wd_b[...]) + bd_ref[...]

    @pl.when(i >= nv_ref[0])
    def _():
        o_ref[...] = jnp.zeros_like(o_ref)


def _experts(tile_expert, n_valid, xs, wg, bg, wu, bu, wd, bd):
    p_rows = xs.shape[0]
    n_tiles = p_rows // TM

    def x_map(i, te, nv):
        return (jnp.minimum(i, nv[0] - 1), 0)

    def w_map(i, te, nv):
        return (te[i], 0, 0)

    w_spec = pl.BlockSpec((None, D_MODEL, D_MODEL), w_map)
    b_spec = pl.BlockSpec((None, 1, D_MODEL), w_map)
    return pl.pallas_call(
        _experts_kernel,
        grid_spec=pltpu.PrefetchScalarGridSpec(
            num_scalar_prefetch=2,
            grid=(n_tiles,),
            in_specs=[pl.BlockSpec((TM, D_MODEL), x_map), w_spec, b_spec, w_spec, b_spec, w_spec, b_spec],
            out_specs=pl.BlockSpec((TM, D_MODEL), lambda i, te, nv: (i, 0)),
            scratch_shapes=[pltpu.VMEM((D_MODEL, D_MODEL), BF16)] * 3,
        ),
        out_shape=jax.ShapeDtypeStruct((p_rows, D_MODEL), F32),
        compiler_params=pltpu.CompilerParams(
            dimension_semantics=("arbitrary",), vmem_limit_bytes=VMEM_LIMIT),
        name="experts",
    )(tile_expert, n_valid, xs, wg, bg, wu, bu, wd, bd)


def _combine_kernel(pos_ref, gate_ref, x1_ref, fw_ref, eo_ref, out_ref, buf, sem):
    def issue(t, carry):
        for k in range(TOP_K):
            _row_copy(eo_ref, pos_ref[k, t], buf.at[k], t, sem).start()
        return carry

    lax.fori_loop(0, TT, issue, 0)

    def drain(t, carry):
        for k in range(TOP_K):
            _row_copy(eo_ref, pos_ref[k, t], buf.at[k], t, sem).wait()
        return carry

    lax.fori_loop(0, TT, drain, 0)

    gates = gate_ref[...]
    y = x1_ref[...]
    for k in range(TOP_K):
        y = y + gates[:, k:k + 1] * buf[k]
    ms = jnp.mean(y * y, axis=-1, keepdims=True)
    out_ref[...] = y * lax.rsqrt(ms + EPS) * fw_ref[...]


def _combine(pos, gates, x1, final_w, expert_out):
    t = x1.shape[0]
    return pl.pallas_call(
        _combine_kernel,
        grid=(t // TT,),
        in_specs=[
            pl.BlockSpec((8, TT), lambda i: (0, i), memory_space=pltpu.SMEM),
            pl.BlockSpec((TT, LANES), lambda i: (i, 0)),
            pl.BlockSpec((TT, D_MODEL), lambda i: (i, 0)),
            pl.BlockSpec((1, D_MODEL), lambda i: (0, 0)),
            pl.BlockSpec(memory_space=pl.ANY),
        ],
        out_specs=pl.BlockSpec((TT, D_MODEL), lambda i: (i, 0)),
        out_shape=jax.ShapeDtypeStruct((t, D_MODEL), F32),
        scratch_shapes=[pltpu.VMEM((TOP_K, TT, D_MODEL), F32), pltpu.SemaphoreType.DMA(())],
        compiler_params=pltpu.CompilerParams(
            dimension_semantics=("arbitrary",), vmem_limit_bytes=VMEM_LIMIT),
        name="combine",
    )(pos, gates, x1, final_w, expert_out)


def _pad_lanes(v, offset, fill=0.0):
    row = jnp.full((1, LANES), fill, F32)
    return row.at[0, offset:offset + v.shape[0]].set(v.astype(F32))


def kernel(x, norm_mix_w, w_in, gla_w_alpha_up, gla_b_alpha, gla_norm_w, ssd_conv_w, ssd_conv_b,
           ssd_dt_bias, ssd_A_log, ssd_D, ssd_norm_w, w_out, norm_ffn_w, router_w, router_b,
           moe_w_gate, moe_b_gate, moe_w_up, moe_b_up, moe_w_down, moe_b_down, final_norm_w):
    bsz, seqlen, d = x.shape
    t = bsz * seqlen
    depth = w_in.shape[0]
    assert depth == 1, "the final RMSNorm is fused into the (single) layer's combine step"
    p_rows = t * TOP_K + N_EXPERTS * TM
    n_tiles = p_rows // TM
    for l in range(depth):
        w = w_in[l]
        w_all = jnp.concatenate(
            [w[:, 0:1536], w[:, 1552:3088], w[:, 1536:1552], w[:, 3088:3096],
             jnp.zeros((d, N_SMALL - GLA_GATE_RANK - SSD_HEADS), w.dtype)], axis=1).astype(BF16)
        wup = jnp.zeros((N_SMALL, GLA_KW), F32).at[0:GLA_GATE_RANK].set(gla_w_alpha_up[l])
        dtb = _pad_lanes(ssd_dt_bias[l], DT_COL)
        aneg = _pad_lanes(-jnp.exp(ssd_A_log[l].astype(F32)), DT_COL)
        dexp = jnp.repeat(ssd_D[l].astype(F32), SSD_HEADDIM)[None, :]
        rw = jnp.zeros((d, LANES), F32).at[:, 0:N_EXPERTS].set(router_w[l])
        rb = _pad_lanes(router_b[l], 0, fill=-1e30)

        pm, ps = _inproj(x.reshape(t, d), norm_mix_w[l][None, :], w_all)
        x1, h2, logits = _mixer(
            pm.reshape(bsz, seqlen, N_MAIN), ps.reshape(bsz, seqlen, N_SMALL), x,
            wup, gla_b_alpha[l][None, :], gla_norm_w[l][None, :], ssd_conv_w[l],
            ssd_conv_b[l][None, :], dtb, aneg, dexp, ssd_norm_w[l][None, :],
            w_out[l].astype(BF16), norm_ffn_w[l][None, :], rw, rb)

        pos, gates, counts = _route(logits.reshape(t, LANES))

        cnt = counts[0, 0:N_EXPERTS].astype(jnp.int32)
        ends = jnp.cumsum(((cnt + TM - 1) // TM) * TM)
        n_valid = (ends[-1] // TM).astype(jnp.int32)
        starts = jnp.arange(n_tiles, dtype=jnp.int32) * TM
        tile_expert = jnp.sum(starts[:, None] >= ends[None, :], axis=1).astype(jnp.int32)
        last_expert = tile_expert[jnp.maximum(n_valid - 1, 0)]
        tile_expert = jnp.where(starts < ends[-1], tile_expert, last_expert)

        xs = _dispatch(pos, h2.reshape(t, d), jnp.zeros((p_rows, d), F32))
        eo = _experts(tile_expert, n_valid.reshape(1), xs,
                      moe_w_gate[l], moe_b_gate[l][:, None, :], moe_w_up[l], moe_b_up[l][:, None, :],
                      moe_w_down[l], moe_b_down[l][:, None, :])
        x = _combine(pos, gates, x1.reshape(t, d), final_norm_w[None, :], eo).reshape(bsz, seqlen, d)
    return x
```

```python
import functools

import jax
import jax.numpy as jnp
from jax import lax
from jax.experimental import pallas as pl
from jax.experimental.pallas import tpu as pltpu

F32 = jnp.float32
BF16 = jnp.bfloat16

D_MODEL = 1024
GLA_WIDTH = 512
GLA_HEADS = 4
GLA_DV = 128
GLA_DK = 64
GLA_KW = 256
GLA_GATE_RANK = 16
GLA_GATE_NORM = 16.0
SSD_WIDTH = 512
SSD_HEADDIM = 64
SSD_HEADS = 8
SSD_GROUPS = 2
SSD_HPG = 4
SSD_STATE = 128
SSD_CONV = 4
SSD_CONV_CH = 1024
N_EXPERTS = 32
TOP_K = 4
SWIGLU_LIMIT = 7.0
SWIGLU_ALPHA = 1.702
EPS = 1e-6
GROUP_EPS = 1e-5

LANES = 128
N_MAIN = 3072
N_SMALL = LANES
DT_COL = GLA_GATE_RANK

GLA_CHUNK = 64
SSD_CHUNK = 128
TM_PROJ = 512
TL = 256
TR = 512
TC = 2048
TT = 512
TM_LOG2 = 8
TM = 1 << TM_LOG2
N_COLS = 256
N_ISSUE = 2 * (D_MODEL // N_COLS)
VMEM_LIMIT = 56 * 1024 * 1024


def _dot(a, b):
    return jnp.dot(a, b, preferred_element_type=F32)


def _dot_nt(a, b):
    return lax.dot_general(a, b, (((1,), (1,)), ((), ())), preferred_element_type=F32)


def _dot_tn(a, b):
    return lax.dot_general(a, b, (((0,), (0,)), ((), ())), preferred_element_type=F32)


def _split3(a):
    hi = a.astype(BF16)
    r1 = a - hi.astype(F32)
    mid = r1.astype(BF16)
    lo = (r1 - mid.astype(F32)).astype(BF16)
    return hi, mid, lo


def _dot_sel_lhs(sel, a):
    hi, mid, lo = _split3(a)
    return _dot(sel, hi) + _dot(sel, mid) + _dot(sel, lo)


def _dot_sel_rhs(a, sel):
    hi, mid, lo = _split3(a)
    return _dot(hi, sel) + _dot(mid, sel) + _dot(lo, sel)


def _dot_hi(a, b):
    a_hi = a.astype(BF16)
    a_lo = (a - a_hi.astype(F32)).astype(BF16)
    b_hi = b.astype(BF16)
    b_lo = (b - b_hi.astype(F32)).astype(BF16)
    return _dot(a_hi, b_hi) + _dot(a_lo, b_hi) + _dot(a_hi, b_lo)


def _dot_hi_nt(a, b):
    a_hi = a.astype(BF16)
    a_lo = (a - a_hi.astype(F32)).astype(BF16)
    b_hi = b.astype(BF16)
    b_lo = (b - b_hi.astype(F32)).astype(BF16)
    return _dot_nt(a_hi, b_hi) + _dot_nt(a_lo, b_hi) + _dot_nt(a_hi, b_lo)


def _softplus(x):
    return jnp.maximum(x, 0.0) + jnp.log1p(jnp.exp(-jnp.abs(x)))


def _silu(x):
    return x * jax.nn.sigmoid(x)


def _iota(shape, dim):
    return lax.broadcasted_iota(jnp.int32, shape, dim)


def _inproj_kernel(x_ref, nw_ref, w_ref, pm_ref, ps_ref):
    x = x_ref[...]
    ms = jnp.mean(x * x, axis=-1, keepdims=True)
    h = (x * lax.rsqrt(ms + EPS) * nw_ref[...]).astype(BF16)
    step = 512
    for n0 in range(0, N_MAIN, step):
        pm_ref[:, n0:n0 + step] = _dot(h, w_ref[:, n0:n0 + step]).astype(BF16)
    ps_ref[...] = _dot(h, w_ref[:, N_MAIN:N_MAIN + N_SMALL])


def _inproj(x2d, norm_w, w_all):
    t = x2d.shape[0]
    return pl.pallas_call(
        _inproj_kernel,
        grid=(t // TM_PROJ,),
        in_specs=[
            pl.BlockSpec((TM_PROJ, D_MODEL), lambda i: (i, 0)),
            pl.BlockSpec((1, D_MODEL), lambda i: (0, 0)),
            pl.BlockSpec((D_MODEL, N_MAIN + N_SMALL), lambda i: (0, 0)),
        ],
        out_specs=[
            pl.BlockSpec((TM_PROJ, N_MAIN), lambda i: (i, 0)),
            pl.BlockSpec((TM_PROJ, N_SMALL), lambda i: (i, 0)),
        ],
        out_shape=[
            jax.ShapeDtypeStruct((t, N_MAIN), BF16),
            jax.ShapeDtypeStruct((t, N_SMALL), F32),
        ],
        compiler_params=pltpu.CompilerParams(
            dimension_semantics=("arbitrary",), vmem_limit_bytes=VMEM_LIMIT),
        name="inproj",
    )(x2d, norm_w, w_all)


def _mixer_kernel(pm_ref, ps_ref, x_ref, wup_ref, balpha_ref, gnw_ref, convw_ref, convb_ref,
                  dtb_ref, aneg_ref, dexp_ref, snw_ref, wout_ref, nfw_ref, rw_ref, rb_ref,
                  x1_ref, h2_ref, lg_ref,
                  gla_state, ssd_state, conv_tail, mix_scr):
    @pl.when(pl.program_id(1) == 0)
    def _():
        gla_state[...] = jnp.zeros_like(gla_state)
        ssd_state[...] = jnp.zeros_like(ssd_state)
        conv_tail[...] = jnp.zeros_like(conv_tail)

    small = ps_ref[...]

    row = _iota((TL, TL), 0)
    col = _iota((TL, TL), 1)
    causal = col <= row
    cum64 = jnp.where(causal & ((row // GLA_CHUNK) == (col // GLA_CHUNK)), 1.0, 0.0).astype(BF16)
    cum128 = jnp.where(causal & ((row // SSD_CHUNK) == (col // SSD_CHUNK)), 1.0, 0.0).astype(BF16)

    xa = _dot_hi(small, wup_ref[...]) + balpha_ref[...]
    log_a = (jnp.minimum(xa, 0.0) - jnp.log1p(jnp.exp(-jnp.abs(xa)))) * (1.0 / GLA_GATE_NORM)
    bcum = _dot_sel_lhs(cum64, log_a)

    lane_kw = _iota((GLA_CHUNK, GLA_KW), 1)
    head_masks = [(lane_kw // GLA_DK) == h for h in range(GLA_HEADS)]
    lane_kw_s = _iota((GLA_DV, GLA_KW), 1)
    head_masks_s = [(lane_kw_s // GLA_DK) == h for h in range(GLA_HEADS)]
    tril64 = _iota((GLA_CHUNK, GLA_CHUNK), 1) <= _iota((GLA_CHUNK, GLA_CHUNK), 0)
    q_scale = GLA_DK ** -0.5

    for c in range(TL // GLA_CHUNK):
        rs = slice(c * GLA_CHUNK, (c + 1) * GLA_CHUNK)
        bc = bcum[rs]
        b_mid = bc[GLA_CHUNK // 2:GLA_CHUNK // 2 + 1]
        b_last = bc[GLA_CHUNK - 1:GLA_CHUNK]
        qc = pm_ref[rs, 0:GLA_KW].astype(F32) * q_scale
        kc = pm_ref[rs, GLA_KW:2 * GLA_KW].astype(F32)
        vc = pm_ref[rs, 2 * GLA_KW:2 * GLA_KW + GLA_WIDTH]
        q_in = (qc * jnp.exp(bc - b_mid)).astype(BF16)
        k_in = (kc * jnp.exp(b_mid - bc)).astype(BF16)
        q_st = (qc * jnp.exp(bc)).astype(BF16)
        k_st = (kc * jnp.exp(b_last - bc)).astype(BF16)
        st = gla_state[...]
        st_b = st.astype(BF16)
        zero_b = jnp.zeros_like(q_in)
        for h in range(GLA_HEADS):
            scores = _dot_nt(jnp.where(head_masks[h], q_in, zero_b), k_in)
            scores = jnp.where(tril64, scores, 0.0).astype(BF16)
            o_h = _dot(scores, vc[:, h * GLA_DV:(h + 1) * GLA_DV])
            o_h = o_h + _dot_nt(jnp.where(head_masks[h], q_st, zero_b), st_b)
            mix_scr[rs, h * GLA_DV:(h + 1) * GLA_DV] = o_h
        upd = _dot_tn(vc, k_st)
        new_st = st * jnp.exp(b_last)
        for h in range(GLA_HEADS):
            new_st = new_st + jnp.where(head_masks_s[h], upd[h * GLA_DV:(h + 1) * GLA_DV], 0.0)
        gla_state[...] = new_st

    xbc = pm_ref[:, 2048:3072].astype(F32)
    tail = conv_tail[...]
    conv_tail[...] = xbc[TL - 8:TL]
    row8 = _iota((8, SSD_CONV_CH), 0)
    conv = xbc * convw_ref[SSD_CONV - 1:SSD_CONV, :]
    for s in range(1, SSD_CONV):
        shifted = pltpu.roll(xbc, s, 0)
        head = jnp.where(row8 < s, pltpu.roll(tail, s, 0), shifted[0:8])
        shifted = jnp.concatenate([head, shifted[8:]], axis=0)
        conv = conv + shifted * convw_ref[SSD_CONV - 1 - s:SSD_CONV - s, :]
    act = _silu(conv + convb_ref[...])
    xs = act[:, 0:SSD_WIDTH]
    bm = act[:, SSD_WIDTH:SSD_WIDTH + SSD_GROUPS * SSD_STATE].astype(BF16)
    cm = act[:, SSD_WIDTH + SSD_GROUPS * SSD_STATE:].astype(BF16)

    dt_full = _softplus(small + dtb_ref[...])
    a_full = dt_full * aneg_ref[...]
    acum = _dot_sel_lhs(cum128, a_full)
    acum_t = acum.T

    e_row = _iota((N_SMALL, SSD_WIDTH), 0)
    e_col = _iota((N_SMALL, SSD_WIDTH), 1)
    spread64 = jnp.where(e_row == DT_COL + e_col // SSD_HEADDIM, 1.0, 0.0).astype(BF16)
    e_row2 = _iota((N_SMALL, SSD_HEADS * LANES), 0)
    e_col2 = _iota((N_SMALL, SSD_HEADS * LANES), 1)
    spread128 = jnp.where(e_row2 == DT_COL + e_col2 // LANES, 1.0, 0.0).astype(BF16)
    dt_e = _dot_sel_rhs(dt_full, spread64)
    ac_e = _dot_sel_rhs(acum, spread64)
    ac_w = _dot_sel_rhs(acum, spread128)

    tril128 = _iota((SSD_CHUNK, SSD_CHUNK), 1) <= _iota((SSD_CHUNK, SSD_CHUNK), 0)
    lane_g = _iota((SSD_CHUNK, SSD_HPG * SSD_HEADDIM), 1)
    for c in range(TL // SSD_CHUNK):
        rs = slice(c * SSD_CHUNK, (c + 1) * SSD_CHUNK)
        ac_c = ac_e[rs]
        a_last = ac_c[SSD_CHUNK - 1:SSD_CHUNK]
        dt_c = dt_e[rs]
        xs_c = xs[rs]
        x_dt = (xs_c * dt_c).astype(BF16)
        x_w = (xs_c * (jnp.exp(a_last - ac_c) * dt_c)).astype(BF16)
        e_ac = jnp.exp(ac_c)
        for g in range(SSD_GROUPS):
            gs = slice(g * SSD_STATE, (g + 1) * SSD_STATE)
            ws = slice(g * SSD_HPG * SSD_HEADDIM, (g + 1) * SSD_HPG * SSD_HEADDIM)
            c_g = cm[rs, gs]
            b_g = bm[rs, gs]
            cb = _dot_nt(c_g, b_g)
            x_dt_g = x_dt[:, ws]
            lhs_parts = []
            rhs_parts = []
            for hh in range(SSD_HPG):
                h = g * SSD_HPG + hh
                seg = ac_w[rs, h * LANES:(h + 1) * LANES] - acum_t[DT_COL + h:DT_COL + h + 1, rs]
                lmat = jnp.where(tril128, jnp.exp(jnp.where(tril128, seg, 0.0)), 0.0)
                lhs_parts.append((cb * lmat).astype(BF16))
                rhs_parts.append(jnp.where((lane_g // SSD_HEADDIM) == hh, x_dt_g,
                                           jnp.zeros_like(x_dt_g)))
            intra = _dot(jnp.concatenate(lhs_parts, axis=1), jnp.concatenate(rhs_parts, axis=0))
            st = ssd_state[g]
            inter = _dot(c_g, st.astype(BF16)) * e_ac[:, ws]
            mix_scr[rs, GLA_WIDTH + g * 256:GLA_WIDTH + (g + 1) * 256] = intra + inter
            ssd_state[g] = st * jnp.exp(a_last[:, ws]) + _dot_tn(b_g, x_w[:, ws])

    o = mix_scr[:, 0:GLA_WIDTH]
    g_gate = _silu(pm_ref[:, 1024:1536].astype(F32))
    gla_parts = []
    for h in range(GLA_HEADS):
        o_h = o[:, h * GLA_DV:(h + 1) * GLA_DV]
        ms = jnp.mean(o_h * o_h, axis=-1, keepdims=True)
        gla_parts.append(o_h * lax.rsqrt(ms + GROUP_EPS))
    gla_out = jnp.concatenate(gla_parts, axis=1) * gnw_ref[...] * g_gate

    y = mix_scr[:, GLA_WIDTH:] + dexp_ref[...] * xs
    y = y * _silu(pm_ref[:, 1536:2048].astype(F32))
    ssd_parts = []
    for g in range(SSD_GROUPS):
        y_g = y[:, g * 256:(g + 1) * 256]
        ms = jnp.mean(y_g * y_g, axis=-1, keepdims=True)
        ssd_parts.append(y_g * lax.rsqrt(ms + GROUP_EPS))
    ssd_out = jnp.concatenate(ssd_parts, axis=1) * snw_ref[...]

    mixed = jnp.concatenate([gla_out, ssd_out], axis=1).astype(BF16)
    x1 = x_ref[...] + _dot(mixed, wout_ref[...])
    x1_ref[...] = x1

    ms = jnp.mean(x1 * x1, axis=-1, keepdims=True)
    h2 = x1 * lax.rsqrt(ms + EPS) * nfw_ref[...]
    h2_ref[...] = h2
    lg_ref[...] = _dot_hi(h2, rw_ref[...]) + rb_ref[...]


def _mixer(pm, ps, x, wup, balpha, gnw, convw, convb, dtb, aneg, dexp, snw, wout, nfw, rw, rb):
    bsz, seqlen, _ = x.shape

    def full(a):
        return pl.BlockSpec(a.shape, lambda b, l: (0,) * a.ndim)

    def tile(width):
        return pl.BlockSpec((None, TL, width), lambda b, l: (b, l, 0))

    params = (wup, balpha, gnw, convw, convb, dtb, aneg, dexp, snw, wout, nfw, rw, rb)
    return pl.pallas_call(
        _mixer_kernel,
        grid=(bsz, seqlen // TL),
        in_specs=[tile(N_MAIN), tile(N_SMALL), tile(D_MODEL)] + [full(p) for p in params],
        out_specs=[tile(D_MODEL), tile(D_MODEL), tile(LANES)],
        out_shape=[
            jax.ShapeDtypeStruct((bsz, seqlen, D_MODEL), F32),
            jax.ShapeDtypeStruct((bsz, seqlen, D_MODEL), F32),
            jax.ShapeDtypeStruct((bsz, seqlen, LANES), F32),
        ],
        scratch_shapes=[
            pltpu.VMEM((GLA_DV, GLA_KW), F32),
            pltpu.VMEM((SSD_GROUPS, SSD_STATE, SSD_HPG * SSD_HEADDIM), F32),
            pltpu.VMEM((8, SSD_CONV_CH), F32),
            pltpu.VMEM((TL, D_MODEL), F32),
        ],
        compiler_params=pltpu.CompilerParams(
            dimension_semantics=("arbitrary", "arbitrary"), vmem_limit_bytes=VMEM_LIMIT),
        name="mixer",
    )(pm, ps, x, *params)


def _route_kernel(lg_ref, pos_ref, gate_ref, cnt_ref, cnt_scr, run_scr):
    phase = pl.program_id(0)
    step = pl.program_id(1)

    @pl.when((phase == 0) & (step == 0))
    def _():
        cnt_scr[...] = jnp.zeros_like(cnt_scr)
        run_scr[...] = jnp.zeros_like(run_scr)

    lg = lg_ref[...]
    lane = _iota((TR, LANES), 1)
    work = lg
    onehots = []
    vals = []
    for _ in range(TOP_K):
        m = jnp.max(work, axis=-1, keepdims=True)
        idx = jnp.min(jnp.where(work == m, lane, LANES), axis=-1, keepdims=True)
        oh = lane == idx
        onehots.append(oh)
        vals.append(m)
        work = jnp.where(oh, -jnp.inf, work)
    multi = jnp.where(onehots[0] | onehots[1] | onehots[2] | onehots[3], 1.0, 0.0)
    tile_cnt = jnp.sum(multi, axis=0, keepdims=True)

    @pl.when(phase == 0)
    def _():
        cnt_scr[...] = cnt_scr[...] + tile_cnt

    @pl.when(phase == 1)
    def _():
        counts = cnt_scr[...]
        padded = jnp.floor((counts + (TM - 1)) * (1.0 / TM)) * TM
        upper = jnp.where(_iota((LANES, LANES), 0) < _iota((LANES, LANES), 1), 1.0, 0.0).astype(BF16)
        offs = _dot_sel_rhs(jnp.broadcast_to(padded, (8, LANES)), upper)[0:1]
        strict = jnp.where(_iota((TR, TR), 1) < _iota((TR, TR), 0), 1.0, 0.0).astype(BF16)
        rank = _dot(strict, multi.astype(BF16)) + run_scr[...]
        run_scr[...] = run_scr[...] + tile_cnt
        base = offs + rank
        exps = [jnp.exp(v - vals[0]) for v in vals]
        den = exps[0] + exps[1] + exps[2] + exps[3]
        pos_mat = jnp.zeros((TR, LANES), F32)
        gate_mat = jnp.zeros((TR, LANES), F32)
        for k in range(TOP_K):
            pos_k = jnp.sum(jnp.where(onehots[k], base, 0.0), axis=-1, keepdims=True)
            pos_mat = jnp.where(lane == k, pos_k, pos_mat)
            gate_mat = jnp.where(lane == k, exps[k] / den, gate_mat)
        pos_ref[...] = pos_mat.T[0:8].astype(jnp.int32)
        gate_ref[...] = gate_mat
        cnt_ref[...] = jnp.broadcast_to(counts, (8, LANES))


def _route(logits):
    t = logits.shape[0]
    return pl.pallas_call(
        _route_kernel,
        grid=(2, t // TR),
        in_specs=[pl.BlockSpec((TR, LANES), lambda p, i: (i, 0))],
        out_specs=[
            pl.BlockSpec((8, TR), lambda p, i: (0, i * p)),
            pl.BlockSpec((TR, LANES), lambda p, i: (i * p, 0)),
            pl.BlockSpec((8, LANES), lambda p, i: (0, 0)),
        ],
        out_shape=[
            jax.ShapeDtypeStruct((8, t), jnp.int32),
            jax.ShapeDtypeStruct((t, LANES), F32),
            jax.ShapeDtypeStruct((8, LANES), F32),
        ],
        scratch_shapes=[pltpu.VMEM((1, LANES), F32), pltpu.VMEM((1, LANES), F32)],
        compiler_params=pltpu.CompilerParams(
            dimension_semantics=("arbitrary", "arbitrary"), vmem_limit_bytes=VMEM_LIMIT),
        name="route",
    )(logits)


def _dump_row(slot, n_tokens):
    tile = lax.shift_right_logical(slot, TM_LOG2)
    return TOP_K * n_tokens + (tile & 1) * TM + (slot & (TM - 1))


def _codes_kernel(pos_ref, cnt_ref, codes_ref, *, n_tokens):
    step = pl.program_id(0)

    def fill(p, carry):
        codes_ref[p] = _dump_row(p, n_tokens)
        return carry

    @pl.when(step == 0)
    def _():
        def per_expert(e, start):
            c = cnt_ref[e]
            end = start + lax.shift_left(lax.shift_right_logical(c + (TM - 1), TM_LOG2), TM_LOG2)
            lax.fori_loop(start + c, end, fill, 0)
            return end
        total = lax.fori_loop(0, N_EXPERTS, per_expert, 0)
        lax.fori_loop(total, codes_ref.shape[0], fill, 0)

    base = step * TC

    def body(t, carry):
        for k in range(TOP_K):
            codes_ref[pos_ref[k, t]] = k * n_tokens + base + t
        return carry

    lax.fori_loop(0, TC, body, 0, unroll=4)


def _codes(pos, counts, p_rows):
    t = pos.shape[1]
    return pl.pallas_call(
        functools.partial(_codes_kernel, n_tokens=t),
        grid=(t // TC,),
        in_specs=[
            pl.BlockSpec((8, TC), lambda i: (0, i), memory_space=pltpu.SMEM),
            pl.BlockSpec(memory_space=pltpu.SMEM),
        ],
        out_specs=pl.BlockSpec(memory_space=pltpu.SMEM),
        out_shape=jax.ShapeDtypeStruct((p_rows,), jnp.int32),
        compiler_params=pltpu.CompilerParams(dimension_semantics=("arbitrary",)),
        name="codes",
    )(pos, counts)


def _experts_kernel(te_ref, nv_ref, codes_ref, h_ref, wg_ref, bg_ref, wu_ref, bu_ref, wd_ref, bd_ref,
                    y_ref, xbuf, obuf, act, wg_b, wu_b, wd_b, gsem, ssem, *, n_tokens):
    i = pl.program_id(0)
    n_valid = nv_ref[0]
    slot = i & 1
    other = 1 - slot

    def gather_row(tile, r, s):
        src = codes_ref[tile * TM + r] & (n_tokens - 1)
        return pltpu.make_async_copy(h_ref.at[pl.ds(src, 1), :], xbuf.at[s, pl.ds(r, 1), :], gsem.at[s])

    def scatter_row(dst, r, s):
        return pltpu.make_async_copy(obuf.at[s, pl.ds(r, 1), :], y_ref.at[pl.ds(dst, 1), :], ssem.at[s])

    def tile_gather_done(s):
        return pltpu.make_async_copy(h_ref.at[pl.ds(0, TM), :], xbuf.at[s], gsem.at[s])

    def tile_scatter_done(s):
        return pltpu.make_async_copy(obuf.at[s], y_ref.at[pl.ds(0, TM), :], ssem.at[s])

    @pl.when(i < n_valid)
    def _():
        @pl.when(i == 0)
        def _():
            obuf[...] = jnp.zeros_like(obuf)

            def first(r, carry):
                gather_row(0, r, 0).start()
                scatter_row(TOP_K * n_tokens + r, r, 1).start()
                return carry
            lax.fori_loop(0, TM, first, 0)
            tile_scatter_done(1).wait()

        prev = jnp.maximum(i - 1, 0)

        @pl.when((i == 0) | (te_ref[i] != te_ref[prev]))
        def _():
            wg_b[...] = wg_ref[...].astype(BF16)
            wu_b[...] = wu_ref[...].astype(BF16)
            wd_b[...] = wd_ref[...].astype(BF16)

        tile_gather_done(slot).wait()
        nxt = jnp.minimum(i + 1, n_valid - 1)
        is_first = i == 0

        def issue(chunk):
            rows = TM // N_ISSUE
            for r in range(chunk * rows, (chunk + 1) * rows):
                gather_row(nxt, r, other).start()
                dst = jnp.where(is_first, TOP_K * n_tokens + TM + r, codes_ref[prev * TM + r])
                scatter_row(dst, r, other).start()

        xb = xbuf[slot].astype(BF16)
        for n in range(D_MODEL // N_COLS):
            cs = slice(n * N_COLS, (n + 1) * N_COLS)
            gate = _dot(xb, wg_b[:, cs]) + bg_ref[:, cs]
            up = _dot(xb, wu_b[:, cs]) + bu_ref[:, cs]
            gate = jnp.minimum(gate, SWIGLU_LIMIT)
            up = jnp.clip(up, -SWIGLU_LIMIT, SWIGLU_LIMIT)
            act[:, cs] = ((up + 1.0) * (gate * jax.nn.sigmoid(SWIGLU_ALPHA * gate))).astype(BF16)
            issue(n)

        @pl.when(i >= 1)
        def _():
            tile_scatter_done(slot).wait()

        a = act[...]
        for n in range(D_MODEL // N_COLS):
            cs = slice(n * N_COLS, (n + 1) * N_COLS)
            obuf[slot, :, cs] = _dot(a, wd_b[:, cs]) + bd_ref[:, cs]
            issue(D_MODEL // N_COLS + n)

        @pl.when(i == n_valid - 1)
        def _():
            tile_gather_done(other).wait()
            tile_scatter_done(other).wait()

            def last(r, carry):
                scatter_row(codes_ref[i * TM + r], r, slot).start()
                return carry
            lax.fori_loop(0, TM, last, 0)
            tile_scatter_done(slot).wait()


def _experts(tile_expert, n_valid, codes, h2, wg, bg, wu, bu, wd, bd):
    t = h2.shape[0]
    n_tiles = codes.shape[0] // TM

    def w_map(i, te, nv, cd):
        return (te[i], 0, 0)

    w_spec = pl.BlockSpec((None, D_MODEL, D_MODEL), w_map)
    b_spec = pl.BlockSpec((None, 1, D_MODEL), w_map)
    return pl.pallas_call(
        functools.partial(_experts_kernel, n_tokens=t),
        grid_spec=pltpu.PrefetchScalarGridSpec(
            num_scalar_prefetch=3,
            grid=(n_tiles,),
            in_specs=[pl.BlockSpec(memory_space=pl.ANY), w_spec, b_spec, w_spec, b_spec, w_spec, b_spec],
            out_specs=pl.BlockSpec(memory_space=pl.ANY),
            scratch_shapes=[
                pltpu.VMEM((2, TM, D_MODEL), F32),
                pltpu.VMEM((2, TM, D_MODEL), F32),
                pltpu.VMEM((TM, D_MODEL), BF16),
                pltpu.VMEM((D_MODEL, D_MODEL), BF16),
                pltpu.VMEM((D_MODEL, D_MODEL), BF16),
                pltpu.VMEM((D_MODEL, D_MODEL), BF16),
                pltpu.SemaphoreType.DMA((2,)),
                pltpu.SemaphoreType.DMA((2,)),
            ],
        ),
        out_shape=jax.ShapeDtypeStruct((TOP_K * t + 2 * TM, D_MODEL), F32),
        compiler_params=pltpu.CompilerParams(
            dimension_semantics=("arbitrary",), vmem_limit_bytes=VMEM_LIMIT),
        name="experts",
    )(tile_expert, n_valid, codes, h2, wg, bg, wu, bu, wd, bd)


def _combine_kernel(gate_ref, x1_ref, fw_ref, y0_ref, y1_ref, y2_ref, y3_ref, out_ref):
    gates = gate_ref[...]
    y = x1_ref[...]
    for k, y_ref in enumerate((y0_ref, y1_ref, y2_ref, y3_ref)):
        y = y + gates[:, k:k + 1] * y_ref[...]
    ms = jnp.mean(y * y, axis=-1, keepdims=True)
    out_ref[...] = y * lax.rsqrt(ms + EPS) * fw_ref[...]


def _combine(gates, x1, final_w, y4):
    t = x1.shape[0]
    steps = t // TT

    def y_spec(k):
        return pl.BlockSpec((TT, D_MODEL), lambda i: (k * steps + i, 0))

    return pl.pallas_call(
        _combine_kernel,
        grid=(steps,),
        in_specs=[
            pl.BlockSpec((TT, LANES), lambda i: (i, 0)),
            pl.BlockSpec((TT, D_MODEL), lambda i: (i, 0)),
            pl.BlockSpec((1, D_MODEL), lambda i: (0, 0)),
            y_spec(0), y_spec(1), y_spec(2), y_spec(3),
        ],
        out_specs=pl.BlockSpec((TT, D_MODEL), lambda i: (i, 0)),
        out_shape=jax.ShapeDtypeStruct((t, D_MODEL), F32),
        compiler_params=pltpu.CompilerParams(
            dimension_semantics=("arbitrary",), vmem_limit_bytes=VMEM_LIMIT),
        name="combine",
    )(gates, x1, final_w, y4, y4, y4, y4)


def _pad_lanes(v, offset, fill=0.0):
    row = jnp.full((1, LANES), fill, F32)
    return row.at[0, offset:offset + v.shape[0]].set(v.astype(F32))


def kernel(x, norm_mix_w, w_in, gla_w_alpha_up, gla_b_alpha, gla_norm_w, ssd_conv_w, ssd_conv_b,
           ssd_dt_bias, ssd_A_log, ssd_D, ssd_norm_w, w_out, norm_ffn_w, router_w, router_b,
           moe_w_gate, moe_b_gate, moe_w_up, moe_b_up, moe_w_down, moe_b_down, final_norm_w):
    bsz, seqlen, d = x.shape
    t = bsz * seqlen
    depth = w_in.shape[0]
    assert depth == 1, "the final RMSNorm is fused into the (single) layer's combine step"
    assert t & (t - 1) == 0, "slot codes pack (choice, token) with a power-of-two token count"
    p_rows = t * TOP_K + N_EXPERTS * TM
    n_tiles = p_rows // TM
    for l in range(depth):
        w = w_in[l]
        w_all = jnp.concatenate(
            [w[:, 0:1536], w[:, 1552:3088], w[:, 1536:1552], w[:, 3088:3096],
             jnp.zeros((d, N_SMALL - GLA_GATE_RANK - SSD_HEADS), w.dtype)], axis=1).astype(BF16)
        wup = jnp.zeros((N_SMALL, GLA_KW), F32).at[0:GLA_GATE_RANK].set(gla_w_alpha_up[l])
        dtb = _pad_lanes(ssd_dt_bias[l], DT_COL)
        aneg = _pad_lanes(-jnp.exp(ssd_A_log[l].astype(F32)), DT_COL)
        dexp = jnp.repeat(ssd_D[l].astype(F32), SSD_HEADDIM)[None, :]
        rw = jnp.zeros((d, LANES), F32).at[:, 0:N_EXPERTS].set(router_w[l])
        rb = _pad_lanes(router_b[l], 0, fill=-1e30)

        pm, ps = _inproj(x.reshape(t, d), norm_mix_w[l][None, :], w_all)
        x1, h2, logits = _mixer(
            pm.reshape(bsz, seqlen, N_MAIN), ps.reshape(bsz, seqlen, N_SMALL), x,
            wup, gla_b_alpha[l][None, :], gla_norm_w[l][None, :], ssd_conv_w[l],
            ssd_conv_b[l][None, :], dtb, aneg, dexp, ssd_norm_w[l][None, :],
            w_out[l].astype(BF16), norm_ffn_w[l][None, :], rw, rb)

        pos, gates, counts = _route(logits.reshape(t, LANES))

        cnt = counts[0, 0:N_EXPERTS].astype(jnp.int32)
        ends = jnp.cumsum(((cnt + TM - 1) // TM) * TM)
        n_valid = (ends[-1] // TM).astype(jnp.int32)
        starts = jnp.arange(n_tiles, dtype=jnp.int32) * TM
        tile_expert = jnp.sum(starts[:, None] >= ends[None, :], axis=1).astype(jnp.int32)
        last_expert = tile_expert[jnp.maximum(n_valid - 1, 0)]
        tile_expert = jnp.where(starts < ends[-1], tile_expert, last_expert)

        codes = _codes(pos, counts[0].astype(jnp.int32), p_rows)
        y4 = _experts(tile_expert, n_valid.reshape(1), codes, h2.reshape(t, d),
                      moe_w_gate[l], moe_b_gate[l][:, None, :], moe_w_up[l], moe_b_up[l][:, None, :],
                      moe_w_down[l], moe_b_down[l][:, None, :])
        x = _combine(gates, x1.reshape(t, d), final_norm_w[None, :], y4).reshape(bsz, seqlen, d)
    return x
```

```python
import functools

import jax
import jax.numpy as jnp
from jax import lax
from jax.experimental import pallas as pl
from jax.experimental.pallas import tpu as pltpu

F32 = jnp.float32
BF16 = jnp.bfloat16

D_MODEL = 1024
GLA_WIDTH = 512
GLA_HEADS = 4
GLA_DV = 128
GLA_DK = 64
GLA_KW = 256
GLA_GATE_RANK = 16
GLA_GATE_NORM = 16.0
SSD_WIDTH = 512
SSD_HEADDIM = 64
SSD_HEADS = 8
SSD_GROUPS = 2
SSD_HPG = 4
SSD_STATE = 128
SSD_CONV = 4
SSD_CONV_CH = 1024
N_EXPERTS = 32
TOP_K = 4
SWIGLU_LIMIT = 7.0
SWIGLU_ALPHA = 1.702
EPS = 1e-6
GROUP_EPS = 1e-5

LANES = 128
ROW_TILE = D_MODEL // LANES
N_MAIN = 3072
N_SMALL = LANES
DT_COL = GLA_GATE_RANK

GLA_CHUNK = 64
SSD_CHUNK = 128
TM_PROJ = 512
TL = 256
TR = 512
TC = 2048
TT = 512
TM_LOG2 = 8
TM = 1 << TM_LOG2
N_COLS = 256
VMEM_LIMIT = 56 * 1024 * 1024


def _dot(a, b):
    return jnp.dot(a, b, preferred_element_type=F32)


def _dot_nt(a, b):
    return lax.dot_general(a, b, (((1,), (1,)), ((), ())), preferred_element_type=F32)


def _dot_tn(a, b):
    return lax.dot_general(a, b, (((0,), (0,)), ((), ())), preferred_element_type=F32)


def _split3(a):
    hi = a.astype(BF16)
    r1 = a - hi.astype(F32)
    mid = r1.astype(BF16)
    lo = (r1 - mid.astype(F32)).astype(BF16)
    return hi, mid, lo


def _dot_sel_lhs(sel, a):
    hi, mid, lo = _split3(a)
    return _dot(sel, hi) + _dot(sel, mid) + _dot(sel, lo)


def _dot_sel_rhs(a, sel):
    hi, mid, lo = _split3(a)
    return _dot(hi, sel) + _dot(mid, sel) + _dot(lo, sel)


def _dot_hi(a, b):
    a_hi = a.astype(BF16)
    a_lo = (a - a_hi.astype(F32)).astype(BF16)
    b_hi = b.astype(BF16)
    b_lo = (b - b_hi.astype(F32)).astype(BF16)
    return _dot(a_hi, b_hi) + _dot(a_lo, b_hi) + _dot(a_hi, b_lo)


def _dot_hi_nt(a, b):
    a_hi = a.astype(BF16)
    a_lo = (a - a_hi.astype(F32)).astype(BF16)
    b_hi = b.astype(BF16)
    b_lo = (b - b_hi.astype(F32)).astype(BF16)
    return _dot_nt(a_hi, b_hi) + _dot_nt(a_lo, b_hi) + _dot_nt(a_hi, b_lo)


def _softplus(x):
    return jnp.maximum(x, 0.0) + jnp.log1p(jnp.exp(-jnp.abs(x)))


def _silu(x):
    return x * jax.nn.sigmoid(x)


def _iota(shape, dim):
    return lax.broadcasted_iota(jnp.int32, shape, dim)


def _store_row_tiles(ref, value):
    rows = value.shape[0]
    for j in range(ROW_TILE):
        ref[pl.ds(j, rows, stride=ROW_TILE), :] = value[:, j * LANES:(j + 1) * LANES]


def _load_row_tiles(ref, rows):
    return jnp.concatenate([ref[pl.ds(j, rows, stride=ROW_TILE), :] for j in range(ROW_TILE)], axis=1)


def _inproj_kernel(x_ref, nw_ref, w_ref, pm_ref, ps_ref):
    x = x_ref[...]
    ms = jnp.mean(x * x, axis=-1, keepdims=True)
    h = (x * lax.rsqrt(ms + EPS) * nw_ref[...]).astype(BF16)
    step = 512
    for n0 in range(0, N_MAIN, step):
        pm_ref[:, n0:n0 + step] = _dot(h, w_ref[:, n0:n0 + step]).astype(BF16)
    ps_ref[...] = _dot(h, w_ref[:, N_MAIN:N_MAIN + N_SMALL])


def _inproj(x2d, norm_w, w_all):
    t = x2d.shape[0]
    return pl.pallas_call(
        _inproj_kernel,
        grid=(t // TM_PROJ,),
        in_specs=[
            pl.BlockSpec((TM_PROJ, D_MODEL), lambda i: (i, 0)),
            pl.BlockSpec((1, D_MODEL), lambda i: (0, 0)),
            pl.BlockSpec((D_MODEL, N_MAIN + N_SMALL), lambda i: (0, 0)),
        ],
        out_specs=[
            pl.BlockSpec((TM_PROJ, N_MAIN), lambda i: (i, 0)),
            pl.BlockSpec((TM_PROJ, N_SMALL), lambda i: (i, 0)),
        ],
        out_shape=[
            jax.ShapeDtypeStruct((t, N_MAIN), BF16),
            jax.ShapeDtypeStruct((t, N_SMALL), F32),
        ],
        compiler_params=pltpu.CompilerParams(
            dimension_semantics=("arbitrary",), vmem_limit_bytes=VMEM_LIMIT),
        name="inproj",
    )(x2d, norm_w, w_all)


def _mixer_kernel(pm_ref, ps_ref, x_ref, wup_ref, balpha_ref, gnw_ref, convw_ref, convb_ref,
                  dtb_ref, aneg_ref, dexp_ref, snw_ref, wout_ref, nfw_ref, rw_ref, rb_ref,
                  x1_ref, h2_ref, lg_ref,
                  gla_state, ssd_state, conv_tail, mix_scr):
    @pl.when(pl.program_id(1) == 0)
    def _():
        gla_state[...] = jnp.zeros_like(gla_state)
        ssd_state[...] = jnp.zeros_like(ssd_state)
        conv_tail[...] = jnp.zeros_like(conv_tail)

    small = ps_ref[...]

    row = _iota((TL, TL), 0)
    col = _iota((TL, TL), 1)
    causal = col <= row
    cum64 = jnp.where(causal & ((row // GLA_CHUNK) == (col // GLA_CHUNK)), 1.0, 0.0).astype(BF16)
    cum128 = jnp.where(causal & ((row // SSD_CHUNK) == (col // SSD_CHUNK)), 1.0, 0.0).astype(BF16)

    xa = _dot_hi(small, wup_ref[...]) + balpha_ref[...]
    log_a = (jnp.minimum(xa, 0.0) - jnp.log1p(jnp.exp(-jnp.abs(xa)))) * (1.0 / GLA_GATE_NORM)
    bcum = _dot_sel_lhs(cum64, log_a)

    lane_kw = _iota((GLA_CHUNK, GLA_KW), 1)
    head_masks = [(lane_kw // GLA_DK) == h for h in range(GLA_HEADS)]
    lane_kw_s = _iota((GLA_DV, GLA_KW), 1)
    head_masks_s = [(lane_kw_s // GLA_DK) == h for h in range(GLA_HEADS)]
    tril64 = _iota((GLA_CHUNK, GLA_CHUNK), 1) <= _iota((GLA_CHUNK, GLA_CHUNK), 0)
    q_scale = GLA_DK ** -0.5

    for c in range(TL // GLA_CHUNK):
        rs = slice(c * GLA_CHUNK, (c + 1) * GLA_CHUNK)
        bc = bcum[rs]
        b_mid = bc[GLA_CHUNK // 2:GLA_CHUNK // 2 + 1]
        b_last = bc[GLA_CHUNK - 1:GLA_CHUNK]
        qc = pm_ref[rs, 0:GLA_KW].astype(F32) * q_scale
        kc = pm_ref[rs, GLA_KW:2 * GLA_KW].astype(F32)
        vc = pm_ref[rs, 2 * GLA_KW:2 * GLA_KW + GLA_WIDTH]
        q_in = (qc * jnp.exp(bc - b_mid)).astype(BF16)
        k_in = (kc * jnp.exp(b_mid - bc)).astype(BF16)
        q_st = (qc * jnp.exp(bc)).astype(BF16)
        k_st = (kc * jnp.exp(b_last - bc)).astype(BF16)
        st = gla_state[...]
        st_b = st.astype(BF16)
        zero_b = jnp.zeros_like(q_in)
        for h in range(GLA_HEADS):
            scores = _dot_nt(jnp.where(head_masks[h], q_in, zero_b), k_in)
            scores = jnp.where(tril64, scores, 0.0).astype(BF16)
            o_h = _dot(scores, vc[:, h * GLA_DV:(h + 1) * GLA_DV])
            o_h = o_h + _dot_nt(jnp.where(head_masks[h], q_st, zero_b), st_b)
            mix_scr[rs, h * GLA_DV:(h + 1) * GLA_DV] = o_h
        upd = _dot_tn(vc, k_st)
        new_st = st * jnp.exp(b_last)
        for h in range(GLA_HEADS):
            new_st = new_st + jnp.where(head_masks_s[h], upd[h * GLA_DV:(h + 1) * GLA_DV], 0.0)
        gla_state[...] = new_st

    xbc = pm_ref[:, 2048:3072].astype(F32)
    tail = conv_tail[...]
    conv_tail[...] = xbc[TL - 8:TL]
    row8 = _iota((8, SSD_CONV_CH), 0)
    conv = xbc * convw_ref[SSD_CONV - 1:SSD_CONV, :]
    for s in range(1, SSD_CONV):
        shifted = pltpu.roll(xbc, s, 0)
        head = jnp.where(row8 < s, pltpu.roll(tail, s, 0), shifted[0:8])
        shifted = jnp.concatenate([head, shifted[8:]], axis=0)
        conv = conv + shifted * convw_ref[SSD_CONV - 1 - s:SSD_CONV - s, :]
    act = _silu(conv + convb_ref[...])
    xs = act[:, 0:SSD_WIDTH]
    bm = act[:, SSD_WIDTH:SSD_WIDTH + SSD_GROUPS * SSD_STATE].astype(BF16)
    cm = act[:, SSD_WIDTH + SSD_GROUPS * SSD_STATE:].astype(BF16)

    dt_full = _softplus(small + dtb_ref[...])
    a_full = dt_full * aneg_ref[...]
    acum = _dot_sel_lhs(cum128, a_full)
    acum_t = acum.T

    e_row = _iota((N_SMALL, SSD_WIDTH), 0)
    e_col = _iota((N_SMALL, SSD_WIDTH), 1)
    spread64 = jnp.where(e_row == DT_COL + e_col // SSD_HEADDIM, 1.0, 0.0).astype(BF16)
    e_row2 = _iota((N_SMALL, SSD_HEADS * LANES), 0)
    e_col2 = _iota((N_SMALL, SSD_HEADS * LANES), 1)
    spread128 = jnp.where(e_row2 == DT_COL + e_col2 // LANES, 1.0, 0.0).astype(BF16)
    dt_e = _dot_sel_rhs(dt_full, spread64)
    ac_e = _dot_sel_rhs(acum, spread64)
    ac_w = _dot_sel_rhs(acum, spread128)

    tril128 = _iota((SSD_CHUNK, SSD_CHUNK), 1) <= _iota((SSD_CHUNK, SSD_CHUNK), 0)
    lane_g = _iota((SSD_CHUNK, SSD_HPG * SSD_HEADDIM), 1)
    for c in range(TL // SSD_CHUNK):
        rs = slice(c * SSD_CHUNK, (c + 1) * SSD_CHUNK)
        ac_c = ac_e[rs]
        a_last = ac_c[SSD_CHUNK - 1:SSD_CHUNK]
        dt_c = dt_e[rs]
        xs_c = xs[rs]
        x_dt = (xs_c * dt_c).astype(BF16)
        x_w = (xs_c * (jnp.exp(a_last - ac_c) * dt_c)).astype(BF16)
        e_ac = jnp.exp(ac_c)
        for g in range(SSD_GROUPS):
            gs = slice(g * SSD_STATE, (g + 1) * SSD_STATE)
            ws = slice(g * SSD_HPG * SSD_HEADDIM, (g + 1) * SSD_HPG * SSD_HEADDIM)
            c_g = cm[rs, gs]
            b_g = bm[rs, gs]
            cb = _dot_nt(c_g, b_g)
            x_dt_g = x_dt[:, ws]
            lhs_parts = []
            rhs_parts = []
            for hh in range(SSD_HPG):
                h = g * SSD_HPG + hh
                seg = ac_w[rs, h * LANES:(h + 1) * LANES] - acum_t[DT_COL + h:DT_COL + h + 1, rs]
                lmat = jnp.where(tril128, jnp.exp(jnp.where(tril128, seg, 0.0)), 0.0)
                lhs_parts.append((cb * lmat).astype(BF16))
                rhs_parts.append(jnp.where((lane_g // SSD_HEADDIM) == hh, x_dt_g,
                                           jnp.zeros_like(x_dt_g)))
            intra = _dot(jnp.concatenate(lhs_parts, axis=1), jnp.concatenate(rhs_parts, axis=0))
            st = ssd_state[g]
            inter = _dot(c_g, st.astype(BF16)) * e_ac[:, ws]
            mix_scr[rs, GLA_WIDTH + g * 256:GLA_WIDTH + (g + 1) * 256] = intra + inter
            ssd_state[g] = st * jnp.exp(a_last[:, ws]) + _dot_tn(b_g, x_w[:, ws])

    o = mix_scr[:, 0:GLA_WIDTH]
    g_gate = _silu(pm_ref[:, 1024:1536].astype(F32))
    gla_parts = []
    for h in range(GLA_HEADS):
        o_h = o[:, h * GLA_DV:(h + 1) * GLA_DV]
        ms = jnp.mean(o_h * o_h, axis=-1, keepdims=True)
        gla_parts.append(o_h * lax.rsqrt(ms + GROUP_EPS))
    gla_out = jnp.concatenate(gla_parts, axis=1) * gnw_ref[...] * g_gate

    y = mix_scr[:, GLA_WIDTH:] + dexp_ref[...] * xs
    y = y * _silu(pm_ref[:, 1536:2048].astype(F32))
    ssd_parts = []
    for g in range(SSD_GROUPS):
        y_g = y[:, g * 256:(g + 1) * 256]
        ms = jnp.mean(y_g * y_g, axis=-1, keepdims=True)
        ssd_parts.append(y_g * lax.rsqrt(ms + GROUP_EPS))
    ssd_out = jnp.concatenate(ssd_parts, axis=1) * snw_ref[...]

    mixed = jnp.concatenate([gla_out, ssd_out], axis=1).astype(BF16)
    x1 = x_ref[...] + _dot(mixed, wout_ref[...])
    x1_ref[...] = x1

    ms = jnp.mean(x1 * x1, axis=-1, keepdims=True)
    h2 = x1 * lax.rsqrt(ms + EPS) * nfw_ref[...]
    _store_row_tiles(h2_ref, h2)
    lg_ref[...] = _dot_hi(h2, rw_ref[...]) + rb_ref[...]


def _mixer(pm, ps, x, wup, balpha, gnw, convw, convb, dtb, aneg, dexp, snw, wout, nfw, rw, rb):
    bsz, seqlen, _ = x.shape

    def full(a):
        return pl.BlockSpec(a.shape, lambda b, l: (0,) * a.ndim)

    def tile(width):
        return pl.BlockSpec((None, TL, width), lambda b, l: (b, l, 0))

    params = (wup, balpha, gnw, convw, convb, dtb, aneg, dexp, snw, wout, nfw, rw, rb)
    return pl.pallas_call(
        _mixer_kernel,
        grid=(bsz, seqlen // TL),
        in_specs=[tile(N_MAIN), tile(N_SMALL), tile(D_MODEL)] + [full(p) for p in params],
        out_specs=[tile(D_MODEL), pl.BlockSpec((None, TL * ROW_TILE, LANES), lambda b, l: (b, l, 0)),
                   tile(LANES)],
        out_shape=[
            jax.ShapeDtypeStruct((bsz, seqlen, D_MODEL), F32),
            jax.ShapeDtypeStruct((bsz, seqlen * ROW_TILE, LANES), F32),
            jax.ShapeDtypeStruct((bsz, seqlen, LANES), F32),
        ],
        scratch_shapes=[
            pltpu.VMEM((GLA_DV, GLA_KW), F32),
            pltpu.VMEM((SSD_GROUPS, SSD_STATE, SSD_HPG * SSD_HEADDIM), F32),
            pltpu.VMEM((8, SSD_CONV_CH), F32),
            pltpu.VMEM((TL, D_MODEL), F32),
        ],
        compiler_params=pltpu.CompilerParams(
            dimension_semantics=("arbitrary", "arbitrary"), vmem_limit_bytes=VMEM_LIMIT),
        name="mixer",
    )(pm, ps, x, *params)


def _route_kernel(lg_ref, pos_ref, gate_ref, cnt_ref, cnt_scr, run_scr):
    phase = pl.program_id(0)
    step = pl.program_id(1)

    @pl.when((phase == 0) & (step == 0))
    def _():
        cnt_scr[...] = jnp.zeros_like(cnt_scr)
        run_scr[...] = jnp.zeros_like(run_scr)

    lg = lg_ref[...]
    lane = _iota((TR, LANES), 1)
    work = lg
    onehots = []
    vals = []
    for _ in range(TOP_K):
        m = jnp.max(work, axis=-1, keepdims=True)
        idx = jnp.min(jnp.where(work == m, lane, LANES), axis=-1, keepdims=True)
        oh = lane == idx
        onehots.append(oh)
        vals.append(m)
        work = jnp.where(oh, -jnp.inf, work)
    multi = jnp.where(onehots[0] | onehots[1] | onehots[2] | onehots[3], 1.0, 0.0)
    tile_cnt = jnp.sum(multi, axis=0, keepdims=True)

    @pl.when(phase == 0)
    def _():
        cnt_scr[...] = cnt_scr[...] + tile_cnt

    @pl.when(phase == 1)
    def _():
        counts = cnt_scr[...]
        padded = jnp.floor((counts + (TM - 1)) * (1.0 / TM)) * TM
        upper = jnp.where(_iota((LANES, LANES), 0) < _iota((LANES, LANES), 1), 1.0, 0.0).astype(BF16)
        offs = _dot_sel_rhs(jnp.broadcast_to(padded, (8, LANES)), upper)[0:1]
        strict = jnp.where(_iota((TR, TR), 1) < _iota((TR, TR), 0), 1.0, 0.0).astype(BF16)
        rank = _dot(strict, multi.astype(BF16)) + run_scr[...]
        run_scr[...] = run_scr[...] + tile_cnt
        base = offs + rank
        exps = [jnp.exp(v - vals[0]) for v in vals]
        den = exps[0] + exps[1] + exps[2] + exps[3]
        pos_mat = jnp.zeros((TR, LANES), F32)
        gate_mat = jnp.zeros((TR, LANES), F32)
        for k in range(TOP_K):
            pos_k = jnp.sum(jnp.where(onehots[k], base, 0.0), axis=-1, keepdims=True)
            pos_mat = jnp.where(lane == k, pos_k, pos_mat)
            gate_mat = jnp.where(lane == k, exps[k] / den, gate_mat)
        pos_ref[...] = pos_mat.T[0:8].astype(jnp.int32)
        gate_ref[...] = gate_mat
        cnt_ref[...] = jnp.broadcast_to(counts, (8, LANES))


def _route(logits):
    t = logits.shape[0]
    return pl.pallas_call(
        _route_kernel,
        grid=(2, t // TR),
        in_specs=[pl.BlockSpec((TR, LANES), lambda p, i: (i, 0))],
        out_specs=[
            pl.BlockSpec((8, TR), lambda p, i: (0, i * p)),
            pl.BlockSpec((TR, LANES), lambda p, i: (i * p, 0)),
            pl.BlockSpec((8, LANES), lambda p, i: (0, 0)),
        ],
        out_shape=[
            jax.ShapeDtypeStruct((8, t), jnp.int32),
            jax.ShapeDtypeStruct((t, LANES), F32),
            jax.ShapeDtypeStruct((8, LANES), F32),
        ],
        scratch_shapes=[pltpu.VMEM((1, LANES), F32), pltpu.VMEM((1, LANES), F32)],
        compiler_params=pltpu.CompilerParams(
            dimension_semantics=("arbitrary", "arbitrary"), vmem_limit_bytes=VMEM_LIMIT),
        name="route",
    )(logits)


def _dump_row(slot, n_tokens):
    tile = lax.shift_right_logical(slot, TM_LOG2)
    return TOP_K * n_tokens + (tile & 1) * TM + (slot & (TM - 1))


def _codes_kernel(pos_ref, cnt_ref, codes_ref, *, n_tokens):
    step = pl.program_id(0)

    def fill(p, carry):
        codes_ref[p] = _dump_row(p, n_tokens)
        return carry

    @pl.when(step == 0)
    def _():
        def per_expert(e, start):
            c = cnt_ref[e]
            end = start + lax.shift_left(lax.shift_right_logical(c + (TM - 1), TM_LOG2), TM_LOG2)
            lax.fori_loop(start + c, end, fill, 0)
            return end
        total = lax.fori_loop(0, N_EXPERTS, per_expert, 0)
        lax.fori_loop(total, codes_ref.shape[0], fill, 0)

    base = step * TC

    def body(t, carry):
        for k in range(TOP_K):
            codes_ref[pos_ref[k, t]] = k * n_tokens + base + t
        return carry

    lax.fori_loop(0, TC, body, 0, unroll=4)


def _codes(pos, counts, p_rows):
    t = pos.shape[1]
    return pl.pallas_call(
        functools.partial(_codes_kernel, n_tokens=t),
        grid=(t // TC,),
        in_specs=[
            pl.BlockSpec((8, TC), lambda i: (0, i), memory_space=pltpu.SMEM),
            pl.BlockSpec(memory_space=pltpu.SMEM),
        ],
        out_specs=pl.BlockSpec(memory_space=pltpu.SMEM),
        out_shape=jax.ShapeDtypeStruct((p_rows,), jnp.int32),
        compiler_params=pltpu.CompilerParams(dimension_semantics=("arbitrary",)),
        name="codes",
    )(pos, counts)


def _experts_kernel(te_ref, nv_ref, codes_ref, h_ref, wg_ref, bg_ref, wu_ref, bu_ref, wd_ref, bd_ref,
                    y_ref, xbuf, obuf, act, wg_b, wu_b, wd_b, gsem, ssem, *, n_tokens):
    i = pl.program_id(0)
    n_valid = nv_ref[0]
    slot = i & 1
    other = 1 - slot

    def row_tile(row):
        return pl.ds(pl.multiple_of(row * ROW_TILE, ROW_TILE), ROW_TILE)

    def gather_row(tile, r, s):
        src = codes_ref[tile * TM + r] & (n_tokens - 1)
        return pltpu.make_async_copy(h_ref.at[row_tile(src), :], xbuf.at[s, row_tile(r), :], gsem.at[s])

    def scatter_row(dst, r, s):
        return pltpu.make_async_copy(obuf.at[s, row_tile(r), :], y_ref.at[row_tile(dst), :], ssem.at[s])

    def tile_gather_done(s):
        return pltpu.make_async_copy(h_ref.at[pl.ds(0, TM * ROW_TILE), :], xbuf.at[s], gsem.at[s])

    def tile_scatter_done(s):
        return pltpu.make_async_copy(obuf.at[s], y_ref.at[pl.ds(0, TM * ROW_TILE), :], ssem.at[s])

    @pl.when(i < n_valid)
    def _():
        @pl.when(i == 0)
        def _():
            obuf[...] = jnp.zeros_like(obuf)

            def first(r, carry):
                gather_row(0, r, 0).start()
                scatter_row(TOP_K * n_tokens + r, r, 1).start()
                return carry
            lax.fori_loop(0, TM, first, 0)
            tile_scatter_done(1).wait()

        prev = jnp.maximum(i - 1, 0)

        @pl.when((i == 0) | (te_ref[i] != te_ref[prev]))
        def _():
            wg_b[...] = wg_ref[...].astype(BF16)
            wu_b[...] = wu_ref[...].astype(BF16)
            wd_b[...] = wd_ref[...].astype(BF16)

        tile_gather_done(slot).wait()

        @pl.when(i >= 1)
        def _():
            tile_scatter_done(slot).wait()

        nxt = jnp.minimum(i + 1, n_valid - 1)
        is_first = i == 0

        def issue(r, carry):
            gather_row(nxt, r, other).start(priority=0)
            dst = jnp.where(is_first, TOP_K * n_tokens + TM + r, codes_ref[prev * TM + r])
            scatter_row(dst, r, other).start(priority=1)
            return carry

        lax.fori_loop(0, TM, issue, 0, unroll=8)

        xb = _load_row_tiles(xbuf.at[slot], TM).astype(BF16)
        for n in range(D_MODEL // N_COLS):
            cs = slice(n * N_COLS, (n + 1) * N_COLS)
            gate = _dot(xb, wg_b[:, cs]) + bg_ref[:, cs]
            up = _dot(xb, wu_b[:, cs]) + bu_ref[:, cs]
            gate = jnp.minimum(gate, SWIGLU_LIMIT)
            up = jnp.clip(up, -SWIGLU_LIMIT, SWIGLU_LIMIT)
            act[:, cs] = ((up + 1.0) * (gate * jax.nn.sigmoid(SWIGLU_ALPHA * gate))).astype(BF16)
        a = act[...]
        for n in range(D_MODEL // N_COLS):
            cs = slice(n * N_COLS, (n + 1) * N_COLS)
            out = _dot(a, wd_b[:, cs]) + bd_ref[:, cs]
            for j in range(N_COLS // LANES):
                obuf[slot, pl.ds(n * (N_COLS // LANES) + j, TM, stride=ROW_TILE), :] = (
                    out[:, j * LANES:(j + 1) * LANES])

        @pl.when(i == n_valid - 1)
        def _():
            tile_gather_done(other).wait()
            tile_scatter_done(other).wait()

            def last(r, carry):
                scatter_row(codes_ref[i * TM + r], r, slot).start()
                return carry
            lax.fori_loop(0, TM, last, 0)
            tile_scatter_done(slot).wait()


def _experts(tile_expert, n_valid, codes, h2, wg, bg, wu, bu, wd, bd):
    t = h2.shape[0] // ROW_TILE
    n_tiles = codes.shape[0] // TM

    def w_map(i, te, nv, cd):
        return (te[i], 0, 0)

    w_spec = pl.BlockSpec((None, D_MODEL, D_MODEL), w_map)
    b_spec = pl.BlockSpec((None, 1, D_MODEL), w_map)
    return pl.pallas_call(
        functools.partial(_experts_kernel, n_tokens=t),
        grid_spec=pltpu.PrefetchScalarGridSpec(
            num_scalar_prefetch=3,
            grid=(n_tiles,),
            in_specs=[pl.BlockSpec(memory_space=pl.ANY), w_spec, b_spec, w_spec, b_spec, w_spec, b_spec],
            out_specs=pl.BlockSpec(memory_space=pl.ANY),
            scratch_shapes=[
                pltpu.VMEM((2, TM * ROW_TILE, LANES), F32),
                pltpu.VMEM((2, TM * ROW_TILE, LANES), F32),
                pltpu.VMEM((TM, D_MODEL), BF16),
                pltpu.VMEM((D_MODEL, D_MODEL), BF16),
                pltpu.VMEM((D_MODEL, D_MODEL), BF16),
                pltpu.VMEM((D_MODEL, D_MODEL), BF16),
                pltpu.SemaphoreType.DMA((2,)),
                pltpu.SemaphoreType.DMA((2,)),
            ],
        ),
        out_shape=jax.ShapeDtypeStruct(((TOP_K * t + 2 * TM) * ROW_TILE, LANES), F32),
        compiler_params=pltpu.CompilerParams(
            dimension_semantics=("arbitrary",), vmem_limit_bytes=VMEM_LIMIT),
        name="experts",
    )(tile_expert, n_valid, codes, h2, wg, bg, wu, bu, wd, bd)


def _combine_kernel(gate_ref, x1_ref, fw_ref, y0_ref, y1_ref, y2_ref, y3_ref, out_ref):
    gates = gate_ref[...]
    y = x1_ref[...]
    for k, y_ref in enumerate((y0_ref, y1_ref, y2_ref, y3_ref)):
        y = y + gates[:, k:k + 1] * _load_row_tiles(y_ref, TT)
    ms = jnp.mean(y * y, axis=-1, keepdims=True)
    out_ref[...] = y * lax.rsqrt(ms + EPS) * fw_ref[...]


def _combine(gates, x1, final_w, y4):
    t = x1.shape[0]
    steps = t // TT

    def y_spec(k):
        return pl.BlockSpec((TT * ROW_TILE, LANES), lambda i: (k * steps + i, 0))

    return pl.pallas_call(
        _combine_kernel,
        grid=(steps,),
        in_specs=[
            pl.BlockSpec((TT, LANES), lambda i: (i, 0)),
            pl.BlockSpec((TT, D_MODEL), lambda i: (i, 0)),
            pl.BlockSpec((1, D_MODEL), lambda i: (0, 0)),
            y_spec(0), y_spec(1), y_spec(2), y_spec(3),
        ],
        out_specs=pl.BlockSpec((TT, D_MODEL), lambda i: (i, 0)),
        out_shape=jax.ShapeDtypeStruct((t, D_MODEL), F32),
        compiler_params=pltpu.CompilerParams(
            dimension_semantics=("arbitrary",), vmem_limit_bytes=VMEM_LIMIT),
        name="combine",
    )(gates, x1, final_w, y4, y4, y4, y4)


def _pad_lanes(v, offset, fill=0.0):
    row = jnp.full((1, LANES), fill, F32)
    return row.at[0, offset:offset + v.shape[0]].set(v.astype(F32))


def kernel(x, norm_mix_w, w_in, gla_w_alpha_up, gla_b_alpha, gla_norm_w, ssd_conv_w, ssd_conv_b,
           ssd_dt_bias, ssd_A_log, ssd_D, ssd_norm_w, w_out, norm_ffn_w, router_w, router_b,
           moe_w_gate, moe_b_gate, moe_w_up, moe_b_up, moe_w_down, moe_b_down, final_norm_w):
    bsz, seqlen, d = x.shape
    t = bsz * seqlen
    depth = w_in.shape[0]
    assert depth == 1, "the final RMSNorm is fused into the (single) layer's combine step"
    assert t & (t - 1) == 0, "slot codes pack (choice, token) with a power-of-two token count"
    p_rows = t * TOP_K + N_EXPERTS * TM
    n_tiles = p_rows // TM
    for l in range(depth):
        w = w_in[l]
        w_all = jnp.concatenate(
            [w[:, 0:1536], w[:, 1552:3088], w[:, 1536:1552], w[:, 3088:3096],
             jnp.zeros((d, N_SMALL - GLA_GATE_RANK - SSD_HEADS), w.dtype)], axis=1).astype(BF16)
        wup = jnp.zeros((N_SMALL, GLA_KW), F32).at[0:GLA_GATE_RANK].set(gla_w_alpha_up[l])
        dtb = _pad_lanes(ssd_dt_bias[l], DT_COL)
        aneg = _pad_lanes(-jnp.exp(ssd_A_log[l].astype(F32)), DT_COL)
        dexp = jnp.repeat(ssd_D[l].astype(F32), SSD_HEADDIM)[None, :]
        rw = jnp.zeros((d, LANES), F32).at[:, 0:N_EXPERTS].set(router_w[l])
        rb = _pad_lanes(router_b[l], 0, fill=-1e30)

        pm, ps = _inproj(x.reshape(t, d), norm_mix_w[l][None, :], w_all)
        x1, h2, logits = _mixer(
            pm.reshape(bsz, seqlen, N_MAIN), ps.reshape(bsz, seqlen, N_SMALL), x,
            wup, gla_b_alpha[l][None, :], gla_norm_w[l][None, :], ssd_conv_w[l],
            ssd_conv_b[l][None, :], dtb, aneg, dexp, ssd_norm_w[l][None, :],
            w_out[l].astype(BF16), norm_ffn_w[l][None, :], rw, rb)

        pos, gates, counts = _route(logits.reshape(t, LANES))

        cnt = counts[0, 0:N_EXPERTS].astype(jnp.int32)
        ends = jnp.cumsum(((cnt + TM - 1) // TM) * TM)
        n_valid = (ends[-1] // TM).astype(jnp.int32)
        starts = jnp.arange(n_tiles, dtype=jnp.int32) * TM
        tile_expert = jnp.sum(starts[:, None] >= ends[None, :], axis=1).astype(jnp.int32)
        last_expert = tile_expert[jnp.maximum(n_valid - 1, 0)]
        tile_expert = jnp.where(starts < ends[-1], tile_expert, last_expert)

        codes = _codes(pos, counts[0].astype(jnp.int32), p_rows)
        y4 = _experts(tile_expert, n_valid.reshape(1), codes, h2.reshape(t * ROW_TILE, LANES),
                      moe_w_gate[l], moe_b_gate[l][:, None, :], moe_w_up[l], moe_b_up[l][:, None, :],
                      moe_w_down[l], moe_b_down[l][:, None, :])
        x = _combine(gates, x1.reshape(t, d), final_norm_w[None, :], y4).reshape(bsz, seqlen, d)
    return x
```

```python
import functools

import jax
import jax.numpy as jnp
from jax import lax
from jax.experimental import pallas as pl
from jax.experimental.pallas import tpu as pltpu

F32 = jnp.float32
BF16 = jnp.bfloat16

D_MODEL = 1024
GLA_WIDTH = 512
GLA_HEADS = 4
GLA_DV = 128
GLA_DK = 64
GLA_KW = 256
GLA_GATE_RANK = 16
GLA_GATE_NORM = 16.0
SSD_WIDTH = 512
SSD_HEADDIM = 64
SSD_HEADS = 8
SSD_GROUPS = 2
SSD_HPG = 4
SSD_STATE = 128
SSD_CONV = 4
SSD_CONV_CH = 1024
N_EXPERTS = 32
TOP_K = 4
SWIGLU_LIMIT = 7.0
SWIGLU_ALPHA = 1.702
EPS = 1e-6
GROUP_EPS = 1e-5

LANES = 128
ROW_TILE = D_MODEL // LANES
N_MAIN = 3072
N_SMALL = LANES
DT_COL = GLA_GATE_RANK

GLA_CHUNK = 64
SSD_CHUNK = 128
TM_PROJ = 512
TL = 256
TR = 512
TC = 2048
TT = 512
TM_LOG2 = 8
TM = 1 << TM_LOG2
N_COLS = 512
N_CHUNKS = D_MODEL // N_COLS
VMEM_LIMIT = 56 * 1024 * 1024


def _dot(a, b):
    return jnp.dot(a, b, preferred_element_type=F32)


def _dot_nt(a, b):
    return lax.dot_general(a, b, (((1,), (1,)), ((), ())), preferred_element_type=F32)


def _dot_tn(a, b):
    return lax.dot_general(a, b, (((0,), (0,)), ((), ())), preferred_element_type=F32)


def _split3(a):
    hi = a.astype(BF16)
    r1 = a - hi.astype(F32)
    mid = r1.astype(BF16)
    lo = (r1 - mid.astype(F32)).astype(BF16)
    return hi, mid, lo


def _dot_sel_lhs(sel, a):
    hi, mid, lo = _split3(a)
    return _dot(sel, hi) + _dot(sel, mid) + _dot(sel, lo)


def _dot_sel_rhs(a, sel):
    hi, mid, lo = _split3(a)
    return _dot(hi, sel) + _dot(mid, sel) + _dot(lo, sel)


def _dot_hi(a, b):
    a_hi = a.astype(BF16)
    a_lo = (a - a_hi.astype(F32)).astype(BF16)
    b_hi = b.astype(BF16)
    b_lo = (b - b_hi.astype(F32)).astype(BF16)
    return _dot(a_hi, b_hi) + _dot(a_lo, b_hi) + _dot(a_hi, b_lo)


def _dot_hi_nt(a, b):
    a_hi = a.astype(BF16)
    a_lo = (a - a_hi.astype(F32)).astype(BF16)
    b_hi = b.astype(BF16)
    b_lo = (b - b_hi.astype(F32)).astype(BF16)
    return _dot_nt(a_hi, b_hi) + _dot_nt(a_lo, b_hi) + _dot_nt(a_hi, b_lo)


def _softplus(x):
    return jnp.maximum(x, 0.0) + jnp.log1p(jnp.exp(-jnp.abs(x)))


def _silu(x):
    return x * jax.nn.sigmoid(x)


def _iota(shape, dim):
    return lax.broadcasted_iota(jnp.int32, shape, dim)


def _store_row_tiles(ref, value):
    rows = value.shape[0]
    for j in range(ROW_TILE):
        ref[pl.ds(j, rows, stride=ROW_TILE), :] = value[:, j * LANES:(j + 1) * LANES]


def _load_row_tiles(ref, rows):
    return jnp.concatenate([ref[pl.ds(j, rows, stride=ROW_TILE), :] for j in range(ROW_TILE)], axis=1)


def _inproj_kernel(x_ref, nw_ref, w_ref, pm_ref, ps_ref):
    x = x_ref[...]
    ms = jnp.mean(x * x, axis=-1, keepdims=True)
    h = (x * lax.rsqrt(ms + EPS) * nw_ref[...]).astype(BF16)
    step = 512
    for n0 in range(0, N_MAIN, step):
        pm_ref[:, n0:n0 + step] = _dot(h, w_ref[:, n0:n0 + step]).astype(BF16)
    ps_ref[...] = _dot(h, w_ref[:, N_MAIN:N_MAIN + N_SMALL])


def _inproj(x2d, norm_w, w_all):
    t = x2d.shape[0]
    return pl.pallas_call(
        _inproj_kernel,
        grid=(t // TM_PROJ,),
        in_specs=[
            pl.BlockSpec((TM_PROJ, D_MODEL), lambda i: (i, 0)),
            pl.BlockSpec((1, D_MODEL), lambda i: (0, 0)),
            pl.BlockSpec((D_MODEL, N_MAIN + N_SMALL), lambda i: (0, 0)),
        ],
        out_specs=[
            pl.BlockSpec((TM_PROJ, N_MAIN), lambda i: (i, 0)),
            pl.BlockSpec((TM_PROJ, N_SMALL), lambda i: (i, 0)),
        ],
        out_shape=[
            jax.ShapeDtypeStruct((t, N_MAIN), BF16),
            jax.ShapeDtypeStruct((t, N_SMALL), F32),
        ],
        compiler_params=pltpu.CompilerParams(
            dimension_semantics=("arbitrary",), vmem_limit_bytes=VMEM_LIMIT),
        name="inproj",
    )(x2d, norm_w, w_all)


def _mixer_kernel(pm_ref, ps_ref, x_ref, wup_ref, balpha_ref, gnw_ref, convw_ref, convb_ref,
                  dtb_ref, aneg_ref, dexp_ref, snw_ref, wout_ref, nfw_ref, rw_ref, rb_ref,
                  x1_ref, h2_ref, lg_ref,
                  gla_state, ssd_state, conv_tail, mix_scr):
    @pl.when(pl.program_id(1) == 0)
    def _():
        gla_state[...] = jnp.zeros_like(gla_state)
        ssd_state[...] = jnp.zeros_like(ssd_state)
        conv_tail[...] = jnp.zeros_like(conv_tail)

    small = ps_ref[...]

    row = _iota((TL, TL), 0)
    col = _iota((TL, TL), 1)
    causal = col <= row
    cum64 = jnp.where(causal & ((row // GLA_CHUNK) == (col // GLA_CHUNK)), 1.0, 0.0).astype(BF16)
    cum128 = jnp.where(causal & ((row // SSD_CHUNK) == (col // SSD_CHUNK)), 1.0, 0.0).astype(BF16)

    xa = _dot_hi(small, wup_ref[...]) + balpha_ref[...]
    log_a = (jnp.minimum(xa, 0.0) - jnp.log1p(jnp.exp(-jnp.abs(xa)))) * (1.0 / GLA_GATE_NORM)
    bcum = _dot_sel_lhs(cum64, log_a)

    lane_kw = _iota((GLA_CHUNK, GLA_KW), 1)
    head_masks = [(lane_kw // GLA_DK) == h for h in range(GLA_HEADS)]
    lane_kw_s = _iota((GLA_DV, GLA_KW), 1)
    head_masks_s = [(lane_kw_s // GLA_DK) == h for h in range(GLA_HEADS)]
    tril64 = _iota((GLA_CHUNK, GLA_CHUNK), 1) <= _iota((GLA_CHUNK, GLA_CHUNK), 0)
    q_scale = GLA_DK ** -0.5

    for c in range(TL // GLA_CHUNK):
        rs = slice(c * GLA_CHUNK, (c + 1) * GLA_CHUNK)
        bc = bcum[rs]
        b_mid = bc[GLA_CHUNK // 2:GLA_CHUNK // 2 + 1]
        b_last = bc[GLA_CHUNK - 1:GLA_CHUNK]
        qc = pm_ref[rs, 0:GLA_KW].astype(F32) * q_scale
        kc = pm_ref[rs, GLA_KW:2 * GLA_KW].astype(F32)
        vc = pm_ref[rs, 2 * GLA_KW:2 * GLA_KW + GLA_WIDTH]
        q_in = (qc * jnp.exp(bc - b_mid)).astype(BF16)
        k_in = (kc * jnp.exp(b_mid - bc)).astype(BF16)
        q_st = (qc * jnp.exp(bc)).astype(BF16)
        k_st = (kc * jnp.exp(b_last - bc)).astype(BF16)
        st = gla_state[...]
        st_b = st.astype(BF16)
        zero_b = jnp.zeros_like(q_in)
        for h in range(GLA_HEADS):
            scores = _dot_nt(jnp.where(head_masks[h], q_in, zero_b), k_in)
            scores = jnp.where(tril64, scores, 0.0).astype(BF16)
            o_h = _dot(scores, vc[:, h * GLA_DV:(h + 1) * GLA_DV])
            o_h = o_h + _dot_nt(jnp.where(head_masks[h], q_st, zero_b), st_b)
            mix_scr[rs, h * GLA_DV:(h + 1) * GLA_DV] = o_h
        upd = _dot_tn(vc, k_st)
        new_st = st * jnp.exp(b_last)
        for h in range(GLA_HEADS):
            new_st = new_st + jnp.where(head_masks_s[h], upd[h * GLA_DV:(h + 1) * GLA_DV], 0.0)
        gla_state[...] = new_st

    xbc = pm_ref[:, 2048:3072].astype(F32)
    tail = conv_tail[...]
    conv_tail[...] = xbc[TL - 8:TL]
    row8 = _iota((8, SSD_CONV_CH), 0)
    conv = xbc * convw_ref[SSD_CONV - 1:SSD_CONV, :]
    for s in range(1, SSD_CONV):
        shifted = pltpu.roll(xbc, s, 0)
        head = jnp.where(row8 < s, pltpu.roll(tail, s, 0), shifted[0:8])
        shifted = jnp.concatenate([head, shifted[8:]], axis=0)
        conv = conv + shifted * convw_ref[SSD_CONV - 1 - s:SSD_CONV - s, :]
    act = _silu(conv + convb_ref[...])
    xs = act[:, 0:SSD_WIDTH]
    bm = act[:, SSD_WIDTH:SSD_WIDTH + SSD_GROUPS * SSD_STATE].astype(BF16)
    cm = act[:, SSD_WIDTH + SSD_GROUPS * SSD_STATE:].astype(BF16)

    dt_full = _softplus(small + dtb_ref[...])
    a_full = dt_full * aneg_ref[...]
    acum = _dot_sel_lhs(cum128, a_full)
    acum_t = acum.T

    e_row = _iota((N_SMALL, SSD_WIDTH), 0)
    e_col = _iota((N_SMALL, SSD_WIDTH), 1)
    spread64 = jnp.where(e_row == DT_COL + e_col // SSD_HEADDIM, 1.0, 0.0).astype(BF16)
    e_row2 = _iota((N_SMALL, SSD_HEADS * LANES), 0)
    e_col2 = _iota((N_SMALL, SSD_HEADS * LANES), 1)
    spread128 = jnp.where(e_row2 == DT_COL + e_col2 // LANES, 1.0, 0.0).astype(BF16)
    dt_e = _dot_sel_rhs(dt_full, spread64)
    ac_e = _dot_sel_rhs(acum, spread64)
    ac_w = _dot_sel_rhs(acum, spread128)

    tril128 = _iota((SSD_CHUNK, SSD_CHUNK), 1) <= _iota((SSD_CHUNK, SSD_CHUNK), 0)
    lane_g = _iota((SSD_CHUNK, SSD_HPG * SSD_HEADDIM), 1)
    for c in range(TL // SSD_CHUNK):
        rs = slice(c * SSD_CHUNK, (c + 1) * SSD_CHUNK)
        ac_c = ac_e[rs]
        a_last = ac_c[SSD_CHUNK - 1:SSD_CHUNK]
        dt_c = dt_e[rs]
        xs_c = xs[rs]
        x_dt = (xs_c * dt_c).astype(BF16)
        x_w = (xs_c * (jnp.exp(a_last - ac_c) * dt_c)).astype(BF16)
        e_ac = jnp.exp(ac_c)
        for g in range(SSD_GROUPS):
            gs = slice(g * SSD_STATE, (g + 1) * SSD_STATE)
            ws = slice(g * SSD_HPG * SSD_HEADDIM, (g + 1) * SSD_HPG * SSD_HEADDIM)
            c_g = cm[rs, gs]
            b_g = bm[rs, gs]
            cb = _dot_nt(c_g, b_g)
            x_dt_g = x_dt[:, ws]
            lhs_parts = []
            rhs_parts = []
            for hh in range(SSD_HPG):
                h = g * SSD_HPG + hh
                seg = ac_w[rs, h * LANES:(h + 1) * LANES] - acum_t[DT_COL + h:DT_COL + h + 1, rs]
                lmat = jnp.where(tril128, jnp.exp(jnp.where(tril128, seg, 0.0)), 0.0)
                lhs_parts.append((cb * lmat).astype(BF16))
                rhs_parts.append(jnp.where((lane_g // SSD_HEADDIM) == hh, x_dt_g,
                                           jnp.zeros_like(x_dt_g)))
            intra = _dot(jnp.concatenate(lhs_parts, axis=1), jnp.concatenate(rhs_parts, axis=0))
            st = ssd_state[g]
            inter = _dot(c_g, st.astype(BF16)) * e_ac[:, ws]
            mix_scr[rs, GLA_WIDTH + g * 256:GLA_WIDTH + (g + 1) * 256] = intra + inter
            ssd_state[g] = st * jnp.exp(a_last[:, ws]) + _dot_tn(b_g, x_w[:, ws])

    o = mix_scr[:, 0:GLA_WIDTH]
    g_gate = _silu(pm_ref[:, 1024:1536].astype(F32))
    gla_parts = []
    for h in range(GLA_HEADS):
        o_h = o[:, h * GLA_DV:(h + 1) * GLA_DV]
        ms = jnp.mean(o_h * o_h, axis=-1, keepdims=True)
        gla_parts.append(o_h * lax.rsqrt(ms + GROUP_EPS))
    gla_out = jnp.concatenate(gla_parts, axis=1) * gnw_ref[...] * g_gate

    y = mix_scr[:, GLA_WIDTH:] + dexp_ref[...] * xs
    y = y * _silu(pm_ref[:, 1536:2048].astype(F32))
    ssd_parts = []
    for g in range(SSD_GROUPS):
        y_g = y[:, g * 256:(g + 1) * 256]
        ms = jnp.mean(y_g * y_g, axis=-1, keepdims=True)
        ssd_parts.append(y_g * lax.rsqrt(ms + GROUP_EPS))
    ssd_out = jnp.concatenate(ssd_parts, axis=1) * snw_ref[...]

    mixed = jnp.concatenate([gla_out, ssd_out], axis=1).astype(BF16)
    x1 = x_ref[...] + _dot(mixed, wout_ref[...])
    x1_ref[...] = x1

    ms = jnp.mean(x1 * x1, axis=-1, keepdims=True)
    h2 = x1 * lax.rsqrt(ms + EPS) * nfw_ref[...]
    _store_row_tiles(h2_ref, h2)
    lg_ref[...] = _dot_hi(h2, rw_ref[...]) + rb_ref[...]


def _mixer(pm, ps, x, wup, balpha, gnw, convw, convb, dtb, aneg, dexp, snw, wout, nfw, rw, rb):
    bsz, seqlen, _ = x.shape

    def full(a):
        return pl.BlockSpec(a.shape, lambda b, l: (0,) * a.ndim)

    def tile(width):
        return pl.BlockSpec((None, TL, width), lambda b, l: (b, l, 0))

    params = (wup, balpha, gnw, convw, convb, dtb, aneg, dexp, snw, wout, nfw, rw, rb)
    return pl.pallas_call(
        _mixer_kernel,
        grid=(bsz, seqlen // TL),
        in_specs=[tile(N_MAIN), tile(N_SMALL), tile(D_MODEL)] + [full(p) for p in params],
        out_specs=[tile(D_MODEL), pl.BlockSpec((None, TL * ROW_TILE, LANES), lambda b, l: (b, l, 0)),
                   tile(LANES)],
        out_shape=[
            jax.ShapeDtypeStruct((bsz, seqlen, D_MODEL), F32),
            jax.ShapeDtypeStruct((bsz, seqlen * ROW_TILE, LANES), F32),
            jax.ShapeDtypeStruct((bsz, seqlen, LANES), F32),
        ],
        scratch_shapes=[
            pltpu.VMEM((GLA_DV, GLA_KW), F32),
            pltpu.VMEM((SSD_GROUPS, SSD_STATE, SSD_HPG * SSD_HEADDIM), F32),
            pltpu.VMEM((8, SSD_CONV_CH), F32),
            pltpu.VMEM((TL, D_MODEL), F32),
        ],
        compiler_params=pltpu.CompilerParams(
            dimension_semantics=("arbitrary", "arbitrary"), vmem_limit_bytes=VMEM_LIMIT),
        name="mixer",
    )(pm, ps, x, *params)


def _route_kernel(lg_ref, pos_ref, gate_ref, cnt_ref, cnt_scr, run_scr):
    phase = pl.program_id(0)
    step = pl.program_id(1)

    @pl.when((phase == 0) & (step == 0))
    def _():
        cnt_scr[...] = jnp.zeros_like(cnt_scr)
        run_scr[...] = jnp.zeros_like(run_scr)

    lg = lg_ref[...]
    lane = _iota((TR, LANES), 1)
    work = lg
    onehots = []
    vals = []
    for _ in range(TOP_K):
        m = jnp.max(work, axis=-1, keepdims=True)
        idx = jnp.min(jnp.where(work == m, lane, LANES), axis=-1, keepdims=True)
        oh = lane == idx
        onehots.append(oh)
        vals.append(m)
        work = jnp.where(oh, -jnp.inf, work)
    multi = jnp.where(onehots[0] | onehots[1] | onehots[2] | onehots[3], 1.0, 0.0)
    tile_cnt = jnp.sum(multi, axis=0, keepdims=True)

    @pl.when(phase == 0)
    def _():
        cnt_scr[...] = cnt_scr[...] + tile_cnt

    @pl.when(phase == 1)
    def _():
        counts = cnt_scr[...]
        padded = jnp.floor((counts + (TM - 1)) * (1.0 / TM)) * TM
        upper = jnp.where(_iota((LANES, LANES), 0) < _iota((LANES, LANES), 1), 1.0, 0.0).astype(BF16)
        offs = _dot_sel_rhs(jnp.broadcast_to(padded, (8, LANES)), upper)[0:1]
        strict = jnp.where(_iota((TR, TR), 1) < _iota((TR, TR), 0), 1.0, 0.0).astype(BF16)
        rank = _dot(strict, multi.astype(BF16)) + run_scr[...]
        run_scr[...] = run_scr[...] + tile_cnt
        base = offs + rank
        exps = [jnp.exp(v - vals[0]) for v in vals]
        den = exps[0] + exps[1] + exps[2] + exps[3]
        pos_mat = jnp.zeros((TR, LANES), F32)
        gate_mat = jnp.zeros((TR, LANES), F32)
        for k in range(TOP_K):
            pos_k = jnp.sum(jnp.where(onehots[k], base, 0.0), axis=-1, keepdims=True)
            pos_mat = jnp.where(lane == k, pos_k, pos_mat)
            gate_mat = jnp.where(lane == k, exps[k] / den, gate_mat)
        pos_ref[...] = pos_mat.T[0:8].astype(jnp.int32)
        gate_ref[...] = gate_mat
        cnt_ref[...] = jnp.broadcast_to(counts, (8, LANES))


def _route(logits):
    t = logits.shape[0]
    return pl.pallas_call(
        _route_kernel,
        grid=(2, t // TR),
        in_specs=[pl.BlockSpec((TR, LANES), lambda p, i: (i, 0))],
        out_specs=[
            pl.BlockSpec((8, TR), lambda p, i: (0, i * p)),
            pl.BlockSpec((TR, LANES), lambda p, i: (i * p, 0)),
            pl.BlockSpec((8, LANES), lambda p, i: (0, 0)),
        ],
        out_shape=[
            jax.ShapeDtypeStruct((8, t), jnp.int32),
            jax.ShapeDtypeStruct((t, LANES), F32),
            jax.ShapeDtypeStruct((8, LANES), F32),
        ],
        scratch_shapes=[pltpu.VMEM((1, LANES), F32), pltpu.VMEM((1, LANES), F32)],
        compiler_params=pltpu.CompilerParams(
            dimension_semantics=("arbitrary", "arbitrary"), vmem_limit_bytes=VMEM_LIMIT),
        name="route",
    )(logits)


def _dump_row(slot, n_tokens):
    tile = lax.shift_right_logical(slot, TM_LOG2)
    return TOP_K * n_tokens + (tile & 1) * TM + (slot & (TM - 1))


def _codes_kernel(pos_ref, cnt_ref, codes_ref, *, n_tokens):
    step = pl.program_id(0)

    def fill(p, carry):
        codes_ref[p] = _dump_row(p, n_tokens)
        return carry

    @pl.when(step == 0)
    def _():
        def per_expert(e, start):
            c = cnt_ref[e]
            end = start + lax.shift_left(lax.shift_right_logical(c + (TM - 1), TM_LOG2), TM_LOG2)
            lax.fori_loop(start + c, end, fill, 0)
            return end
        total = lax.fori_loop(0, N_EXPERTS, per_expert, 0)
        lax.fori_loop(total, codes_ref.shape[0], fill, 0)

    base = step * TC

    def body(t, carry):
        for k in range(TOP_K):
            codes_ref[pos_ref[k, t]] = k * n_tokens + base + t
        return carry

    lax.fori_loop(0, TC, body, 0, unroll=4)


def _codes(pos, counts, p_rows):
    t = pos.shape[1]
    return pl.pallas_call(
        functools.partial(_codes_kernel, n_tokens=t),
        grid=(t // TC,),
        in_specs=[
            pl.BlockSpec((8, TC), lambda i: (0, i), memory_space=pltpu.SMEM),
            pl.BlockSpec(memory_space=pltpu.SMEM),
        ],
        out_specs=pl.BlockSpec(memory_space=pltpu.SMEM),
        out_shape=jax.ShapeDtypeStruct((p_rows,), jnp.int32),
        compiler_params=pltpu.CompilerParams(dimension_semantics=("arbitrary",)),
        name="codes",
    )(pos, counts)


def _experts_kernel(te_ref, nv_ref, codes_ref, h_ref, wg_ref, bg_ref, wu_ref, bu_ref, wd_ref, bd_ref,
                    y_ref, xbuf, obuf, act, wg_b, wu_b, wd_b, gsem, ssem, *, n_tokens):
    i = pl.program_id(0)
    n_valid = nv_ref[0]
    slot = i & 1
    other = 1 - slot

    def row_tile(row):
        return pl.ds(pl.multiple_of(row * ROW_TILE, ROW_TILE), ROW_TILE)

    def gather_row(tile, r, s):
        src = codes_ref[tile * TM + r] & (n_tokens - 1)
        return pltpu.make_async_copy(h_ref.at[row_tile(src), :], xbuf.at[s, row_tile(r), :], gsem.at[s])

    def scatter_row(dst, r, s):
        return pltpu.make_async_copy(obuf.at[s, row_tile(r), :], y_ref.at[row_tile(dst), :], ssem.at[s])

    def tile_gather_done(s):
        return pltpu.make_async_copy(h_ref.at[pl.ds(0, TM * ROW_TILE), :], xbuf.at[s], gsem.at[s])

    def tile_scatter_done(s):
        return pltpu.make_async_copy(obuf.at[s], y_ref.at[pl.ds(0, TM * ROW_TILE), :], ssem.at[s])

    @pl.when(i < n_valid)
    def _():
        @pl.when(i == 0)
        def _():
            obuf[...] = jnp.zeros_like(obuf)

            def first(r, carry):
                gather_row(0, r, 0).start()
                scatter_row(TOP_K * n_tokens + r, r, 1).start()
                return carry
            lax.fori_loop(0, TM, first, 0)
            tile_scatter_done(1).wait()

        prev = jnp.maximum(i - 1, 0)

        @pl.when((i == 0) | (te_ref[i] != te_ref[prev]))
        def _():
            for c in range(N_CHUNKS):
                cs = slice(c * N_COLS, (c + 1) * N_COLS)
                wg_b[c] = wg_ref[:, cs].astype(BF16)
                wu_b[c] = wu_ref[:, cs].astype(BF16)
                wd_b[c] = wd_ref[:, cs].astype(BF16)

        tile_gather_done(slot).wait()

        @pl.when(i >= 1)
        def _():
            tile_scatter_done(slot).wait()

        nxt = jnp.minimum(i + 1, n_valid - 1)
        is_first = i == 0
        rows_per_issue = TM // (2 * N_CHUNKS)

        def issue(group):
            for u in range(rows_per_issue):
                r = group * rows_per_issue + u
                gather_row(nxt, r, other).start(priority=0)
                dst = jnp.where(is_first, TOP_K * n_tokens + TM + r, codes_ref[prev * TM + r])
                scatter_row(dst, r, other).start(priority=1)

        xb = _load_row_tiles(xbuf.at[slot], TM).astype(BF16)

        def gate_up_chunk(n, carry):
            issue(n)
            gate = _dot(xb, wg_b[n]) + bg_ref[pl.ds(n, 1), :]
            up = _dot(xb, wu_b[n]) + bu_ref[pl.ds(n, 1), :]
            gate = jnp.minimum(gate, SWIGLU_LIMIT)
            up = jnp.clip(up, -SWIGLU_LIMIT, SWIGLU_LIMIT)
            act[n] = ((up + 1.0) * (gate * jax.nn.sigmoid(SWIGLU_ALPHA * gate))).astype(BF16)
            return carry

        lax.fori_loop(0, N_CHUNKS, gate_up_chunk, 0)
        a = jnp.concatenate([act[c] for c in range(N_CHUNKS)], axis=1)

        def down_chunk(n, carry):
            issue(N_CHUNKS + n)
            out = _dot(a, wd_b[n]) + bd_ref[pl.ds(n, 1), :]
            for j in range(N_COLS // LANES):
                obuf[slot, pl.ds(n * (N_COLS // LANES) + j, TM, stride=ROW_TILE), :] = (
                    out[:, j * LANES:(j + 1) * LANES])
            return carry

        lax.fori_loop(0, N_CHUNKS, down_chunk, 0)

        @pl.when(i == n_valid - 1)
        def _():
            tile_gather_done(other).wait()
            tile_scatter_done(other).wait()

            def last(r, carry):
                scatter_row(codes_ref[i * TM + r], r, slot).start()
                return carry
            lax.fori_loop(0, TM, last, 0)
            tile_scatter_done(slot).wait()


def _experts(tile_expert, n_valid, codes, h2, wg, bg, wu, bu, wd, bd):
    t = h2.shape[0] // ROW_TILE
    n_tiles = codes.shape[0] // TM

    def w_map(i, te, nv, cd):
        return (te[i], 0, 0)

    w_spec = pl.BlockSpec((None, D_MODEL, D_MODEL), w_map)
    b_spec = pl.BlockSpec((None, N_CHUNKS, N_COLS), w_map)
    return pl.pallas_call(
        functools.partial(_experts_kernel, n_tokens=t),
        grid_spec=pltpu.PrefetchScalarGridSpec(
            num_scalar_prefetch=3,
            grid=(n_tiles,),
            in_specs=[pl.BlockSpec(memory_space=pl.ANY), w_spec, b_spec, w_spec, b_spec, w_spec, b_spec],
            out_specs=pl.BlockSpec(memory_space=pl.ANY),
            scratch_shapes=[
                pltpu.VMEM((2, TM * ROW_TILE, LANES), F32),
                pltpu.VMEM((2, TM * ROW_TILE, LANES), F32),
                pltpu.VMEM((N_CHUNKS, TM, N_COLS), BF16),
                pltpu.VMEM((N_CHUNKS, D_MODEL, N_COLS), BF16),
                pltpu.VMEM((N_CHUNKS, D_MODEL, N_COLS), BF16),
                pltpu.VMEM((N_CHUNKS, D_MODEL, N_COLS), BF16),
                pltpu.SemaphoreType.DMA((2,)),
                pltpu.SemaphoreType.DMA((2,)),
            ],
        ),
        out_shape=jax.ShapeDtypeStruct(((TOP_K * t + 2 * TM) * ROW_TILE, LANES), F32),
        compiler_params=pltpu.CompilerParams(
            dimension_semantics=("arbitrary",), vmem_limit_bytes=VMEM_LIMIT),
        name="experts",
    )(tile_expert, n_valid, codes, h2, wg, bg, wu, bu, wd, bd)


def _combine_kernel(gate_ref, x1_ref, fw_ref, y0_ref, y1_ref, y2_ref, y3_ref, out_ref):
    gates = gate_ref[...]
    y = x1_ref[...]
    for k, y_ref in enumerate((y0_ref, y1_ref, y2_ref, y3_ref)):
        y = y + gates[:, k:k + 1] * _load_row_tiles(y_ref, TT)
    ms = jnp.mean(y * y, axis=-1, keepdims=True)
    out_ref[...] = y * lax.rsqrt(ms + EPS) * fw_ref[...]


def _combine(gates, x1, final_w, y4):
    t = x1.shape[0]
    steps = t // TT

    def y_spec(k):
        return pl.BlockSpec((TT * ROW_TILE, LANES), lambda i: (k * steps + i, 0))

    return pl.pallas_call(
        _combine_kernel,
        grid=(steps,),
        in_specs=[
            pl.BlockSpec((TT, LANES), lambda i: (i, 0)),
            pl.BlockSpec((TT, D_MODEL), lambda i: (i, 0)),
            pl.BlockSpec((1, D_MODEL), lambda i: (0, 0)),
            y_spec(0), y_spec(1), y_spec(2), y_spec(3),
        ],
        out_specs=pl.BlockSpec((TT, D_MODEL), lambda i: (i, 0)),
        out_shape=jax.ShapeDtypeStruct((t, D_MODEL), F32),
        compiler_params=pltpu.CompilerParams(
            dimension_semantics=("arbitrary",), vmem_limit_bytes=VMEM_LIMIT),
        name="combine",
    )(gates, x1, final_w, y4, y4, y4, y4)


def _pad_lanes(v, offset, fill=0.0):
    row = jnp.full((1, LANES), fill, F32)
    return row.at[0, offset:offset + v.shape[0]].set(v.astype(F32))


def kernel(x, norm_mix_w, w_in, gla_w_alpha_up, gla_b_alpha, gla_norm_w, ssd_conv_w, ssd_conv_b,
           ssd_dt_bias, ssd_A_log, ssd_D, ssd_norm_w, w_out, norm_ffn_w, router_w, router_b,
           moe_w_gate, moe_b_gate, moe_w_up, moe_b_up, moe_w_down, moe_b_down, final_norm_w):
    bsz, seqlen, d = x.shape
    t = bsz * seqlen
    depth = w_in.shape[0]
    assert depth == 1, "the final RMSNorm is fused into the (single) layer's combine step"
    assert t & (t - 1) == 0, "slot codes pack (choice, token) with a power-of-two token count"
    p_rows = t * TOP_K + N_EXPERTS * TM
    n_tiles = p_rows // TM
    for l in range(depth):
        w = w_in[l]
        w_all = jnp.concatenate(
            [w[:, 0:1536], w[:, 1552:3088], w[:, 1536:1552], w[:, 3088:3096],
             jnp.zeros((d, N_SMALL - GLA_GATE_RANK - SSD_HEADS), w.dtype)], axis=1).astype(BF16)
        wup = jnp.zeros((N_SMALL, GLA_KW), F32).at[0:GLA_GATE_RANK].set(gla_w_alpha_up[l])
        dtb = _pad_lanes(ssd_dt_bias[l], DT_COL)
        aneg = _pad_lanes(-jnp.exp(ssd_A_log[l].astype(F32)), DT_COL)
        dexp = jnp.repeat(ssd_D[l].astype(F32), SSD_HEADDIM)[None, :]
        rw = jnp.zeros((d, LANES), F32).at[:, 0:N_EXPERTS].set(router_w[l])
        rb = _pad_lanes(router_b[l], 0, fill=-1e30)

        pm, ps = _inproj(x.reshape(t, d), norm_mix_w[l][None, :], w_all)
        x1, h2, logits = _mixer(
            pm.reshape(bsz, seqlen, N_MAIN), ps.reshape(bsz, seqlen, N_SMALL), x,
            wup, gla_b_alpha[l][None, :], gla_norm_w[l][None, :], ssd_conv_w[l],
            ssd_conv_b[l][None, :], dtb, aneg, dexp, ssd_norm_w[l][None, :],
            w_out[l].astype(BF16), norm_ffn_w[l][None, :], rw, rb)

        pos, gates, counts = _route(logits.reshape(t, LANES))

        cnt = counts[0, 0:N_EXPERTS].astype(jnp.int32)
        ends = jnp.cumsum(((cnt + TM - 1) // TM) * TM)
        n_valid = (ends[-1] // TM).astype(jnp.int32)
        starts = jnp.arange(n_tiles, dtype=jnp.int32) * TM
        tile_expert = jnp.sum(starts[:, None] >= ends[None, :], axis=1).astype(jnp.int32)
        last_expert = tile_expert[jnp.maximum(n_valid - 1, 0)]
        tile_expert = jnp.where(starts < ends[-1], tile_expert, last_expert)

        codes = _codes(pos, counts[0].astype(jnp.int32), p_rows)
        y4 = _experts(tile_expert, n_valid.reshape(1), codes, h2.reshape(t * ROW_TILE, LANES),
                      moe_w_gate[l], moe_b_gate[l].reshape(N_EXPERTS, N_CHUNKS, N_COLS),
                      moe_w_up[l], moe_b_up[l].reshape(N_EXPERTS, N_CHUNKS, N_COLS),
                      moe_w_down[l], moe_b_down[l].reshape(N_EXPERTS, N_CHUNKS, N_COLS))
        x = _combine(gates, x1.reshape(t, d), final_norm_w[None, :], y4).reshape(bsz, seqlen, d)
    return x
```

```python
import functools

import jax
import jax.numpy as jnp
from jax import lax
from jax.experimental import pallas as pl
from jax.experimental.pallas import tpu as pltpu

F32 = jnp.float32
BF16 = jnp.bfloat16

D_MODEL = 1024
GLA_WIDTH = 512
GLA_HEADS = 4
GLA_DV = 128
GLA_DK = 64
GLA_KW = 256
GLA_GATE_RANK = 16
GLA_GATE_NORM = 16.0
SSD_WIDTH = 512
SSD_HEADDIM = 64
SSD_HEADS = 8
SSD_GROUPS = 2
SSD_HPG = 4
SSD_STATE = 128
SSD_CONV = 4
SSD_CONV_CH = 1024
N_EXPERTS = 32
TOP_K = 4
SWIGLU_LIMIT = 7.0
SWIGLU_ALPHA = 1.702
EPS = 1e-6
GROUP_EPS = 1e-5

LANES = 128
ROW_TILE = D_MODEL // LANES
N_MAIN = 3072
N_SMALL = LANES
DT_COL = GLA_GATE_RANK

GLA_CHUNK = 64
SSD_CHUNK = 128
TM_PROJ = 512
TL = 256
TR = 512
TC = 8192
TT = 512
TM_LOG2 = 8
TM = 1 << TM_LOG2
N_COLS = 256
N_CHUNKS = D_MODEL // N_COLS
VMEM_LIMIT = 56 * 1024 * 1024


def _dot(a, b):
    return jnp.dot(a, b, preferred_element_type=F32)


def _dot_nt(a, b):
    return lax.dot_general(a, b, (((1,), (1,)), ((), ())), preferred_element_type=F32)


def _dot_tn(a, b):
    return lax.dot_general(a, b, (((0,), (0,)), ((), ())), preferred_element_type=F32)


def _split3(a):
    hi = a.astype(BF16)
    r1 = a - hi.astype(F32)
    mid = r1.astype(BF16)
    lo = (r1 - mid.astype(F32)).astype(BF16)
    return hi, mid, lo


def _dot_sel_lhs(sel, a):
    hi, mid, lo = _split3(a)
    return _dot(sel, hi) + _dot(sel, mid) + _dot(sel, lo)


def _dot_sel_rhs(a, sel, terms=3):
    parts = _split3(a)[:terms]
    out = _dot(parts[0], sel)
    for p in parts[1:]:
        out = out + _dot(p, sel)
    return out


def _dot_hi(a, b):
    a_hi = a.astype(BF16)
    a_lo = (a - a_hi.astype(F32)).astype(BF16)
    b_hi = b.astype(BF16)
    b_lo = (b - b_hi.astype(F32)).astype(BF16)
    return _dot(a_hi, b_hi) + _dot(a_lo, b_hi) + _dot(a_hi, b_lo)


def _dot_hi_nt(a, b):
    a_hi = a.astype(BF16)
    a_lo = (a - a_hi.astype(F32)).astype(BF16)
    b_hi = b.astype(BF16)
    b_lo = (b - b_hi.astype(F32)).astype(BF16)
    return _dot_nt(a_hi, b_hi) + _dot_nt(a_lo, b_hi) + _dot_nt(a_hi, b_lo)


def _softplus(x):
    return jnp.maximum(x, 0.0) + jnp.log1p(jnp.exp(-jnp.abs(x)))


def _silu(x):
    return x * jax.nn.sigmoid(x)


def _iota(shape, dim):
    return lax.broadcasted_iota(jnp.int32, shape, dim)


def _store_row_tiles(ref, value):
    rows = value.shape[0]
    for j in range(ROW_TILE):
        ref[pl.ds(j, rows, stride=ROW_TILE), :] = value[:, j * LANES:(j + 1) * LANES]


def _load_row_tiles(ref, rows):
    return jnp.concatenate([ref[pl.ds(j, rows, stride=ROW_TILE), :] for j in range(ROW_TILE)], axis=1)


def _inproj_kernel(x_ref, nw_ref, w_ref, pm_ref, ps_ref):
    x = x_ref[...]
    ms = jnp.mean(x * x, axis=-1, keepdims=True)
    h = (x * lax.rsqrt(ms + EPS) * nw_ref[...]).astype(BF16)
    step = 512
    for n0 in range(0, N_MAIN, step):
        pm_ref[:, n0:n0 + step] = _dot(h, w_ref[:, n0:n0 + step]).astype(BF16)
    ps_ref[...] = _dot(h, w_ref[:, N_MAIN:N_MAIN + N_SMALL])


def _inproj(x2d, norm_w, w_all):
    t = x2d.shape[0]
    return pl.pallas_call(
        _inproj_kernel,
        grid=(t // TM_PROJ,),
        in_specs=[
            pl.BlockSpec((TM_PROJ, D_MODEL), lambda i: (i, 0)),
            pl.BlockSpec((1, D_MODEL), lambda i: (0, 0)),
            pl.BlockSpec((D_MODEL, N_MAIN + N_SMALL), lambda i: (0, 0)),
        ],
        out_specs=[
            pl.BlockSpec((TM_PROJ, N_MAIN), lambda i: (i, 0)),
            pl.BlockSpec((TM_PROJ, N_SMALL), lambda i: (i, 0)),
        ],
        out_shape=[
            jax.ShapeDtypeStruct((t, N_MAIN), BF16),
            jax.ShapeDtypeStruct((t, N_SMALL), F32),
        ],
        compiler_params=pltpu.CompilerParams(
            dimension_semantics=("arbitrary",), vmem_limit_bytes=VMEM_LIMIT),
        name="inproj",
    )(x2d, norm_w, w_all)


def _mixer_kernel(pm_ref, ps_ref, x_ref, wup_ref, balpha_ref, gnw_ref, convw_ref, convb_ref,
                  dtb_ref, aneg_ref, dexp_ref, snw_ref, wout_ref, nfw_ref, rw_ref, rb_ref,
                  x1_ref, h2_ref, lg_ref,
                  gla_state, ssd_state, conv_tail, mix_scr):
    @pl.when(pl.program_id(1) == 0)
    def _():
        gla_state[...] = jnp.zeros_like(gla_state)
        ssd_state[...] = jnp.zeros_like(ssd_state)
        conv_tail[...] = jnp.zeros_like(conv_tail)

    small = ps_ref[...]

    row = _iota((TL, TL), 0)
    col = _iota((TL, TL), 1)
    causal = col <= row
    cum64 = jnp.where(causal & ((row // GLA_CHUNK) == (col // GLA_CHUNK)), 1.0, 0.0).astype(BF16)
    cum128 = jnp.where(causal & ((row // SSD_CHUNK) == (col // SSD_CHUNK)), 1.0, 0.0).astype(BF16)

    xa = _dot_hi(small, wup_ref[...]) + balpha_ref[...]
    log_a = (jnp.minimum(xa, 0.0) - jnp.log1p(jnp.exp(-jnp.abs(xa)))) * (1.0 / GLA_GATE_NORM)
    bcum = _dot_sel_lhs(cum64, log_a)

    lane_kw = _iota((GLA_CHUNK, GLA_KW), 1)
    head_masks = [(lane_kw // GLA_DK) == h for h in range(GLA_HEADS)]
    lane_kw_s = _iota((GLA_DV, GLA_KW), 1)
    head_masks_s = [(lane_kw_s // GLA_DK) == h for h in range(GLA_HEADS)]
    tril64 = _iota((GLA_CHUNK, GLA_CHUNK), 1) <= _iota((GLA_CHUNK, GLA_CHUNK), 0)
    q_scale = GLA_DK ** -0.5

    for c in range(TL // GLA_CHUNK):
        rs = slice(c * GLA_CHUNK, (c + 1) * GLA_CHUNK)
        bc = bcum[rs]
        b_mid = bc[GLA_CHUNK // 2:GLA_CHUNK // 2 + 1]
        b_last = bc[GLA_CHUNK - 1:GLA_CHUNK]
        qc = pm_ref[rs, 0:GLA_KW].astype(F32) * q_scale
        kc = pm_ref[rs, GLA_KW:2 * GLA_KW].astype(F32)
        vc = pm_ref[rs, 2 * GLA_KW:2 * GLA_KW + GLA_WIDTH]
        q_in = (qc * jnp.exp(bc - b_mid)).astype(BF16)
        k_in = (kc * jnp.exp(b_mid - bc)).astype(BF16)
        q_st = (qc * jnp.exp(bc)).astype(BF16)
        k_st = (kc * jnp.exp(b_last - bc)).astype(BF16)
        st = gla_state[...]
        st_b = st.astype(BF16)
        zero_b = jnp.zeros_like(q_in)
        for h in range(GLA_HEADS):
            scores = _dot_nt(jnp.where(head_masks[h], q_in, zero_b), k_in)
            scores = jnp.where(tril64, scores, 0.0).astype(BF16)
            o_h = _dot(scores, vc[:, h * GLA_DV:(h + 1) * GLA_DV])
            o_h = o_h + _dot_nt(jnp.where(head_masks[h], q_st, zero_b), st_b)
            mix_scr[rs, h * GLA_DV:(h + 1) * GLA_DV] = o_h
        upd = _dot_tn(vc, k_st)
        new_st = st * jnp.exp(b_last)
        for h in range(GLA_HEADS):
            new_st = new_st + jnp.where(head_masks_s[h], upd[h * GLA_DV:(h + 1) * GLA_DV], 0.0)
        gla_state[...] = new_st

    xbc = pm_ref[:, 2048:3072].astype(F32)
    tail = conv_tail[...]
    conv_tail[...] = xbc[TL - 8:TL]
    row8 = _iota((8, SSD_CONV_CH), 0)
    conv = xbc * convw_ref[SSD_CONV - 1:SSD_CONV, :]
    for s in range(1, SSD_CONV):
        shifted = pltpu.roll(xbc, s, 0)
        head = jnp.where(row8 < s, pltpu.roll(tail, s, 0), shifted[0:8])
        shifted = jnp.concatenate([head, shifted[8:]], axis=0)
        conv = conv + shifted * convw_ref[SSD_CONV - 1 - s:SSD_CONV - s, :]
    act = _silu(conv + convb_ref[...])
    xs = act[:, 0:SSD_WIDTH]
    bm = act[:, SSD_WIDTH:SSD_WIDTH + SSD_GROUPS * SSD_STATE].astype(BF16)
    cm = act[:, SSD_WIDTH + SSD_GROUPS * SSD_STATE:].astype(BF16)

    dt_full = _softplus(small + dtb_ref[...])
    a_full = dt_full * aneg_ref[...]
    acum = _dot_sel_lhs(cum128, a_full)
    acum_t = acum.T

    e_row = _iota((N_SMALL, SSD_WIDTH), 0)
    e_col = _iota((N_SMALL, SSD_WIDTH), 1)
    spread64 = jnp.where(e_row == DT_COL + e_col // SSD_HEADDIM, 1.0, 0.0).astype(BF16)
    e_row2 = _iota((N_SMALL, SSD_HEADS * LANES), 0)
    e_col2 = _iota((N_SMALL, SSD_HEADS * LANES), 1)
    spread128 = jnp.where(e_row2 == DT_COL + e_col2 // LANES, 1.0, 0.0).astype(BF16)
    dt_e = _dot_sel_rhs(dt_full, spread64, terms=1)
    ac_e = _dot_sel_rhs(acum, spread64, terms=2)
    ac_w = _dot_sel_rhs(acum, spread128, terms=2)

    tril128 = _iota((SSD_CHUNK, SSD_CHUNK), 1) <= _iota((SSD_CHUNK, SSD_CHUNK), 0)
    lane_g = _iota((SSD_CHUNK, SSD_HPG * SSD_HEADDIM), 1)
    for c in range(TL // SSD_CHUNK):
        rs = slice(c * SSD_CHUNK, (c + 1) * SSD_CHUNK)
        ac_c = ac_e[rs]
        a_last = ac_c[SSD_CHUNK - 1:SSD_CHUNK]
        dt_c = dt_e[rs]
        xs_c = xs[rs]
        x_dt = (xs_c * dt_c).astype(BF16)
        x_w = (xs_c * (jnp.exp(a_last - ac_c) * dt_c)).astype(BF16)
        e_ac = jnp.exp(ac_c)
        for g in range(SSD_GROUPS):
            gs = slice(g * SSD_STATE, (g + 1) * SSD_STATE)
            ws = slice(g * SSD_HPG * SSD_HEADDIM, (g + 1) * SSD_HPG * SSD_HEADDIM)
            c_g = cm[rs, gs]
            b_g = bm[rs, gs]
            cb = _dot_nt(c_g, b_g)
            x_dt_g = x_dt[:, ws]
            lhs_parts = []
            rhs_parts = []
            for hh in range(SSD_HPG):
                h = g * SSD_HPG + hh
                seg = ac_w[rs, h * LANES:(h + 1) * LANES] - acum_t[DT_COL + h:DT_COL + h + 1, rs]
                lmat = jnp.where(tril128, jnp.exp(jnp.where(tril128, seg, 0.0)), 0.0)
                lhs_parts.append((cb * lmat).astype(BF16))
                rhs_parts.append(jnp.where((lane_g // SSD_HEADDIM) == hh, x_dt_g,
                                           jnp.zeros_like(x_dt_g)))
            intra = _dot(jnp.concatenate(lhs_parts, axis=1), jnp.concatenate(rhs_parts, axis=0))
            st = ssd_state[g]
            inter = _dot(c_g, st.astype(BF16)) * e_ac[:, ws]
            mix_scr[rs, GLA_WIDTH + g * 256:GLA_WIDTH + (g + 1) * 256] = intra + inter
            ssd_state[g] = st * jnp.exp(a_last[:, ws]) + _dot_tn(b_g, x_w[:, ws])

    o = mix_scr[:, 0:GLA_WIDTH]
    g_gate = _silu(pm_ref[:, 1024:1536].astype(F32))
    gla_parts = []
    for h in range(GLA_HEADS):
        o_h = o[:, h * GLA_DV:(h + 1) * GLA_DV]
        ms = jnp.mean(o_h * o_h, axis=-1, keepdims=True)
        gla_parts.append(o_h * lax.rsqrt(ms + GROUP_EPS))
    gla_out = jnp.concatenate(gla_parts, axis=1) * gnw_ref[...] * g_gate

    y = mix_scr[:, GLA_WIDTH:] + dexp_ref[...] * xs
    y = y * _silu(pm_ref[:, 1536:2048].astype(F32))
    ssd_parts = []
    for g in range(SSD_GROUPS):
        y_g = y[:, g * 256:(g + 1) * 256]
        ms = jnp.mean(y_g * y_g, axis=-1, keepdims=True)
        ssd_parts.append(y_g * lax.rsqrt(ms + GROUP_EPS))
    ssd_out = jnp.concatenate(ssd_parts, axis=1) * snw_ref[...]

    mixed = jnp.concatenate([gla_out, ssd_out], axis=1).astype(BF16)
    x1 = x_ref[...] + _dot(mixed, wout_ref[...])
    x1_ref[...] = x1

    ms = jnp.mean(x1 * x1, axis=-1, keepdims=True)
    h2 = x1 * lax.rsqrt(ms + EPS) * nfw_ref[...]
    _store_row_tiles(h2_ref, h2)
    lg_ref[...] = _dot_hi_nt(rw_ref[...], h2) + rb_ref[...]


def _mixer(pm, ps, x, wup, balpha, gnw, convw, convb, dtb, aneg, dexp, snw, wout, nfw, rw, rb):
    bsz, seqlen, _ = x.shape

    def full(a):
        return pl.BlockSpec(a.shape, lambda b, l: (0,) * a.ndim)

    def tile(width):
        return pl.BlockSpec((None, TL, width), lambda b, l: (b, l, 0))

    params = (wup, balpha, gnw, convw, convb, dtb, aneg, dexp, snw, wout, nfw, rw, rb)
    return pl.pallas_call(
        _mixer_kernel,
        grid=(bsz, seqlen // TL),
        in_specs=[tile(N_MAIN), tile(N_SMALL), tile(D_MODEL)] + [full(p) for p in params],
        out_specs=[tile(D_MODEL), pl.BlockSpec((None, TL * ROW_TILE, LANES), lambda b, l: (b, l, 0)),
                   pl.BlockSpec((LANES, TL), lambda b, l: (0, b * (seqlen // TL) + l))],
        out_shape=[
            jax.ShapeDtypeStruct((bsz, seqlen, D_MODEL), F32),
            jax.ShapeDtypeStruct((bsz, seqlen * ROW_TILE, LANES), F32),
            jax.ShapeDtypeStruct((LANES, bsz * seqlen), F32),
        ],
        scratch_shapes=[
            pltpu.VMEM((GLA_DV, GLA_KW), F32),
            pltpu.VMEM((SSD_GROUPS, SSD_STATE, SSD_HPG * SSD_HEADDIM), F32),
            pltpu.VMEM((8, SSD_CONV_CH), F32),
            pltpu.VMEM((TL, D_MODEL), F32),
        ],
        compiler_params=pltpu.CompilerParams(
            dimension_semantics=("arbitrary", "arbitrary"), vmem_limit_bytes=VMEM_LIMIT),
        name="mixer",
    )(pm, ps, x, *params)


def _route_kernel(lg_ref, pos_ref, gate_ref, cnt_ref, cnt_scr, run_scr):
    phase = pl.program_id(0)
    step = pl.program_id(1)

    @pl.when((phase == 0) & (step == 0))
    def _():
        cnt_scr[...] = jnp.zeros_like(cnt_scr)
        run_scr[...] = jnp.zeros_like(run_scr)

    lg = lg_ref[0:N_EXPERTS, :]
    row = _iota((N_EXPERTS, TR), 0)
    work = lg
    onehots = []
    vals = []
    for _ in range(TOP_K):
        m = jnp.max(work, axis=0, keepdims=True)
        idx = jnp.min(jnp.where(work == m, row, N_EXPERTS), axis=0, keepdims=True)
        oh = row == idx
        onehots.append(oh)
        vals.append(m)
        work = jnp.where(oh, -jnp.inf, work)
    multi = jnp.where(onehots[0] | onehots[1] | onehots[2] | onehots[3], 1.0, 0.0).astype(BF16)
    tile_cnt = _dot(multi, jnp.ones((TR, LANES), BF16))

    @pl.when(phase == 0)
    def _():
        cnt_scr[...] = cnt_scr[...] + tile_cnt

    @pl.when(phase == 1)
    def _():
        counts = cnt_scr[...]
        padded = jnp.floor((counts + (TM - 1)) * (1.0 / TM)) * TM
        lower = jnp.where(_iota((N_EXPERTS, N_EXPERTS), 1) < _iota((N_EXPERTS, N_EXPERTS), 0),
                          1.0, 0.0).astype(BF16)
        offs = _dot_sel_lhs(lower, padded)
        before = jnp.where(_iota((TR, TR), 0) < _iota((TR, TR), 1), 1.0, 0.0).astype(BF16)
        rank = _dot(multi, before)
        start = offs + run_scr[...]
        run_scr[...] = run_scr[...] + tile_cnt
        base = rank + jnp.concatenate([start] * (TR // LANES), axis=1)
        exps = [jnp.exp(v - vals[0]) for v in vals]
        den = exps[0] + exps[1] + exps[2] + exps[3]
        pos_rows = [jnp.sum(jnp.where(oh, base, 0.0), axis=0, keepdims=True) for oh in onehots]
        pos_ref[...] = jnp.concatenate(pos_rows + [jnp.zeros((8 - TOP_K, TR), F32)], axis=0).astype(jnp.int32)
        gate_rows = [e / den for e in exps]
        gate_ref[...] = jnp.concatenate(gate_rows + [jnp.zeros((LANES - TOP_K, TR), F32)], axis=0).T
        cnt_ref[...] = counts


def _route(logits_t):
    t = logits_t.shape[1]
    return pl.pallas_call(
        _route_kernel,
        grid=(2, t // TR),
        in_specs=[pl.BlockSpec((LANES, TR), lambda p, i: (0, i))],
        out_specs=[
            pl.BlockSpec((8, TR), lambda p, i: (0, i * p)),
            pl.BlockSpec((TR, LANES), lambda p, i: (i * p, 0)),
            pl.BlockSpec((N_EXPERTS, LANES), lambda p, i: (0, 0)),
        ],
        out_shape=[
            jax.ShapeDtypeStruct((8, t), jnp.int32),
            jax.ShapeDtypeStruct((t, LANES), F32),
            jax.ShapeDtypeStruct((N_EXPERTS, LANES), F32),
        ],
        scratch_shapes=[pltpu.VMEM((N_EXPERTS, LANES), F32), pltpu.VMEM((N_EXPERTS, LANES), F32)],
        compiler_params=pltpu.CompilerParams(
            dimension_semantics=("arbitrary", "arbitrary"), vmem_limit_bytes=VMEM_LIMIT),
        name="route",
    )(logits_t)


def _dump_row(slot, n_tokens):
    tile = lax.shift_right_logical(slot, TM_LOG2)
    return TOP_K * n_tokens + (tile & 1) * TM + (slot & (TM - 1))


def _codes_kernel(pos_ref, cnt_ref, codes_ref, *, n_tokens):
    step = pl.program_id(0)

    def fill(p, carry):
        codes_ref[p] = _dump_row(p, n_tokens)
        return carry

    @pl.when(step == 0)
    def _():
        def per_expert(e, start):
            c = cnt_ref[e]
            end = start + lax.shift_left(lax.shift_right_logical(c + (TM - 1), TM_LOG2), TM_LOG2)
            lax.fori_loop(start + c, end, fill, 0)
            return end
        total = lax.fori_loop(0, N_EXPERTS, per_expert, 0)
        lax.fori_loop(total, codes_ref.shape[0], fill, 0)

    base = step * TC

    def body(j, carry):
        codes_ref[pos_ref[j]] = base + j
        return carry

    lax.fori_loop(0, TC, body, 0, unroll=16)


def _codes(pos_flat, counts, p_rows):
    n = pos_flat.shape[0]
    return pl.pallas_call(
        functools.partial(_codes_kernel, n_tokens=n // TOP_K),
        grid=(n // TC,),
        in_specs=[
            pl.BlockSpec((TC,), lambda i: (i,), memory_space=pltpu.SMEM),
            pl.BlockSpec(memory_space=pltpu.SMEM),
        ],
        out_specs=pl.BlockSpec(memory_space=pltpu.SMEM),
        out_shape=jax.ShapeDtypeStruct((p_rows,), jnp.int32),
        compiler_params=pltpu.CompilerParams(dimension_semantics=("arbitrary",)),
        name="codes",
    )(pos_flat, counts)


def _experts_kernel(te_ref, nv_ref, codes_ref, h_ref, wg_ref, bg_ref, wu_ref, bu_ref, wd_ref, bd_ref,
                    y_ref, xbuf, obuf, act, wg_b, wu_b, wd_b, gsem, ssem, *, n_tokens):
    i = pl.program_id(0)
    n_valid = nv_ref[0]
    slot = i & 1
    other = 1 - slot

    def row_tile(row):
        return pl.ds(pl.multiple_of(row * ROW_TILE, ROW_TILE), ROW_TILE)

    def gather_row(tile, r, s):
        src = codes_ref[tile * TM + r] & (n_tokens - 1)
        return pltpu.make_async_copy(h_ref.at[row_tile(src), :], xbuf.at[s, row_tile(r), :], gsem.at[s])

    def scatter_row(dst, r, s):
        return pltpu.make_async_copy(obuf.at[s, row_tile(r), :], y_ref.at[row_tile(dst), :], ssem.at[s])

    def tile_gather_done(s):
        return pltpu.make_async_copy(h_ref.at[pl.ds(0, TM * ROW_TILE), :], xbuf.at[s], gsem.at[s])

    def tile_scatter_done(s):
        return pltpu.make_async_copy(obuf.at[s], y_ref.at[pl.ds(0, TM * ROW_TILE), :], ssem.at[s])

    @pl.when(i < n_valid)
    def _():
        @pl.when(i == 0)
        def _():
            obuf[...] = jnp.zeros_like(obuf)

            def first(r, carry):
                gather_row(0, r, 0).start()
                scatter_row(TOP_K * n_tokens + r, r, 1).start()
                return carry
            lax.fori_loop(0, TM, first, 0)
            tile_scatter_done(1).wait()

        prev = jnp.maximum(i - 1, 0)

        @pl.when((i == 0) | (te_ref[i] != te_ref[prev]))
        def _():
            for c in range(N_CHUNKS):
                cs = slice(c * N_COLS, (c + 1) * N_COLS)
                wg_b[c] = wg_ref[:, cs].astype(BF16)
                wu_b[c] = wu_ref[:, cs].astype(BF16)
                wd_b[c] = wd_ref[:, cs].astype(BF16)

        tile_gather_done(slot).wait()

        @pl.when(i >= 1)
        def _():
            tile_scatter_done(slot).wait()

        nxt = jnp.minimum(i + 1, n_valid - 1)
        is_first = i == 0

        def issue(r, carry):
            gather_row(nxt, r, other).start(priority=0)
            dst = jnp.where(is_first, TOP_K * n_tokens + TM + r, codes_ref[prev * TM + r])
            scatter_row(dst, r, other).start(priority=1)
            return carry

        lax.fori_loop(0, TM, issue, 0, unroll=8)

        xb = _load_row_tiles(xbuf.at[slot], TM).astype(BF16)
        for n in range(N_CHUNKS):
            gate = _dot(xb, wg_b[n]) + bg_ref[n:n + 1, :]
            up = _dot(xb, wu_b[n]) + bu_ref[n:n + 1, :]
            gate = jnp.minimum(gate, SWIGLU_LIMIT)
            up = jnp.clip(up, -SWIGLU_LIMIT, SWIGLU_LIMIT)
            act[n] = ((up + 1.0) * (gate * jax.nn.sigmoid(SWIGLU_ALPHA * gate))).astype(BF16)
        a = jnp.concatenate([act[c] for c in range(N_CHUNKS)], axis=1)
        for n in range(N_CHUNKS):
            out = _dot(a, wd_b[n]) + bd_ref[n:n + 1, :]
            for j in range(N_COLS // LANES):
                obuf[slot, pl.ds(n * (N_COLS // LANES) + j, TM, stride=ROW_TILE), :] = (
                    out[:, j * LANES:(j + 1) * LANES])

        @pl.when(i == n_valid - 1)
        def _():
            tile_gather_done(other).wait()
            tile_scatter_done(other).wait()

            def last(r, carry):
                scatter_row(codes_ref[i * TM + r], r, slot).start()
                return carry
            lax.fori_loop(0, TM, last, 0)
            tile_scatter_done(slot).wait()


def _experts(tile_expert, n_valid, codes, h2, wg, bg, wu, bu, wd, bd):
    t = h2.shape[0] // ROW_TILE
    n_tiles = codes.shape[0] // TM

    def w_map(i, te, nv, cd):
        return (te[i], 0, 0)

    w_spec = pl.BlockSpec((None, D_MODEL, D_MODEL), w_map)
    b_spec = pl.BlockSpec((None, N_CHUNKS, N_COLS), w_map)
    return pl.pallas_call(
        functools.partial(_experts_kernel, n_tokens=t),
        grid_spec=pltpu.PrefetchScalarGridSpec(
            num_scalar_prefetch=3,
            grid=(n_tiles,),
            in_specs=[pl.BlockSpec(memory_space=pl.ANY), w_spec, b_spec, w_spec, b_spec, w_spec, b_spec],
            out_specs=pl.BlockSpec(memory_space=pl.ANY),
            scratch_shapes=[
                pltpu.VMEM((2, TM * ROW_TILE, LANES), F32),
                pltpu.VMEM((2, TM * ROW_TILE, LANES), F32),
                pltpu.VMEM((N_CHUNKS, TM, N_COLS), BF16),
                pltpu.VMEM((N_CHUNKS, D_MODEL, N_COLS), BF16),
                pltpu.VMEM((N_CHUNKS, D_MODEL, N_COLS), BF16),
                pltpu.VMEM((N_CHUNKS, D_MODEL, N_COLS), BF16),
                pltpu.SemaphoreType.DMA((2,)),
                pltpu.SemaphoreType.DMA((2,)),
            ],
        ),
        out_shape=jax.ShapeDtypeStruct(((TOP_K * t + 2 * TM) * ROW_TILE, LANES), F32),
        compiler_params=pltpu.CompilerParams(
            dimension_semantics=("arbitrary",), vmem_limit_bytes=VMEM_LIMIT),
        name="experts",
    )(tile_expert, n_valid, codes, h2, wg, bg, wu, bu, wd, bd)


def _combine_kernel(gate_ref, x1_ref, fw_ref, y0_ref, y1_ref, y2_ref, y3_ref, out_ref):
    gates = gate_ref[...]
    y = x1_ref[...]
    for k, y_ref in enumerate((y0_ref, y1_ref, y2_ref, y3_ref)):
        y = y + gates[:, k:k + 1] * _load_row_tiles(y_ref, TT)
    ms = jnp.mean(y * y, axis=-1, keepdims=True)
    out_ref[...] = y * lax.rsqrt(ms + EPS) * fw_ref[...]


def _combine(gates, x1, final_w, y4):
    t = x1.shape[0]
    steps = t // TT

    def y_spec(k):
        return pl.BlockSpec((TT * ROW_TILE, LANES), lambda i: (k * steps + i, 0))

    return pl.pallas_call(
        _combine_kernel,
        grid=(steps,),
        in_specs=[
            pl.BlockSpec((TT, LANES), lambda i: (i, 0)),
            pl.BlockSpec((TT, D_MODEL), lambda i: (i, 0)),
            pl.BlockSpec((1, D_MODEL), lambda i: (0, 0)),
            y_spec(0), y_spec(1), y_spec(2), y_spec(3),
        ],
        out_specs=pl.BlockSpec((TT, D_MODEL), lambda i: (i, 0)),
        out_shape=jax.ShapeDtypeStruct((t, D_MODEL), F32),
        compiler_params=pltpu.CompilerParams(
            dimension_semantics=("arbitrary",), vmem_limit_bytes=VMEM_LIMIT),
        name="combine",
    )(gates, x1, final_w, y4, y4, y4, y4)


def _pad_lanes(v, offset, fill=0.0):
    row = jnp.full((1, LANES), fill, F32)
    return row.at[0, offset:offset + v.shape[0]].set(v.astype(F32))


def kernel(x, norm_mix_w, w_in, gla_w_alpha_up, gla_b_alpha, gla_norm_w, ssd_conv_w, ssd_conv_b,
           ssd_dt_bias, ssd_A_log, ssd_D, ssd_norm_w, w_out, norm_ffn_w, router_w, router_b,
           moe_w_gate, moe_b_gate, moe_w_up, moe_b_up, moe_w_down, moe_b_down, final_norm_w):
    bsz, seqlen, d = x.shape
    t = bsz * seqlen
    depth = w_in.shape[0]
    assert depth == 1, "the final RMSNorm is fused into the (single) layer's combine step"
    assert t & (t - 1) == 0, "slot codes pack (choice, token) with a power-of-two token count"
    p_rows = t * TOP_K + N_EXPERTS * TM
    n_tiles = p_rows // TM
    for l in range(depth):
        w = w_in[l]
        w_all = jnp.concatenate(
            [w[:, 0:1536], w[:, 1552:3088], w[:, 1536:1552], w[:, 3088:3096],
             jnp.zeros((d, N_SMALL - GLA_GATE_RANK - SSD_HEADS), w.dtype)], axis=1).astype(BF16)
        wup = jnp.zeros((N_SMALL, GLA_KW), F32).at[0:GLA_GATE_RANK].set(gla_w_alpha_up[l])
        dtb = _pad_lanes(ssd_dt_bias[l], DT_COL)
        aneg = _pad_lanes(-jnp.exp(ssd_A_log[l].astype(F32)), DT_COL)
        dexp = jnp.repeat(ssd_D[l].astype(F32), SSD_HEADDIM)[None, :]
        rw = jnp.zeros((LANES, d), F32).at[0:N_EXPERTS].set(router_w[l].T)
        rb = jnp.zeros((LANES, TL), F32).at[0:N_EXPERTS].set(
            jnp.broadcast_to(router_b[l].astype(F32)[:, None], (N_EXPERTS, TL)))

        pm, ps = _inproj(x.reshape(t, d), norm_mix_w[l][None, :], w_all)
        x1, h2, logits = _mixer(
            pm.reshape(bsz, seqlen, N_MAIN), ps.reshape(bsz, seqlen, N_SMALL), x,
            wup, gla_b_alpha[l][None, :], gla_norm_w[l][None, :], ssd_conv_w[l],
            ssd_conv_b[l][None, :], dtb, aneg, dexp, ssd_norm_w[l][None, :],
            w_out[l].astype(BF16), norm_ffn_w[l][None, :], rw, rb)

        pos, gates, counts = _route(logits)

        cnt = counts[:, 0].astype(jnp.int32)
        ends = jnp.cumsum(((cnt + TM - 1) // TM) * TM)
        n_valid = (ends[-1] // TM).astype(jnp.int32)
        starts = jnp.arange(n_tiles, dtype=jnp.int32) * TM
        tile_expert = jnp.sum(starts[:, None] >= ends[None, :], axis=1).astype(jnp.int32)
        last_expert = tile_expert[jnp.maximum(n_valid - 1, 0)]
        tile_expert = jnp.where(starts < ends[-1], tile_expert, last_expert)

        codes = _codes(pos[0:TOP_K].reshape(TOP_K * t), cnt, p_rows)
        y4 = _experts(tile_expert, n_valid.reshape(1), codes, h2.reshape(t * ROW_TILE, LANES),
                      moe_w_gate[l], moe_b_gate[l].reshape(N_EXPERTS, N_CHUNKS, N_COLS),
                      moe_w_up[l], moe_b_up[l].reshape(N_EXPERTS, N_CHUNKS, N_COLS),
                      moe_w_down[l], moe_b_down[l].reshape(N_EXPERTS, N_CHUNKS, N_COLS))
        x = _combine(gates, x1.reshape(t, d), final_norm_w[None, :], y4).reshape(bsz, seqlen, d)
    return x
```

```python
import functools

import jax
import jax.numpy as jnp
from jax import lax
from jax.experimental import pallas as pl
from jax.experimental.pallas import tpu as pltpu

F32 = jnp.float32
BF16 = jnp.bfloat16

D_MODEL = 1024
GLA_WIDTH = 512
GLA_HEADS = 4
GLA_DV = 128
GLA_DK = 64
GLA_KW = 256
GLA_GATE_RANK = 16
GLA_GATE_NORM = 16.0
SSD_WIDTH = 512
SSD_HEADDIM = 64
SSD_HEADS = 8
SSD_GROUPS = 2
SSD_HPG = 4
SSD_STATE = 128
SSD_CONV = 4
SSD_CONV_CH = 1024
N_EXPERTS = 32
TOP_K = 4
SWIGLU_LIMIT = 7.0
SWIGLU_ALPHA = 1.702
EPS = 1e-6
GROUP_EPS = 1e-5

LANES = 128
ROW_TILE = D_MODEL // LANES
N_MAIN = 3072
N_SMALL = LANES
DT_COL = GLA_GATE_RANK

GLA_CHUNK = 64
SSD_CHUNK = 128
TM_PROJ = 512
TL = 256
TR = 512
TT = 512
TM_LOG2 = 8
TM = 1 << TM_LOG2
N_COLS = 256
N_CHUNKS = D_MODEL // N_COLS
VMEM_LIMIT = 56 * 1024 * 1024


def _dot(a, b):
    return jnp.dot(a, b, preferred_element_type=F32)


def _dot_nt(a, b):
    return lax.dot_general(a, b, (((1,), (1,)), ((), ())), preferred_element_type=F32)


def _dot_tn(a, b):
    return lax.dot_general(a, b, (((0,), (0,)), ((), ())), preferred_element_type=F32)


def _split3(a):
    hi = a.astype(BF16)
    r1 = a - hi.astype(F32)
    mid = r1.astype(BF16)
    lo = (r1 - mid.astype(F32)).astype(BF16)
    return hi, mid, lo


def _dot_sel_lhs(sel, a):
    hi, mid, lo = _split3(a)
    return _dot(sel, hi) + _dot(sel, mid) + _dot(sel, lo)


def _dot_sel_rhs(a, sel, terms=3):
    parts = _split3(a)[:terms]
    out = _dot(parts[0], sel)
    for p in parts[1:]:
        out = out + _dot(p, sel)
    return out


def _dot_hi(a, b):
    a_hi = a.astype(BF16)
    a_lo = (a - a_hi.astype(F32)).astype(BF16)
    b_hi = b.astype(BF16)
    b_lo = (b - b_hi.astype(F32)).astype(BF16)
    return _dot(a_hi, b_hi) + _dot(a_lo, b_hi) + _dot(a_hi, b_lo)


def _dot_hi_nt(a, b):
    a_hi = a.astype(BF16)
    a_lo = (a - a_hi.astype(F32)).astype(BF16)
    b_hi = b.astype(BF16)
    b_lo = (b - b_hi.astype(F32)).astype(BF16)
    return _dot_nt(a_hi, b_hi) + _dot_nt(a_lo, b_hi) + _dot_nt(a_hi, b_lo)


def _softplus(x):
    return jnp.maximum(x, 0.0) + jnp.log1p(jnp.exp(-jnp.abs(x)))


def _silu(x):
    return x * jax.nn.sigmoid(x)


def _iota(shape, dim):
    return lax.broadcasted_iota(jnp.int32, shape, dim)


def _store_row_tiles(ref, value):
    rows = value.shape[0]
    for j in range(ROW_TILE):
        ref[pl.ds(j, rows, stride=ROW_TILE), :] = value[:, j * LANES:(j + 1) * LANES]


def _load_row_tiles(ref, rows):
    return jnp.concatenate([ref[pl.ds(j, rows, stride=ROW_TILE), :] for j in range(ROW_TILE)], axis=1)


def _inproj_kernel(x_ref, nw_ref, w_ref, pm_ref, ps_ref):
    x = x_ref[...]
    ms = jnp.mean(x * x, axis=-1, keepdims=True)
    h = (x * lax.rsqrt(ms + EPS) * nw_ref[...]).astype(BF16)
    step = 512
    for n0 in range(0, N_MAIN, step):
        pm_ref[:, n0:n0 + step] = _dot(h, w_ref[:, n0:n0 + step]).astype(BF16)
    ps_ref[...] = _dot(h, w_ref[:, N_MAIN:N_MAIN + N_SMALL])


def _inproj(x2d, norm_w, w_all):
    t = x2d.shape[0]
    return pl.pallas_call(
        _inproj_kernel,
        grid=(t // TM_PROJ,),
        in_specs=[
            pl.BlockSpec((TM_PROJ, D_MODEL), lambda i: (i, 0)),
            pl.BlockSpec((1, D_MODEL), lambda i: (0, 0)),
            pl.BlockSpec((D_MODEL, N_MAIN + N_SMALL), lambda i: (0, 0)),
        ],
        out_specs=[
            pl.BlockSpec((TM_PROJ, N_MAIN), lambda i: (i, 0)),
            pl.BlockSpec((TM_PROJ, N_SMALL), lambda i: (i, 0)),
        ],
        out_shape=[
            jax.ShapeDtypeStruct((t, N_MAIN), BF16),
            jax.ShapeDtypeStruct((t, N_SMALL), F32),
        ],
        compiler_params=pltpu.CompilerParams(
            dimension_semantics=("arbitrary",), vmem_limit_bytes=VMEM_LIMIT),
        name="inproj",
    )(x2d, norm_w, w_all)


def _mixer_kernel(pm_ref, ps_ref, x_ref, wup_ref, balpha_ref, gnw_ref, convw_ref, convb_ref,
                  dtb_ref, aneg_ref, dexp_ref, snw_ref, wout_ref, nfw_ref, rw_ref, rb_ref,
                  x1_ref, h2_ref, lg_ref,
                  gla_state, ssd_state, conv_tail, mix_scr):
    @pl.when(pl.program_id(1) == 0)
    def _():
        gla_state[...] = jnp.zeros_like(gla_state)
        ssd_state[...] = jnp.zeros_like(ssd_state)
        conv_tail[...] = jnp.zeros_like(conv_tail)

    small = ps_ref[...]

    row = _iota((TL, TL), 0)
    col = _iota((TL, TL), 1)
    causal = col <= row
    cum64 = jnp.where(causal & ((row // GLA_CHUNK) == (col // GLA_CHUNK)), 1.0, 0.0).astype(BF16)
    cum128 = jnp.where(causal & ((row // SSD_CHUNK) == (col // SSD_CHUNK)), 1.0, 0.0).astype(BF16)

    xa = _dot_hi(small, wup_ref[...]) + balpha_ref[...]
    log_a = (jnp.minimum(xa, 0.0) - jnp.log1p(jnp.exp(-jnp.abs(xa)))) * (1.0 / GLA_GATE_NORM)
    bcum = _dot_sel_lhs(cum64, log_a)

    lane_kw = _iota((GLA_CHUNK, GLA_KW), 1)
    head_masks = [(lane_kw // GLA_DK) == h for h in range(GLA_HEADS)]
    lane_kw_s = _iota((GLA_DV, GLA_KW), 1)
    head_masks_s = [(lane_kw_s // GLA_DK) == h for h in range(GLA_HEADS)]
    tril64 = _iota((GLA_CHUNK, GLA_CHUNK), 1) <= _iota((GLA_CHUNK, GLA_CHUNK), 0)
    q_scale = GLA_DK ** -0.5

    for c in range(TL // GLA_CHUNK):
        rs = slice(c * GLA_CHUNK, (c + 1) * GLA_CHUNK)
        bc = bcum[rs]
        b_mid = bc[GLA_CHUNK // 2:GLA_CHUNK // 2 + 1]
        b_last = bc[GLA_CHUNK - 1:GLA_CHUNK]
        qc = pm_ref[rs, 0:GLA_KW].astype(F32) * q_scale
        kc = pm_ref[rs, GLA_KW:2 * GLA_KW].astype(F32)
        vc = pm_ref[rs, 2 * GLA_KW:2 * GLA_KW + GLA_WIDTH]
        q_in = (qc * jnp.exp(bc - b_mid)).astype(BF16)
        k_in = (kc * jnp.exp(b_mid - bc)).astype(BF16)
        q_st = (qc * jnp.exp(bc)).astype(BF16)
        k_st = (kc * jnp.exp(b_last - bc)).astype(BF16)
        st = gla_state[...]
        st_b = st.astype(BF16)
        zero_b = jnp.zeros_like(q_in)
        for h in range(GLA_HEADS):
            scores = _dot_nt(jnp.where(head_masks[h], q_in, zero_b), k_in)
            scores = jnp.where(tril64, scores, 0.0).astype(BF16)
            o_h = _dot(scores, vc[:, h * GLA_DV:(h + 1) * GLA_DV])
            o_h = o_h + _dot_nt(jnp.where(head_masks[h], q_st, zero_b), st_b)
            mix_scr[rs, h * GLA_DV:(h + 1) * GLA_DV] = o_h
        upd = _dot_tn(vc, k_st)
        new_st = st * jnp.exp(b_last)
        for h in range(GLA_HEADS):
            new_st = new_st + jnp.where(head_masks_s[h], upd[h * GLA_DV:(h + 1) * GLA_DV], 0.0)
        gla_state[...] = new_st

    xbc = pm_ref[:, 2048:3072].astype(F32)
    tail = conv_tail[...]
    conv_tail[...] = xbc[TL - 8:TL]
    row8 = _iota((8, SSD_CONV_CH), 0)
    conv = xbc * convw_ref[SSD_CONV - 1:SSD_CONV, :]
    for s in range(1, SSD_CONV):
        shifted = pltpu.roll(xbc, s, 0)
        head = jnp.where(row8 < s, pltpu.roll(tail, s, 0), shifted[0:8])
        shifted = jnp.concatenate([head, shifted[8:]], axis=0)
        conv = conv + shifted * convw_ref[SSD_CONV - 1 - s:SSD_CONV - s, :]
    act = _silu(conv + convb_ref[...])
    xs = act[:, 0:SSD_WIDTH]
    bm = act[:, SSD_WIDTH:SSD_WIDTH + SSD_GROUPS * SSD_STATE].astype(BF16)
    cm = act[:, SSD_WIDTH + SSD_GROUPS * SSD_STATE:].astype(BF16)

    dt_full = _softplus(small + dtb_ref[...])
    a_full = dt_full * aneg_ref[...]
    acum = _dot_sel_lhs(cum128, a_full)
    acum_t = acum.T

    e_row = _iota((N_SMALL, SSD_WIDTH), 0)
    e_col = _iota((N_SMALL, SSD_WIDTH), 1)
    spread64 = jnp.where(e_row == DT_COL + e_col // SSD_HEADDIM, 1.0, 0.0).astype(BF16)
    e_row2 = _iota((N_SMALL, SSD_HEADS * LANES), 0)
    e_col2 = _iota((N_SMALL, SSD_HEADS * LANES), 1)
    spread128 = jnp.where(e_row2 == DT_COL + e_col2 // LANES, 1.0, 0.0).astype(BF16)
    dt_e = _dot_sel_rhs(dt_full, spread64, terms=1)
    ac_e = _dot_sel_rhs(acum, spread64, terms=2)
    ac_w = _dot_sel_rhs(acum, spread128, terms=2)

    tril128 = _iota((SSD_CHUNK, SSD_CHUNK), 1) <= _iota((SSD_CHUNK, SSD_CHUNK), 0)
    lane_g = _iota((SSD_CHUNK, SSD_HPG * SSD_HEADDIM), 1)
    for c in range(TL // SSD_CHUNK):
        rs = slice(c * SSD_CHUNK, (c + 1) * SSD_CHUNK)
        ac_c = ac_e[rs]
        a_last = ac_c[SSD_CHUNK - 1:SSD_CHUNK]
        dt_c = dt_e[rs]
        xs_c = xs[rs]
        x_dt = (xs_c * dt_c).astype(BF16)
        x_w = (xs_c * (jnp.exp(a_last - ac_c) * dt_c)).astype(BF16)
        e_ac = jnp.exp(ac_c)
        for g in range(SSD_GROUPS):
            gs = slice(g * SSD_STATE, (g + 1) * SSD_STATE)
            ws = slice(g * SSD_HPG * SSD_HEADDIM, (g + 1) * SSD_HPG * SSD_HEADDIM)
            c_g = cm[rs, gs]
            b_g = bm[rs, gs]
            cb = _dot_nt(c_g, b_g)
            x_dt_g = x_dt[:, ws]
            lhs_parts = []
            rhs_parts = []
            for hh in range(SSD_HPG):
                h = g * SSD_HPG + hh
                seg = ac_w[rs, h * LANES:(h + 1) * LANES] - acum_t[DT_COL + h:DT_COL + h + 1, rs]
                lmat = jnp.where(tril128, jnp.exp(jnp.where(tril128, seg, 0.0)), 0.0)
                lhs_parts.append((cb * lmat).astype(BF16))
                rhs_parts.append(jnp.where((lane_g // SSD_HEADDIM) == hh, x_dt_g,
                                           jnp.zeros_like(x_dt_g)))
            intra = _dot(jnp.concatenate(lhs_parts, axis=1), jnp.concatenate(rhs_parts, axis=0))
            st = ssd_state[g]
            inter = _dot(c_g, st.astype(BF16)) * e_ac[:, ws]
            mix_scr[rs, GLA_WIDTH + g * 256:GLA_WIDTH + (g + 1) * 256] = intra + inter
            ssd_state[g] = st * jnp.exp(a_last[:, ws]) + _dot_tn(b_g, x_w[:, ws])

    o = mix_scr[:, 0:GLA_WIDTH]
    g_gate = _silu(pm_ref[:, 1024:1536].astype(F32))
    gla_parts = []
    for h in range(GLA_HEADS):
        o_h = o[:, h * GLA_DV:(h + 1) * GLA_DV]
        ms = jnp.mean(o_h * o_h, axis=-1, keepdims=True)
        gla_parts.append(o_h * lax.rsqrt(ms + GROUP_EPS))
    gla_out = jnp.concatenate(gla_parts, axis=1) * gnw_ref[...] * g_gate

    y = mix_scr[:, GLA_WIDTH:] + dexp_ref[...] * xs
    y = y * _silu(pm_ref[:, 1536:2048].astype(F32))
    ssd_parts = []
    for g in range(SSD_GROUPS):
        y_g = y[:, g * 256:(g + 1) * 256]
        ms = jnp.mean(y_g * y_g, axis=-1, keepdims=True)
        ssd_parts.append(y_g * lax.rsqrt(ms + GROUP_EPS))
    ssd_out = jnp.concatenate(ssd_parts, axis=1) * snw_ref[...]

    mixed = jnp.concatenate([gla_out, ssd_out], axis=1).astype(BF16)
    x1 = x_ref[...] + _dot(mixed, wout_ref[...])
    x1_ref[...] = x1

    ms = jnp.mean(x1 * x1, axis=-1, keepdims=True)
    h2 = x1 * lax.rsqrt(ms + EPS) * nfw_ref[...]
    _store_row_tiles(h2_ref, h2)
    lg_ref[...] = _dot_hi_nt(rw_ref[...], h2) + rb_ref[...]


def _mixer(pm, ps, x, wup, balpha, gnw, convw, convb, dtb, aneg, dexp, snw, wout, nfw, rw, rb):
    bsz, seqlen, _ = x.shape

    def full(a):
        return pl.BlockSpec(a.shape, lambda b, l: (0,) * a.ndim)

    def tile(width):
        return pl.BlockSpec((None, TL, width), lambda b, l: (b, l, 0))

    params = (wup, balpha, gnw, convw, convb, dtb, aneg, dexp, snw, wout, nfw, rw, rb)
    return pl.pallas_call(
        _mixer_kernel,
        grid=(bsz, seqlen // TL),
        in_specs=[tile(N_MAIN), tile(N_SMALL), tile(D_MODEL)] + [full(p) for p in params],
        out_specs=[tile(D_MODEL), pl.BlockSpec((None, TL * ROW_TILE, LANES), lambda b, l: (b, l, 0)),
                   pl.BlockSpec((LANES, TL), lambda b, l: (0, b * (seqlen // TL) + l))],
        out_shape=[
            jax.ShapeDtypeStruct((bsz, seqlen, D_MODEL), F32),
            jax.ShapeDtypeStruct((bsz, seqlen * ROW_TILE, LANES), F32),
            jax.ShapeDtypeStruct((LANES, bsz * seqlen), F32),
        ],
        scratch_shapes=[
            pltpu.VMEM((GLA_DV, GLA_KW), F32),
            pltpu.VMEM((SSD_GROUPS, SSD_STATE, SSD_HPG * SSD_HEADDIM), F32),
            pltpu.VMEM((8, SSD_CONV_CH), F32),
            pltpu.VMEM((TL, D_MODEL), F32),
        ],
        compiler_params=pltpu.CompilerParams(
            dimension_semantics=("arbitrary", "arbitrary"), vmem_limit_bytes=VMEM_LIMIT),
        name="mixer",
    )(pm, ps, x, *params)


def _route_kernel(lg_ref, pos_ref, gate_ref, cnt_ref, cnt_scr, run_scr):
    phase = pl.program_id(0)
    step = pl.program_id(1)

    @pl.when((phase == 0) & (step == 0))
    def _():
        cnt_scr[...] = jnp.zeros_like(cnt_scr)
        run_scr[...] = jnp.zeros_like(run_scr)

    lg = lg_ref[0:N_EXPERTS, :]
    row = _iota((N_EXPERTS, TR), 0)
    work = lg
    onehots = []
    vals = []
    for _ in range(TOP_K):
        m = jnp.max(work, axis=0, keepdims=True)
        idx = jnp.min(jnp.where(work == m, row, N_EXPERTS), axis=0, keepdims=True)
        oh = row == idx
        onehots.append(oh)
        vals.append(m)
        work = jnp.where(oh, -jnp.inf, work)
    multi = jnp.where(onehots[0] | onehots[1] | onehots[2] | onehots[3], 1.0, 0.0).astype(BF16)
    tile_cnt = _dot(multi, jnp.ones((TR, LANES), BF16))

    @pl.when(phase == 0)
    def _():
        cnt_scr[...] = cnt_scr[...] + tile_cnt

    @pl.when(phase == 1)
    def _():
        counts = cnt_scr[...]
        padded = jnp.floor((counts + (TM - 1)) * (1.0 / TM)) * TM
        lower = jnp.where(_iota((N_EXPERTS, N_EXPERTS), 1) < _iota((N_EXPERTS, N_EXPERTS), 0),
                          1.0, 0.0).astype(BF16)
        offs = _dot_sel_lhs(lower, padded)
        before = jnp.where(_iota((TR, TR), 0) < _iota((TR, TR), 1), 1.0, 0.0).astype(BF16)
        rank = _dot(multi, before)
        start = offs + run_scr[...]
        run_scr[...] = run_scr[...] + tile_cnt
        base = rank + jnp.concatenate([start] * (TR // LANES), axis=1)
        exps = [jnp.exp(v - vals[0]) for v in vals]
        den = exps[0] + exps[1] + exps[2] + exps[3]
        pos_rows = [jnp.sum(jnp.where(oh, base, 0.0), axis=0, keepdims=True) for oh in onehots]
        pos_ref[...] = jnp.concatenate(pos_rows + [jnp.zeros((8 - TOP_K, TR), F32)], axis=0).astype(jnp.int32)
        gate_rows = [e / den for e in exps]
        gate_ref[...] = jnp.concatenate(gate_rows + [jnp.zeros((LANES - TOP_K, TR), F32)], axis=0).T
        cnt_ref[...] = counts


def _route(logits_t):
    t = logits_t.shape[1]
    return pl.pallas_call(
        _route_kernel,
        grid=(2, t // TR),
        in_specs=[pl.BlockSpec((LANES, TR), lambda p, i: (0, i))],
        out_specs=[
            pl.BlockSpec((8, TR), lambda p, i: (0, i * p)),
            pl.BlockSpec((TR, LANES), lambda p, i: (i * p, 0)),
            pl.BlockSpec((N_EXPERTS, LANES), lambda p, i: (0, 0)),
        ],
        out_shape=[
            jax.ShapeDtypeStruct((8, t), jnp.int32),
            jax.ShapeDtypeStruct((t, LANES), F32),
            jax.ShapeDtypeStruct((N_EXPERTS, LANES), F32),
        ],
        scratch_shapes=[pltpu.VMEM((N_EXPERTS, LANES), F32), pltpu.VMEM((N_EXPERTS, LANES), F32)],
        compiler_params=pltpu.CompilerParams(
            dimension_semantics=("arbitrary", "arbitrary"), vmem_limit_bytes=VMEM_LIMIT),
        name="route",
    )(logits_t)


def _dump_rows(n_slots, n_tokens):
    slot = jnp.arange(n_slots, dtype=jnp.int32)
    return TOP_K * n_tokens + ((slot // TM) % 2) * TM + slot % TM


def _codes_kernel(pos_ref, dump_ref, codes_ref, sem):
    fill = pltpu.make_async_copy(dump_ref, codes_ref, sem)
    fill.start()
    fill.wait()

    def body(q, carry):
        codes_ref[pos_ref[q]] = q
        return carry

    lax.fori_loop(0, pos_ref.shape[0], body, 0, unroll=16)


def _codes(pos_flat, dump_rows):
    return pl.pallas_call(
        _codes_kernel,
        in_specs=[
            pl.BlockSpec(memory_space=pltpu.SMEM),
            pl.BlockSpec(memory_space=pl.ANY),
        ],
        out_specs=pl.BlockSpec(memory_space=pltpu.SMEM),
        out_shape=jax.ShapeDtypeStruct(dump_rows.shape, jnp.int32),
        scratch_shapes=[pltpu.SemaphoreType.DMA(())],
        name="codes",
    )(pos_flat, dump_rows)


def _experts_kernel(te_ref, nv_ref, codes_ref, h_ref, wg_ref, bg_ref, wu_ref, bu_ref, wd_ref, bd_ref,
                    y_ref, xbuf, obuf, act, wg_b, wu_b, wd_b, gsem, ssem, *, n_tokens):
    i = pl.program_id(0)
    n_valid = nv_ref[0]
    slot = i & 1
    other = 1 - slot

    def row_tile(row):
        return pl.ds(pl.multiple_of(row * ROW_TILE, ROW_TILE), ROW_TILE)

    def gather_row(tile, r, s):
        src = codes_ref[tile * TM + r] & (n_tokens - 1)
        return pltpu.make_async_copy(h_ref.at[row_tile(src), :], xbuf.at[s, row_tile(r), :], gsem.at[s])

    def scatter_row(dst, r, s):
        return pltpu.make_async_copy(obuf.at[s, row_tile(r), :], y_ref.at[row_tile(dst), :], ssem.at[s])

    def tile_gather_done(s):
        return pltpu.make_async_copy(h_ref.at[pl.ds(0, TM * ROW_TILE), :], xbuf.at[s], gsem.at[s])

    def tile_scatter_done(s):
        return pltpu.make_async_copy(obuf.at[s], y_ref.at[pl.ds(0, TM * ROW_TILE), :], ssem.at[s])

    @pl.when(i < n_valid)
    def _():
        @pl.when(i == 0)
        def _():
            obuf[...] = jnp.zeros_like(obuf)

            def first(r, carry):
                gather_row(0, r, 0).start()
                scatter_row(TOP_K * n_tokens + r, r, 1).start()
                return carry
            lax.fori_loop(0, TM, first, 0)
            tile_scatter_done(1).wait()

        prev = jnp.maximum(i - 1, 0)

        @pl.when((i == 0) | (te_ref[i] != te_ref[prev]))
        def _():
            for c in range(N_CHUNKS):
                cs = slice(c * N_COLS, (c + 1) * N_COLS)
                wg_b[c] = wg_ref[:, cs].astype(BF16)
                wu_b[c] = wu_ref[:, cs].astype(BF16)
                wd_b[c] = wd_ref[:, cs].astype(BF16)

        tile_gather_done(slot).wait()

        @pl.when(i >= 1)
        def _():
            tile_scatter_done(slot).wait()

        nxt = jnp.minimum(i + 1, n_valid - 1)
        is_first = i == 0

        def issue(r, carry):
            gather_row(nxt, r, other).start(priority=0)
            dst = jnp.where(is_first, TOP_K * n_tokens + TM + r, codes_ref[prev * TM + r])
            scatter_row(dst, r, other).start(priority=1)
            return carry

        lax.fori_loop(0, TM, issue, 0, unroll=8)

        xb = _load_row_tiles(xbuf.at[slot], TM).astype(BF16)
        for n in range(N_CHUNKS):
            gate = _dot(xb, wg_b[n]) + bg_ref[n:n + 1, :]
            up = _dot(xb, wu_b[n]) + bu_ref[n:n + 1, :]
            gate = jnp.minimum(gate, SWIGLU_LIMIT)
            up = jnp.clip(up, -SWIGLU_LIMIT, SWIGLU_LIMIT)
            act[n] = ((up + 1.0) * (gate * jax.nn.sigmoid(SWIGLU_ALPHA * gate))).astype(BF16)
        a = jnp.concatenate([act[c] for c in range(N_CHUNKS)], axis=1)
        for n in range(N_CHUNKS):
            out = _dot(a, wd_b[n]) + bd_ref[n:n + 1, :]
            for j in range(N_COLS // LANES):
                obuf[slot, pl.ds(n * (N_COLS // LANES) + j, TM, stride=ROW_TILE), :] = (
                    out[:, j * LANES:(j + 1) * LANES])

        @pl.when(i == n_valid - 1)
        def _():
            tile_gather_done(other).wait()
            tile_scatter_done(other).wait()

            def last(r, carry):
                scatter_row(codes_ref[i * TM + r], r, slot).start()
                return carry
            lax.fori_loop(0, TM, last, 0)
            tile_scatter_done(slot).wait()


def _experts(tile_expert, n_valid, codes, h2, wg, bg, wu, bu, wd, bd):
    t = h2.shape[0] // ROW_TILE
    n_tiles = codes.shape[0] // TM

    def w_map(i, te, nv, cd):
        return (te[i], 0, 0)

    w_spec = pl.BlockSpec((None, D_MODEL, D_MODEL), w_map)
    b_spec = pl.BlockSpec((None, N_CHUNKS, N_COLS), w_map)
    return pl.pallas_call(
        functools.partial(_experts_kernel, n_tokens=t),
        grid_spec=pltpu.PrefetchScalarGridSpec(
            num_scalar_prefetch=3,
            grid=(n_tiles,),
            in_specs=[pl.BlockSpec(memory_space=pl.ANY), w_spec, b_spec, w_spec, b_spec, w_spec, b_spec],
            out_specs=pl.BlockSpec(memory_space=pl.ANY),
            scratch_shapes=[
                pltpu.VMEM((2, TM * ROW_TILE, LANES), F32),
                pltpu.VMEM((2, TM * ROW_TILE, LANES), F32),
                pltpu.VMEM((N_CHUNKS, TM, N_COLS), BF16),
                pltpu.VMEM((N_CHUNKS, D_MODEL, N_COLS), BF16),
                pltpu.VMEM((N_CHUNKS, D_MODEL, N_COLS), BF16),
                pltpu.VMEM((N_CHUNKS, D_MODEL, N_COLS), BF16),
                pltpu.SemaphoreType.DMA((2,)),
                pltpu.SemaphoreType.DMA((2,)),
            ],
        ),
        out_shape=jax.ShapeDtypeStruct(((TOP_K * t + 2 * TM) * ROW_TILE, LANES), F32),
        compiler_params=pltpu.CompilerParams(
            dimension_semantics=("arbitrary",), vmem_limit_bytes=VMEM_LIMIT),
        name="experts",
    )(tile_expert, n_valid, codes, h2, wg, bg, wu, bu, wd, bd)


def _combine_kernel(gate_ref, x1_ref, fw_ref, y0_ref, y1_ref, y2_ref, y3_ref, out_ref):
    gates = gate_ref[...]
    y = x1_ref[...]
    for k, y_ref in enumerate((y0_ref, y1_ref, y2_ref, y3_ref)):
        y = y + gates[:, k:k + 1] * _load_row_tiles(y_ref, TT)
    ms = jnp.mean(y * y, axis=-1, keepdims=True)
    out_ref[...] = y * lax.rsqrt(ms + EPS) * fw_ref[...]


def _combine(gates, x1, final_w, y4):
    t = x1.shape[0]
    steps = t // TT

    def y_spec(k):
        return pl.BlockSpec((TT * ROW_TILE, LANES), lambda i: (k * steps + i, 0))

    return pl.pallas_call(
        _combine_kernel,
        grid=(steps,),
        in_specs=[
            pl.BlockSpec((TT, LANES), lambda i: (i, 0)),
            pl.BlockSpec((TT, D_MODEL), lambda i: (i, 0)),
            pl.BlockSpec((1, D_MODEL), lambda i: (0, 0)),
            y_spec(0), y_spec(1), y_spec(2), y_spec(3),
        ],
        out_specs=pl.BlockSpec((TT, D_MODEL), lambda i: (i, 0)),
        out_shape=jax.ShapeDtypeStruct((t, D_MODEL), F32),
        compiler_params=pltpu.CompilerParams(
            dimension_semantics=("arbitrary",), vmem_limit_bytes=VMEM_LIMIT),
        name="combine",
    )(gates, x1, final_w, y4, y4, y4, y4)


def _pad_lanes(v, offset, fill=0.0):
    row = jnp.full((1, LANES), fill, F32)
    return row.at[0, offset:offset + v.shape[0]].set(v.astype(F32))


def kernel(x, norm_mix_w, w_in, gla_w_alpha_up, gla_b_alpha, gla_norm_w, ssd_conv_w, ssd_conv_b,
           ssd_dt_bias, ssd_A_log, ssd_D, ssd_norm_w, w_out, norm_ffn_w, router_w, router_b,
           moe_w_gate, moe_b_gate, moe_w_up, moe_b_up, moe_w_down, moe_b_down, final_norm_w):
    bsz, seqlen, d = x.shape
    t = bsz * seqlen
    depth = w_in.shape[0]
    assert depth == 1, "the final RMSNorm is fused into the (single) layer's combine step"
    assert t & (t - 1) == 0, "slot codes pack (choice, token) with a power-of-two token count"
    p_rows = t * TOP_K + N_EXPERTS * TM
    n_tiles = p_rows // TM
    for l in range(depth):
        w = w_in[l]
        w_all = jnp.concatenate(
            [w[:, 0:1536], w[:, 1552:3088], w[:, 1536:1552], w[:, 3088:3096],
             jnp.zeros((d, N_SMALL - GLA_GATE_RANK - SSD_HEADS), w.dtype)], axis=1).astype(BF16)
        wup = jnp.zeros((N_SMALL, GLA_KW), F32).at[0:GLA_GATE_RANK].set(gla_w_alpha_up[l])
        dtb = _pad_lanes(ssd_dt_bias[l], DT_COL)
        aneg = _pad_lanes(-jnp.exp(ssd_A_log[l].astype(F32)), DT_COL)
        dexp = jnp.repeat(ssd_D[l].astype(F32), SSD_HEADDIM)[None, :]
        rw = jnp.zeros((LANES, d), F32).at[0:N_EXPERTS].set(router_w[l].T)
        rb = jnp.zeros((LANES, TL), F32).at[0:N_EXPERTS].set(
            jnp.broadcast_to(router_b[l].astype(F32)[:, None], (N_EXPERTS, TL)))

        pm, ps = _inproj(x.reshape(t, d), norm_mix_w[l][None, :], w_all)
        x1, h2, logits = _mixer(
            pm.reshape(bsz, seqlen, N_MAIN), ps.reshape(bsz, seqlen, N_SMALL), x,
            wup, gla_b_alpha[l][None, :], gla_norm_w[l][None, :], ssd_conv_w[l],
            ssd_conv_b[l][None, :], dtb, aneg, dexp, ssd_norm_w[l][None, :],
            w_out[l].astype(BF16), norm_ffn_w[l][None, :], rw, rb)

        pos, gates, counts = _route(logits)

        cnt = counts[:, 0].astype(jnp.int32)
        ends = jnp.cumsum(((cnt + TM - 1) // TM) * TM)
        n_valid = (ends[-1] // TM).astype(jnp.int32)
        starts = jnp.arange(n_tiles, dtype=jnp.int32) * TM
        tile_expert = jnp.sum(starts[:, None] >= ends[None, :], axis=1).astype(jnp.int32)
        last_expert = tile_expert[jnp.maximum(n_valid - 1, 0)]
        tile_expert = jnp.where(starts < ends[-1], tile_expert, last_expert)

        codes = _codes(pos[0:TOP_K].reshape(TOP_K * t), _dump_rows(p_rows, t))
        y4 = _experts(tile_expert, n_valid.reshape(1), codes, h2.reshape(t * ROW_TILE, LANES),
                      moe_w_gate[l], moe_b_gate[l].reshape(N_EXPERTS, N_CHUNKS, N_COLS),
                      moe_w_up[l], moe_b_up[l].reshape(N_EXPERTS, N_CHUNKS, N_COLS),
                      moe_w_down[l], moe_b_down[l].reshape(N_EXPERTS, N_CHUNKS, N_COLS))
        x = _combine(gates, x1.reshape(t, d), final_norm_w[None, :], y4).reshape(bsz, seqlen, d)
    return x
```

```python
import functools

import jax
import jax.numpy as jnp
from jax import lax
from jax.experimental import pallas as pl
from jax.experimental.pallas import tpu as pltpu

F32 = jnp.float32
BF16 = jnp.bfloat16

D_MODEL = 1024
GLA_WIDTH = 512
GLA_HEADS = 4
GLA_DV = 128
GLA_DK = 64
GLA_KW = 256
GLA_GATE_RANK = 16
GLA_GATE_NORM = 16.0
SSD_WIDTH = 512
SSD_HEADDIM = 64
SSD_HEADS = 8
SSD_GROUPS = 2
SSD_HPG = 4
SSD_STATE = 128
SSD_CONV = 4
SSD_CONV_CH = 1024
N_EXPERTS = 32
TOP_K = 4
SWIGLU_LIMIT = 7.0
SWIGLU_ALPHA = 1.702
EPS = 1e-6
GROUP_EPS = 1e-5

LANES = 128
ROW_TILE = D_MODEL // LANES
BF16_ROWS = 16
N_MAIN = 3072
N_SMALL = LANES
DT_COL = GLA_GATE_RANK

GLA_CHUNK = 64
SSD_CHUNK = 128
TM_PROJ = 512
TL = 256
TR = 512
TT = 512
TM_LOG2 = 8
TM = 1 << TM_LOG2
N_COLS = 256
N_CHUNKS = D_MODEL // N_COLS
VMEM_LIMIT = 56 * 1024 * 1024


def _dot(a, b):
    return jnp.dot(a, b, preferred_element_type=F32)


def _dot_nt(a, b):
    return lax.dot_general(a, b, (((1,), (1,)), ((), ())), preferred_element_type=F32)


def _dot_tn(a, b):
    return lax.dot_general(a, b, (((0,), (0,)), ((), ())), preferred_element_type=F32)


def _split3(a):
    hi = a.astype(BF16)
    r1 = a - hi.astype(F32)
    mid = r1.astype(BF16)
    lo = (r1 - mid.astype(F32)).astype(BF16)
    return hi, mid, lo


def _dot_sel_lhs(sel, a):
    hi, mid, lo = _split3(a)
    return _dot(sel, hi) + _dot(sel, mid) + _dot(sel, lo)


def _dot_sel_rhs(a, sel, terms=3):
    parts = _split3(a)[:terms]
    out = _dot(parts[0], sel)
    for p in parts[1:]:
        out = out + _dot(p, sel)
    return out


def _dot_hi(a, b):
    a_hi = a.astype(BF16)
    a_lo = (a - a_hi.astype(F32)).astype(BF16)
    b_hi = b.astype(BF16)
    b_lo = (b - b_hi.astype(F32)).astype(BF16)
    return _dot(a_hi, b_hi) + _dot(a_lo, b_hi) + _dot(a_hi, b_lo)


def _dot_hi_nt(a, b):
    a_hi = a.astype(BF16)
    a_lo = (a - a_hi.astype(F32)).astype(BF16)
    b_hi = b.astype(BF16)
    b_lo = (b - b_hi.astype(F32)).astype(BF16)
    return _dot_nt(a_hi, b_hi) + _dot_nt(a_lo, b_hi) + _dot_nt(a_hi, b_lo)


def _softplus(x):
    return jnp.maximum(x, 0.0) + jnp.log1p(jnp.exp(-jnp.abs(x)))


def _silu(x):
    return x * jax.nn.sigmoid(x)


def _iota(shape, dim):
    return lax.broadcasted_iota(jnp.int32, shape, dim)


def _store_row_tiles(ref, value):
    rows = value.shape[0]
    for j in range(ROW_TILE):
        ref[pl.ds(j, rows, stride=ROW_TILE), :] = value[:, j * LANES:(j + 1) * LANES]


def _load_row_tiles(ref, rows):
    return jnp.concatenate([ref[pl.ds(j, rows, stride=ROW_TILE), :] for j in range(ROW_TILE)], axis=1)


def _inproj_kernel(x_ref, nw_ref, w_ref, pm_ref, ps_ref):
    x = x_ref[...]
    ms = jnp.mean(x * x, axis=-1, keepdims=True)
    h = (x * lax.rsqrt(ms + EPS) * nw_ref[...]).astype(BF16)
    step = 512
    for n0 in range(0, N_MAIN, step):
        pm_ref[:, n0:n0 + step] = _dot(h, w_ref[:, n0:n0 + step]).astype(BF16)
    ps_ref[...] = _dot(h, w_ref[:, N_MAIN:N_MAIN + N_SMALL])


def _inproj(x2d, norm_w, w_all):
    t = x2d.shape[0]
    return pl.pallas_call(
        _inproj_kernel,
        grid=(t // TM_PROJ,),
        in_specs=[
            pl.BlockSpec((TM_PROJ, D_MODEL), lambda i: (i, 0)),
            pl.BlockSpec((1, D_MODEL), lambda i: (0, 0)),
            pl.BlockSpec((D_MODEL, N_MAIN + N_SMALL), lambda i: (0, 0)),
        ],
        out_specs=[
            pl.BlockSpec((TM_PROJ, N_MAIN), lambda i: (i, 0)),
            pl.BlockSpec((TM_PROJ, N_SMALL), lambda i: (i, 0)),
        ],
        out_shape=[
            jax.ShapeDtypeStruct((t, N_MAIN), BF16),
            jax.ShapeDtypeStruct((t, N_SMALL), F32),
        ],
        compiler_params=pltpu.CompilerParams(
            dimension_semantics=("arbitrary",), vmem_limit_bytes=VMEM_LIMIT),
        name="inproj",
    )(x2d, norm_w, w_all)


def _mixer_kernel(pm_ref, ps_ref, x_ref, wup_ref, balpha_ref, gnw_ref, convw_ref, convb_ref,
                  dtb_ref, aneg_ref, dexp_ref, snw_ref, wout_ref, nfw_ref, rw_ref, rb_ref,
                  x1_ref, h2_ref, lg_ref,
                  gla_state, ssd_state, conv_tail, mix_scr):
    @pl.when(pl.program_id(1) == 0)
    def _():
        gla_state[...] = jnp.zeros_like(gla_state)
        ssd_state[...] = jnp.zeros_like(ssd_state)
        conv_tail[...] = jnp.zeros_like(conv_tail)

    small = ps_ref[...]

    row = _iota((TL, TL), 0)
    col = _iota((TL, TL), 1)
    causal = col <= row
    cum64 = jnp.where(causal & ((row // GLA_CHUNK) == (col // GLA_CHUNK)), 1.0, 0.0).astype(BF16)
    cum128 = jnp.where(causal & ((row // SSD_CHUNK) == (col // SSD_CHUNK)), 1.0, 0.0).astype(BF16)

    xa = _dot_hi(small, wup_ref[...]) + balpha_ref[...]
    log_a = (jnp.minimum(xa, 0.0) - jnp.log1p(jnp.exp(-jnp.abs(xa)))) * (1.0 / GLA_GATE_NORM)
    bcum = _dot_sel_lhs(cum64, log_a)

    lane_kw = _iota((GLA_CHUNK, GLA_KW), 1)
    head_masks = [(lane_kw // GLA_DK) == h for h in range(GLA_HEADS)]
    lane_kw_s = _iota((GLA_DV, GLA_KW), 1)
    head_masks_s = [(lane_kw_s // GLA_DK) == h for h in range(GLA_HEADS)]
    tril64 = _iota((GLA_CHUNK, GLA_CHUNK), 1) <= _iota((GLA_CHUNK, GLA_CHUNK), 0)
    q_scale = GLA_DK ** -0.5

    for c in range(TL // GLA_CHUNK):
        rs = slice(c * GLA_CHUNK, (c + 1) * GLA_CHUNK)
        bc = bcum[rs]
        b_mid = bc[GLA_CHUNK // 2:GLA_CHUNK // 2 + 1]
        b_last = bc[GLA_CHUNK - 1:GLA_CHUNK]
        qc = pm_ref[rs, 0:GLA_KW].astype(F32) * q_scale
        kc = pm_ref[rs, GLA_KW:2 * GLA_KW].astype(F32)
        vc = pm_ref[rs, 2 * GLA_KW:2 * GLA_KW + GLA_WIDTH]
        q_in = (qc * jnp.exp(bc - b_mid)).astype(BF16)
        k_in = (kc * jnp.exp(b_mid - bc)).astype(BF16)
        q_st = (qc * jnp.exp(bc)).astype(BF16)
        k_st = (kc * jnp.exp(b_last - bc)).astype(BF16)
        st = gla_state[...]
        st_b = st.astype(BF16)
        zero_b = jnp.zeros_like(q_in)
        for h in range(GLA_HEADS):
            scores = _dot_nt(jnp.where(head_masks[h], q_in, zero_b), k_in)
            scores = jnp.where(tril64, scores, 0.0).astype(BF16)
            o_h = _dot(scores, vc[:, h * GLA_DV:(h + 1) * GLA_DV])
            o_h = o_h + _dot_nt(jnp.where(head_masks[h], q_st, zero_b), st_b)
            mix_scr[rs, h * GLA_DV:(h + 1) * GLA_DV] = o_h
        upd = _dot_tn(vc, k_st)
        new_st = st * jnp.exp(b_last)
        for h in range(GLA_HEADS):
            new_st = new_st + jnp.where(head_masks_s[h], upd[h * GLA_DV:(h + 1) * GLA_DV], 0.0)
        gla_state[...] = new_st

    xbc = pm_ref[:, 2048:3072].astype(F32)
    tail = conv_tail[...]
    conv_tail[...] = xbc[TL - 8:TL]
    row8 = _iota((8, SSD_CONV_CH), 0)
    conv = xbc * convw_ref[SSD_CONV - 1:SSD_CONV, :]
    for s in range(1, SSD_CONV):
        shifted = pltpu.roll(xbc, s, 0)
        head = jnp.where(row8 < s, pltpu.roll(tail, s, 0), shifted[0:8])
        shifted = jnp.concatenate([head, shifted[8:]], axis=0)
        conv = conv + shifted * convw_ref[SSD_CONV - 1 - s:SSD_CONV - s, :]
    act = _silu(conv + convb_ref[...])
    xs = act[:, 0:SSD_WIDTH]
    bm = act[:, SSD_WIDTH:SSD_WIDTH + SSD_GROUPS * SSD_STATE].astype(BF16)
    cm = act[:, SSD_WIDTH + SSD_GROUPS * SSD_STATE:].astype(BF16)

    dt_full = _softplus(small + dtb_ref[...])
    a_full = dt_full * aneg_ref[...]
    acum = _dot_sel_lhs(cum128, a_full)
    acum_t = acum.T

    e_row = _iota((N_SMALL, SSD_WIDTH), 0)
    e_col = _iota((N_SMALL, SSD_WIDTH), 1)
    spread64 = jnp.where(e_row == DT_COL + e_col // SSD_HEADDIM, 1.0, 0.0).astype(BF16)
    e_row2 = _iota((N_SMALL, SSD_HEADS * LANES), 0)
    e_col2 = _iota((N_SMALL, SSD_HEADS * LANES), 1)
    spread128 = jnp.where(e_row2 == DT_COL + e_col2 // LANES, 1.0, 0.0).astype(BF16)
    dt_e = _dot_sel_rhs(dt_full, spread64, terms=1)
    ac_e = _dot_sel_rhs(acum, spread64, terms=2)
    ac_w = _dot_sel_rhs(acum, spread128, terms=2)

    tril128 = _iota((SSD_CHUNK, SSD_CHUNK), 1) <= _iota((SSD_CHUNK, SSD_CHUNK), 0)
    lane_g = _iota((SSD_CHUNK, SSD_HPG * SSD_HEADDIM), 1)
    for c in range(TL // SSD_CHUNK):
        rs = slice(c * SSD_CHUNK, (c + 1) * SSD_CHUNK)
        ac_c = ac_e[rs]
        a_last = ac_c[SSD_CHUNK - 1:SSD_CHUNK]
        dt_c = dt_e[rs]
        xs_c = xs[rs]
        x_dt = (xs_c * dt_c).astype(BF16)
        x_w = (xs_c * (jnp.exp(a_last - ac_c) * dt_c)).astype(BF16)
        e_ac = jnp.exp(ac_c)
        for g in range(SSD_GROUPS):
            gs = slice(g * SSD_STATE, (g + 1) * SSD_STATE)
            ws = slice(g * SSD_HPG * SSD_HEADDIM, (g + 1) * SSD_HPG * SSD_HEADDIM)
            c_g = cm[rs, gs]
            b_g = bm[rs, gs]
            cb = _dot_nt(c_g, b_g)
            x_dt_g = x_dt[:, ws]
            lhs_parts = []
            rhs_parts = []
            for hh in range(SSD_HPG):
                h = g * SSD_HPG + hh
                seg = ac_w[rs, h * LANES:(h + 1) * LANES] - acum_t[DT_COL + h:DT_COL + h + 1, rs]
                lmat = jnp.where(tril128, jnp.exp(jnp.where(tril128, seg, 0.0)), 0.0)
                lhs_parts.append((cb * lmat).astype(BF16))
                rhs_parts.append(jnp.where((lane_g // SSD_HEADDIM) == hh, x_dt_g,
                                           jnp.zeros_like(x_dt_g)))
            intra = _dot(jnp.concatenate(lhs_parts, axis=1), jnp.concatenate(rhs_parts, axis=0))
            st = ssd_state[g]
            inter = _dot(c_g, st.astype(BF16)) * e_ac[:, ws]
            mix_scr[rs, GLA_WIDTH + g * 256:GLA_WIDTH + (g + 1) * 256] = intra + inter
            ssd_state[g] = st * jnp.exp(a_last[:, ws]) + _dot_tn(b_g, x_w[:, ws])

    o = mix_scr[:, 0:GLA_WIDTH]
    g_gate = _silu(pm_ref[:, 1024:1536].astype(F32))
    gla_parts = []
    for h in range(GLA_HEADS):
        o_h = o[:, h * GLA_DV:(h + 1) * GLA_DV]
        ms = jnp.mean(o_h * o_h, axis=-1, keepdims=True)
        gla_parts.append(o_h * lax.rsqrt(ms + GROUP_EPS))
    gla_out = jnp.concatenate(gla_parts, axis=1) * gnw_ref[...] * g_gate

    y = mix_scr[:, GLA_WIDTH:] + dexp_ref[...] * xs
    y = y * _silu(pm_ref[:, 1536:2048].astype(F32))
    ssd_parts = []
    for g in range(SSD_GROUPS):
        y_g = y[:, g * 256:(g + 1) * 256]
        ms = jnp.mean(y_g * y_g, axis=-1, keepdims=True)
        ssd_parts.append(y_g * lax.rsqrt(ms + GROUP_EPS))
    ssd_out = jnp.concatenate(ssd_parts, axis=1) * snw_ref[...]

    mixed = jnp.concatenate([gla_out, ssd_out], axis=1).astype(BF16)
    x1 = x_ref[...] + _dot(mixed, wout_ref[...])
    x1_ref[...] = x1

    ms = jnp.mean(x1 * x1, axis=-1, keepdims=True)
    h2 = x1 * lax.rsqrt(ms + EPS) * nfw_ref[...]
    _store_row_tiles(h2_ref, h2)
    lg_ref[...] = _dot_hi_nt(rw_ref[...], h2) + rb_ref[...]


def _mixer(pm, ps, x, wup, balpha, gnw, convw, convb, dtb, aneg, dexp, snw, wout, nfw, rw, rb):
    bsz, seqlen, _ = x.shape

    def full(a):
        return pl.BlockSpec(a.shape, lambda b, l: (0,) * a.ndim)

    def tile(width):
        return pl.BlockSpec((None, TL, width), lambda b, l: (b, l, 0))

    params = (wup, balpha, gnw, convw, convb, dtb, aneg, dexp, snw, wout, nfw, rw, rb)
    return pl.pallas_call(
        _mixer_kernel,
        grid=(bsz, seqlen // TL),
        in_specs=[tile(N_MAIN), tile(N_SMALL), tile(D_MODEL)] + [full(p) for p in params],
        out_specs=[tile(D_MODEL), pl.BlockSpec((None, TL * ROW_TILE, LANES), lambda b, l: (b, l, 0)),
                   pl.BlockSpec((LANES, TL), lambda b, l: (0, b * (seqlen // TL) + l))],
        out_shape=[
            jax.ShapeDtypeStruct((bsz, seqlen, D_MODEL), F32),
            jax.ShapeDtypeStruct((bsz, seqlen * ROW_TILE, LANES), F32),
            jax.ShapeDtypeStruct((LANES, bsz * seqlen), F32),
        ],
        scratch_shapes=[
            pltpu.VMEM((GLA_DV, GLA_KW), F32),
            pltpu.VMEM((SSD_GROUPS, SSD_STATE, SSD_HPG * SSD_HEADDIM), F32),
            pltpu.VMEM((8, SSD_CONV_CH), F32),
            pltpu.VMEM((TL, D_MODEL), F32),
        ],
        compiler_params=pltpu.CompilerParams(
            dimension_semantics=("arbitrary", "arbitrary"), vmem_limit_bytes=VMEM_LIMIT),
        name="mixer",
    )(pm, ps, x, *params)


def _route_kernel(lg_ref, pos_ref, gate_ref, cnt_ref, cnt_scr, run_scr):
    phase = pl.program_id(0)
    step = pl.program_id(1)

    @pl.when((phase == 0) & (step == 0))
    def _():
        cnt_scr[...] = jnp.zeros_like(cnt_scr)
        run_scr[...] = jnp.zeros_like(run_scr)

    lg = lg_ref[0:N_EXPERTS, :]
    row = _iota((N_EXPERTS, TR), 0)
    work = lg
    onehots = []
    vals = []
    for _ in range(TOP_K):
        m = jnp.max(work, axis=0, keepdims=True)
        idx = jnp.min(jnp.where(work == m, row, N_EXPERTS), axis=0, keepdims=True)
        oh = row == idx
        onehots.append(oh)
        vals.append(m)
        work = jnp.where(oh, -jnp.inf, work)
    multi = jnp.where(onehots[0] | onehots[1] | onehots[2] | onehots[3], 1.0, 0.0).astype(BF16)
    tile_cnt = _dot(multi, jnp.ones((TR, LANES), BF16))

    @pl.when(phase == 0)
    def _():
        cnt_scr[...] = cnt_scr[...] + tile_cnt

    @pl.when(phase == 1)
    def _():
        counts = cnt_scr[...]
        padded = jnp.floor((counts + (TM - 1)) * (1.0 / TM)) * TM
        lower = jnp.where(_iota((N_EXPERTS, N_EXPERTS), 1) < _iota((N_EXPERTS, N_EXPERTS), 0),
                          1.0, 0.0).astype(BF16)
        offs = _dot_sel_lhs(lower, padded)
        before = jnp.where(_iota((TR, TR), 0) < _iota((TR, TR), 1), 1.0, 0.0).astype(BF16)
        rank = _dot(multi, before)
        start = offs + run_scr[...]
        run_scr[...] = run_scr[...] + tile_cnt
        base = rank + jnp.concatenate([start] * (TR // LANES), axis=1)
        exps = [jnp.exp(v - vals[0]) for v in vals]
        den = exps[0] + exps[1] + exps[2] + exps[3]
        pos_rows = [jnp.sum(jnp.where(oh, base, 0.0), axis=0, keepdims=True) for oh in onehots]
        pos_ref[...] = jnp.concatenate(pos_rows + [jnp.zeros((8 - TOP_K, TR), F32)], axis=0).astype(jnp.int32)
        gate_rows = [e / den for e in exps]
        gate_ref[...] = jnp.concatenate(gate_rows + [jnp.zeros((LANES - TOP_K, TR), F32)], axis=0).T
        cnt_ref[...] = counts


def _route(logits_t):
    t = logits_t.shape[1]
    return pl.pallas_call(
        _route_kernel,
        grid=(2, t // TR),
        in_specs=[pl.BlockSpec((LANES, TR), lambda p, i: (0, i))],
        out_specs=[
            pl.BlockSpec((8, TR), lambda p, i: (0, i * p)),
            pl.BlockSpec((TR, LANES), lambda p, i: (i * p, 0)),
            pl.BlockSpec((N_EXPERTS, LANES), lambda p, i: (0, 0)),
        ],
        out_shape=[
            jax.ShapeDtypeStruct((8, t), jnp.int32),
            jax.ShapeDtypeStruct((t, LANES), F32),
            jax.ShapeDtypeStruct((N_EXPERTS, LANES), F32),
        ],
        scratch_shapes=[pltpu.VMEM((N_EXPERTS, LANES), F32), pltpu.VMEM((N_EXPERTS, LANES), F32)],
        compiler_params=pltpu.CompilerParams(
            dimension_semantics=("arbitrary", "arbitrary"), vmem_limit_bytes=VMEM_LIMIT),
        name="route",
    )(logits_t)


def _dump_rows(n_slots, n_tokens):
    slot = jnp.arange(n_slots, dtype=jnp.int32)
    return TOP_K * n_tokens + ((slot // TM) % 2) * TM + slot % TM


def _codes_kernel(pos_ref, dump_ref, codes_ref, sem):
    fill = pltpu.make_async_copy(dump_ref, codes_ref, sem)
    fill.start()
    fill.wait()

    def body(q, carry):
        codes_ref[pos_ref[q]] = q
        return carry

    lax.fori_loop(0, pos_ref.shape[0], body, 0, unroll=16)


def _codes(pos_flat, dump_rows):
    return pl.pallas_call(
        _codes_kernel,
        in_specs=[
            pl.BlockSpec(memory_space=pltpu.SMEM),
            pl.BlockSpec(memory_space=pl.ANY),
        ],
        out_specs=pl.BlockSpec(memory_space=pltpu.SMEM),
        out_shape=jax.ShapeDtypeStruct(dump_rows.shape, jnp.int32),
        scratch_shapes=[pltpu.SemaphoreType.DMA(())],
        name="codes",
    )(pos_flat, dump_rows)


def _experts_kernel(te_ref, nv_ref, codes_ref, h_ref, wg_ref, bg_ref, wu_ref, bu_ref, wd_ref, bd_ref,
                    y_ref, xbuf, obuf, xbf, act, wg_b, wu_b, wd_b, gsem, ssem, *, n_tokens):
    i = pl.program_id(0)
    n_valid = nv_ref[0]
    slot = i & 1
    other = 1 - slot

    def row_tile(row):
        return pl.ds(pl.multiple_of(row * ROW_TILE, ROW_TILE), ROW_TILE)

    def gather_row(tile, r, s):
        src = codes_ref[tile * TM + r] & (n_tokens - 1)
        return pltpu.make_async_copy(h_ref.at[row_tile(src), :], xbuf.at[s, row_tile(r), :], gsem.at[s])

    def scatter_row(dst, r, s):
        return pltpu.make_async_copy(obuf.at[s, row_tile(r), :], y_ref.at[row_tile(dst), :], ssem.at[s])

    def tile_gather_done(s):
        return pltpu.make_async_copy(h_ref.at[pl.ds(0, TM * ROW_TILE), :], xbuf.at[s], gsem.at[s])

    def tile_scatter_done(s):
        return pltpu.make_async_copy(obuf.at[s], y_ref.at[pl.ds(0, TM * ROW_TILE), :], ssem.at[s])

    @pl.when(i < n_valid)
    def _():
        @pl.when(i == 0)
        def _():
            obuf[...] = jnp.zeros_like(obuf)

            def first(r, carry):
                gather_row(0, r, 0).start()
                scatter_row(TOP_K * n_tokens + r, r, 1).start()
                return carry
            lax.fori_loop(0, TM, first, 0)
            tile_scatter_done(1).wait()

        prev = jnp.maximum(i - 1, 0)

        @pl.when((i == 0) | (te_ref[i] != te_ref[prev]))
        def _():
            for c in range(N_CHUNKS):
                cs = slice(c * N_COLS, (c + 1) * N_COLS)
                wg_b[c] = wg_ref[:, cs].astype(BF16)
                wu_b[c] = wu_ref[:, cs].astype(BF16)
                wd_b[c] = wd_ref[:, cs].astype(BF16)

        tile_gather_done(slot).wait()

        @pl.when(i >= 1)
        def _():
            tile_scatter_done(slot).wait()

        nxt = jnp.minimum(i + 1, n_valid - 1)
        is_first = i == 0

        def issue(g, carry):
            for u in range(BF16_ROWS):
                r = g * BF16_ROWS + u
                gather_row(nxt, r, other).start(priority=0)
                dst = jnp.where(is_first, TOP_K * n_tokens + TM + r, codes_ref[prev * TM + r])
                scatter_row(dst, r, other).start(priority=1)
            first_row = pl.multiple_of(g * BF16_ROWS, BF16_ROWS)
            rows = jnp.concatenate(
                [xbuf[slot, pl.ds(first_row * ROW_TILE + j, BF16_ROWS, stride=ROW_TILE), :]
                 for j in range(ROW_TILE)], axis=1)
            xbf[pl.ds(first_row, BF16_ROWS), :] = rows.astype(BF16)
            return carry

        lax.fori_loop(0, TM // BF16_ROWS, issue, 0)

        xb = xbf[...]
        for n in range(N_CHUNKS):
            gate = _dot(xb, wg_b[n]) + bg_ref[n:n + 1, :]
            up = _dot(xb, wu_b[n]) + bu_ref[n:n + 1, :]
            gate = jnp.minimum(gate, SWIGLU_LIMIT)
            up = jnp.clip(up, -SWIGLU_LIMIT, SWIGLU_LIMIT)
            act[n] = ((up + 1.0) * (gate * jax.nn.sigmoid(SWIGLU_ALPHA * gate))).astype(BF16)
        a = jnp.concatenate([act[c] for c in range(N_CHUNKS)], axis=1)
        for n in range(N_CHUNKS):
            out = _dot(a, wd_b[n]) + bd_ref[n:n + 1, :]
            for j in range(N_COLS // LANES):
                obuf[slot, pl.ds(n * (N_COLS // LANES) + j, TM, stride=ROW_TILE), :] = (
                    out[:, j * LANES:(j + 1) * LANES])

        @pl.when(i == n_valid - 1)
        def _():
            tile_gather_done(other).wait()
            tile_scatter_done(other).wait()

            def last(r, carry):
                scatter_row(codes_ref[i * TM + r], r, slot).start()
                return carry
            lax.fori_loop(0, TM, last, 0)
            tile_scatter_done(slot).wait()


def _experts(tile_expert, n_valid, codes, h2, wg, bg, wu, bu, wd, bd):
    t = h2.shape[0] // ROW_TILE
    n_tiles = codes.shape[0] // TM

    def w_map(i, te, nv, cd):
        return (te[i], 0, 0)

    w_spec = pl.BlockSpec((None, D_MODEL, D_MODEL), w_map)
    b_spec = pl.BlockSpec((None, N_CHUNKS, N_COLS), w_map)
    return pl.pallas_call(
        functools.partial(_experts_kernel, n_tokens=t),
        grid_spec=pltpu.PrefetchScalarGridSpec(
            num_scalar_prefetch=3,
            grid=(n_tiles,),
            in_specs=[pl.BlockSpec(memory_space=pl.ANY), w_spec, b_spec, w_spec, b_spec, w_spec, b_spec],
            out_specs=pl.BlockSpec(memory_space=pl.ANY),
            scratch_shapes=[
                pltpu.VMEM((2, TM * ROW_TILE, LANES), F32),
                pltpu.VMEM((2, TM * ROW_TILE, LANES), F32),
                pltpu.VMEM((TM, D_MODEL), BF16),
                pltpu.VMEM((N_CHUNKS, TM, N_COLS), BF16),
                pltpu.VMEM((N_CHUNKS, D_MODEL, N_COLS), BF16),
                pltpu.VMEM((N_CHUNKS, D_MODEL, N_COLS), BF16),
                pltpu.VMEM((N_CHUNKS, D_MODEL, N_COLS), BF16),
                pltpu.SemaphoreType.DMA((2,)),
                pltpu.SemaphoreType.DMA((2,)),
            ],
        ),
        out_shape=jax.ShapeDtypeStruct(((TOP_K * t + 2 * TM) * ROW_TILE, LANES), F32),
        compiler_params=pltpu.CompilerParams(
            dimension_semantics=("arbitrary",), vmem_limit_bytes=VMEM_LIMIT),
        name="experts",
    )(tile_expert, n_valid, codes, h2, wg, bg, wu, bu, wd, bd)


def _combine_kernel(gate_ref, x1_ref, fw_ref, y0_ref, y1_ref, y2_ref, y3_ref, out_ref):
    gates = gate_ref[...]
    y = x1_ref[...]
    for k, y_ref in enumerate((y0_ref, y1_ref, y2_ref, y3_ref)):
        y = y + gates[:, k:k + 1] * _load_row_tiles(y_ref, TT)
    ms = jnp.mean(y * y, axis=-1, keepdims=True)
    out_ref[...] = y * lax.rsqrt(ms + EPS) * fw_ref[...]


def _combine(gates, x1, final_w, y4):
    t = x1.shape[0]
    steps = t // TT

    def y_spec(k):
        return pl.BlockSpec((TT * ROW_TILE, LANES), lambda i: (k * steps + i, 0))

    return pl.pallas_call(
        _combine_kernel,
        grid=(steps,),
        in_specs=[
            pl.BlockSpec((TT, LANES), lambda i: (i, 0)),
            pl.BlockSpec((TT, D_MODEL), lambda i: (i, 0)),
            pl.BlockSpec((1, D_MODEL), lambda i: (0, 0)),
            y_spec(0), y_spec(1), y_spec(2), y_spec(3),
        ],
        out_specs=pl.BlockSpec((TT, D_MODEL), lambda i: (i, 0)),
        out_shape=jax.ShapeDtypeStruct((t, D_MODEL), F32),
        compiler_params=pltpu.CompilerParams(
            dimension_semantics=("arbitrary",), vmem_limit_bytes=VMEM_LIMIT),
        name="combine",
    )(gates, x1, final_w, y4, y4, y4, y4)


def _pad_lanes(v, offset, fill=0.0):
    row = jnp.full((1, LANES), fill, F32)
    return row.at[0, offset:offset + v.shape[0]].set(v.astype(F32))


def kernel(x, norm_mix_w, w_in, gla_w_alpha_up, gla_b_alpha, gla_norm_w, ssd_conv_w, ssd_conv_b,
           ssd_dt_bias, ssd_A_log, ssd_D, ssd_norm_w, w_out, norm_ffn_w, router_w, router_b,
           moe_w_gate, moe_b_gate, moe_w_up, moe_b_up, moe_w_down, moe_b_down, final_norm_w):
    bsz, seqlen, d = x.shape
    t = bsz * seqlen
    depth = w_in.shape[0]
    assert depth == 1, "the final RMSNorm is fused into the (single) layer's combine step"
    assert t & (t - 1) == 0, "slot codes pack (choice, token) with a power-of-two token count"
    p_rows = t * TOP_K + N_EXPERTS * TM
    n_tiles = p_rows // TM
    for l in range(depth):
        w = w_in[l]
        w_all = jnp.concatenate(
            [w[:, 0:1536], w[:, 1552:3088], w[:, 1536:1552], w[:, 3088:3096],
             jnp.zeros((d, N_SMALL - GLA_GATE_RANK - SSD_HEADS), w.dtype)], axis=1).astype(BF16)
        wup = jnp.zeros((N_SMALL, GLA_KW), F32).at[0:GLA_GATE_RANK].set(gla_w_alpha_up[l])
        dtb = _pad_lanes(ssd_dt_bias[l], DT_COL)
        aneg = _pad_lanes(-jnp.exp(ssd_A_log[l].astype(F32)), DT_COL)
        dexp = jnp.repeat(ssd_D[l].astype(F32), SSD_HEADDIM)[None, :]
        rw = jnp.zeros((LANES, d), F32).at[0:N_EXPERTS].set(router_w[l].T)
        rb = jnp.zeros((LANES, TL), F32).at[0:N_EXPERTS].set(
            jnp.broadcast_to(router_b[l].astype(F32)[:, None], (N_EXPERTS, TL)))

        pm, ps = _inproj(x.reshape(t, d), norm_mix_w[l][None, :], w_all)
        x1, h2, logits = _mixer(
            pm.reshape(bsz, seqlen, N_MAIN), ps.reshape(bsz, seqlen, N_SMALL), x,
            wup, gla_b_alpha[l][None, :], gla_norm_w[l][None, :], ssd_conv_w[l],
            ssd_conv_b[l][None, :], dtb, aneg, dexp, ssd_norm_w[l][None, :],
            w_out[l].astype(BF16), norm_ffn_w[l][None, :], rw, rb)

        pos, gates, counts = _route(logits)

        cnt = counts[:, 0].astype(jnp.int32)
        ends = jnp.cumsum(((cnt + TM - 1) // TM) * TM)
        n_valid = (ends[-1] // TM).astype(jnp.int32)
        starts = jnp.arange(n_tiles, dtype=jnp.int32) * TM
        tile_expert = jnp.sum(starts[:, None] >= ends[None, :], axis=1).astype(jnp.int32)
        last_expert = tile_expert[jnp.maximum(n_valid - 1, 0)]
        tile_expert = jnp.where(starts < ends[-1], tile_expert, last_expert)

        codes = _codes(pos[0:TOP_K].reshape(TOP_K * t), _dump_rows(p_rows, t))
        y4 = _experts(tile_expert, n_valid.reshape(1), codes, h2.reshape(t * ROW_TILE, LANES),
                      moe_w_gate[l], moe_b_gate[l].reshape(N_EXPERTS, N_CHUNKS, N_COLS),
                      moe_w_up[l], moe_b_up[l].reshape(N_EXPERTS, N_CHUNKS, N_COLS),
                      moe_w_down[l], moe_b_down[l].reshape(N_EXPERTS, N_CHUNKS, N_COLS))
        x = _combine(gates, x1.reshape(t, d), final_norm_w[None, :], y4).reshape(bsz, seqlen, d)
    return x
```

```python
import functools

import jax
import jax.numpy as jnp
from jax import lax
from jax.experimental import pallas as pl
from jax.experimental.pallas import tpu as pltpu

F32 = jnp.float32
BF16 = jnp.bfloat16

D_MODEL = 1024
GLA_WIDTH = 512
GLA_HEADS = 4
GLA_DV = 128
GLA_DK = 64
GLA_KW = 256
GLA_GATE_RANK = 16
GLA_GATE_NORM = 16.0
SSD_WIDTH = 512
SSD_HEADDIM = 64
SSD_HEADS = 8
SSD_GROUPS = 2
SSD_HPG = 4
SSD_STATE = 128
SSD_CONV = 4
SSD_CONV_CH = 1024
N_EXPERTS = 32
TOP_K = 4
SWIGLU_LIMIT = 7.0
SWIGLU_ALPHA = 1.702
EPS = 1e-6
GROUP_EPS = 1e-5

LANES = 128
ROW_TILE = D_MODEL // LANES
N_MAIN = 3072
N_SMALL = LANES
DT_COL = GLA_GATE_RANK

GLA_CHUNK = 64
SSD_CHUNK = 128
TM_PROJ = 512
TL = 256
TT_LOG2 = 8
TT = 1 << TT_LOG2
STAGE_ROWS = 4 * TT
TM_LOG2 = 8
TM = 1 << TM_LOG2
N_COLS = 256
N_CHUNKS = D_MODEL // N_COLS
VMEM_LIMIT = 56 * 1024 * 1024


def _dot(a, b):
    return jnp.dot(a, b, preferred_element_type=F32)


def _dot_nt(a, b):
    return lax.dot_general(a, b, (((1,), (1,)), ((), ())), preferred_element_type=F32)


def _dot_tn(a, b):
    return lax.dot_general(a, b, (((0,), (0,)), ((), ())), preferred_element_type=F32)


def _split3(a):
    hi = a.astype(BF16)
    r1 = a - hi.astype(F32)
    mid = r1.astype(BF16)
    lo = (r1 - mid.astype(F32)).astype(BF16)
    return hi, mid, lo


def _dot_sel_lhs(sel, a):
    hi, mid, lo = _split3(a)
    return _dot(sel, hi) + _dot(sel, mid) + _dot(sel, lo)


def _dot_sel_rhs(a, sel, terms=3):
    parts = _split3(a)[:terms]
    out = _dot(parts[0], sel)
    for p in parts[1:]:
        out = out + _dot(p, sel)
    return out


def _dot_hi(a, b):
    a_hi = a.astype(BF16)
    a_lo = (a - a_hi.astype(F32)).astype(BF16)
    b_hi = b.astype(BF16)
    b_lo = (b - b_hi.astype(F32)).astype(BF16)
    return _dot(a_hi, b_hi) + _dot(a_lo, b_hi) + _dot(a_hi, b_lo)


def _dot_hi_nt(a, b):
    a_hi = a.astype(BF16)
    a_lo = (a - a_hi.astype(F32)).astype(BF16)
    b_hi = b.astype(BF16)
    b_lo = (b - b_hi.astype(F32)).astype(BF16)
    return _dot_nt(a_hi, b_hi) + _dot_nt(a_lo, b_hi) + _dot_nt(a_hi, b_lo)


def _softplus(x):
    return jnp.maximum(x, 0.0) + jnp.log1p(jnp.exp(-jnp.abs(x)))


def _silu(x):
    return x * jax.nn.sigmoid(x)


def _iota(shape, dim):
    return lax.broadcasted_iota(jnp.int32, shape, dim)


def _load_row_tiles(ref, rows):
    return jnp.concatenate([ref[pl.ds(j, rows, stride=ROW_TILE), :] for j in range(ROW_TILE)], axis=1)


def _inproj_kernel(x_ref, nw_ref, w_ref, pm_ref, ps_ref):
    x = x_ref[...]
    ms = jnp.mean(x * x, axis=-1, keepdims=True)
    h = (x * lax.rsqrt(ms + EPS) * nw_ref[...]).astype(BF16)
    step = 512
    for n0 in range(0, N_MAIN, step):
        pm_ref[:, n0:n0 + step] = _dot(h, w_ref[:, n0:n0 + step]).astype(BF16)
    ps_ref[...] = _dot(h, w_ref[:, N_MAIN:N_MAIN + N_SMALL])


def _inproj(x2d, norm_w, w_all):
    t = x2d.shape[0]
    return pl.pallas_call(
        _inproj_kernel,
        grid=(t // TM_PROJ,),
        in_specs=[
            pl.BlockSpec((TM_PROJ, D_MODEL), lambda i: (i, 0)),
            pl.BlockSpec((1, D_MODEL), lambda i: (0, 0)),
            pl.BlockSpec((D_MODEL, N_MAIN + N_SMALL), lambda i: (0, 0)),
        ],
        out_specs=[
            pl.BlockSpec((TM_PROJ, N_MAIN), lambda i: (i, 0)),
            pl.BlockSpec((TM_PROJ, N_SMALL), lambda i: (i, 0)),
        ],
        out_shape=[
            jax.ShapeDtypeStruct((t, N_MAIN), BF16),
            jax.ShapeDtypeStruct((t, N_SMALL), F32),
        ],
        compiler_params=pltpu.CompilerParams(
            dimension_semantics=("arbitrary",), vmem_limit_bytes=VMEM_LIMIT),
        name="inproj",
    )(x2d, norm_w, w_all)


def _mixer_kernel(pm_ref, ps_ref, x_ref, wup_ref, balpha_ref, gnw_ref, convw_ref, convb_ref,
                  dtb_ref, aneg_ref, dexp_ref, snw_ref, wout_ref, nfw_ref, rw_ref, rb_ref,
                  x1_ref, h2_ref, lg_ref,
                  gla_state, ssd_state, conv_tail, mix_scr):
    @pl.when(pl.program_id(1) == 0)
    def _():
        gla_state[...] = jnp.zeros_like(gla_state)
        ssd_state[...] = jnp.zeros_like(ssd_state)
        conv_tail[...] = jnp.zeros_like(conv_tail)

    small = ps_ref[...]

    row = _iota((TL, TL), 0)
    col = _iota((TL, TL), 1)
    causal = col <= row
    cum64 = jnp.where(causal & ((row // GLA_CHUNK) == (col // GLA_CHUNK)), 1.0, 0.0).astype(BF16)
    cum128 = jnp.where(causal & ((row // SSD_CHUNK) == (col // SSD_CHUNK)), 1.0, 0.0).astype(BF16)

    xa = _dot_hi(small, wup_ref[...]) + balpha_ref[...]
    log_a = (jnp.minimum(xa, 0.0) - jnp.log1p(jnp.exp(-jnp.abs(xa)))) * (1.0 / GLA_GATE_NORM)
    bcum = _dot_sel_lhs(cum64, log_a)

    lane_kw = _iota((GLA_CHUNK, GLA_KW), 1)
    head_masks = [(lane_kw // GLA_DK) == h for h in range(GLA_HEADS)]
    lane_kw_s = _iota((GLA_DV, GLA_KW), 1)
    head_masks_s = [(lane_kw_s // GLA_DK) == h for h in range(GLA_HEADS)]
    tril64 = _iota((GLA_CHUNK, GLA_CHUNK), 1) <= _iota((GLA_CHUNK, GLA_CHUNK), 0)
    q_scale = GLA_DK ** -0.5

    for c in range(TL // GLA_CHUNK):
        rs = slice(c * GLA_CHUNK, (c + 1) * GLA_CHUNK)
        bc = bcum[rs]
        b_mid = bc[GLA_CHUNK // 2:GLA_CHUNK // 2 + 1]
        b_last = bc[GLA_CHUNK - 1:GLA_CHUNK]
        qc = pm_ref[rs, 0:GLA_KW].astype(F32) * q_scale
        kc = pm_ref[rs, GLA_KW:2 * GLA_KW].astype(F32)
        vc = pm_ref[rs, 2 * GLA_KW:2 * GLA_KW + GLA_WIDTH]
        q_in = (qc * jnp.exp(bc - b_mid)).astype(BF16)
        k_in = (kc * jnp.exp(b_mid - bc)).astype(BF16)
        q_st = (qc * jnp.exp(bc)).astype(BF16)
        k_st = (kc * jnp.exp(b_last - bc)).astype(BF16)
        st = gla_state[...]
        st_b = st.astype(BF16)
        zero_b = jnp.zeros_like(q_in)
        for h in range(GLA_HEADS):
            scores = _dot_nt(jnp.where(head_masks[h], q_in, zero_b), k_in)
            scores = jnp.where(tril64, scores, 0.0).astype(BF16)
            o_h = _dot(scores, vc[:, h * GLA_DV:(h + 1) * GLA_DV])
            o_h = o_h + _dot_nt(jnp.where(head_masks[h], q_st, zero_b), st_b)
            mix_scr[rs, h * GLA_DV:(h + 1) * GLA_DV] = o_h
        upd = _dot_tn(vc, k_st)
        new_st = st * jnp.exp(b_last)
        for h in range(GLA_HEADS):
            new_st = new_st + jnp.where(head_masks_s[h], upd[h * GLA_DV:(h + 1) * GLA_DV], 0.0)
        gla_state[...] = new_st

    xbc = pm_ref[:, 2048:3072].astype(F32)
    tail = conv_tail[...]
    conv_tail[...] = xbc[TL - 8:TL]
    row8 = _iota((8, SSD_CONV_CH), 0)
    conv = xbc * convw_ref[SSD_CONV - 1:SSD_CONV, :]
    for s in range(1, SSD_CONV):
        shifted = pltpu.roll(xbc, s, 0)
        head = jnp.where(row8 < s, pltpu.roll(tail, s, 0), shifted[0:8])
        shifted = jnp.concatenate([head, shifted[8:]], axis=0)
        conv = conv + shifted * convw_ref[SSD_CONV - 1 - s:SSD_CONV - s, :]
    act = _silu(conv + convb_ref[...])
    xs = act[:, 0:SSD_WIDTH]
    bm = act[:, SSD_WIDTH:SSD_WIDTH + SSD_GROUPS * SSD_STATE].astype(BF16)
    cm = act[:, SSD_WIDTH + SSD_GROUPS * SSD_STATE:].astype(BF16)

    dt_full = _softplus(small + dtb_ref[...])
    a_full = dt_full * aneg_ref[...]
    acum = _dot_sel_lhs(cum128, a_full)
    acum_t = acum.T

    e_row = _iota((N_SMALL, SSD_WIDTH), 0)
    e_col = _iota((N_SMALL, SSD_WIDTH), 1)
    spread64 = jnp.where(e_row == DT_COL + e_col // SSD_HEADDIM, 1.0, 0.0).astype(BF16)
    e_row2 = _iota((N_SMALL, SSD_HEADS * LANES), 0)
    e_col2 = _iota((N_SMALL, SSD_HEADS * LANES), 1)
    spread128 = jnp.where(e_row2 == DT_COL + e_col2 // LANES, 1.0, 0.0).astype(BF16)
    dt_e = _dot_sel_rhs(dt_full, spread64, terms=1)
    ac_e = _dot_sel_rhs(acum, spread64, terms=2)
    ac_w = _dot_sel_rhs(acum, spread128, terms=2)

    tril128 = _iota((SSD_CHUNK, SSD_CHUNK), 1) <= _iota((SSD_CHUNK, SSD_CHUNK), 0)
    lane_g = _iota((SSD_CHUNK, SSD_HPG * SSD_HEADDIM), 1)
    for c in range(TL // SSD_CHUNK):
        rs = slice(c * SSD_CHUNK, (c + 1) * SSD_CHUNK)
        ac_c = ac_e[rs]
        a_last = ac_c[SSD_CHUNK - 1:SSD_CHUNK]
        dt_c = dt_e[rs]
        xs_c = xs[rs]
        x_dt = (xs_c * dt_c).astype(BF16)
        x_w = (xs_c * (jnp.exp(a_last - ac_c) * dt_c)).astype(BF16)
        e_ac = jnp.exp(ac_c)
        for g in range(SSD_GROUPS):
            gs = slice(g * SSD_STATE, (g + 1) * SSD_STATE)
            ws = slice(g * SSD_HPG * SSD_HEADDIM, (g + 1) * SSD_HPG * SSD_HEADDIM)
            c_g = cm[rs, gs]
            b_g = bm[rs, gs]
            cb = _dot_nt(c_g, b_g)
            x_dt_g = x_dt[:, ws]
            lhs_parts = []
            rhs_parts = []
            for hh in range(SSD_HPG):
                h = g * SSD_HPG + hh
                seg = ac_w[rs, h * LANES:(h + 1) * LANES] - acum_t[DT_COL + h:DT_COL + h + 1, rs]
                lmat = jnp.where(tril128, jnp.exp(jnp.where(tril128, seg, 0.0)), 0.0)
                lhs_parts.append((cb * lmat).astype(BF16))
                rhs_parts.append(jnp.where((lane_g // SSD_HEADDIM) == hh, x_dt_g,
                                           jnp.zeros_like(x_dt_g)))
            intra = _dot(jnp.concatenate(lhs_parts, axis=1), jnp.concatenate(rhs_parts, axis=0))
            st = ssd_state[g]
            inter = _dot(c_g, st.astype(BF16)) * e_ac[:, ws]
            mix_scr[rs, GLA_WIDTH + g * 256:GLA_WIDTH + (g + 1) * 256] = intra + inter
            ssd_state[g] = st * jnp.exp(a_last[:, ws]) + _dot_tn(b_g, x_w[:, ws])

    o = mix_scr[:, 0:GLA_WIDTH]
    g_gate = _silu(pm_ref[:, 1024:1536].astype(F32))
    gla_parts = []
    for h in range(GLA_HEADS):
        o_h = o[:, h * GLA_DV:(h + 1) * GLA_DV]
        ms = jnp.mean(o_h * o_h, axis=-1, keepdims=True)
        gla_parts.append(o_h * lax.rsqrt(ms + GROUP_EPS))
    gla_out = jnp.concatenate(gla_parts, axis=1) * gnw_ref[...] * g_gate

    y = mix_scr[:, GLA_WIDTH:] + dexp_ref[...] * xs
    y = y * _silu(pm_ref[:, 1536:2048].astype(F32))
    ssd_parts = []
    for g in range(SSD_GROUPS):
        y_g = y[:, g * 256:(g + 1) * 256]
        ms = jnp.mean(y_g * y_g, axis=-1, keepdims=True)
        ssd_parts.append(y_g * lax.rsqrt(ms + GROUP_EPS))
    ssd_out = jnp.concatenate(ssd_parts, axis=1) * snw_ref[...]

    mixed = jnp.concatenate([gla_out, ssd_out], axis=1).astype(BF16)
    x1 = x_ref[...] + _dot(mixed, wout_ref[...])
    x1_ref[...] = x1

    ms = jnp.mean(x1 * x1, axis=-1, keepdims=True)
    h2 = x1 * lax.rsqrt(ms + EPS) * nfw_ref[...]
    h2_ref[...] = h2.astype(BF16)
    lg_ref[...] = _dot_hi_nt(rw_ref[...], h2) + rb_ref[...]


def _mixer(pm, ps, x, wup, balpha, gnw, convw, convb, dtb, aneg, dexp, snw, wout, nfw, rw, rb):
    bsz, seqlen, _ = x.shape

    def full(a):
        return pl.BlockSpec(a.shape, lambda b, l: (0,) * a.ndim)

    def tile(width):
        return pl.BlockSpec((None, TL, width), lambda b, l: (b, l, 0))

    params = (wup, balpha, gnw, convw, convb, dtb, aneg, dexp, snw, wout, nfw, rw, rb)
    return pl.pallas_call(
        _mixer_kernel,
        grid=(bsz, seqlen // TL),
        in_specs=[tile(N_MAIN), tile(N_SMALL), tile(D_MODEL)] + [full(p) for p in params],
        out_specs=[tile(D_MODEL), tile(D_MODEL),
                   pl.BlockSpec((LANES, TL), lambda b, l: (0, b * (seqlen // TL) + l))],
        out_shape=[
            jax.ShapeDtypeStruct((bsz, seqlen, D_MODEL), F32),
            jax.ShapeDtypeStruct((bsz, seqlen, D_MODEL), BF16),
            jax.ShapeDtypeStruct((LANES, bsz * seqlen), F32),
        ],
        scratch_shapes=[
            pltpu.VMEM((GLA_DV, GLA_KW), F32),
            pltpu.VMEM((SSD_GROUPS, SSD_STATE, SSD_HPG * SSD_HEADDIM), F32),
            pltpu.VMEM((8, SSD_CONV_CH), F32),
            pltpu.VMEM((TL, D_MODEL), F32),
        ],
        compiler_params=pltpu.CompilerParams(
            dimension_semantics=("arbitrary", "arbitrary"), vmem_limit_bytes=VMEM_LIMIT),
        name="mixer",
    )(pm, ps, x, *params)


def _route_kernel(lg_ref, info_ref, lp_ref, start_ref, len_ref, loff_ref, cnt_ref, cnt_scr, run_scr):
    phase = pl.program_id(0)
    step = pl.program_id(1)

    @pl.when((phase == 0) & (step == 0))
    def _():
        cnt_scr[...] = jnp.zeros_like(cnt_scr)
        run_scr[...] = jnp.zeros_like(run_scr)

    lg = lg_ref[0:N_EXPERTS, :]
    row = _iota((N_EXPERTS, TT), 0)
    work = lg
    onehots = []
    vals = []
    for _ in range(TOP_K):
        m = jnp.max(work, axis=0, keepdims=True)
        idx = jnp.min(jnp.where(work == m, row, N_EXPERTS), axis=0, keepdims=True)
        oh = row == idx
        onehots.append(oh)
        vals.append(m)
        work = jnp.where(oh, -jnp.inf, work)
    multi = jnp.where(onehots[0] | onehots[1] | onehots[2] | onehots[3], 1.0, 0.0).astype(BF16)
    tile_cnt = _dot(multi, jnp.ones((TT, LANES), BF16))

    @pl.when(phase == 0)
    def _():
        cnt_scr[...] = cnt_scr[...] + tile_cnt

    @pl.when(phase == 1)
    def _():
        counts = cnt_scr[...]
        padded = jnp.floor((counts + (TM - 1)) * (1.0 / TM)) * TM
        lower = jnp.where(_iota((N_EXPERTS, N_EXPERTS), 1) < _iota((N_EXPERTS, N_EXPERTS), 0),
                          1.0, 0.0).astype(BF16)
        offs = _dot_sel_lhs(lower, padded)
        loff = _dot_sel_lhs(lower, tile_cnt)
        before = jnp.where(_iota((TT, TT), 0) < _iota((TT, TT), 1), 1.0, 0.0).astype(BF16)
        rank = _dot(multi, before)
        start_ref[...] = (offs + run_scr[...]).astype(jnp.int32)
        len_ref[...] = tile_cnt.astype(jnp.int32)
        loff_ref[...] = loff.astype(jnp.int32)
        run_scr[...] = run_scr[...] + tile_cnt
        local = rank + jnp.concatenate([loff] * (TT // LANES), axis=1)
        exps = [jnp.exp(v - vals[0]) for v in vals]
        den = exps[0] + exps[1] + exps[2] + exps[3]
        lp_rows = [jnp.sum(jnp.where(oh, local, 0.0), axis=0, keepdims=True) for oh in onehots]
        lp_ref[...] = jnp.concatenate(lp_rows + [jnp.zeros((8 - TOP_K, TT), F32)], axis=0).astype(jnp.int32)
        gate_rows = [e / den for e in exps]
        info = jnp.concatenate(gate_rows + lp_rows + [jnp.zeros((LANES - 2 * TOP_K, TT), F32)], axis=0)
        info_ref[...] = info.T
        cnt_ref[...] = counts


def _route(logits_t):
    t = logits_t.shape[1]
    steps = t // TT
    run_spec = pl.BlockSpec((N_EXPERTS, LANES), lambda p, i: (i * p, 0))
    run_shape = jax.ShapeDtypeStruct((steps * N_EXPERTS, LANES), jnp.int32)
    return pl.pallas_call(
        _route_kernel,
        grid=(2, steps),
        in_specs=[pl.BlockSpec((LANES, TT), lambda p, i: (0, i))],
        out_specs=[
            pl.BlockSpec((TT, LANES), lambda p, i: (i * p, 0)),
            pl.BlockSpec((8, TT), lambda p, i: (0, i * p)),
            run_spec, run_spec, run_spec,
            pl.BlockSpec((N_EXPERTS, LANES), lambda p, i: (0, 0)),
        ],
        out_shape=[
            jax.ShapeDtypeStruct((t, LANES), F32),
            jax.ShapeDtypeStruct((8, t), jnp.int32),
            run_shape, run_shape, run_shape,
            jax.ShapeDtypeStruct((N_EXPERTS, LANES), F32),
        ],
        scratch_shapes=[pltpu.VMEM((N_EXPERTS, LANES), F32), pltpu.VMEM((N_EXPERTS, LANES), F32)],
        compiler_params=pltpu.CompilerParams(
            dimension_semantics=("arbitrary", "arbitrary"), vmem_limit_bytes=VMEM_LIMIT),
        name="route",
    )(logits_t)


def _rows(first, count):
    return pl.ds(pl.multiple_of(first * ROW_TILE, ROW_TILE), count * ROW_TILE)


def _for_each_piece(length, max_log2, fn):
    for b in reversed(range(max_log2 + 1)):
        size = 1 << b
        offset = lax.shift_left(lax.shift_right_logical(length, b + 1), b + 1)

        @pl.when((length & size) != 0)
        def _(offset=offset, size=size):
            fn(offset, size)


def _dispatch_kernel(start_ref, len_ref, loff_ref, pstart_ref, plen_ref,
                     lp_ref, h_ref, xs_ref, stage, zeros, sem, zsem):
    j = pl.program_id(0)
    slot = j & 1

    def tile_done(s):
        return pltpu.make_async_copy(stage.at[s], xs_ref.at[pl.ds(0, STAGE_ROWS * ROW_TILE), :], sem.at[s])

    @pl.when(j == 0)
    def _():
        zeros[...] = jnp.zeros_like(zeros)
        for e in range(N_EXPERTS):
            def put(offset, size, e=e):
                cp = pltpu.make_async_copy(zeros.at[pl.ds(0, size * ROW_TILE), :],
                                           xs_ref.at[_rows(pstart_ref[e] + offset, size), :], zsem)
                cp.start()
                cp.wait()
            _for_each_piece(plen_ref[e], TM_LOG2 - 1, put)

        def put_block(blk, carry):
            cp = pltpu.make_async_copy(
                zeros, xs_ref.at[_rows(pstart_ref[N_EXPERTS] + blk * (TM // 2), TM // 2), :], zsem)
            cp.start()
            cp.wait()
            return carry
        lax.fori_loop(0, plen_ref[N_EXPERTS], put_block, 0)

    @pl.when(j >= 2)
    def _():
        tile_done(slot).wait()

    h = h_ref[...]
    lp = lp_ref[...]
    for c in range(STAGE_ROWS // TT):
        r = _iota((TT, TT), 0) + c * TT
        hit = (r == lp[0:1]) | (r == lp[1:2]) | (r == lp[2:3]) | (r == lp[3:4])
        rows = _dot(jnp.where(hit, 1.0, 0.0).astype(BF16), h)
        for q in range(ROW_TILE):
            stage[slot, pl.ds(c * TT * ROW_TILE + q, TT, stride=ROW_TILE), :] = rows[:, q * LANES:(q + 1) * LANES]

    for e in range(N_EXPERTS):
        src = loff_ref[j * N_EXPERTS + e]
        dst = start_ref[j * N_EXPERTS + e]

        def put(offset, size, src=src, dst=dst):
            pltpu.make_async_copy(stage.at[slot, _rows(src + offset, size), :],
                                  xs_ref.at[_rows(dst + offset, size), :], sem.at[slot]).start()
        _for_each_piece(len_ref[j * N_EXPERTS + e], TT_LOG2, put)

    @pl.when(j == pl.num_programs(0) - 1)
    def _():
        tile_done(1 - slot).wait()
        tile_done(slot).wait()


def _dispatch(starts, lens, loffs, pad_starts, pad_lens, lp, h2, p_rows):
    t = h2.shape[0]
    return pl.pallas_call(
        _dispatch_kernel,
        grid_spec=pltpu.PrefetchScalarGridSpec(
            num_scalar_prefetch=5,
            grid=(t // TT,),
            in_specs=[
                pl.BlockSpec((8, TT), lambda j, *_: (0, j)),
                pl.BlockSpec((TT, D_MODEL), lambda j, *_: (j, 0)),
            ],
            out_specs=pl.BlockSpec(memory_space=pl.ANY),
            scratch_shapes=[
                pltpu.VMEM((2, STAGE_ROWS * ROW_TILE, LANES), F32),
                pltpu.VMEM((TM // 2 * ROW_TILE, LANES), F32),
                pltpu.SemaphoreType.DMA((2,)),
                pltpu.SemaphoreType.DMA(()),
            ],
        ),
        out_shape=jax.ShapeDtypeStruct((p_rows * ROW_TILE, LANES), F32),
        compiler_params=pltpu.CompilerParams(
            dimension_semantics=("arbitrary",), vmem_limit_bytes=VMEM_LIMIT),
        name="dispatch",
    )(starts, lens, loffs, pad_starts, pad_lens, lp, h2)


def _experts_kernel(te_ref, nv_ref, x_ref, wg_ref, bg_ref, wu_ref, bu_ref, wd_ref, bd_ref,
                    o_ref, act, wg_b, wu_b, wd_b):
    i = pl.program_id(0)
    prev = jnp.maximum(i - 1, 0)

    @pl.when(i >= nv_ref[0])
    def _():
        o_ref[...] = jnp.zeros_like(o_ref)

    @pl.when(i < nv_ref[0])
    def _():
        @pl.when((i == 0) | (te_ref[i] != te_ref[prev]))
        def _():
            for c in range(N_CHUNKS):
                cs = slice(c * N_COLS, (c + 1) * N_COLS)
                wg_b[c] = wg_ref[:, cs].astype(BF16)
                wu_b[c] = wu_ref[:, cs].astype(BF16)
                wd_b[c] = wd_ref[:, cs].astype(BF16)

        xb = _load_row_tiles(x_ref, TM).astype(BF16)
        for n in range(N_CHUNKS):
            gate = _dot(xb, wg_b[n]) + bg_ref[n:n + 1, :]
            up = _dot(xb, wu_b[n]) + bu_ref[n:n + 1, :]
            gate = jnp.minimum(gate, SWIGLU_LIMIT)
            up = jnp.clip(up, -SWIGLU_LIMIT, SWIGLU_LIMIT)
            act[n] = ((up + 1.0) * (gate * jax.nn.sigmoid(SWIGLU_ALPHA * gate))).astype(BF16)
        a = jnp.concatenate([act[c] for c in range(N_CHUNKS)], axis=1)
        for n in range(N_CHUNKS):
            out = _dot(a, wd_b[n]) + bd_ref[n:n + 1, :]
            for q in range(N_COLS // LANES):
                o_ref[pl.ds(n * (N_COLS // LANES) + q, TM, stride=ROW_TILE), :] = (
                    out[:, q * LANES:(q + 1) * LANES])


def _experts(tile_expert, n_valid, xs, wg, bg, wu, bu, wd, bd):
    n_tiles = xs.shape[0] // (TM * ROW_TILE)

    def x_map(i, te, nv):
        return (jnp.minimum(i, nv[0] - 1), 0)

    def w_map(i, te, nv):
        return (te[i], 0, 0)

    w_spec = pl.BlockSpec((None, D_MODEL, D_MODEL), w_map)
    b_spec = pl.BlockSpec((None, N_CHUNKS, N_COLS), w_map)
    return pl.pallas_call(
        _experts_kernel,
        grid_spec=pltpu.PrefetchScalarGridSpec(
            num_scalar_prefetch=2,
            grid=(n_tiles,),
            in_specs=[pl.BlockSpec((TM * ROW_TILE, LANES), x_map),
                      w_spec, b_spec, w_spec, b_spec, w_spec, b_spec],
            out_specs=pl.BlockSpec((TM * ROW_TILE, LANES), lambda i, te, nv: (i, 0)),
            scratch_shapes=[
                pltpu.VMEM((N_CHUNKS, TM, N_COLS), BF16),
                pltpu.VMEM((N_CHUNKS, D_MODEL, N_COLS), BF16),
                pltpu.VMEM((N_CHUNKS, D_MODEL, N_COLS), BF16),
                pltpu.VMEM((N_CHUNKS, D_MODEL, N_COLS), BF16),
            ],
        ),
        out_shape=jax.ShapeDtypeStruct(xs.shape, F32),
        compiler_params=pltpu.CompilerParams(
            dimension_semantics=("arbitrary",), vmem_limit_bytes=VMEM_LIMIT),
        name="experts",
    )(tile_expert, n_valid, xs, wg, bg, wu, bu, wd, bd)


def _combine_kernel(start_ref, len_ref, loff_ref, info_ref, x1_ref, fw_ref, eo_ref, out_ref, stage, sem):
    j = pl.program_id(0)
    slot = j & 1
    last = pl.num_programs(0) - 1

    def fetch(tile, s):
        for e in range(N_EXPERTS):
            src = start_ref[tile * N_EXPERTS + e]
            dst = loff_ref[tile * N_EXPERTS + e]

            def get(offset, size, src=src, dst=dst):
                pltpu.make_async_copy(eo_ref.at[_rows(src + offset, size), :],
                                      stage.at[s, _rows(dst + offset, size), :], sem.at[s]).start()
            _for_each_piece(len_ref[tile * N_EXPERTS + e], TT_LOG2, get)

    @pl.when(j == 0)
    def _():
        fetch(0, 0)

    @pl.when(j < last)
    def _():
        fetch(j + 1, 1 - slot)

    pltpu.make_async_copy(eo_ref.at[pl.ds(0, STAGE_ROWS * ROW_TILE), :], stage.at[slot], sem.at[slot]).wait()

    info = info_ref[...]
    y = x1_ref[...]
    for c in range(STAGE_ROWS // TT):
        r = (_iota((TT, TT), 1) + c * TT).astype(F32)
        g = jnp.zeros((TT, TT), F32)
        for k in range(TOP_K):
            g = g + jnp.where(r == info[:, TOP_K + k:TOP_K + k + 1], info[:, k:k + 1], 0.0)
        g_hi = g.astype(BF16)
        g_lo = (g - g_hi.astype(F32)).astype(BF16)
        rows = jnp.concatenate(
            [stage[slot, pl.ds(c * TT * ROW_TILE + q, TT, stride=ROW_TILE), :] for q in range(ROW_TILE)],
            axis=1).astype(BF16)
        y = y + _dot(g_hi, rows) + _dot(g_lo, rows)
    ms = jnp.mean(y * y, axis=-1, keepdims=True)
    out_ref[...] = y * lax.rsqrt(ms + EPS) * fw_ref[...]


def _combine(starts, lens, loffs, info, x1, final_w, eo):
    t = x1.shape[0]
    return pl.pallas_call(
        _combine_kernel,
        grid_spec=pltpu.PrefetchScalarGridSpec(
            num_scalar_prefetch=3,
            grid=(t // TT,),
            in_specs=[
                pl.BlockSpec((TT, LANES), lambda j, *_: (j, 0)),
                pl.BlockSpec((TT, D_MODEL), lambda j, *_: (j, 0)),
                pl.BlockSpec((1, D_MODEL), lambda j, *_: (0, 0)),
                pl.BlockSpec(memory_space=pl.ANY),
            ],
            out_specs=pl.BlockSpec((TT, D_MODEL), lambda j, *_: (j, 0)),
            scratch_shapes=[
                pltpu.VMEM((2, STAGE_ROWS * ROW_TILE, LANES), F32),
                pltpu.SemaphoreType.DMA((2,)),
            ],
        ),
        out_shape=jax.ShapeDtypeStruct((t, D_MODEL), F32),
        compiler_params=pltpu.CompilerParams(
            dimension_semantics=("arbitrary",), vmem_limit_bytes=VMEM_LIMIT),
        name="combine",
    )(starts, lens, loffs, info, x1, final_w, eo)


def _pad_lanes(v, offset, fill=0.0):
    row = jnp.full((1, LANES), fill, F32)
    return row.at[0, offset:offset + v.shape[0]].set(v.astype(F32))


def kernel(x, norm_mix_w, w_in, gla_w_alpha_up, gla_b_alpha, gla_norm_w, ssd_conv_w, ssd_conv_b,
           ssd_dt_bias, ssd_A_log, ssd_D, ssd_norm_w, w_out, norm_ffn_w, router_w, router_b,
           moe_w_gate, moe_b_gate, moe_w_up, moe_b_up, moe_w_down, moe_b_down, final_norm_w):
    bsz, seqlen, d = x.shape
    t = bsz * seqlen
    depth = w_in.shape[0]
    assert depth == 1, "the final RMSNorm is fused into the (single) layer's combine step"
    p_rows = t * TOP_K + N_EXPERTS * TM
    n_tiles = p_rows // TM
    for l in range(depth):
        w = w_in[l]
        w_all = jnp.concatenate(
            [w[:, 0:1536], w[:, 1552:3088], w[:, 1536:1552], w[:, 3088:3096],
             jnp.zeros((d, N_SMALL - GLA_GATE_RANK - SSD_HEADS), w.dtype)], axis=1).astype(BF16)
        wup = jnp.zeros((N_SMALL, GLA_KW), F32).at[0:GLA_GATE_RANK].set(gla_w_alpha_up[l])
        dtb = _pad_lanes(ssd_dt_bias[l], DT_COL)
        aneg = _pad_lanes(-jnp.exp(ssd_A_log[l].astype(F32)), DT_COL)
        dexp = jnp.repeat(ssd_D[l].astype(F32), SSD_HEADDIM)[None, :]
        rw = jnp.zeros((LANES, d), F32).at[0:N_EXPERTS].set(router_w[l].T)
        rb = jnp.zeros((LANES, TL), F32).at[0:N_EXPERTS].set(
            jnp.broadcast_to(router_b[l].astype(F32)[:, None], (N_EXPERTS, TL)))

        pm, ps = _inproj(x.reshape(t, d), norm_mix_w[l][None, :], w_all)
        x1, h2, logits = _mixer(
            pm.reshape(bsz, seqlen, N_MAIN), ps.reshape(bsz, seqlen, N_SMALL), x,
            wup, gla_b_alpha[l][None, :], gla_norm_w[l][None, :], ssd_conv_w[l],
            ssd_conv_b[l][None, :], dtb, aneg, dexp, ssd_norm_w[l][None, :],
            w_out[l].astype(BF16), norm_ffn_w[l][None, :], rw, rb)

        info, lp, starts, lens, loffs, counts = _route(logits)
        starts, lens, loffs = starts[:, 0], lens[:, 0], loffs[:, 0]

        cnt = counts[:, 0].astype(jnp.int32)
        padded = ((cnt + TM - 1) // TM) * TM
        ends = jnp.cumsum(padded)
        n_valid = (ends[-1] // TM).astype(jnp.int32)
        tile_starts = jnp.arange(n_tiles, dtype=jnp.int32) * TM
        tile_expert = jnp.sum(tile_starts[:, None] >= ends[None, :], axis=1).astype(jnp.int32)
        last_expert = tile_expert[jnp.maximum(n_valid - 1, 0)]
        tile_expert = jnp.where(tile_starts < ends[-1], tile_expert, last_expert)
        pad_starts = jnp.concatenate([ends - padded + cnt, ends[-1:]])
        pad_lens = jnp.concatenate([padded - cnt, (p_rows - ends[-1:]) // (TM // 2)])

        xs = _dispatch(starts, lens, loffs, pad_starts, pad_lens, lp, h2.reshape(t, d), p_rows)
        eo = _experts(tile_expert, n_valid.reshape(1), xs,
                      moe_w_gate[l], moe_b_gate[l].reshape(N_EXPERTS, N_CHUNKS, N_COLS),
                      moe_w_up[l], moe_b_up[l].reshape(N_EXPERTS, N_CHUNKS, N_COLS),
                      moe_w_down[l], moe_b_down[l].reshape(N_EXPERTS, N_CHUNKS, N_COLS))
        x = _combine(starts, lens, loffs, info, x1.reshape(t, d), final_norm_w[None, :], eo
                     ).reshape(bsz, seqlen, d)
    return x
```

```python
import functools

import jax
import jax.numpy as jnp
from jax import lax
from jax.experimental import pallas as pl
from jax.experimental.pallas import tpu as pltpu

F32 = jnp.float32
BF16 = jnp.bfloat16

D_MODEL = 1024
GLA_WIDTH = 512
GLA_HEADS = 4
GLA_DV = 128
GLA_DK = 64
GLA_KW = 256
GLA_GATE_RANK = 16
GLA_GATE_NORM = 16.0
SSD_WIDTH = 512
SSD_HEADDIM = 64
SSD_HEADS = 8
SSD_GROUPS = 2
SSD_HPG = 4
SSD_STATE = 128
SSD_CONV = 4
SSD_CONV_CH = 1024
N_EXPERTS = 32
TOP_K = 4
SWIGLU_LIMIT = 7.0
SWIGLU_ALPHA = 1.702
EPS = 1e-6
GROUP_EPS = 1e-5

LANES = 128
ROW_TILE = D_MODEL // LANES
N_MAIN = 3072
N_SMALL = LANES
DT_COL = GLA_GATE_RANK

GLA_CHUNK = 64
SSD_CHUNK = 128
TM_PROJ = 512
TL = 256
TT_LOG2 = 8
TT = 1 << TT_LOG2
STAGE_ROWS = 4 * TT
TM_LOG2 = 9
TM = 1 << TM_LOG2
assert STAGE_ROWS * D_MODEL * 4 <= (1 << 17) * 32
N_COLS = 256
N_CHUNKS = D_MODEL // N_COLS
VMEM_LIMIT = 56 * 1024 * 1024


def _dot(a, b):
    return jnp.dot(a, b, preferred_element_type=F32)


def _dot_nt(a, b):
    return lax.dot_general(a, b, (((1,), (1,)), ((), ())), preferred_element_type=F32)


def _dot_tn(a, b):
    return lax.dot_general(a, b, (((0,), (0,)), ((), ())), preferred_element_type=F32)


def _split3(a):
    hi = a.astype(BF16)
    r1 = a - hi.astype(F32)
    mid = r1.astype(BF16)
    lo = (r1 - mid.astype(F32)).astype(BF16)
    return hi, mid, lo


def _dot_sel_lhs(sel, a):
    hi, mid, lo = _split3(a)
    return _dot(sel, hi) + _dot(sel, mid) + _dot(sel, lo)


def _dot_sel_rhs(a, sel, terms=3):
    parts = _split3(a)[:terms]
    out = _dot(parts[0], sel)
    for p in parts[1:]:
        out = out + _dot(p, sel)
    return out


def _dot_hi(a, b):
    a_hi = a.astype(BF16)
    a_lo = (a - a_hi.astype(F32)).astype(BF16)
    b_hi = b.astype(BF16)
    b_lo = (b - b_hi.astype(F32)).astype(BF16)
    return _dot(a_hi, b_hi) + _dot(a_lo, b_hi) + _dot(a_hi, b_lo)


def _dot_hi_nt(a, b):
    a_hi = a.astype(BF16)
    a_lo = (a - a_hi.astype(F32)).astype(BF16)
    b_hi = b.astype(BF16)
    b_lo = (b - b_hi.astype(F32)).astype(BF16)
    return _dot_nt(a_hi, b_hi) + _dot_nt(a_lo, b_hi) + _dot_nt(a_hi, b_lo)


def _softplus(x):
    return jnp.maximum(x, 0.0) + jnp.log1p(jnp.exp(-jnp.abs(x)))


def _silu(x):
    return x * jax.nn.sigmoid(x)


def _iota(shape, dim):
    return lax.broadcasted_iota(jnp.int32, shape, dim)


def _load_row_tiles(ref, rows):
    return jnp.concatenate([ref[pl.ds(j, rows, stride=ROW_TILE), :] for j in range(ROW_TILE)], axis=1)


def _inproj_kernel(x_ref, nw_ref, w_ref, pm_ref, ps_ref):
    x = x_ref[...]
    ms = jnp.mean(x * x, axis=-1, keepdims=True)
    h = (x * lax.rsqrt(ms + EPS) * nw_ref[...]).astype(BF16)
    step = 512
    for n0 in range(0, N_MAIN, step):
        pm_ref[:, n0:n0 + step] = _dot(h, w_ref[:, n0:n0 + step]).astype(BF16)
    ps_ref[...] = _dot(h, w_ref[:, N_MAIN:N_MAIN + N_SMALL])


def _inproj(x2d, norm_w, w_all):
    t = x2d.shape[0]
    return pl.pallas_call(
        _inproj_kernel,
        grid=(t // TM_PROJ,),
        in_specs=[
            pl.BlockSpec((TM_PROJ, D_MODEL), lambda i: (i, 0)),
            pl.BlockSpec((1, D_MODEL), lambda i: (0, 0)),
            pl.BlockSpec((D_MODEL, N_MAIN + N_SMALL), lambda i: (0, 0)),
        ],
        out_specs=[
            pl.BlockSpec((TM_PROJ, N_MAIN), lambda i: (i, 0)),
            pl.BlockSpec((TM_PROJ, N_SMALL), lambda i: (i, 0)),
        ],
        out_shape=[
            jax.ShapeDtypeStruct((t, N_MAIN), BF16),
            jax.ShapeDtypeStruct((t, N_SMALL), F32),
        ],
        compiler_params=pltpu.CompilerParams(
            dimension_semantics=("arbitrary",), vmem_limit_bytes=VMEM_LIMIT),
        name="inproj",
    )(x2d, norm_w, w_all)


def _mixer_kernel(pm_ref, ps_ref, x_ref, wup_ref, balpha_ref, gnw_ref, convw_ref, convb_ref,
                  dtb_ref, aneg_ref, dexp_ref, snw_ref, wout_ref, nfw_ref, rw_ref, rb_ref,
                  x1_ref, h2_ref, lg_ref,
                  gla_state, ssd_state, conv_tail, mix_scr):
    @pl.when(pl.program_id(1) == 0)
    def _():
        gla_state[...] = jnp.zeros_like(gla_state)
        ssd_state[...] = jnp.zeros_like(ssd_state)
        conv_tail[...] = jnp.zeros_like(conv_tail)

    small = ps_ref[...]

    row = _iota((TL, TL), 0)
    col = _iota((TL, TL), 1)
    causal = col <= row
    cum64 = jnp.where(causal & ((row // GLA_CHUNK) == (col // GLA_CHUNK)), 1.0, 0.0).astype(BF16)
    cum128 = jnp.where(causal & ((row // SSD_CHUNK) == (col // SSD_CHUNK)), 1.0, 0.0).astype(BF16)

    xa = _dot_hi(small, wup_ref[...]) + balpha_ref[...]
    log_a = (jnp.minimum(xa, 0.0) - jnp.log1p(jnp.exp(-jnp.abs(xa)))) * (1.0 / GLA_GATE_NORM)
    bcum = _dot_sel_lhs(cum64, log_a)

    lane_kw = _iota((GLA_CHUNK, GLA_KW), 1)
    head_masks = [(lane_kw // GLA_DK) == h for h in range(GLA_HEADS)]
    lane_kw_s = _iota((GLA_DV, GLA_KW), 1)
    head_masks_s = [(lane_kw_s // GLA_DK) == h for h in range(GLA_HEADS)]
    tril64 = _iota((GLA_CHUNK, GLA_CHUNK), 1) <= _iota((GLA_CHUNK, GLA_CHUNK), 0)
    q_scale = GLA_DK ** -0.5

    for c in range(TL // GLA_CHUNK):
        rs = slice(c * GLA_CHUNK, (c + 1) * GLA_CHUNK)
        bc = bcum[rs]
        b_mid = bc[GLA_CHUNK // 2:GLA_CHUNK // 2 + 1]
        b_last = bc[GLA_CHUNK - 1:GLA_CHUNK]
        qc = pm_ref[rs, 0:GLA_KW].astype(F32) * q_scale
        kc = pm_ref[rs, GLA_KW:2 * GLA_KW].astype(F32)
        vc = pm_ref[rs, 2 * GLA_KW:2 * GLA_KW + GLA_WIDTH]
        q_in = (qc * jnp.exp(bc - b_mid)).astype(BF16)
        k_in = (kc * jnp.exp(b_mid - bc)).astype(BF16)
        q_st = (qc * jnp.exp(bc)).astype(BF16)
        k_st = (kc * jnp.exp(b_last - bc)).astype(BF16)
        st = gla_state[...]
        st_b = st.astype(BF16)
        zero_b = jnp.zeros_like(q_in)
        for h in range(GLA_HEADS):
            scores = _dot_nt(jnp.where(head_masks[h], q_in, zero_b), k_in)
            scores = jnp.where(tril64, scores, 0.0).astype(BF16)
            o_h = _dot(scores, vc[:, h * GLA_DV:(h + 1) * GLA_DV])
            o_h = o_h + _dot_nt(jnp.where(head_masks[h], q_st, zero_b), st_b)
            mix_scr[rs, h * GLA_DV:(h + 1) * GLA_DV] = o_h
        upd = _dot_tn(vc, k_st)
        new_st = st * jnp.exp(b_last)
        for h in range(GLA_HEADS):
            new_st = new_st + jnp.where(head_masks_s[h], upd[h * GLA_DV:(h + 1) * GLA_DV], 0.0)
        gla_state[...] = new_st

    xbc = pm_ref[:, 2048:3072].astype(F32)
    tail = conv_tail[...]
    conv_tail[...] = xbc[TL - 8:TL]
    row8 = _iota((8, SSD_CONV_CH), 0)
    conv = xbc * convw_ref[SSD_CONV - 1:SSD_CONV, :]
    for s in range(1, SSD_CONV):
        shifted = pltpu.roll(xbc, s, 0)
        head = jnp.where(row8 < s, pltpu.roll(tail, s, 0), shifted[0:8])
        shifted = jnp.concatenate([head, shifted[8:]], axis=0)
        conv = conv + shifted * convw_ref[SSD_CONV - 1 - s:SSD_CONV - s, :]
    act = _silu(conv + convb_ref[...])
    xs = act[:, 0:SSD_WIDTH]
    bm = act[:, SSD_WIDTH:SSD_WIDTH + SSD_GROUPS * SSD_STATE].astype(BF16)
    cm = act[:, SSD_WIDTH + SSD_GROUPS * SSD_STATE:].astype(BF16)

    dt_full = _softplus(small + dtb_ref[...])
    a_full = dt_full * aneg_ref[...]
    acum = _dot_sel_lhs(cum128, a_full)
    acum_t = acum.T

    e_row = _iota((N_SMALL, SSD_WIDTH), 0)
    e_col = _iota((N_SMALL, SSD_WIDTH), 1)
    spread64 = jnp.where(e_row == DT_COL + e_col // SSD_HEADDIM, 1.0, 0.0).astype(BF16)
    e_row2 = _iota((N_SMALL, SSD_HEADS * LANES), 0)
    e_col2 = _iota((N_SMALL, SSD_HEADS * LANES), 1)
    spread128 = jnp.where(e_row2 == DT_COL + e_col2 // LANES, 1.0, 0.0).astype(BF16)
    dt_e = _dot_sel_rhs(dt_full, spread64, terms=1)
    ac_e = _dot_sel_rhs(acum, spread64, terms=2)
    ac_w = _dot_sel_rhs(acum, spread128, terms=2)

    tril128 = _iota((SSD_CHUNK, SSD_CHUNK), 1) <= _iota((SSD_CHUNK, SSD_CHUNK), 0)
    lane_g = _iota((SSD_CHUNK, SSD_HPG * SSD_HEADDIM), 1)
    for c in range(TL // SSD_CHUNK):
        rs = slice(c * SSD_CHUNK, (c + 1) * SSD_CHUNK)
        ac_c = ac_e[rs]
        a_last = ac_c[SSD_CHUNK - 1:SSD_CHUNK]
        dt_c = dt_e[rs]
        xs_c = xs[rs]
        x_dt = (xs_c * dt_c).astype(BF16)
        x_w = (xs_c * (jnp.exp(a_last - ac_c) * dt_c)).astype(BF16)
        e_ac = jnp.exp(ac_c)
        for g in range(SSD_GROUPS):
            gs = slice(g * SSD_STATE, (g + 1) * SSD_STATE)
            ws = slice(g * SSD_HPG * SSD_HEADDIM, (g + 1) * SSD_HPG * SSD_HEADDIM)
            c_g = cm[rs, gs]
            b_g = bm[rs, gs]
            cb = _dot_nt(c_g, b_g)
            x_dt_g = x_dt[:, ws]
            lhs_parts = []
            rhs_parts = []
            for hh in range(SSD_HPG):
                h = g * SSD_HPG + hh
                seg = ac_w[rs, h * LANES:(h + 1) * LANES] - acum_t[DT_COL + h:DT_COL + h + 1, rs]
                lmat = jnp.where(tril128, jnp.exp(jnp.where(tril128, seg, 0.0)), 0.0)
                lhs_parts.append((cb * lmat).astype(BF16))
                rhs_parts.append(jnp.where((lane_g // SSD_HEADDIM) == hh, x_dt_g,
                                           jnp.zeros_like(x_dt_g)))
            intra = _dot(jnp.concatenate(lhs_parts, axis=1), jnp.concatenate(rhs_parts, axis=0))
            st = ssd_state[g]
            inter = _dot(c_g, st.astype(BF16)) * e_ac[:, ws]
            mix_scr[rs, GLA_WIDTH + g * 256:GLA_WIDTH + (g + 1) * 256] = intra + inter
            ssd_state[g] = st * jnp.exp(a_last[:, ws]) + _dot_tn(b_g, x_w[:, ws])

    o = mix_scr[:, 0:GLA_WIDTH]
    g_gate = _silu(pm_ref[:, 1024:1536].astype(F32))
    gla_parts = []
    for h in range(GLA_HEADS):
        o_h = o[:, h * GLA_DV:(h + 1) * GLA_DV]
        ms = jnp.mean(o_h * o_h, axis=-1, keepdims=True)
        gla_parts.append(o_h * lax.rsqrt(ms + GROUP_EPS))
    gla_out = jnp.concatenate(gla_parts, axis=1) * gnw_ref[...] * g_gate

    y = mix_scr[:, GLA_WIDTH:] + dexp_ref[...] * xs
    y = y * _silu(pm_ref[:, 1536:2048].astype(F32))
    ssd_parts = []
    for g in range(SSD_GROUPS):
        y_g = y[:, g * 256:(g + 1) * 256]
        ms = jnp.mean(y_g * y_g, axis=-1, keepdims=True)
        ssd_parts.append(y_g * lax.rsqrt(ms + GROUP_EPS))
    ssd_out = jnp.concatenate(ssd_parts, axis=1) * snw_ref[...]

    mixed = jnp.concatenate([gla_out, ssd_out], axis=1).astype(BF16)
    x1 = x_ref[...] + _dot(mixed, wout_ref[...])
    x1_ref[...] = x1

    ms = jnp.mean(x1 * x1, axis=-1, keepdims=True)
    h2 = x1 * lax.rsqrt(ms + EPS) * nfw_ref[...]
    h2_ref[...] = h2.astype(BF16)
    lg_ref[...] = _dot_hi_nt(rw_ref[...], h2) + rb_ref[...]


def _mixer(pm, ps, x, wup, balpha, gnw, convw, convb, dtb, aneg, dexp, snw, wout, nfw, rw, rb):
    bsz, seqlen, _ = x.shape

    def full(a):
        return pl.BlockSpec(a.shape, lambda b, l: (0,) * a.ndim)

    def tile(width):
        return pl.BlockSpec((None, TL, width), lambda b, l: (b, l, 0))

    params = (wup, balpha, gnw, convw, convb, dtb, aneg, dexp, snw, wout, nfw, rw, rb)
    return pl.pallas_call(
        _mixer_kernel,
        grid=(bsz, seqlen // TL),
        in_specs=[tile(N_MAIN), tile(N_SMALL), tile(D_MODEL)] + [full(p) for p in params],
        out_specs=[tile(D_MODEL), tile(D_MODEL),
                   pl.BlockSpec((LANES, TL), lambda b, l: (0, b * (seqlen // TL) + l))],
        out_shape=[
            jax.ShapeDtypeStruct((bsz, seqlen, D_MODEL), F32),
            jax.ShapeDtypeStruct((bsz, seqlen, D_MODEL), BF16),
            jax.ShapeDtypeStruct((LANES, bsz * seqlen), F32),
        ],
        scratch_shapes=[
            pltpu.VMEM((GLA_DV, GLA_KW), F32),
            pltpu.VMEM((SSD_GROUPS, SSD_STATE, SSD_HPG * SSD_HEADDIM), F32),
            pltpu.VMEM((8, SSD_CONV_CH), F32),
            pltpu.VMEM((TL, D_MODEL), F32),
        ],
        compiler_params=pltpu.CompilerParams(
            dimension_semantics=("arbitrary", "arbitrary"), vmem_limit_bytes=VMEM_LIMIT),
        name="mixer",
    )(pm, ps, x, *params)


def _route_kernel(lg_ref, info_ref, lp_ref, start_ref, len_ref, loff_ref, cnt_ref, cnt_scr, run_scr):
    phase = pl.program_id(0)
    step = pl.program_id(1)

    @pl.when((phase == 0) & (step == 0))
    def _():
        cnt_scr[...] = jnp.zeros_like(cnt_scr)
        run_scr[...] = jnp.zeros_like(run_scr)

    lg = lg_ref[0:N_EXPERTS, :]
    row = _iota((N_EXPERTS, TT), 0)
    work = lg
    onehots = []
    vals = []
    for _ in range(TOP_K):
        m = jnp.max(work, axis=0, keepdims=True)
        idx = jnp.min(jnp.where(work == m, row, N_EXPERTS), axis=0, keepdims=True)
        oh = row == idx
        onehots.append(oh)
        vals.append(m)
        work = jnp.where(oh, -jnp.inf, work)
    multi = jnp.where(onehots[0] | onehots[1] | onehots[2] | onehots[3], 1.0, 0.0).astype(BF16)
    tile_cnt = _dot(multi, jnp.ones((TT, LANES), BF16))

    @pl.when(phase == 0)
    def _():
        cnt_scr[...] = cnt_scr[...] + tile_cnt

    @pl.when(phase == 1)
    def _():
        counts = cnt_scr[...]
        padded = jnp.floor((counts + (TM - 1)) * (1.0 / TM)) * TM
        lower = jnp.where(_iota((N_EXPERTS, N_EXPERTS), 1) < _iota((N_EXPERTS, N_EXPERTS), 0),
                          1.0, 0.0).astype(BF16)
        offs = _dot_sel_lhs(lower, padded)
        loff = _dot_sel_lhs(lower, tile_cnt)
        before = jnp.where(_iota((TT, TT), 0) < _iota((TT, TT), 1), 1.0, 0.0).astype(BF16)
        rank = _dot(multi, before)
        start_ref[...] = (offs + run_scr[...]).astype(jnp.int32)
        len_ref[...] = tile_cnt.astype(jnp.int32)
        loff_ref[...] = loff.astype(jnp.int32)
        run_scr[...] = run_scr[...] + tile_cnt
        local = rank + jnp.concatenate([loff] * (TT // LANES), axis=1)
        exps = [jnp.exp(v - vals[0]) for v in vals]
        den = exps[0] + exps[1] + exps[2] + exps[3]
        lp_rows = [jnp.sum(jnp.where(oh, local, 0.0), axis=0, keepdims=True) for oh in onehots]
        lp_ref[...] = jnp.concatenate(lp_rows + [jnp.zeros((8 - TOP_K, TT), F32)], axis=0).astype(jnp.int32)
        gate_rows = [e / den for e in exps]
        info = jnp.concatenate(gate_rows + lp_rows + [jnp.zeros((LANES - 2 * TOP_K, TT), F32)], axis=0)
        info_ref[...] = info.T
        cnt_ref[...] = counts


def _route(logits_t):
    t = logits_t.shape[1]
    steps = t // TT
    run_spec = pl.BlockSpec((N_EXPERTS, LANES), lambda p, i: (i * p, 0))
    run_shape = jax.ShapeDtypeStruct((steps * N_EXPERTS, LANES), jnp.int32)
    return pl.pallas_call(
        _route_kernel,
        grid=(2, steps),
        in_specs=[pl.BlockSpec((LANES, TT), lambda p, i: (0, i))],
        out_specs=[
            pl.BlockSpec((TT, LANES), lambda p, i: (i * p, 0)),
            pl.BlockSpec((8, TT), lambda p, i: (0, i * p)),
            run_spec, run_spec, run_spec,
            pl.BlockSpec((N_EXPERTS, LANES), lambda p, i: (0, 0)),
        ],
        out_shape=[
            jax.ShapeDtypeStruct((t, LANES), F32),
            jax.ShapeDtypeStruct((8, t), jnp.int32),
            run_shape, run_shape, run_shape,
            jax.ShapeDtypeStruct((N_EXPERTS, LANES), F32),
        ],
        scratch_shapes=[pltpu.VMEM((N_EXPERTS, LANES), F32), pltpu.VMEM((N_EXPERTS, LANES), F32)],
        compiler_params=pltpu.CompilerParams(
            dimension_semantics=("arbitrary", "arbitrary"), vmem_limit_bytes=VMEM_LIMIT),
        name="route",
    )(logits_t)


def _rows(first, count):
    return pl.ds(pl.multiple_of(first * ROW_TILE, ROW_TILE), count * ROW_TILE)


def _for_each_piece(length, max_log2, fn):
    for b in reversed(range(max_log2 + 1)):
        size = 1 << b
        offset = lax.shift_left(lax.shift_right_logical(length, b + 1), b + 1)

        @pl.when((length & size) != 0)
        def _(offset=offset, size=size):
            fn(offset, size)


def _dispatch_kernel(start_ref, len_ref, loff_ref, pstart_ref, plen_ref,
                     lp_ref, h_ref, xs_ref, stage, zeros, sem, zsem):
    j = pl.program_id(0)
    slot = j & 1

    def tile_done(s):
        return pltpu.make_async_copy(stage.at[s], xs_ref.at[pl.ds(0, STAGE_ROWS * ROW_TILE), :], sem.at[s])

    @pl.when(j == 0)
    def _():
        zeros[...] = jnp.zeros_like(zeros)
        for e in range(N_EXPERTS):
            def put(offset, size, e=e):
                cp = pltpu.make_async_copy(zeros.at[pl.ds(0, size * ROW_TILE), :],
                                           xs_ref.at[_rows(pstart_ref[e] + offset, size), :], zsem)
                cp.start()
                cp.wait()
            _for_each_piece(plen_ref[e], TM_LOG2 - 1, put)

        def put_block(blk, carry):
            cp = pltpu.make_async_copy(
                zeros, xs_ref.at[_rows(pstart_ref[N_EXPERTS] + blk * (TM // 2), TM // 2), :], zsem)
            cp.start()
            cp.wait()
            return carry
        lax.fori_loop(0, plen_ref[N_EXPERTS], put_block, 0)

    @pl.when(j >= 2)
    def _():
        tile_done(slot).wait()

    h = h_ref[...]
    lp = lp_ref[...]
    for c in range(STAGE_ROWS // TT):
        r = _iota((TT, TT), 0) + c * TT
        hit = (r == lp[0:1]) | (r == lp[1:2]) | (r == lp[2:3]) | (r == lp[3:4])
        rows = _dot(jnp.where(hit, 1.0, 0.0).astype(BF16), h)
        for q in range(ROW_TILE):
            stage[slot, pl.ds(c * TT * ROW_TILE + q, TT, stride=ROW_TILE), :] = rows[:, q * LANES:(q + 1) * LANES]

    for e in range(N_EXPERTS):
        src = loff_ref[j * N_EXPERTS + e]
        dst = start_ref[j * N_EXPERTS + e]

        def put(offset, size, src=src, dst=dst):
            pltpu.make_async_copy(stage.at[slot, _rows(src + offset, size), :],
                                  xs_ref.at[_rows(dst + offset, size), :], sem.at[slot]).start()
        _for_each_piece(len_ref[j * N_EXPERTS + e], TT_LOG2, put)

    @pl.when(j == pl.num_programs(0) - 1)
    def _():
        tile_done(1 - slot).wait()
        tile_done(slot).wait()


def _dispatch(starts, lens, loffs, pad_starts, pad_lens, lp, h2, p_rows):
    t = h2.shape[0]
    return pl.pallas_call(
        _dispatch_kernel,
        grid_spec=pltpu.PrefetchScalarGridSpec(
            num_scalar_prefetch=5,
            grid=(t // TT,),
            in_specs=[
                pl.BlockSpec((8, TT), lambda j, *_: (0, j)),
                pl.BlockSpec((TT, D_MODEL), lambda j, *_: (j, 0)),
            ],
            out_specs=pl.BlockSpec(memory_space=pl.ANY),
            scratch_shapes=[
                pltpu.VMEM((2, STAGE_ROWS * ROW_TILE, LANES), F32),
                pltpu.VMEM((TM // 2 * ROW_TILE, LANES), F32),
                pltpu.SemaphoreType.DMA((2,)),
                pltpu.SemaphoreType.DMA(()),
            ],
        ),
        out_shape=jax.ShapeDtypeStruct((p_rows * ROW_TILE, LANES), F32),
        compiler_params=pltpu.CompilerParams(
            dimension_semantics=("arbitrary",), vmem_limit_bytes=VMEM_LIMIT),
        name="dispatch",
    )(starts, lens, loffs, pad_starts, pad_lens, lp, h2)


def _experts_kernel(te_ref, nv_ref, x_ref, wg_ref, bg_ref, wu_ref, bu_ref, wd_ref, bd_ref,
                    o_ref, act, wg_b, wu_b, wd_b):
    i = pl.program_id(0)
    prev = jnp.maximum(i - 1, 0)

    @pl.when(i >= nv_ref[0])
    def _():
        o_ref[...] = jnp.zeros_like(o_ref)

    @pl.when(i < nv_ref[0])
    def _():
        @pl.when((i == 0) | (te_ref[i] != te_ref[prev]))
        def _():
            for c in range(N_CHUNKS):
                cs = slice(c * N_COLS, (c + 1) * N_COLS)
                wg_b[c] = wg_ref[:, cs].astype(BF16)
                wu_b[c] = wu_ref[:, cs].astype(BF16)
                wd_b[c] = wd_ref[:, cs].astype(BF16)

        xb = _load_row_tiles(x_ref, TM).astype(BF16)
        for n in range(N_CHUNKS):
            gate = _dot(xb, wg_b[n]) + bg_ref[n:n + 1, :]
            up = _dot(xb, wu_b[n]) + bu_ref[n:n + 1, :]
            gate = jnp.minimum(gate, SWIGLU_LIMIT)
            up = jnp.clip(up, -SWIGLU_LIMIT, SWIGLU_LIMIT)
            act[n] = ((up + 1.0) * (gate * jax.nn.sigmoid(SWIGLU_ALPHA * gate))).astype(BF16)
        a = jnp.concatenate([act[c] for c in range(N_CHUNKS)], axis=1)
        for n in range(N_CHUNKS):
            out = _dot(a, wd_b[n]) + bd_ref[n:n + 1, :]
            for q in range(N_COLS // LANES):
                o_ref[pl.ds(n * (N_COLS // LANES) + q, TM, stride=ROW_TILE), :] = (
                    out[:, q * LANES:(q + 1) * LANES])


def _experts(tile_expert, n_valid, xs, wg, bg, wu, bu, wd, bd):
    n_tiles = xs.shape[0] // (TM * ROW_TILE)

    def x_map(i, te, nv):
        return (jnp.minimum(i, nv[0] - 1), 0)

    def w_map(i, te, nv):
        return (te[i], 0, 0)

    w_spec = pl.BlockSpec((None, D_MODEL, D_MODEL), w_map)
    b_spec = pl.BlockSpec((None, N_CHUNKS, N_COLS), w_map)
    return pl.pallas_call(
        _experts_kernel,
        grid_spec=pltpu.PrefetchScalarGridSpec(
            num_scalar_prefetch=2,
            grid=(n_tiles,),
            in_specs=[pl.BlockSpec((TM * ROW_TILE, LANES), x_map),
                      w_spec, b_spec, w_spec, b_spec, w_spec, b_spec],
            out_specs=pl.BlockSpec((TM * ROW_TILE, LANES), lambda i, te, nv: (i, 0)),
            scratch_shapes=[
                pltpu.VMEM((N_CHUNKS, TM, N_COLS), BF16),
                pltpu.VMEM((N_CHUNKS, D_MODEL, N_COLS), BF16),
                pltpu.VMEM((N_CHUNKS, D_MODEL, N_COLS), BF16),
                pltpu.VMEM((N_CHUNKS, D_MODEL, N_COLS), BF16),
            ],
        ),
        out_shape=jax.ShapeDtypeStruct(xs.shape, F32),
        compiler_params=pltpu.CompilerParams(
            dimension_semantics=("arbitrary",), vmem_limit_bytes=VMEM_LIMIT),
        name="experts",
    )(tile_expert, n_valid, xs, wg, bg, wu, bu, wd, bd)


def _combine_kernel(start_ref, len_ref, loff_ref, info_ref, x1_ref, fw_ref, eo_ref, out_ref, stage, sem):
    j = pl.program_id(0)
    slot = j & 1
    last = pl.num_programs(0) - 1

    def fetch(tile, s):
        for e in range(N_EXPERTS):
            src = start_ref[tile * N_EXPERTS + e]
            dst = loff_ref[tile * N_EXPERTS + e]

            def get(offset, size, src=src, dst=dst):
                pltpu.make_async_copy(eo_ref.at[_rows(src + offset, size), :],
                                      stage.at[s, _rows(dst + offset, size), :], sem.at[s]).start()
            _for_each_piece(len_ref[tile * N_EXPERTS + e], TT_LOG2, get)

    @pl.when(j == 0)
    def _():
        fetch(0, 0)

    @pl.when(j < last)
    def _():
        fetch(j + 1, 1 - slot)

    pltpu.make_async_copy(eo_ref.at[pl.ds(0, STAGE_ROWS * ROW_TILE), :], stage.at[slot], sem.at[slot]).wait()

    info = info_ref[...]
    y = x1_ref[...]
    for c in range(STAGE_ROWS // TT):
        r = (_iota((TT, TT), 1) + c * TT).astype(F32)
        g = jnp.zeros((TT, TT), F32)
        for k in range(TOP_K):
            g = g + jnp.where(r == info[:, TOP_K + k:TOP_K + k + 1], info[:, k:k + 1], 0.0)
        g_hi = g.astype(BF16)
        g_lo = (g - g_hi.astype(F32)).astype(BF16)
        rows = jnp.concatenate(
            [stage[slot, pl.ds(c * TT * ROW_TILE + q, TT, stride=ROW_TILE), :] for q in range(ROW_TILE)],
            axis=1).astype(BF16)
        y = y + _dot(g_hi, rows) + _dot(g_lo, rows)
    ms = jnp.mean(y * y, axis=-1, keepdims=True)
    out_ref[...] = y * lax.rsqrt(ms + EPS) * fw_ref[...]


def _combine(starts, lens, loffs, info, x1, final_w, eo):
    t = x1.shape[0]
    return pl.pallas_call(
        _combine_kernel,
        grid_spec=pltpu.PrefetchScalarGridSpec(
            num_scalar_prefetch=3,
            grid=(t // TT,),
            in_specs=[
                pl.BlockSpec((TT, LANES), lambda j, *_: (j, 0)),
                pl.BlockSpec((TT, D_MODEL), lambda j, *_: (j, 0)),
                pl.BlockSpec((1, D_MODEL), lambda j, *_: (0, 0)),
                pl.BlockSpec(memory_space=pl.ANY),
            ],
            out_specs=pl.BlockSpec((TT, D_MODEL), lambda j, *_: (j, 0)),
            scratch_shapes=[
                pltpu.VMEM((2, STAGE_ROWS * ROW_TILE, LANES), F32),
                pltpu.SemaphoreType.DMA((2,)),
            ],
        ),
        out_shape=jax.ShapeDtypeStruct((t, D_MODEL), F32),
        compiler_params=pltpu.CompilerParams(
            dimension_semantics=("arbitrary",), vmem_limit_bytes=VMEM_LIMIT),
        name="combine",
    )(starts, lens, loffs, info, x1, final_w, eo)


def _pad_lanes(v, offset, fill=0.0):
    row = jnp.full((1, LANES), fill, F32)
    return row.at[0, offset:offset + v.shape[0]].set(v.astype(F32))


def kernel(x, norm_mix_w, w_in, gla_w_alpha_up, gla_b_alpha, gla_norm_w, ssd_conv_w, ssd_conv_b,
           ssd_dt_bias, ssd_A_log, ssd_D, ssd_norm_w, w_out, norm_ffn_w, router_w, router_b,
           moe_w_gate, moe_b_gate, moe_w_up, moe_b_up, moe_w_down, moe_b_down, final_norm_w):
    bsz, seqlen, d = x.shape
    t = bsz * seqlen
    depth = w_in.shape[0]
    assert depth == 1, "the final RMSNorm is fused into the (single) layer's combine step"
    p_rows = t * TOP_K + N_EXPERTS * TM
    n_tiles = p_rows // TM
    for l in range(depth):
        w = w_in[l]
        w_all = jnp.concatenate(
            [w[:, 0:1536], w[:, 1552:3088], w[:, 1536:1552], w[:, 3088:3096],
             jnp.zeros((d, N_SMALL - GLA_GATE_RANK - SSD_HEADS), w.dtype)], axis=1).astype(BF16)
        wup = jnp.zeros((N_SMALL, GLA_KW), F32).at[0:GLA_GATE_RANK].set(gla_w_alpha_up[l])
        dtb = _pad_lanes(ssd_dt_bias[l], DT_COL)
        aneg = _pad_lanes(-jnp.exp(ssd_A_log[l].astype(F32)), DT_COL)
        dexp = jnp.repeat(ssd_D[l].astype(F32), SSD_HEADDIM)[None, :]
        rw = jnp.zeros((LANES, d), F32).at[0:N_EXPERTS].set(router_w[l].T)
        rb = jnp.zeros((LANES, TL), F32).at[0:N_EXPERTS].set(
            jnp.broadcast_to(router_b[l].astype(F32)[:, None], (N_EXPERTS, TL)))

        pm, ps = _inproj(x.reshape(t, d), norm_mix_w[l][None, :], w_all)
        x1, h2, logits = _mixer(
            pm.reshape(bsz, seqlen, N_MAIN), ps.reshape(bsz, seqlen, N_SMALL), x,
            wup, gla_b_alpha[l][None, :], gla_norm_w[l][None, :], ssd_conv_w[l],
            ssd_conv_b[l][None, :], dtb, aneg, dexp, ssd_norm_w[l][None, :],
            w_out[l].astype(BF16), norm_ffn_w[l][None, :], rw, rb)

        info, lp, starts, lens, loffs, counts = _route(logits)
        starts, lens, loffs = starts[:, 0], lens[:, 0], loffs[:, 0]

        cnt = counts[:, 0].astype(jnp.int32)
        padded = ((cnt + TM - 1) // TM) * TM
        ends = jnp.cumsum(padded)
        n_valid = (ends[-1] // TM).astype(jnp.int32)
        tile_starts = jnp.arange(n_tiles, dtype=jnp.int32) * TM
        tile_expert = jnp.sum(tile_starts[:, None] >= ends[None, :], axis=1).astype(jnp.int32)
        last_expert = tile_expert[jnp.maximum(n_valid - 1, 0)]
        tile_expert = jnp.where(tile_starts < ends[-1], tile_expert, last_expert)
        pad_starts = jnp.concatenate([ends - padded + cnt, ends[-1:]])
        pad_lens = jnp.concatenate([padded - cnt, (p_rows - ends[-1:]) // (TM // 2)])

        xs = _dispatch(starts, lens, loffs, pad_starts, pad_lens, lp, h2.reshape(t, d), p_rows)
        eo = _experts(tile_expert, n_valid.reshape(1), xs,
                      moe_w_gate[l], moe_b_gate[l].reshape(N_EXPERTS, N_CHUNKS, N_COLS),
                      moe_w_up[l], moe_b_up[l].reshape(N_EXPERTS, N_CHUNKS, N_COLS),
                      moe_w_down[l], moe_b_down[l].reshape(N_EXPERTS, N_CHUNKS, N_COLS))
        x = _combine(starts, lens, loffs, info, x1.reshape(t, d), final_norm_w[None, :], eo
                     ).reshape(bsz, seqlen, d)
    return x
```

```python
import functools

import jax
import jax.numpy as jnp
from jax import lax
from jax.experimental import pallas as pl
from jax.experimental.pallas import tpu as pltpu

F32 = jnp.float32
BF16 = jnp.bfloat16

D_MODEL = 1024
GLA_WIDTH = 512
GLA_HEADS = 4
GLA_DV = 128
GLA_DK = 64
GLA_KW = 256
GLA_GATE_RANK = 16
GLA_GATE_NORM = 16.0
SSD_WIDTH = 512
SSD_HEADDIM = 64
SSD_HEADS = 8
SSD_GROUPS = 2
SSD_HPG = 4
SSD_STATE = 128
SSD_CONV = 4
SSD_CONV_CH = 1024
N_EXPERTS = 32
TOP_K = 4
SWIGLU_LIMIT = 7.0
SWIGLU_ALPHA = 1.702
EPS = 1e-6
GROUP_EPS = 1e-5

LANES = 128
ROW_TILE = D_MODEL // LANES
N_MAIN = 3072
N_SMALL = LANES
DT_COL = GLA_GATE_RANK

GLA_CHUNK = 64
SSD_CHUNK = 128
TM_PROJ = 512
TL = 256
TT_LOG2 = 8
TT = 1 << TT_LOG2
STAGE_ROWS = 4 * TT
TM_LOG2 = 9
TM = 1 << TM_LOG2
WAIT_ROWS = 512
assert STAGE_ROWS * D_MODEL * 4 <= (1 << 17) * 32
N_COLS = 256
N_CHUNKS = D_MODEL // N_COLS
VMEM_LIMIT = 56 * 1024 * 1024


def _dot(a, b):
    return jnp.dot(a, b, preferred_element_type=F32)


def _dot_nt(a, b):
    return lax.dot_general(a, b, (((1,), (1,)), ((), ())), preferred_element_type=F32)


def _dot_tn(a, b):
    return lax.dot_general(a, b, (((0,), (0,)), ((), ())), preferred_element_type=F32)


def _split3(a):
    hi = a.astype(BF16)
    r1 = a - hi.astype(F32)
    mid = r1.astype(BF16)
    lo = (r1 - mid.astype(F32)).astype(BF16)
    return hi, mid, lo


def _dot_sel_lhs(sel, a):
    hi, mid, lo = _split3(a)
    return _dot(sel, hi) + _dot(sel, mid) + _dot(sel, lo)


def _dot_sel_rhs(a, sel, terms=3):
    parts = _split3(a)[:terms]
    out = _dot(parts[0], sel)
    for p in parts[1:]:
        out = out + _dot(p, sel)
    return out


def _dot_hi(a, b):
    a_hi = a.astype(BF16)
    a_lo = (a - a_hi.astype(F32)).astype(BF16)
    b_hi = b.astype(BF16)
    b_lo = (b - b_hi.astype(F32)).astype(BF16)
    return _dot(a_hi, b_hi) + _dot(a_lo, b_hi) + _dot(a_hi, b_lo)


def _dot_hi_nt(a, b):
    a_hi = a.astype(BF16)
    a_lo = (a - a_hi.astype(F32)).astype(BF16)
    b_hi = b.astype(BF16)
    b_lo = (b - b_hi.astype(F32)).astype(BF16)
    return _dot_nt(a_hi, b_hi) + _dot_nt(a_lo, b_hi) + _dot_nt(a_hi, b_lo)


def _softplus(x):
    return jnp.maximum(x, 0.0) + jnp.log1p(jnp.exp(-jnp.abs(x)))


def _silu(x):
    return x * jax.nn.sigmoid(x)


def _iota(shape, dim):
    return lax.broadcasted_iota(jnp.int32, shape, dim)


def _load_row_tiles(ref, rows):
    return jnp.concatenate([ref[pl.ds(j, rows, stride=ROW_TILE), :] for j in range(ROW_TILE)], axis=1)


def _inproj_kernel(x_ref, nw_ref, w_ref, pm_ref, ps_ref):
    x = x_ref[...]
    ms = jnp.mean(x * x, axis=-1, keepdims=True)
    h = (x * lax.rsqrt(ms + EPS) * nw_ref[...]).astype(BF16)
    step = 512
    for n0 in range(0, N_MAIN, step):
        pm_ref[:, n0:n0 + step] = _dot(h, w_ref[:, n0:n0 + step]).astype(BF16)
    ps_ref[...] = _dot(h, w_ref[:, N_MAIN:N_MAIN + N_SMALL])


def _inproj(x2d, norm_w, w_all):
    t = x2d.shape[0]
    return pl.pallas_call(
        _inproj_kernel,
        grid=(t // TM_PROJ,),
        in_specs=[
            pl.BlockSpec((TM_PROJ, D_MODEL), lambda i: (i, 0)),
            pl.BlockSpec((1, D_MODEL), lambda i: (0, 0)),
            pl.BlockSpec((D_MODEL, N_MAIN + N_SMALL), lambda i: (0, 0)),
        ],
        out_specs=[
            pl.BlockSpec((TM_PROJ, N_MAIN), lambda i: (i, 0)),
            pl.BlockSpec((TM_PROJ, N_SMALL), lambda i: (i, 0)),
        ],
        out_shape=[
            jax.ShapeDtypeStruct((t, N_MAIN), BF16),
            jax.ShapeDtypeStruct((t, N_SMALL), F32),
        ],
        compiler_params=pltpu.CompilerParams(
            dimension_semantics=("arbitrary",), vmem_limit_bytes=VMEM_LIMIT),
        name="inproj",
    )(x2d, norm_w, w_all)


def _mixer_kernel(pm_ref, ps_ref, x_ref, wup_ref, balpha_ref, gnw_ref, convw_ref, convb_ref,
                  dtb_ref, aneg_ref, dexp_ref, snw_ref, wout_ref, nfw_ref, rw_ref, rb_ref,
                  x1_ref, h2_ref, lg_ref, cnt_ref,
                  gla_state, ssd_state, conv_tail, mix_scr, cnt_scr):
    @pl.when((pl.program_id(0) == 0) & (pl.program_id(1) == 0))
    def _():
        cnt_scr[...] = jnp.zeros_like(cnt_scr)

    @pl.when(pl.program_id(1) == 0)
    def _():
        gla_state[...] = jnp.zeros_like(gla_state)
        ssd_state[...] = jnp.zeros_like(ssd_state)
        conv_tail[...] = jnp.zeros_like(conv_tail)

    small = ps_ref[...]

    row = _iota((TL, TL), 0)
    col = _iota((TL, TL), 1)
    causal = col <= row
    cum64 = jnp.where(causal & ((row // GLA_CHUNK) == (col // GLA_CHUNK)), 1.0, 0.0).astype(BF16)
    cum128 = jnp.where(causal & ((row // SSD_CHUNK) == (col // SSD_CHUNK)), 1.0, 0.0).astype(BF16)

    xa = _dot_hi(small, wup_ref[...]) + balpha_ref[...]
    log_a = (jnp.minimum(xa, 0.0) - jnp.log1p(jnp.exp(-jnp.abs(xa)))) * (1.0 / GLA_GATE_NORM)
    bcum = _dot_sel_lhs(cum64, log_a)

    lane_kw = _iota((GLA_CHUNK, GLA_KW), 1)
    head_masks = [(lane_kw // GLA_DK) == h for h in range(GLA_HEADS)]
    lane_kw_s = _iota((GLA_DV, GLA_KW), 1)
    head_masks_s = [(lane_kw_s // GLA_DK) == h for h in range(GLA_HEADS)]
    tril64 = _iota((GLA_CHUNK, GLA_CHUNK), 1) <= _iota((GLA_CHUNK, GLA_CHUNK), 0)
    q_scale = GLA_DK ** -0.5

    for c in range(TL // GLA_CHUNK):
        rs = slice(c * GLA_CHUNK, (c + 1) * GLA_CHUNK)
        bc = bcum[rs]
        b_mid = bc[GLA_CHUNK // 2:GLA_CHUNK // 2 + 1]
        b_last = bc[GLA_CHUNK - 1:GLA_CHUNK]
        qc = pm_ref[rs, 0:GLA_KW].astype(F32) * q_scale
        kc = pm_ref[rs, GLA_KW:2 * GLA_KW].astype(F32)
        vc = pm_ref[rs, 2 * GLA_KW:2 * GLA_KW + GLA_WIDTH]
        q_in = (qc * jnp.exp(bc - b_mid)).astype(BF16)
        k_in = (kc * jnp.exp(b_mid - bc)).astype(BF16)
        q_st = (qc * jnp.exp(bc)).astype(BF16)
        k_st = (kc * jnp.exp(b_last - bc)).astype(BF16)
        st = gla_state[...]
        st_b = st.astype(BF16)
        zero_b = jnp.zeros_like(q_in)
        for h in range(GLA_HEADS):
            scores = _dot_nt(jnp.where(head_masks[h], q_in, zero_b), k_in)
            scores = jnp.where(tril64, scores, 0.0).astype(BF16)
            o_h = _dot(scores, vc[:, h * GLA_DV:(h + 1) * GLA_DV])
            o_h = o_h + _dot_nt(jnp.where(head_masks[h], q_st, zero_b), st_b)
            mix_scr[rs, h * GLA_DV:(h + 1) * GLA_DV] = o_h
        upd = _dot_tn(vc, k_st)
        new_st = st * jnp.exp(b_last)
        for h in range(GLA_HEADS):
            new_st = new_st + jnp.where(head_masks_s[h], upd[h * GLA_DV:(h + 1) * GLA_DV], 0.0)
        gla_state[...] = new_st

    xbc = pm_ref[:, 2048:3072].astype(F32)
    tail = conv_tail[...]
    conv_tail[...] = xbc[TL - 8:TL]
    row8 = _iota((8, SSD_CONV_CH), 0)
    conv = xbc * convw_ref[SSD_CONV - 1:SSD_CONV, :]
    for s in range(1, SSD_CONV):
        shifted = pltpu.roll(xbc, s, 0)
        head = jnp.where(row8 < s, pltpu.roll(tail, s, 0), shifted[0:8])
        shifted = jnp.concatenate([head, shifted[8:]], axis=0)
        conv = conv + shifted * convw_ref[SSD_CONV - 1 - s:SSD_CONV - s, :]
    act = _silu(conv + convb_ref[...])
    xs = act[:, 0:SSD_WIDTH]
    bm = act[:, SSD_WIDTH:SSD_WIDTH + SSD_GROUPS * SSD_STATE].astype(BF16)
    cm = act[:, SSD_WIDTH + SSD_GROUPS * SSD_STATE:].astype(BF16)

    dt_full = _softplus(small + dtb_ref[...])
    a_full = dt_full * aneg_ref[...]
    acum = _dot_sel_lhs(cum128, a_full)
    acum_t = acum.T

    e_row = _iota((N_SMALL, SSD_WIDTH), 0)
    e_col = _iota((N_SMALL, SSD_WIDTH), 1)
    spread64 = jnp.where(e_row == DT_COL + e_col // SSD_HEADDIM, 1.0, 0.0).astype(BF16)
    e_row2 = _iota((N_SMALL, SSD_HEADS * LANES), 0)
    e_col2 = _iota((N_SMALL, SSD_HEADS * LANES), 1)
    spread128 = jnp.where(e_row2 == DT_COL + e_col2 // LANES, 1.0, 0.0).astype(BF16)
    dt_e = _dot_sel_rhs(dt_full, spread64, terms=1)
    ac_e = _dot_sel_rhs(acum, spread64, terms=2)
    ac_w = _dot_sel_rhs(acum, spread128, terms=2)

    tril128 = _iota((SSD_CHUNK, SSD_CHUNK), 1) <= _iota((SSD_CHUNK, SSD_CHUNK), 0)
    lane_g = _iota((SSD_CHUNK, SSD_HPG * SSD_HEADDIM), 1)
    for c in range(TL // SSD_CHUNK):
        rs = slice(c * SSD_CHUNK, (c + 1) * SSD_CHUNK)
        ac_c = ac_e[rs]
        a_last = ac_c[SSD_CHUNK - 1:SSD_CHUNK]
        dt_c = dt_e[rs]
        xs_c = xs[rs]
        x_dt = (xs_c * dt_c).astype(BF16)
        x_w = (xs_c * (jnp.exp(a_last - ac_c) * dt_c)).astype(BF16)
        e_ac = jnp.exp(ac_c)
        for g in range(SSD_GROUPS):
            gs = slice(g * SSD_STATE, (g + 1) * SSD_STATE)
            ws = slice(g * SSD_HPG * SSD_HEADDIM, (g + 1) * SSD_HPG * SSD_HEADDIM)
            c_g = cm[rs, gs]
            b_g = bm[rs, gs]
            cb = _dot_nt(c_g, b_g)
            x_dt_g = x_dt[:, ws]
            lhs_parts = []
            rhs_parts = []
            for hh in range(SSD_HPG):
                h = g * SSD_HPG + hh
                seg = ac_w[rs, h * LANES:(h + 1) * LANES] - acum_t[DT_COL + h:DT_COL + h + 1, rs]
                lmat = jnp.where(tril128, jnp.exp(jnp.where(tril128, seg, 0.0)), 0.0)
                lhs_parts.append((cb * lmat).astype(BF16))
                rhs_parts.append(jnp.where((lane_g // SSD_HEADDIM) == hh, x_dt_g,
                                           jnp.zeros_like(x_dt_g)))
            intra = _dot(jnp.concatenate(lhs_parts, axis=1), jnp.concatenate(rhs_parts, axis=0))
            st = ssd_state[g]
            inter = _dot(c_g, st.astype(BF16)) * e_ac[:, ws]
            mix_scr[rs, GLA_WIDTH + g * 256:GLA_WIDTH + (g + 1) * 256] = intra + inter
            ssd_state[g] = st * jnp.exp(a_last[:, ws]) + _dot_tn(b_g, x_w[:, ws])

    o = mix_scr[:, 0:GLA_WIDTH]
    g_gate = _silu(pm_ref[:, 1024:1536].astype(F32))
    gla_parts = []
    for h in range(GLA_HEADS):
        o_h = o[:, h * GLA_DV:(h + 1) * GLA_DV]
        ms = jnp.mean(o_h * o_h, axis=-1, keepdims=True)
        gla_parts.append(o_h * lax.rsqrt(ms + GROUP_EPS))
    gla_out = jnp.concatenate(gla_parts, axis=1) * gnw_ref[...] * g_gate

    y = mix_scr[:, GLA_WIDTH:] + dexp_ref[...] * xs
    y = y * _silu(pm_ref[:, 1536:2048].astype(F32))
    ssd_parts = []
    for g in range(SSD_GROUPS):
        y_g = y[:, g * 256:(g + 1) * 256]
        ms = jnp.mean(y_g * y_g, axis=-1, keepdims=True)
        ssd_parts.append(y_g * lax.rsqrt(ms + GROUP_EPS))
    ssd_out = jnp.concatenate(ssd_parts, axis=1) * snw_ref[...]

    mixed = jnp.concatenate([gla_out, ssd_out], axis=1).astype(BF16)
    x1 = x_ref[...] + _dot(mixed, wout_ref[...])
    x1_ref[...] = x1

    ms = jnp.mean(x1 * x1, axis=-1, keepdims=True)
    h2 = x1 * lax.rsqrt(ms + EPS) * nfw_ref[...]
    h2_ref[...] = h2.astype(BF16)
    lg = _dot_hi_nt(rw_ref[...], h2) + rb_ref[...]
    lg_ref[...] = lg
    cnt_scr[...] = cnt_scr[...] + _tile_counts(_top4(lg[0:N_EXPERTS])[0])[1]
    cnt_ref[...] = cnt_scr[...]


def _mixer(pm, ps, x, wup, balpha, gnw, convw, convb, dtb, aneg, dexp, snw, wout, nfw, rw, rb):
    bsz, seqlen, _ = x.shape

    def full(a):
        return pl.BlockSpec(a.shape, lambda b, l: (0,) * a.ndim)

    def tile(width):
        return pl.BlockSpec((None, TL, width), lambda b, l: (b, l, 0))

    params = (wup, balpha, gnw, convw, convb, dtb, aneg, dexp, snw, wout, nfw, rw, rb)
    return pl.pallas_call(
        _mixer_kernel,
        grid=(bsz, seqlen // TL),
        in_specs=[tile(N_MAIN), tile(N_SMALL), tile(D_MODEL)] + [full(p) for p in params],
        out_specs=[tile(D_MODEL), tile(D_MODEL),
                   pl.BlockSpec((LANES, TL), lambda b, l: (0, b * (seqlen // TL) + l)),
                   pl.BlockSpec((N_EXPERTS, LANES), lambda b, l: (0, 0))],
        out_shape=[
            jax.ShapeDtypeStruct((bsz, seqlen, D_MODEL), F32),
            jax.ShapeDtypeStruct((bsz, seqlen, D_MODEL), BF16),
            jax.ShapeDtypeStruct((LANES, bsz * seqlen), F32),
            jax.ShapeDtypeStruct((N_EXPERTS, LANES), F32),
        ],
        scratch_shapes=[
            pltpu.VMEM((GLA_DV, GLA_KW), F32),
            pltpu.VMEM((SSD_GROUPS, SSD_STATE, SSD_HPG * SSD_HEADDIM), F32),
            pltpu.VMEM((8, SSD_CONV_CH), F32),
            pltpu.VMEM((TL, D_MODEL), F32),
            pltpu.VMEM((N_EXPERTS, LANES), F32),
        ],
        compiler_params=pltpu.CompilerParams(
            dimension_semantics=("arbitrary", "arbitrary"), vmem_limit_bytes=VMEM_LIMIT),
        name="mixer",
    )(pm, ps, x, *params)


def _top4(lg):
    n_e, n_t = lg.shape
    row = _iota((n_e, n_t), 0)
    work = lg
    onehots = []
    vals = []
    for _ in range(TOP_K):
        m = jnp.max(work, axis=0, keepdims=True)
        idx = jnp.min(jnp.where(work == m, row, n_e), axis=0, keepdims=True)
        oh = row == idx
        onehots.append(oh)
        vals.append(m)
        work = jnp.where(oh, -jnp.inf, work)
    return onehots, vals


def _tile_counts(onehots):
    multi = jnp.where(onehots[0] | onehots[1] | onehots[2] | onehots[3], 1.0, 0.0).astype(BF16)
    return multi, _dot(multi, jnp.ones((multi.shape[1], LANES), BF16))


def _route_kernel(lg_ref, cnt_ref, info_ref, lp_ref, start_ref, len_ref, loff_ref, run_scr):
    @pl.when(pl.program_id(0) == 0)
    def _():
        run_scr[...] = jnp.zeros_like(run_scr)

    onehots, vals = _top4(lg_ref[0:N_EXPERTS, :])
    multi, tile_cnt = _tile_counts(onehots)
    counts = cnt_ref[...]
    padded = jnp.floor((counts + (TM - 1)) * (1.0 / TM)) * TM
    lower = jnp.where(_iota((N_EXPERTS, N_EXPERTS), 1) < _iota((N_EXPERTS, N_EXPERTS), 0),
                      1.0, 0.0).astype(BF16)
    offs = _dot_sel_lhs(lower, padded)
    loff = _dot_sel_lhs(lower, tile_cnt)
    before = jnp.where(_iota((TT, TT), 0) < _iota((TT, TT), 1), 1.0, 0.0).astype(BF16)
    rank = _dot(multi, before)
    start_ref[...] = (offs + run_scr[...]).astype(jnp.int32)
    len_ref[...] = tile_cnt.astype(jnp.int32)
    loff_ref[...] = loff.astype(jnp.int32)
    run_scr[...] = run_scr[...] + tile_cnt
    local = rank + jnp.concatenate([loff] * (TT // LANES), axis=1)
    exps = [jnp.exp(v - vals[0]) for v in vals]
    den = exps[0] + exps[1] + exps[2] + exps[3]
    lp_rows = [jnp.sum(jnp.where(oh, local, 0.0), axis=0, keepdims=True) for oh in onehots]
    lp_ref[...] = jnp.concatenate(lp_rows + [jnp.zeros((8 - TOP_K, TT), F32)], axis=0).astype(jnp.int32)
    gate_rows = [e / den for e in exps]
    info = jnp.concatenate(gate_rows + lp_rows + [jnp.zeros((LANES - 2 * TOP_K, TT), F32)], axis=0)
    info_ref[...] = info.T


def _route(logits_t, counts):
    t = logits_t.shape[1]
    steps = t // TT
    run_spec = pl.BlockSpec((N_EXPERTS, LANES), lambda i: (i, 0))
    run_shape = jax.ShapeDtypeStruct((steps * N_EXPERTS, LANES), jnp.int32)
    return pl.pallas_call(
        _route_kernel,
        grid=(steps,),
        in_specs=[pl.BlockSpec((LANES, TT), lambda i: (0, i)),
                  pl.BlockSpec((N_EXPERTS, LANES), lambda i: (0, 0))],
        out_specs=[
            pl.BlockSpec((TT, LANES), lambda i: (i, 0)),
            pl.BlockSpec((8, TT), lambda i: (0, i)),
            run_spec, run_spec, run_spec,
        ],
        out_shape=[
            jax.ShapeDtypeStruct((t, LANES), F32),
            jax.ShapeDtypeStruct((8, t), jnp.int32),
            run_shape, run_shape, run_shape,
        ],
        scratch_shapes=[pltpu.VMEM((N_EXPERTS, LANES), F32)],
        compiler_params=pltpu.CompilerParams(
            dimension_semantics=("arbitrary",), vmem_limit_bytes=VMEM_LIMIT),
        name="route",
    )(logits_t, counts)


def _rows(first, count):
    return pl.ds(pl.multiple_of(first * ROW_TILE, ROW_TILE), count * ROW_TILE)


def _wait_rows(stage_slot, hbm_ref, sem, to_hbm):
    for w in range(STAGE_ROWS // WAIT_ROWS):
        part = stage_slot.at[pl.ds(w * WAIT_ROWS * ROW_TILE, WAIT_ROWS * ROW_TILE), :]
        hbm = hbm_ref.at[pl.ds(0, WAIT_ROWS * ROW_TILE), :]
        src, dst = (part, hbm) if to_hbm else (hbm, part)
        pltpu.make_async_copy(src, dst, sem).wait()


def _for_each_piece(length, max_log2, fn):
    for b in reversed(range(max_log2 + 1)):
        size = 1 << b
        offset = lax.shift_left(lax.shift_right_logical(length, b + 1), b + 1)

        @pl.when((length & size) != 0)
        def _(offset=offset, size=size):
            fn(offset, size)


def _dispatch_kernel(start_ref, len_ref, loff_ref, pstart_ref, plen_ref,
                     lp_ref, h_ref, xs_ref, stage, zeros, sem, zsem):
    j = pl.program_id(0)
    slot = j & 1

    def wait_tile(s):
        _wait_rows(stage.at[s], xs_ref, sem.at[s], to_hbm=True)

    @pl.when(j == 0)
    def _():
        zeros[...] = jnp.zeros_like(zeros)
        for e in range(N_EXPERTS):
            def put(offset, size, e=e):
                cp = pltpu.make_async_copy(zeros.at[pl.ds(0, size * ROW_TILE), :],
                                           xs_ref.at[_rows(pstart_ref[e] + offset, size), :], zsem)
                cp.start()
                cp.wait()
            _for_each_piece(plen_ref[e], TM_LOG2 - 1, put)

        def put_block(blk, carry):
            cp = pltpu.make_async_copy(
                zeros, xs_ref.at[_rows(pstart_ref[N_EXPERTS] + blk * (TM // 2), TM // 2), :], zsem)
            cp.start()
            cp.wait()
            return carry
        lax.fori_loop(0, plen_ref[N_EXPERTS], put_block, 0)

    @pl.when(j >= 2)
    def _():
        wait_tile(slot)

    h = h_ref[...]
    lp = lp_ref[...]
    for c in range(STAGE_ROWS // TT):
        r = _iota((TT, TT), 0) + c * TT
        hit = (r == lp[0:1]) | (r == lp[1:2]) | (r == lp[2:3]) | (r == lp[3:4])
        rows = _dot(jnp.where(hit, 1.0, 0.0).astype(BF16), h)
        for q in range(ROW_TILE):
            stage[slot, pl.ds(c * TT * ROW_TILE + q, TT, stride=ROW_TILE), :] = rows[:, q * LANES:(q + 1) * LANES]

    for e in range(N_EXPERTS):
        src = loff_ref[j * N_EXPERTS + e]
        dst = start_ref[j * N_EXPERTS + e]

        def put(offset, size, src=src, dst=dst):
            pltpu.make_async_copy(stage.at[slot, _rows(src + offset, size), :],
                                  xs_ref.at[_rows(dst + offset, size), :], sem.at[slot]).start()
        _for_each_piece(len_ref[j * N_EXPERTS + e], TT_LOG2, put)

    @pl.when(j == pl.num_programs(0) - 1)
    def _():
        wait_tile(1 - slot)
        wait_tile(slot)


def _dispatch(starts, lens, loffs, pad_starts, pad_lens, lp, h2, p_rows):
    t = h2.shape[0]
    return pl.pallas_call(
        _dispatch_kernel,
        grid_spec=pltpu.PrefetchScalarGridSpec(
            num_scalar_prefetch=5,
            grid=(t // TT,),
            in_specs=[
                pl.BlockSpec((8, TT), lambda j, *_: (0, j)),
                pl.BlockSpec((TT, D_MODEL), lambda j, *_: (j, 0)),
            ],
            out_specs=pl.BlockSpec(memory_space=pl.ANY),
            scratch_shapes=[
                pltpu.VMEM((2, STAGE_ROWS * ROW_TILE, LANES), F32),
                pltpu.VMEM((TM // 2 * ROW_TILE, LANES), F32),
                pltpu.SemaphoreType.DMA((2,)),
                pltpu.SemaphoreType.DMA(()),
            ],
        ),
        out_shape=jax.ShapeDtypeStruct((p_rows * ROW_TILE, LANES), F32),
        compiler_params=pltpu.CompilerParams(
            dimension_semantics=("arbitrary",), vmem_limit_bytes=VMEM_LIMIT),
        name="dispatch",
    )(starts, lens, loffs, pad_starts, pad_lens, lp, h2)


def _experts_kernel(te_ref, nv_ref, x_ref, wg_ref, bg_ref, wu_ref, bu_ref, wd_ref, bd_ref,
                    o_ref, act, wg_b, wu_b, wd_b):
    i = pl.program_id(0)
    prev = jnp.maximum(i - 1, 0)

    @pl.when(i >= nv_ref[0])
    def _():
        o_ref[...] = jnp.zeros_like(o_ref)

    @pl.when(i < nv_ref[0])
    def _():
        @pl.when((i == 0) | (te_ref[i] != te_ref[prev]))
        def _():
            for c in range(N_CHUNKS):
                cs = slice(c * N_COLS, (c + 1) * N_COLS)
                wg_b[c] = wg_ref[:, cs].astype(BF16)
                wu_b[c] = wu_ref[:, cs].astype(BF16)
                wd_b[c] = wd_ref[:, cs].astype(BF16)

        xb = _load_row_tiles(x_ref, TM).astype(BF16)
        for n in range(N_CHUNKS):
            gate = _dot(xb, wg_b[n]) + bg_ref[n:n + 1, :]
            up = _dot(xb, wu_b[n]) + bu_ref[n:n + 1, :]
            gate = jnp.minimum(gate, SWIGLU_LIMIT)
            up = jnp.clip(up, -SWIGLU_LIMIT, SWIGLU_LIMIT)
            act[n] = ((up + 1.0) * (gate * jax.nn.sigmoid(SWIGLU_ALPHA * gate))).astype(BF16)
        a = jnp.concatenate([act[c] for c in range(N_CHUNKS)], axis=1)
        for n in range(N_CHUNKS):
            out = _dot(a, wd_b[n]) + bd_ref[n:n + 1, :]
            for q in range(N_COLS // LANES):
                o_ref[pl.ds(n * (N_COLS // LANES) + q, TM, stride=ROW_TILE), :] = (
                    out[:, q * LANES:(q + 1) * LANES])


def _experts(tile_expert, n_valid, xs, wg, bg, wu, bu, wd, bd):
    n_tiles = xs.shape[0] // (TM * ROW_TILE)

    def x_map(i, te, nv):
        return (jnp.minimum(i, nv[0] - 1), 0)

    def w_map(i, te, nv):
        return (te[i], 0, 0)

    w_spec = pl.BlockSpec((None, D_MODEL, D_MODEL), w_map)
    b_spec = pl.BlockSpec((None, N_CHUNKS, N_COLS), w_map)
    return pl.pallas_call(
        _experts_kernel,
        grid_spec=pltpu.PrefetchScalarGridSpec(
            num_scalar_prefetch=2,
            grid=(n_tiles,),
            in_specs=[pl.BlockSpec((TM * ROW_TILE, LANES), x_map),
                      w_spec, b_spec, w_spec, b_spec, w_spec, b_spec],
            out_specs=pl.BlockSpec((TM * ROW_TILE, LANES), lambda i, te, nv: (i, 0)),
            scratch_shapes=[
                pltpu.VMEM((N_CHUNKS, TM, N_COLS), BF16),
                pltpu.VMEM((N_CHUNKS, D_MODEL, N_COLS), BF16),
                pltpu.VMEM((N_CHUNKS, D_MODEL, N_COLS), BF16),
                pltpu.VMEM((N_CHUNKS, D_MODEL, N_COLS), BF16),
            ],
        ),
        out_shape=jax.ShapeDtypeStruct(xs.shape, F32),
        compiler_params=pltpu.CompilerParams(
            dimension_semantics=("arbitrary",), vmem_limit_bytes=VMEM_LIMIT),
        name="experts",
    )(tile_expert, n_valid, xs, wg, bg, wu, bu, wd, bd)


def _combine_kernel(start_ref, len_ref, loff_ref, info_ref, x1_ref, fw_ref, eo_ref, out_ref, stage, sem):
    j = pl.program_id(0)
    slot = j & 1
    last = pl.num_programs(0) - 1

    def fetch(tile, s):
        for e in range(N_EXPERTS):
            src = start_ref[tile * N_EXPERTS + e]
            dst = loff_ref[tile * N_EXPERTS + e]

            def get(offset, size, src=src, dst=dst):
                pltpu.make_async_copy(eo_ref.at[_rows(src + offset, size), :],
                                      stage.at[s, _rows(dst + offset, size), :], sem.at[s]).start()
            _for_each_piece(len_ref[tile * N_EXPERTS + e], TT_LOG2, get)

    @pl.when(j == 0)
    def _():
        fetch(0, 0)

    @pl.when(j < last)
    def _():
        fetch(j + 1, 1 - slot)

    _wait_rows(stage.at[slot], eo_ref, sem.at[slot], to_hbm=False)

    info = info_ref[...]
    y = x1_ref[...]
    for c in range(STAGE_ROWS // TT):
        r = (_iota((TT, TT), 1) + c * TT).astype(F32)
        g = jnp.zeros((TT, TT), F32)
        for k in range(TOP_K):
            g = g + jnp.where(r == info[:, TOP_K + k:TOP_K + k + 1], info[:, k:k + 1], 0.0)
        g_hi = g.astype(BF16)
        g_lo = (g - g_hi.astype(F32)).astype(BF16)
        rows = jnp.concatenate(
            [stage[slot, pl.ds(c * TT * ROW_TILE + q, TT, stride=ROW_TILE), :] for q in range(ROW_TILE)],
            axis=1).astype(BF16)
        y = y + _dot(g_hi, rows) + _dot(g_lo, rows)
    ms = jnp.mean(y * y, axis=-1, keepdims=True)
    out_ref[...] = y * lax.rsqrt(ms + EPS) * fw_ref[...]


def _combine(starts, lens, loffs, info, x1, final_w, eo):
    t = x1.shape[0]
    return pl.pallas_call(
        _combine_kernel,
        grid_spec=pltpu.PrefetchScalarGridSpec(
            num_scalar_prefetch=3,
            grid=(t // TT,),
            in_specs=[
                pl.BlockSpec((TT, LANES), lambda j, *_: (j, 0)),
                pl.BlockSpec((TT, D_MODEL), lambda j, *_: (j, 0)),
                pl.BlockSpec((1, D_MODEL), lambda j, *_: (0, 0)),
                pl.BlockSpec(memory_space=pl.ANY),
            ],
            out_specs=pl.BlockSpec((TT, D_MODEL), lambda j, *_: (j, 0)),
            scratch_shapes=[
                pltpu.VMEM((2, STAGE_ROWS * ROW_TILE, LANES), F32),
                pltpu.SemaphoreType.DMA((2,)),
            ],
        ),
        out_shape=jax.ShapeDtypeStruct((t, D_MODEL), F32),
        compiler_params=pltpu.CompilerParams(
            dimension_semantics=("arbitrary",), vmem_limit_bytes=VMEM_LIMIT),
        name="combine",
    )(starts, lens, loffs, info, x1, final_w, eo)


def _pad_lanes(v, offset, fill=0.0):
    row = jnp.full((1, LANES), fill, F32)
    return row.at[0, offset:offset + v.shape[0]].set(v.astype(F32))


def kernel(x, norm_mix_w, w_in, gla_w_alpha_up, gla_b_alpha, gla_norm_w, ssd_conv_w, ssd_conv_b,
           ssd_dt_bias, ssd_A_log, ssd_D, ssd_norm_w, w_out, norm_ffn_w, router_w, router_b,
           moe_w_gate, moe_b_gate, moe_w_up, moe_b_up, moe_w_down, moe_b_down, final_norm_w):
    bsz, seqlen, d = x.shape
    t = bsz * seqlen
    depth = w_in.shape[0]
    assert depth == 1, "the final RMSNorm is fused into the (single) layer's combine step"
    p_rows = t * TOP_K + N_EXPERTS * TM
    n_tiles = p_rows // TM
    for l in range(depth):
        w = w_in[l]
        w_all = jnp.concatenate(
            [w[:, 0:1536], w[:, 1552:3088], w[:, 1536:1552], w[:, 3088:3096],
             jnp.zeros((d, N_SMALL - GLA_GATE_RANK - SSD_HEADS), w.dtype)], axis=1).astype(BF16)
        wup = jnp.zeros((N_SMALL, GLA_KW), F32).at[0:GLA_GATE_RANK].set(gla_w_alpha_up[l])
        dtb = _pad_lanes(ssd_dt_bias[l], DT_COL)
        aneg = _pad_lanes(-jnp.exp(ssd_A_log[l].astype(F32)), DT_COL)
        dexp = jnp.repeat(ssd_D[l].astype(F32), SSD_HEADDIM)[None, :]
        rw = jnp.zeros((LANES, d), F32).at[0:N_EXPERTS].set(router_w[l].T)
        rb = jnp.zeros((LANES, TL), F32).at[0:N_EXPERTS].set(
            jnp.broadcast_to(router_b[l].astype(F32)[:, None], (N_EXPERTS, TL)))

        pm, ps = _inproj(x.reshape(t, d), norm_mix_w[l][None, :], w_all)
        x1, h2, logits, counts = _mixer(
            pm.reshape(bsz, seqlen, N_MAIN), ps.reshape(bsz, seqlen, N_SMALL), x,
            wup, gla_b_alpha[l][None, :], gla_norm_w[l][None, :], ssd_conv_w[l],
            ssd_conv_b[l][None, :], dtb, aneg, dexp, ssd_norm_w[l][None, :],
            w_out[l].astype(BF16), norm_ffn_w[l][None, :], rw, rb)

        info, lp, starts, lens, loffs = _route(logits, counts)
        starts, lens, loffs = starts[:, 0], lens[:, 0], loffs[:, 0]

        cnt = counts[:, 0].astype(jnp.int32)
        padded = ((cnt + TM - 1) // TM) * TM
        ends = jnp.cumsum(padded)
        n_valid = (ends[-1] // TM).astype(jnp.int32)
        tile_starts = jnp.arange(n_tiles, dtype=jnp.int32) * TM
        tile_expert = jnp.sum(tile_starts[:, None] >= ends[None, :], axis=1).astype(jnp.int32)
        last_expert = tile_expert[jnp.maximum(n_valid - 1, 0)]
        tile_expert = jnp.where(tile_starts < ends[-1], tile_expert, last_expert)
        pad_starts = jnp.concatenate([ends - padded + cnt, ends[-1:]])
        pad_lens = jnp.concatenate([padded - cnt, (p_rows - ends[-1:]) // (TM // 2)])

        xs = _dispatch(starts, lens, loffs, pad_starts, pad_lens, lp, h2.reshape(t, d), p_rows)
        eo = _experts(tile_expert, n_valid.reshape(1), xs,
                      moe_w_gate[l], moe_b_gate[l].reshape(N_EXPERTS, N_CHUNKS, N_COLS),
                      moe_w_up[l], moe_b_up[l].reshape(N_EXPERTS, N_CHUNKS, N_COLS),
                      moe_w_down[l], moe_b_down[l].reshape(N_EXPERTS, N_CHUNKS, N_COLS))
        x = _combine(starts, lens, loffs, info, x1.reshape(t, d), final_norm_w[None, :], eo
                     ).reshape(bsz, seqlen, d)
    return x
```

```python
import functools

import jax
import jax.numpy as jnp
from jax import lax
from jax.experimental import pallas as pl
from jax.experimental.pallas import tpu as pltpu

F32 = jnp.float32
BF16 = jnp.bfloat16

D_MODEL = 1024
GLA_WIDTH = 512
GLA_HEADS = 4
GLA_DV = 128
GLA_DK = 64
GLA_KW = 256
GLA_GATE_RANK = 16
GLA_GATE_NORM = 16.0
SSD_WIDTH = 512
SSD_HEADDIM = 64
SSD_HEADS = 8
SSD_GROUPS = 2
SSD_HPG = 4
SSD_STATE = 128
SSD_CONV = 4
SSD_CONV_CH = 1024
N_EXPERTS = 32
TOP_K = 4
SWIGLU_LIMIT = 7.0
SWIGLU_ALPHA = 1.702
EPS = 1e-6
GROUP_EPS = 1e-5

LANES = 128
ROW_TILE = D_MODEL // LANES
N_MAIN = 3072
N_SMALL = LANES
DT_COL = GLA_GATE_RANK

GLA_CHUNK = 64
SSD_CHUNK = 128
TM_PROJ = 512
TL = 256
TT_LOG2 = 8
TT = 1 << TT_LOG2
STAGE_ROWS = 4 * TT
TM_LOG2 = 9
TM = 1 << TM_LOG2
WAIT_ROWS = 512
assert STAGE_ROWS * D_MODEL * 4 <= (1 << 17) * 32
N_COLS = 256
N_CHUNKS = D_MODEL // N_COLS
VMEM_LIMIT = 56 * 1024 * 1024


def _dot(a, b):
    return jnp.dot(a, b, preferred_element_type=F32)


def _dot_nt(a, b):
    return lax.dot_general(a, b, (((1,), (1,)), ((), ())), preferred_element_type=F32)


def _dot_tn(a, b):
    return lax.dot_general(a, b, (((0,), (0,)), ((), ())), preferred_element_type=F32)


def _split3(a):
    hi = a.astype(BF16)
    r1 = a - hi.astype(F32)
    mid = r1.astype(BF16)
    lo = (r1 - mid.astype(F32)).astype(BF16)
    return hi, mid, lo


def _dot_sel_lhs(sel, a):
    hi, mid, lo = _split3(a)
    return _dot(sel, hi) + _dot(sel, mid) + _dot(sel, lo)


def _dot_sel_rhs(a, sel, terms=3):
    parts = _split3(a)[:terms]
    out = _dot(parts[0], sel)
    for p in parts[1:]:
        out = out + _dot(p, sel)
    return out


def _dot_hi(a, b):
    a_hi = a.astype(BF16)
    a_lo = (a - a_hi.astype(F32)).astype(BF16)
    b_hi = b.astype(BF16)
    b_lo = (b - b_hi.astype(F32)).astype(BF16)
    return _dot(a_hi, b_hi) + _dot(a_lo, b_hi) + _dot(a_hi, b_lo)


def _dot_hi_nt(a, b):
    a_hi = a.astype(BF16)
    a_lo = (a - a_hi.astype(F32)).astype(BF16)
    b_hi = b.astype(BF16)
    b_lo = (b - b_hi.astype(F32)).astype(BF16)
    return _dot_nt(a_hi, b_hi) + _dot_nt(a_lo, b_hi) + _dot_nt(a_hi, b_lo)


def _softplus(x):
    return jnp.maximum(x, 0.0) + jnp.log1p(jnp.exp(-jnp.abs(x)))


def _silu(x):
    return x * jax.nn.sigmoid(x)


def _iota(shape, dim):
    return lax.broadcasted_iota(jnp.int32, shape, dim)


def _load_row_tiles(ref, rows):
    return jnp.concatenate([ref[pl.ds(j, rows, stride=ROW_TILE), :] for j in range(ROW_TILE)], axis=1)


def _inproj_kernel(x_ref, nw_ref, w_ref, pm_ref, ps_ref):
    x = x_ref[...]
    ms = jnp.mean(x * x, axis=-1, keepdims=True)
    h = (x * lax.rsqrt(ms + EPS) * nw_ref[...]).astype(BF16)
    step = 512
    for n0 in range(0, N_MAIN, step):
        pm_ref[:, n0:n0 + step] = _dot(h, w_ref[:, n0:n0 + step]).astype(BF16)
    ps_ref[...] = _dot(h, w_ref[:, N_MAIN:N_MAIN + N_SMALL])


def _inproj(x2d, norm_w, w_all):
    t = x2d.shape[0]
    return pl.pallas_call(
        _inproj_kernel,
        grid=(t // TM_PROJ,),
        in_specs=[
            pl.BlockSpec((TM_PROJ, D_MODEL), lambda i: (i, 0)),
            pl.BlockSpec((1, D_MODEL), lambda i: (0, 0)),
            pl.BlockSpec((D_MODEL, N_MAIN + N_SMALL), lambda i: (0, 0)),
        ],
        out_specs=[
            pl.BlockSpec((TM_PROJ, N_MAIN), lambda i: (i, 0)),
            pl.BlockSpec((TM_PROJ, N_SMALL), lambda i: (i, 0)),
        ],
        out_shape=[
            jax.ShapeDtypeStruct((t, N_MAIN), BF16),
            jax.ShapeDtypeStruct((t, N_SMALL), F32),
        ],
        compiler_params=pltpu.CompilerParams(
            dimension_semantics=("arbitrary",), vmem_limit_bytes=VMEM_LIMIT),
        name="inproj",
    )(x2d, norm_w, w_all)


def _mixer_kernel(pm_ref, ps_ref, x_ref, wup_ref, balpha_ref, gnw_ref, convw_ref, convb_ref,
                  dtb_ref, aneg_ref, dexp_ref, snw_ref, wout_ref, nfw_ref, rw_ref, rb_ref,
                  x1_ref, h2_ref, lg_ref, cnt_ref,
                  gla_state, ssd_state, conv_tail, mix_scr, cnt_scr):
    @pl.when((pl.program_id(0) == 0) & (pl.program_id(1) == 0))
    def _():
        cnt_scr[...] = jnp.zeros_like(cnt_scr)

    @pl.when(pl.program_id(1) == 0)
    def _():
        gla_state[...] = jnp.zeros_like(gla_state)
        ssd_state[...] = jnp.zeros_like(ssd_state)
        conv_tail[...] = jnp.zeros_like(conv_tail)

    small = ps_ref[...]

    row = _iota((TL, TL), 0)
    col = _iota((TL, TL), 1)
    causal = col <= row
    cum64 = jnp.where(causal & ((row // GLA_CHUNK) == (col // GLA_CHUNK)), 1.0, 0.0).astype(BF16)
    cum128 = jnp.where(causal & ((row // SSD_CHUNK) == (col // SSD_CHUNK)), 1.0, 0.0).astype(BF16)

    xa = _dot_hi(small, wup_ref[...]) + balpha_ref[...]
    log_a = (jnp.minimum(xa, 0.0) - jnp.log1p(jnp.exp(-jnp.abs(xa)))) * (1.0 / GLA_GATE_NORM)
    bcum = _dot_sel_lhs(cum64, log_a)

    lane_kw = _iota((GLA_CHUNK, GLA_KW), 1)
    head_masks = [(lane_kw // GLA_DK) == h for h in range(GLA_HEADS)]
    lane_kw_s = _iota((GLA_DV, GLA_KW), 1)
    head_masks_s = [(lane_kw_s // GLA_DK) == h for h in range(GLA_HEADS)]
    tril64 = _iota((GLA_CHUNK, GLA_CHUNK), 1) <= _iota((GLA_CHUNK, GLA_CHUNK), 0)
    q_scale = GLA_DK ** -0.5

    for c in range(TL // GLA_CHUNK):
        rs = slice(c * GLA_CHUNK, (c + 1) * GLA_CHUNK)
        bc = bcum[rs]
        b_mid = bc[GLA_CHUNK // 2:GLA_CHUNK // 2 + 1]
        b_last = bc[GLA_CHUNK - 1:GLA_CHUNK]
        qc = pm_ref[rs, 0:GLA_KW].astype(F32) * q_scale
        kc = pm_ref[rs, GLA_KW:2 * GLA_KW].astype(F32)
        vc = pm_ref[rs, 2 * GLA_KW:2 * GLA_KW + GLA_WIDTH]
        q_in = (qc * jnp.exp(bc - b_mid)).astype(BF16)
        k_in = (kc * jnp.exp(b_mid - bc)).astype(BF16)
        q_st = (qc * jnp.exp(bc)).astype(BF16)
        k_st = (kc * jnp.exp(b_last - bc)).astype(BF16)
        st = gla_state[...]
        st_b = st.astype(BF16)
        zero_b = jnp.zeros_like(q_in)
        for h in range(GLA_HEADS):
            scores = _dot_nt(jnp.where(head_masks[h], q_in, zero_b), k_in)
            scores = jnp.where(tril64, scores, 0.0).astype(BF16)
            o_h = _dot(scores, vc[:, h * GLA_DV:(h + 1) * GLA_DV])
            o_h = o_h + _dot_nt(jnp.where(head_masks[h], q_st, zero_b), st_b)
            mix_scr[rs, h * GLA_DV:(h + 1) * GLA_DV] = o_h
        upd = _dot_tn(vc, k_st)
        new_st = st * jnp.exp(b_last)
        for h in range(GLA_HEADS):
            new_st = new_st + jnp.where(head_masks_s[h], upd[h * GLA_DV:(h + 1) * GLA_DV], 0.0)
        gla_state[...] = new_st

    xbc = pm_ref[:, 2048:3072].astype(F32)
    tail = conv_tail[...]
    conv_tail[...] = xbc[TL - 8:TL]
    row8 = _iota((8, SSD_CONV_CH), 0)
    conv = xbc * convw_ref[SSD_CONV - 1:SSD_CONV, :]
    for s in range(1, SSD_CONV):
        shifted = pltpu.roll(xbc, s, 0)
        head = jnp.where(row8 < s, pltpu.roll(tail, s, 0), shifted[0:8])
        shifted = jnp.concatenate([head, shifted[8:]], axis=0)
        conv = conv + shifted * convw_ref[SSD_CONV - 1 - s:SSD_CONV - s, :]
    act = _silu(conv + convb_ref[...])
    xs = act[:, 0:SSD_WIDTH]
    bm = act[:, SSD_WIDTH:SSD_WIDTH + SSD_GROUPS * SSD_STATE].astype(BF16)
    cm = act[:, SSD_WIDTH + SSD_GROUPS * SSD_STATE:].astype(BF16)

    dt_full = _softplus(small + dtb_ref[...])
    a_full = dt_full * aneg_ref[...]
    acum = _dot_sel_lhs(cum128, a_full)
    acum_t = acum.T

    e_row = _iota((N_SMALL, SSD_WIDTH), 0)
    e_col = _iota((N_SMALL, SSD_WIDTH), 1)
    spread64 = jnp.where(e_row == DT_COL + e_col // SSD_HEADDIM, 1.0, 0.0).astype(BF16)
    e_row2 = _iota((N_SMALL, SSD_HEADS * LANES), 0)
    e_col2 = _iota((N_SMALL, SSD_HEADS * LANES), 1)
    spread128 = jnp.where(e_row2 == DT_COL + e_col2 // LANES, 1.0, 0.0).astype(BF16)
    dt_e = _dot_sel_rhs(dt_full, spread64, terms=1)
    ac_e = _dot_sel_rhs(acum, spread64, terms=2)
    ac_w = _dot_sel_rhs(acum, spread128, terms=2)

    tril128 = _iota((SSD_CHUNK, SSD_CHUNK), 1) <= _iota((SSD_CHUNK, SSD_CHUNK), 0)
    lane_g = _iota((SSD_CHUNK, SSD_HPG * SSD_HEADDIM), 1)
    for c in range(TL // SSD_CHUNK):
        rs = slice(c * SSD_CHUNK, (c + 1) * SSD_CHUNK)
        ac_c = ac_e[rs]
        a_last = ac_c[SSD_CHUNK - 1:SSD_CHUNK]
        dt_c = dt_e[rs]
        xs_c = xs[rs]
        x_dt = (xs_c * dt_c).astype(BF16)
        x_w = (xs_c * (jnp.exp(a_last - ac_c) * dt_c)).astype(BF16)
        e_ac = jnp.exp(ac_c)
        for g in range(SSD_GROUPS):
            gs = slice(g * SSD_STATE, (g + 1) * SSD_STATE)
            ws = slice(g * SSD_HPG * SSD_HEADDIM, (g + 1) * SSD_HPG * SSD_HEADDIM)
            c_g = cm[rs, gs]
            b_g = bm[rs, gs]
            cb = _dot_nt(c_g, b_g)
            x_dt_g = x_dt[:, ws]
            lhs_parts = []
            rhs_parts = []
            for hh in range(SSD_HPG):
                h = g * SSD_HPG + hh
                seg = ac_w[rs, h * LANES:(h + 1) * LANES] - acum_t[DT_COL + h:DT_COL + h + 1, rs]
                lmat = jnp.where(tril128, jnp.exp(jnp.where(tril128, seg, 0.0)), 0.0)
                lhs_parts.append((cb * lmat).astype(BF16))
                rhs_parts.append(jnp.where((lane_g // SSD_HEADDIM) == hh, x_dt_g,
                                           jnp.zeros_like(x_dt_g)))
            intra = _dot(jnp.concatenate(lhs_parts, axis=1), jnp.concatenate(rhs_parts, axis=0))
            st = ssd_state[g]
            inter = _dot(c_g, st.astype(BF16)) * e_ac[:, ws]
            mix_scr[rs, GLA_WIDTH + g * 256:GLA_WIDTH + (g + 1) * 256] = intra + inter
            ssd_state[g] = st * jnp.exp(a_last[:, ws]) + _dot_tn(b_g, x_w[:, ws])

    o = mix_scr[:, 0:GLA_WIDTH]
    g_gate = _silu(pm_ref[:, 1024:1536].astype(F32))
    gla_parts = []
    for h in range(GLA_HEADS):
        o_h = o[:, h * GLA_DV:(h + 1) * GLA_DV]
        ms = jnp.mean(o_h * o_h, axis=-1, keepdims=True)
        gla_parts.append(o_h * lax.rsqrt(ms + GROUP_EPS))
    gla_out = jnp.concatenate(gla_parts, axis=1) * gnw_ref[...] * g_gate

    y = mix_scr[:, GLA_WIDTH:] + dexp_ref[...] * xs
    y = y * _silu(pm_ref[:, 1536:2048].astype(F32))
    ssd_parts = []
    for g in range(SSD_GROUPS):
        y_g = y[:, g * 256:(g + 1) * 256]
        ms = jnp.mean(y_g * y_g, axis=-1, keepdims=True)
        ssd_parts.append(y_g * lax.rsqrt(ms + GROUP_EPS))
    ssd_out = jnp.concatenate(ssd_parts, axis=1) * snw_ref[...]

    mixed = jnp.concatenate([gla_out, ssd_out], axis=1).astype(BF16)
    x1 = x_ref[...] + _dot(mixed, wout_ref[...])
    x1_ref[...] = x1

    ms = jnp.mean(x1 * x1, axis=-1, keepdims=True)
    h2 = x1 * lax.rsqrt(ms + EPS) * nfw_ref[...]
    h2_ref[...] = h2.astype(BF16)
    lg = _dot_hi_nt(rw_ref[...], h2) + rb_ref[...]
    lg_ref[...] = lg
    cnt_scr[...] = cnt_scr[...] + _tile_counts(_top4(lg[0:N_EXPERTS])[0])[1]
    cnt_ref[...] = cnt_scr[...]


def _mixer(pm, ps, x, wup, balpha, gnw, convw, convb, dtb, aneg, dexp, snw, wout, nfw, rw, rb):
    bsz, seqlen, _ = x.shape

    def full(a):
        return pl.BlockSpec(a.shape, lambda b, l: (0,) * a.ndim)

    def tile(width):
        return pl.BlockSpec((None, TL, width), lambda b, l: (b, l, 0))

    params = (wup, balpha, gnw, convw, convb, dtb, aneg, dexp, snw, wout, nfw, rw, rb)
    return pl.pallas_call(
        _mixer_kernel,
        grid=(bsz, seqlen // TL),
        in_specs=[tile(N_MAIN), tile(N_SMALL), tile(D_MODEL)] + [full(p) for p in params],
        out_specs=[tile(D_MODEL), tile(D_MODEL),
                   pl.BlockSpec((LANES, TL), lambda b, l: (0, b * (seqlen // TL) + l)),
                   pl.BlockSpec((N_EXPERTS, LANES), lambda b, l: (0, 0))],
        out_shape=[
            jax.ShapeDtypeStruct((bsz, seqlen, D_MODEL), F32),
            jax.ShapeDtypeStruct((bsz, seqlen, D_MODEL), BF16),
            jax.ShapeDtypeStruct((LANES, bsz * seqlen), F32),
            jax.ShapeDtypeStruct((N_EXPERTS, LANES), F32),
        ],
        scratch_shapes=[
            pltpu.VMEM((GLA_DV, GLA_KW), F32),
            pltpu.VMEM((SSD_GROUPS, SSD_STATE, SSD_HPG * SSD_HEADDIM), F32),
            pltpu.VMEM((8, SSD_CONV_CH), F32),
            pltpu.VMEM((TL, D_MODEL), F32),
            pltpu.VMEM((N_EXPERTS, LANES), F32),
        ],
        compiler_params=pltpu.CompilerParams(
            dimension_semantics=("arbitrary", "arbitrary"), vmem_limit_bytes=VMEM_LIMIT),
        name="mixer",
    )(pm, ps, x, *params)


def _top4(lg):
    n_e, n_t = lg.shape
    row = _iota((n_e, n_t), 0)
    work = lg
    onehots = []
    vals = []
    for _ in range(TOP_K):
        m = jnp.max(work, axis=0, keepdims=True)
        idx = jnp.min(jnp.where(work == m, row, n_e), axis=0, keepdims=True)
        oh = row == idx
        onehots.append(oh)
        vals.append(m)
        work = jnp.where(oh, -jnp.inf, work)
    return onehots, vals


def _tile_counts(onehots):
    multi = jnp.where(onehots[0] | onehots[1] | onehots[2] | onehots[3], 1.0, 0.0).astype(BF16)
    return multi, _dot(multi, jnp.ones((multi.shape[1], LANES), BF16))


def _route_kernel(lg_ref, cnt_ref, info_ref, lp_ref, start_ref, len_ref, loff_ref, run_scr):
    @pl.when(pl.program_id(0) == 0)
    def _():
        run_scr[...] = jnp.zeros_like(run_scr)

    onehots, vals = _top4(lg_ref[0:N_EXPERTS, :])
    multi, tile_cnt = _tile_counts(onehots)
    counts = cnt_ref[...]
    padded = jnp.floor((counts + (TM - 1)) * (1.0 / TM)) * TM
    lower = jnp.where(_iota((N_EXPERTS, N_EXPERTS), 1) < _iota((N_EXPERTS, N_EXPERTS), 0),
                      1.0, 0.0).astype(BF16)
    offs = _dot_sel_lhs(lower, padded)
    loff = _dot_sel_lhs(lower, tile_cnt)
    before = jnp.where(_iota((TT, TT), 0) < _iota((TT, TT), 1), 1.0, 0.0).astype(BF16)
    rank = _dot(multi, before)
    start_ref[...] = (offs + run_scr[...]).astype(jnp.int32)
    len_ref[...] = tile_cnt.astype(jnp.int32)
    loff_ref[...] = loff.astype(jnp.int32)
    run_scr[...] = run_scr[...] + tile_cnt
    local = rank + jnp.concatenate([loff] * (TT // LANES), axis=1)
    exps = [jnp.exp(v - vals[0]) for v in vals]
    den = exps[0] + exps[1] + exps[2] + exps[3]
    lp_rows = [jnp.sum(jnp.where(oh, local, 0.0), axis=0, keepdims=True) for oh in onehots]
    lp_ref[...] = jnp.concatenate(lp_rows + [jnp.zeros((8 - TOP_K, TT), F32)], axis=0).astype(jnp.int32)
    gate_rows = [e / den for e in exps]
    info = jnp.concatenate(gate_rows + lp_rows + [jnp.zeros((LANES - 2 * TOP_K, TT), F32)], axis=0)
    info_ref[...] = info.T


def _route(logits_t, counts):
    t = logits_t.shape[1]
    steps = t // TT
    run_spec = pl.BlockSpec((N_EXPERTS, LANES), lambda i: (i, 0))
    run_shape = jax.ShapeDtypeStruct((steps * N_EXPERTS, LANES), jnp.int32)
    return pl.pallas_call(
        _route_kernel,
        grid=(steps,),
        in_specs=[pl.BlockSpec((LANES, TT), lambda i: (0, i)),
                  pl.BlockSpec((N_EXPERTS, LANES), lambda i: (0, 0))],
        out_specs=[
            pl.BlockSpec((TT, LANES), lambda i: (i, 0)),
            pl.BlockSpec((8, TT), lambda i: (0, i)),
            run_spec, run_spec, run_spec,
        ],
        out_shape=[
            jax.ShapeDtypeStruct((t, LANES), F32),
            jax.ShapeDtypeStruct((8, t), jnp.int32),
            run_shape, run_shape, run_shape,
        ],
        scratch_shapes=[pltpu.VMEM((N_EXPERTS, LANES), F32)],
        compiler_params=pltpu.CompilerParams(
            dimension_semantics=("arbitrary",), vmem_limit_bytes=VMEM_LIMIT),
        name="route",
    )(logits_t, counts)


def _rows(first, count):
    return pl.ds(pl.multiple_of(first * ROW_TILE, ROW_TILE), count * ROW_TILE)


def _wait_rows(stage_slot, hbm_ref, sem, to_hbm):
    for w in range(STAGE_ROWS // WAIT_ROWS):
        part = stage_slot.at[pl.ds(w * WAIT_ROWS * ROW_TILE, WAIT_ROWS * ROW_TILE), :]
        hbm = hbm_ref.at[pl.ds(0, WAIT_ROWS * ROW_TILE), :]
        src, dst = (part, hbm) if to_hbm else (hbm, part)
        pltpu.make_async_copy(src, dst, sem).wait()


def _for_each_piece(length, max_log2, fn):
    for b in reversed(range(max_log2 + 1)):
        size = 1 << b
        offset = lax.shift_left(lax.shift_right_logical(length, b + 1), b + 1)

        @pl.when((length & size) != 0)
        def _(offset=offset, size=size):
            fn(offset, size)


def _dispatch_kernel(start_ref, len_ref, loff_ref, pstart_ref, plen_ref,
                     lp_ref, h_ref, xs_ref, stage, zeros, sem, zsem):
    j = pl.program_id(0)
    slot = j & 1

    def wait_tile(s):
        _wait_rows(stage.at[s], xs_ref, sem.at[s], to_hbm=True)

    @pl.when(j == 0)
    def _():
        zeros[...] = jnp.zeros_like(zeros)
        for e in range(N_EXPERTS):
            def put(offset, size, e=e):
                cp = pltpu.make_async_copy(zeros.at[pl.ds(0, size * ROW_TILE), :],
                                           xs_ref.at[_rows(pstart_ref[e] + offset, size), :], zsem)
                cp.start()
                cp.wait()
            _for_each_piece(plen_ref[e], TM_LOG2 - 1, put)

        def put_block(blk, carry):
            cp = pltpu.make_async_copy(
                zeros, xs_ref.at[_rows(pstart_ref[N_EXPERTS] + blk * (TM // 2), TM // 2), :], zsem)
            cp.start()
            cp.wait()
            return carry
        lax.fori_loop(0, plen_ref[N_EXPERTS], put_block, 0)

    @pl.when(j >= 2)
    def _():
        wait_tile(slot)

    h = h_ref[...]
    lp = lp_ref[...]
    for c in range(STAGE_ROWS // TT):
        r = _iota((TT, TT), 0) + c * TT
        hit = (r == lp[0:1]) | (r == lp[1:2]) | (r == lp[2:3]) | (r == lp[3:4])
        rows = _dot(jnp.where(hit, 1.0, 0.0).astype(BF16), h)
        for q in range(ROW_TILE):
            stage[slot, pl.ds(c * TT * ROW_TILE + q, TT, stride=ROW_TILE), :] = rows[:, q * LANES:(q + 1) * LANES]

    for e in range(N_EXPERTS):
        src = loff_ref[j * N_EXPERTS + e]
        dst = start_ref[j * N_EXPERTS + e]

        def put(offset, size, src=src, dst=dst):
            pltpu.make_async_copy(stage.at[slot, _rows(src + offset, size), :],
                                  xs_ref.at[_rows(dst + offset, size), :], sem.at[slot]).start()
        _for_each_piece(len_ref[j * N_EXPERTS + e], TT_LOG2, put)

    @pl.when(j == pl.num_programs(0) - 1)
    def _():
        wait_tile(1 - slot)
        wait_tile(slot)


def _dispatch(starts, lens, loffs, pad_starts, pad_lens, lp, h2, p_rows):
    t = h2.shape[0]
    return pl.pallas_call(
        _dispatch_kernel,
        grid_spec=pltpu.PrefetchScalarGridSpec(
            num_scalar_prefetch=5,
            grid=(t // TT,),
            in_specs=[
                pl.BlockSpec((8, TT), lambda j, *_: (0, j)),
                pl.BlockSpec((TT, D_MODEL), lambda j, *_: (j, 0)),
            ],
            out_specs=pl.BlockSpec(memory_space=pl.ANY),
            scratch_shapes=[
                pltpu.VMEM((2, STAGE_ROWS * ROW_TILE, LANES), F32),
                pltpu.VMEM((TM // 2 * ROW_TILE, LANES), F32),
                pltpu.SemaphoreType.DMA((2,)),
                pltpu.SemaphoreType.DMA(()),
            ],
        ),
        out_shape=jax.ShapeDtypeStruct((p_rows * ROW_TILE, LANES), F32),
        compiler_params=pltpu.CompilerParams(
            dimension_semantics=("arbitrary",), vmem_limit_bytes=VMEM_LIMIT),
        name="dispatch",
    )(starts, lens, loffs, pad_starts, pad_lens, lp, h2)


def _experts_kernel(te_ref, nv_ref, nxt_ref, par_ref, x_ref, wg_ref, bg_ref, wu_ref, bu_ref, wd_ref, bd_ref,
                    o_ref, act, wbuf, wg_b, wu_b, wd_b, wsem):
    i = pl.program_id(0)
    prev = jnp.maximum(i - 1, 0)

    def fetch(expert, s):
        return [pltpu.make_async_copy(w_ref.at[expert], wbuf.at[s, m], wsem.at[s])
                for m, w_ref in enumerate((wg_ref, wu_ref, wd_ref))]

    @pl.when(i >= nv_ref[0])
    def _():
        o_ref[...] = jnp.zeros_like(o_ref)

    @pl.when(i < nv_ref[0])
    def _():
        expert = te_ref[i]
        s = par_ref[i]

        @pl.when(i == 0)
        def _():
            for cp in fetch(expert, s):
                cp.start()

        @pl.when((i == 0) | (expert != te_ref[prev]))
        def _():
            for cp in fetch(expert, s):
                cp.wait()
            for c in range(N_CHUNKS):
                cs = slice(c * N_COLS, (c + 1) * N_COLS)
                wg_b[c] = wbuf[s, 0, :, cs].astype(BF16)
                wu_b[c] = wbuf[s, 1, :, cs].astype(BF16)
                wd_b[c] = wbuf[s, 2, :, cs].astype(BF16)

            @pl.when(nxt_ref[i] != expert)
            def _():
                for cp in fetch(nxt_ref[i], 1 - s):
                    cp.start()

        xb = _load_row_tiles(x_ref, TM).astype(BF16)
        for n in range(N_CHUNKS):
            gate = _dot(xb, wg_b[n]) + bg_ref[n:n + 1, :]
            up = _dot(xb, wu_b[n]) + bu_ref[n:n + 1, :]
            gate = jnp.minimum(gate, SWIGLU_LIMIT)
            up = jnp.clip(up, -SWIGLU_LIMIT, SWIGLU_LIMIT)
            act[n] = ((up + 1.0) * (gate * jax.nn.sigmoid(SWIGLU_ALPHA * gate))).astype(BF16)
        a = jnp.concatenate([act[c] for c in range(N_CHUNKS)], axis=1)
        for n in range(N_CHUNKS):
            out = _dot(a, wd_b[n]) + bd_ref[n:n + 1, :]
            for q in range(N_COLS // LANES):
                o_ref[pl.ds(n * (N_COLS // LANES) + q, TM, stride=ROW_TILE), :] = (
                    out[:, q * LANES:(q + 1) * LANES])


def _experts(tile_expert, n_valid, next_expert, parity, xs, wg, bg, wu, bu, wd, bd):
    n_tiles = xs.shape[0] // (TM * ROW_TILE)

    def x_map(i, te, nv, nx, pr):
        return (jnp.minimum(i, nv[0] - 1), 0)

    def b_map(i, te, nv, nx, pr):
        return (te[i], 0, 0)

    w_spec = pl.BlockSpec(memory_space=pl.ANY)
    b_spec = pl.BlockSpec((None, N_CHUNKS, N_COLS), b_map)
    return pl.pallas_call(
        _experts_kernel,
        grid_spec=pltpu.PrefetchScalarGridSpec(
            num_scalar_prefetch=4,
            grid=(n_tiles,),
            in_specs=[pl.BlockSpec((TM * ROW_TILE, LANES), x_map),
                      w_spec, b_spec, w_spec, b_spec, w_spec, b_spec],
            out_specs=pl.BlockSpec((TM * ROW_TILE, LANES), lambda i, te, nv, nx, pr: (i, 0)),
            scratch_shapes=[
                pltpu.VMEM((N_CHUNKS, TM, N_COLS), BF16),
                pltpu.VMEM((2, 3, D_MODEL, D_MODEL), F32),
                pltpu.VMEM((N_CHUNKS, D_MODEL, N_COLS), BF16),
                pltpu.VMEM((N_CHUNKS, D_MODEL, N_COLS), BF16),
                pltpu.VMEM((N_CHUNKS, D_MODEL, N_COLS), BF16),
                pltpu.SemaphoreType.DMA((2,)),
            ],
        ),
        out_shape=jax.ShapeDtypeStruct(xs.shape, F32),
        compiler_params=pltpu.CompilerParams(
            dimension_semantics=("arbitrary",), vmem_limit_bytes=VMEM_LIMIT),
        name="experts",
    )(tile_expert, n_valid, next_expert, parity, xs, wg, bg, wu, bu, wd, bd)


def _combine_kernel(start_ref, len_ref, loff_ref, info_ref, x1_ref, fw_ref, eo_ref, out_ref, stage, sem):
    j = pl.program_id(0)
    slot = j & 1
    last = pl.num_programs(0) - 1

    def fetch(tile, s):
        for e in range(N_EXPERTS):
            src = start_ref[tile * N_EXPERTS + e]
            dst = loff_ref[tile * N_EXPERTS + e]

            def get(offset, size, src=src, dst=dst):
                pltpu.make_async_copy(eo_ref.at[_rows(src + offset, size), :],
                                      stage.at[s, _rows(dst + offset, size), :], sem.at[s]).start()
            _for_each_piece(len_ref[tile * N_EXPERTS + e], TT_LOG2, get)

    @pl.when(j == 0)
    def _():
        fetch(0, 0)

    @pl.when(j < last)
    def _():
        fetch(j + 1, 1 - slot)

    _wait_rows(stage.at[slot], eo_ref, sem.at[slot], to_hbm=False)

    info = info_ref[...]
    y = x1_ref[...]
    for c in range(STAGE_ROWS // TT):
        r = (_iota((TT, TT), 1) + c * TT).astype(F32)
        g = jnp.zeros((TT, TT), F32)
        for k in range(TOP_K):
            g = g + jnp.where(r == info[:, TOP_K + k:TOP_K + k + 1], info[:, k:k + 1], 0.0)
        g_hi = g.astype(BF16)
        g_lo = (g - g_hi.astype(F32)).astype(BF16)
        rows = jnp.concatenate(
            [stage[slot, pl.ds(c * TT * ROW_TILE + q, TT, stride=ROW_TILE), :] for q in range(ROW_TILE)],
            axis=1).astype(BF16)
        y = y + _dot(g_hi, rows) + _dot(g_lo, rows)
    ms = jnp.mean(y * y, axis=-1, keepdims=True)
    out_ref[...] = y * lax.rsqrt(ms + EPS) * fw_ref[...]


def _combine(starts, lens, loffs, info, x1, final_w, eo):
    t = x1.shape[0]
    return pl.pallas_call(
        _combine_kernel,
        grid_spec=pltpu.PrefetchScalarGridSpec(
            num_scalar_prefetch=3,
            grid=(t // TT,),
            in_specs=[
                pl.BlockSpec((TT, LANES), lambda j, *_: (j, 0)),
                pl.BlockSpec((TT, D_MODEL), lambda j, *_: (j, 0)),
                pl.BlockSpec((1, D_MODEL), lambda j, *_: (0, 0)),
                pl.BlockSpec(memory_space=pl.ANY),
            ],
            out_specs=pl.BlockSpec((TT, D_MODEL), lambda j, *_: (j, 0)),
            scratch_shapes=[
                pltpu.VMEM((2, STAGE_ROWS * ROW_TILE, LANES), F32),
                pltpu.SemaphoreType.DMA((2,)),
            ],
        ),
        out_shape=jax.ShapeDtypeStruct((t, D_MODEL), F32),
        compiler_params=pltpu.CompilerParams(
            dimension_semantics=("arbitrary",), vmem_limit_bytes=VMEM_LIMIT),
        name="combine",
    )(starts, lens, loffs, info, x1, final_w, eo)


def _pad_lanes(v, offset, fill=0.0):
    row = jnp.full((1, LANES), fill, F32)
    return row.at[0, offset:offset + v.shape[0]].set(v.astype(F32))


def kernel(x, norm_mix_w, w_in, gla_w_alpha_up, gla_b_alpha, gla_norm_w, ssd_conv_w, ssd_conv_b,
           ssd_dt_bias, ssd_A_log, ssd_D, ssd_norm_w, w_out, norm_ffn_w, router_w, router_b,
           moe_w_gate, moe_b_gate, moe_w_up, moe_b_up, moe_w_down, moe_b_down, final_norm_w):
    bsz, seqlen, d = x.shape
    t = bsz * seqlen
    depth = w_in.shape[0]
    assert depth == 1, "the final RMSNorm is fused into the (single) layer's combine step"
    p_rows = t * TOP_K + N_EXPERTS * TM
    n_tiles = p_rows // TM
    for l in range(depth):
        w = w_in[l]
        w_all = jnp.concatenate(
            [w[:, 0:1536], w[:, 1552:3088], w[:, 1536:1552], w[:, 3088:3096],
             jnp.zeros((d, N_SMALL - GLA_GATE_RANK - SSD_HEADS), w.dtype)], axis=1).astype(BF16)
        wup = jnp.zeros((N_SMALL, GLA_KW), F32).at[0:GLA_GATE_RANK].set(gla_w_alpha_up[l])
        dtb = _pad_lanes(ssd_dt_bias[l], DT_COL)
        aneg = _pad_lanes(-jnp.exp(ssd_A_log[l].astype(F32)), DT_COL)
        dexp = jnp.repeat(ssd_D[l].astype(F32), SSD_HEADDIM)[None, :]
        rw = jnp.zeros((LANES, d), F32).at[0:N_EXPERTS].set(router_w[l].T)
        rb = jnp.zeros((LANES, TL), F32).at[0:N_EXPERTS].set(
            jnp.broadcast_to(router_b[l].astype(F32)[:, None], (N_EXPERTS, TL)))

        pm, ps = _inproj(x.reshape(t, d), norm_mix_w[l][None, :], w_all)
        x1, h2, logits, counts = _mixer(
            pm.reshape(bsz, seqlen, N_MAIN), ps.reshape(bsz, seqlen, N_SMALL), x,
            wup, gla_b_alpha[l][None, :], gla_norm_w[l][None, :], ssd_conv_w[l],
            ssd_conv_b[l][None, :], dtb, aneg, dexp, ssd_norm_w[l][None, :],
            w_out[l].astype(BF16), norm_ffn_w[l][None, :], rw, rb)

        info, lp, starts, lens, loffs = _route(logits, counts)
        starts, lens, loffs = starts[:, 0], lens[:, 0], loffs[:, 0]

        cnt = counts[:, 0].astype(jnp.int32)
        padded = ((cnt + TM - 1) // TM) * TM
        ends = jnp.cumsum(padded)
        n_valid = (ends[-1] // TM).astype(jnp.int32)
        tile_starts = jnp.arange(n_tiles, dtype=jnp.int32) * TM
        tile_expert = jnp.sum(tile_starts[:, None] >= ends[None, :], axis=1).astype(jnp.int32)
        last_expert = tile_expert[jnp.maximum(n_valid - 1, 0)]
        tile_expert = jnp.where(tile_starts < ends[-1], tile_expert, last_expert)
        next_first = jnp.minimum(ends[tile_expert] // TM, n_valid - 1)
        next_expert = tile_expert[next_first]
        new_group = jnp.concatenate([jnp.ones((1,), jnp.int32),
                                     (tile_expert[1:] != tile_expert[:-1]).astype(jnp.int32)])
        parity = (jnp.cumsum(new_group) - 1) % 2
        pad_starts = jnp.concatenate([ends - padded + cnt, ends[-1:]])
        pad_lens = jnp.concatenate([padded - cnt, (p_rows - ends[-1:]) // (TM // 2)])

        xs = _dispatch(starts, lens, loffs, pad_starts, pad_lens, lp, h2.reshape(t, d), p_rows)
        eo = _experts(tile_expert, n_valid.reshape(1), next_expert, parity.astype(jnp.int32), xs,
                      moe_w_gate[l], moe_b_gate[l].reshape(N_EXPERTS, N_CHUNKS, N_COLS),
                      moe_w_up[l], moe_b_up[l].reshape(N_EXPERTS, N_CHUNKS, N_COLS),
                      moe_w_down[l], moe_b_down[l].reshape(N_EXPERTS, N_CHUNKS, N_COLS))
        x = _combine(starts, lens, loffs, info, x1.reshape(t, d), final_norm_w[None, :], eo
                     ).reshape(bsz, seqlen, d)
    return x
```

```python
import functools

import jax
import jax.numpy as jnp
from jax import lax
from jax.experimental import pallas as pl
from jax.experimental.pallas import tpu as pltpu

F32 = jnp.float32
BF16 = jnp.bfloat16

D_MODEL = 1024
GLA_WIDTH = 512
GLA_HEADS = 4
GLA_DV = 128
GLA_DK = 64
GLA_KW = 256
GLA_GATE_RANK = 16
GLA_GATE_NORM = 16.0
SSD_WIDTH = 512
SSD_HEADDIM = 64
SSD_HEADS = 8
SSD_GROUPS = 2
SSD_HPG = 4
SSD_STATE = 128
SSD_CONV = 4
SSD_CONV_CH = 1024
N_EXPERTS = 32
TOP_K = 4
SWIGLU_LIMIT = 7.0
SWIGLU_ALPHA = 1.702
EPS = 1e-6
GROUP_EPS = 1e-5

LANES = 128
ROW_TILE = D_MODEL // LANES
N_MAIN = 3072
N_SMALL = LANES
DT_COL = GLA_GATE_RANK

GLA_CHUNK = 64
SSD_CHUNK = 128
PROJ_COLS = 512
TL = 256
TT_LOG2 = 8
TT = 1 << TT_LOG2
STAGE_ROWS = 4 * TT
TM_LOG2 = 9
TM = 1 << TM_LOG2
WAIT_ROWS = 512
assert STAGE_ROWS * D_MODEL * 4 <= (1 << 17) * 32
N_COLS = 256
N_CHUNKS = D_MODEL // N_COLS
VMEM_LIMIT = 56 * 1024 * 1024


def _dot(a, b):
    return jnp.dot(a, b, preferred_element_type=F32)


def _dot_nt(a, b):
    return lax.dot_general(a, b, (((1,), (1,)), ((), ())), preferred_element_type=F32)


def _dot_tn(a, b):
    return lax.dot_general(a, b, (((0,), (0,)), ((), ())), preferred_element_type=F32)


def _split3(a):
    hi = a.astype(BF16)
    r1 = a - hi.astype(F32)
    mid = r1.astype(BF16)
    lo = (r1 - mid.astype(F32)).astype(BF16)
    return hi, mid, lo


def _dot_sel_lhs(sel, a):
    hi, mid, lo = _split3(a)
    return _dot(sel, hi) + _dot(sel, mid) + _dot(sel, lo)


def _dot_sel_rhs(a, sel, terms=3):
    parts = _split3(a)[:terms]
    out = _dot(parts[0], sel)
    for p in parts[1:]:
        out = out + _dot(p, sel)
    return out


def _dot_hi(a, b):
    a_hi = a.astype(BF16)
    a_lo = (a - a_hi.astype(F32)).astype(BF16)
    b_hi = b.astype(BF16)
    b_lo = (b - b_hi.astype(F32)).astype(BF16)
    return _dot(a_hi, b_hi) + _dot(a_lo, b_hi) + _dot(a_hi, b_lo)


def _dot_hi_nt(a, b):
    a_hi = a.astype(BF16)
    a_lo = (a - a_hi.astype(F32)).astype(BF16)
    b_hi = b.astype(BF16)
    b_lo = (b - b_hi.astype(F32)).astype(BF16)
    return _dot_nt(a_hi, b_hi) + _dot_nt(a_lo, b_hi) + _dot_nt(a_hi, b_lo)


def _softplus(x):
    return jnp.maximum(x, 0.0) + jnp.log1p(jnp.exp(-jnp.abs(x)))


def _silu(x):
    return x * jax.nn.sigmoid(x)


def _iota(shape, dim):
    return lax.broadcasted_iota(jnp.int32, shape, dim)


def _load_row_tiles(ref, rows):
    return jnp.concatenate([ref[pl.ds(j, rows, stride=ROW_TILE), :] for j in range(ROW_TILE)], axis=1)


def _project_parts(x_ref, nmw_ref, win_ref, pm_ref, small_ref):
    state = {}

    def norm():
        x_in = x_ref[...]
        ms = jnp.mean(x_in * x_in, axis=-1, keepdims=True)
        state["h"] = (x_in * lax.rsqrt(ms + EPS) * nmw_ref[...]).astype(BF16)

    def chunk(n0):
        def run():
            pm_ref[:, n0:n0 + PROJ_COLS] = _dot(state["h"], win_ref[:, n0:n0 + PROJ_COLS]).astype(BF16)
        return run

    def small():
        small_ref[...] = _dot(state["h"], win_ref[:, N_MAIN:N_MAIN + N_SMALL])

    return [norm] + [chunk(n0) for n0 in range(0, N_MAIN, PROJ_COLS)] + [small]


def _mixer_kernel(x_ref, xn_ref, nmw_ref, win_ref, *refs, tiles_per_row):
    params, outs = refs[:13], refs[13:17]
    gla_state, ssd_state, conv_tail, mix_scr, cnt_scr, pm_a, pm_b, small_a, small_b = refs[17:]
    g = pl.program_id(0)

    @pl.when(g == 0)
    def _():
        cnt_scr[...] = jnp.zeros_like(cnt_scr)
        for part in _project_parts(x_ref, nmw_ref, win_ref, pm_a, small_a):
            part()

    @pl.when(lax.rem(g, tiles_per_row) == 0)
    def _():
        gla_state[...] = jnp.zeros_like(gla_state)
        ssd_state[...] = jnp.zeros_like(ssd_state)
        conv_tail[...] = jnp.zeros_like(conv_tail)

    for parity, (pm_cur, small_cur, pm_nxt, small_nxt) in enumerate(
            ((pm_a, small_a, pm_b, small_b), (pm_b, small_b, pm_a, small_a))):
        @pl.when((g & 1) == parity)
        def _(pm_cur=pm_cur, small_cur=small_cur, pm_nxt=pm_nxt, small_nxt=small_nxt):
            _mixer_tile(_project_parts(xn_ref, nmw_ref, win_ref, pm_nxt, small_nxt),
                        pm_cur, small_cur, x_ref, *params, *outs,
                        gla_state, ssd_state, conv_tail, mix_scr, cnt_scr)


def _mixer_tile(side_work, pm_ref, small_ref, x_ref, wup_ref, balpha_ref, gnw_ref, convw_ref, convb_ref,
                dtb_ref, aneg_ref, dexp_ref, snw_ref, wout_ref, nfw_ref, rw_ref, rb_ref,
                x1_ref, h2_ref, lg_ref, cnt_ref,
                gla_state, ssd_state, conv_tail, mix_scr, cnt_scr):
    side_work = list(side_work)

    def run_side(n=1):
        for _ in range(min(n, len(side_work))):
            side_work.pop(0)()

    small = small_ref[...]

    row = _iota((TL, TL), 0)
    col = _iota((TL, TL), 1)
    causal = col <= row
    cum64 = jnp.where(causal & ((row // GLA_CHUNK) == (col // GLA_CHUNK)), 1.0, 0.0).astype(BF16)
    cum128 = jnp.where(causal & ((row // SSD_CHUNK) == (col // SSD_CHUNK)), 1.0, 0.0).astype(BF16)

    xa = _dot_hi(small, wup_ref[...]) + balpha_ref[...]
    log_a = (jnp.minimum(xa, 0.0) - jnp.log1p(jnp.exp(-jnp.abs(xa)))) * (1.0 / GLA_GATE_NORM)
    bcum = _dot_sel_lhs(cum64, log_a)

    lane_kw = _iota((GLA_CHUNK, GLA_KW), 1)
    head_masks = [(lane_kw // GLA_DK) == h for h in range(GLA_HEADS)]
    lane_kw_s = _iota((GLA_DV, GLA_KW), 1)
    head_masks_s = [(lane_kw_s // GLA_DK) == h for h in range(GLA_HEADS)]
    tril64 = _iota((GLA_CHUNK, GLA_CHUNK), 1) <= _iota((GLA_CHUNK, GLA_CHUNK), 0)
    q_scale = GLA_DK ** -0.5

    for c in range(TL // GLA_CHUNK):
        rs = slice(c * GLA_CHUNK, (c + 1) * GLA_CHUNK)
        bc = bcum[rs]
        b_mid = bc[GLA_CHUNK // 2:GLA_CHUNK // 2 + 1]
        b_last = bc[GLA_CHUNK - 1:GLA_CHUNK]
        qc = pm_ref[rs, 0:GLA_KW].astype(F32) * q_scale
        kc = pm_ref[rs, GLA_KW:2 * GLA_KW].astype(F32)
        vc = pm_ref[rs, 2 * GLA_KW:2 * GLA_KW + GLA_WIDTH]
        q_in = (qc * jnp.exp(bc - b_mid)).astype(BF16)
        k_in = (kc * jnp.exp(b_mid - bc)).astype(BF16)
        q_st = (qc * jnp.exp(bc)).astype(BF16)
        k_st = (kc * jnp.exp(b_last - bc)).astype(BF16)
        st = gla_state[...]
        st_b = st.astype(BF16)
        zero_b = jnp.zeros_like(q_in)
        for h in range(GLA_HEADS):
            scores = _dot_nt(jnp.where(head_masks[h], q_in, zero_b), k_in)
            scores = jnp.where(tril64, scores, 0.0).astype(BF16)
            o_h = _dot(scores, vc[:, h * GLA_DV:(h + 1) * GLA_DV])
            o_h = o_h + _dot_nt(jnp.where(head_masks[h], q_st, zero_b), st_b)
            mix_scr[rs, h * GLA_DV:(h + 1) * GLA_DV] = o_h
        upd = _dot_tn(vc, k_st)
        new_st = st * jnp.exp(b_last)
        for h in range(GLA_HEADS):
            new_st = new_st + jnp.where(head_masks_s[h], upd[h * GLA_DV:(h + 1) * GLA_DV], 0.0)
        gla_state[...] = new_st
        run_side()

    xbc = pm_ref[:, 2048:3072].astype(F32)
    tail = conv_tail[...]
    conv_tail[...] = xbc[TL - 8:TL]
    row8 = _iota((8, SSD_CONV_CH), 0)
    conv = xbc * convw_ref[SSD_CONV - 1:SSD_CONV, :]
    for s in range(1, SSD_CONV):
        shifted = pltpu.roll(xbc, s, 0)
        head = jnp.where(row8 < s, pltpu.roll(tail, s, 0), shifted[0:8])
        shifted = jnp.concatenate([head, shifted[8:]], axis=0)
        conv = conv + shifted * convw_ref[SSD_CONV - 1 - s:SSD_CONV - s, :]
    act = _silu(conv + convb_ref[...])
    run_side()
    xs = act[:, 0:SSD_WIDTH]
    bm = act[:, SSD_WIDTH:SSD_WIDTH + SSD_GROUPS * SSD_STATE].astype(BF16)
    cm = act[:, SSD_WIDTH + SSD_GROUPS * SSD_STATE:].astype(BF16)

    dt_full = _softplus(small + dtb_ref[...])
    a_full = dt_full * aneg_ref[...]
    acum = _dot_sel_lhs(cum128, a_full)
    acum_t = acum.T

    e_row = _iota((N_SMALL, SSD_WIDTH), 0)
    e_col = _iota((N_SMALL, SSD_WIDTH), 1)
    spread64 = jnp.where(e_row == DT_COL + e_col // SSD_HEADDIM, 1.0, 0.0).astype(BF16)
    e_row2 = _iota((N_SMALL, SSD_HEADS * LANES), 0)
    e_col2 = _iota((N_SMALL, SSD_HEADS * LANES), 1)
    spread128 = jnp.where(e_row2 == DT_COL + e_col2 // LANES, 1.0, 0.0).astype(BF16)
    dt_e = _dot_sel_rhs(dt_full, spread64, terms=1)
    ac_e = _dot_sel_rhs(acum, spread64, terms=2)
    ac_w = _dot_sel_rhs(acum, spread128, terms=2)

    tril128 = _iota((SSD_CHUNK, SSD_CHUNK), 1) <= _iota((SSD_CHUNK, SSD_CHUNK), 0)
    lane_g = _iota((SSD_CHUNK, SSD_HPG * SSD_HEADDIM), 1)
    for c in range(TL // SSD_CHUNK):
        rs = slice(c * SSD_CHUNK, (c + 1) * SSD_CHUNK)
        ac_c = ac_e[rs]
        a_last = ac_c[SSD_CHUNK - 1:SSD_CHUNK]
        dt_c = dt_e[rs]
        xs_c = xs[rs]
        x_dt = (xs_c * dt_c).astype(BF16)
        x_w = (xs_c * (jnp.exp(a_last - ac_c) * dt_c)).astype(BF16)
        e_ac = jnp.exp(ac_c)
        for g in range(SSD_GROUPS):
            gs = slice(g * SSD_STATE, (g + 1) * SSD_STATE)
            ws = slice(g * SSD_HPG * SSD_HEADDIM, (g + 1) * SSD_HPG * SSD_HEADDIM)
            c_g = cm[rs, gs]
            b_g = bm[rs, gs]
            cb = _dot_nt(c_g, b_g)
            x_dt_g = x_dt[:, ws]
            lhs_parts = []
            rhs_parts = []
            for hh in range(SSD_HPG):
                h = g * SSD_HPG + hh
                seg = ac_w[rs, h * LANES:(h + 1) * LANES] - acum_t[DT_COL + h:DT_COL + h + 1, rs]
                lmat = jnp.where(tril128, jnp.exp(jnp.where(tril128, seg, 0.0)), 0.0)
                lhs_parts.append((cb * lmat).astype(BF16))
                rhs_parts.append(jnp.where((lane_g // SSD_HEADDIM) == hh, x_dt_g,
                                           jnp.zeros_like(x_dt_g)))
            intra = _dot(jnp.concatenate(lhs_parts, axis=1), jnp.concatenate(rhs_parts, axis=0))
            st = ssd_state[g]
            inter = _dot(c_g, st.astype(BF16)) * e_ac[:, ws]
            mix_scr[rs, GLA_WIDTH + g * 256:GLA_WIDTH + (g + 1) * 256] = intra + inter
            ssd_state[g] = st * jnp.exp(a_last[:, ws]) + _dot_tn(b_g, x_w[:, ws])
        run_side()

    run_side(len(side_work))

    o = mix_scr[:, 0:GLA_WIDTH]
    g_gate = _silu(pm_ref[:, 1024:1536].astype(F32))
    gla_parts = []
    for h in range(GLA_HEADS):
        o_h = o[:, h * GLA_DV:(h + 1) * GLA_DV]
        ms = jnp.mean(o_h * o_h, axis=-1, keepdims=True)
        gla_parts.append(o_h * lax.rsqrt(ms + GROUP_EPS))
    gla_out = jnp.concatenate(gla_parts, axis=1) * gnw_ref[...] * g_gate

    y = mix_scr[:, GLA_WIDTH:] + dexp_ref[...] * xs
    y = y * _silu(pm_ref[:, 1536:2048].astype(F32))
    ssd_parts = []
    for g in range(SSD_GROUPS):
        y_g = y[:, g * 256:(g + 1) * 256]
        ms = jnp.mean(y_g * y_g, axis=-1, keepdims=True)
        ssd_parts.append(y_g * lax.rsqrt(ms + GROUP_EPS))
    ssd_out = jnp.concatenate(ssd_parts, axis=1) * snw_ref[...]

    mixed = jnp.concatenate([gla_out, ssd_out], axis=1).astype(BF16)
    x1 = x_ref[...] + _dot(mixed, wout_ref[...])
    x1_ref[...] = x1

    ms = jnp.mean(x1 * x1, axis=-1, keepdims=True)
    h2 = x1 * lax.rsqrt(ms + EPS) * nfw_ref[...]
    h2_ref[...] = h2.astype(BF16)
    lg = _dot_hi_nt(rw_ref[...], h2) + rb_ref[...]
    lg_ref[...] = lg
    cnt_scr[...] = cnt_scr[...] + _tile_counts(_top4(lg[0:N_EXPERTS])[0])[1]
    cnt_ref[...] = cnt_scr[...]


def _mixer(x, nmw, w_all, wup, balpha, gnw, convw, convb, dtb, aneg, dexp, snw, wout, nfw, rw, rb):
    bsz, seqlen, _ = x.shape
    per_row = seqlen // TL
    steps = bsz * per_row

    def full(a):
        return pl.BlockSpec(a.shape, lambda g: (0,) * a.ndim)

    def tile(ahead):
        def index(g):
            tile_id = jnp.minimum(g + ahead, steps - 1)
            return (tile_id // per_row, tile_id % per_row, 0)
        return pl.BlockSpec((None, TL, D_MODEL), index)

    params = (nmw, w_all, wup, balpha, gnw, convw, convb, dtb, aneg, dexp, snw, wout, nfw, rw, rb)
    return pl.pallas_call(
        functools.partial(_mixer_kernel, tiles_per_row=per_row),
        grid=(steps,),
        in_specs=[tile(0), tile(1)] + [full(p) for p in params],
        out_specs=[tile(0), tile(0),
                   pl.BlockSpec((LANES, TL), lambda g: (0, g)),
                   pl.BlockSpec((N_EXPERTS, LANES), lambda g: (0, 0))],
        out_shape=[
            jax.ShapeDtypeStruct((bsz, seqlen, D_MODEL), F32),
            jax.ShapeDtypeStruct((bsz, seqlen, D_MODEL), BF16),
            jax.ShapeDtypeStruct((LANES, bsz * seqlen), F32),
            jax.ShapeDtypeStruct((N_EXPERTS, LANES), F32),
        ],
        scratch_shapes=[
            pltpu.VMEM((GLA_DV, GLA_KW), F32),
            pltpu.VMEM((SSD_GROUPS, SSD_STATE, SSD_HPG * SSD_HEADDIM), F32),
            pltpu.VMEM((8, SSD_CONV_CH), F32),
            pltpu.VMEM((TL, D_MODEL), F32),
            pltpu.VMEM((N_EXPERTS, LANES), F32),
            pltpu.VMEM((TL, N_MAIN), BF16),
            pltpu.VMEM((TL, N_MAIN), BF16),
            pltpu.VMEM((TL, N_SMALL), F32),
            pltpu.VMEM((TL, N_SMALL), F32),
        ],
        compiler_params=pltpu.CompilerParams(
            dimension_semantics=("arbitrary",), vmem_limit_bytes=VMEM_LIMIT),
        name="mixer",
    )(x, x, *params)


def _top4(lg):
    n_e, n_t = lg.shape
    row = _iota((n_e, n_t), 0)
    work = lg
    onehots = []
    vals = []
    for _ in range(TOP_K):
        m = jnp.max(work, axis=0, keepdims=True)
        idx = jnp.min(jnp.where(work == m, row, n_e), axis=0, keepdims=True)
        oh = row == idx
        onehots.append(oh)
        vals.append(m)
        work = jnp.where(oh, -jnp.inf, work)
    return onehots, vals


def _tile_counts(onehots):
    multi = jnp.where(onehots[0] | onehots[1] | onehots[2] | onehots[3], 1.0, 0.0).astype(BF16)
    return multi, _dot(multi, jnp.ones((multi.shape[1], LANES), BF16))


def _route_kernel(lg_ref, cnt_ref, info_ref, lp_ref, start_ref, len_ref, loff_ref, run_scr):
    @pl.when(pl.program_id(0) == 0)
    def _():
        run_scr[...] = jnp.zeros_like(run_scr)

    onehots, vals = _top4(lg_ref[0:N_EXPERTS, :])
    multi, tile_cnt = _tile_counts(onehots)
    counts = cnt_ref[...]
    padded = jnp.floor((counts + (TM - 1)) * (1.0 / TM)) * TM
    lower = jnp.where(_iota((N_EXPERTS, N_EXPERTS), 1) < _iota((N_EXPERTS, N_EXPERTS), 0),
                      1.0, 0.0).astype(BF16)
    offs = _dot_sel_lhs(lower, padded)
    loff = _dot_sel_lhs(lower, tile_cnt)
    before = jnp.where(_iota((TT, TT), 0) < _iota((TT, TT), 1), 1.0, 0.0).astype(BF16)
    rank = _dot(multi, before)
    start_ref[...] = (offs + run_scr[...]).astype(jnp.int32)
    len_ref[...] = tile_cnt.astype(jnp.int32)
    loff_ref[...] = loff.astype(jnp.int32)
    run_scr[...] = run_scr[...] + tile_cnt
    local = rank + jnp.concatenate([loff] * (TT // LANES), axis=1)
    exps = [jnp.exp(v - vals[0]) for v in vals]
    den = exps[0] + exps[1] + exps[2] + exps[3]
    lp_rows = [jnp.sum(jnp.where(oh, local, 0.0), axis=0, keepdims=True) for oh in onehots]
    lp_ref[...] = jnp.concatenate(lp_rows + [jnp.zeros((8 - TOP_K, TT), F32)], axis=0).astype(jnp.int32)
    gate_rows = [e / den for e in exps]
    info = jnp.concatenate(gate_rows + lp_rows + [jnp.zeros((LANES - 2 * TOP_K, TT), F32)], axis=0)
    info_ref[...] = info.T


def _route(logits_t, counts):
    t = logits_t.shape[1]
    steps = t // TT
    run_spec = pl.BlockSpec((N_EXPERTS, LANES), lambda i: (i, 0))
    run_shape = jax.ShapeDtypeStruct((steps * N_EXPERTS, LANES), jnp.int32)
    return pl.pallas_call(
        _route_kernel,
        grid=(steps,),
        in_specs=[pl.BlockSpec((LANES, TT), lambda i: (0, i)),
                  pl.BlockSpec((N_EXPERTS, LANES), lambda i: (0, 0))],
        out_specs=[
            pl.BlockSpec((TT, LANES), lambda i: (i, 0)),
            pl.BlockSpec((8, TT), lambda i: (0, i)),
            run_spec, run_spec, run_spec,
        ],
        out_shape=[
            jax.ShapeDtypeStruct((t, LANES), F32),
            jax.ShapeDtypeStruct((8, t), jnp.int32),
            run_shape, run_shape, run_shape,
        ],
        scratch_shapes=[pltpu.VMEM((N_EXPERTS, LANES), F32)],
        compiler_params=pltpu.CompilerParams(
            dimension_semantics=("arbitrary",), vmem_limit_bytes=VMEM_LIMIT),
        name="route",
    )(logits_t, counts)


def _rows(first, count):
    return pl.ds(pl.multiple_of(first * ROW_TILE, ROW_TILE), count * ROW_TILE)


def _wait_rows(stage_slot, hbm_ref, sem, to_hbm):
    for w in range(STAGE_ROWS // WAIT_ROWS):
        part = stage_slot.at[pl.ds(w * WAIT_ROWS * ROW_TILE, WAIT_ROWS * ROW_TILE), :]
        hbm = hbm_ref.at[pl.ds(0, WAIT_ROWS * ROW_TILE), :]
        src, dst = (part, hbm) if to_hbm else (hbm, part)
        pltpu.make_async_copy(src, dst, sem).wait()


def _for_each_piece(length, max_log2, fn):
    for b in reversed(range(max_log2 + 1)):
        size = 1 << b
        offset = lax.shift_left(lax.shift_right_logical(length, b + 1), b + 1)

        @pl.when((length & size) != 0)
        def _(offset=offset, size=size):
            fn(offset, size)


def _dispatch_kernel(start_ref, len_ref, loff_ref, pstart_ref, plen_ref,
                     lp_ref, h_ref, xs_ref, stage, zeros, sem, zsem):
    j = pl.program_id(0)
    slot = j & 1

    def wait_tile(s):
        _wait_rows(stage.at[s], xs_ref, sem.at[s], to_hbm=True)

    @pl.when(j == 0)
    def _():
        zeros[...] = jnp.zeros_like(zeros)
        for e in range(N_EXPERTS):
            def put(offset, size, e=e):
                cp = pltpu.make_async_copy(zeros.at[pl.ds(0, size * ROW_TILE), :],
                                           xs_ref.at[_rows(pstart_ref[e] + offset, size), :], zsem)
                cp.start()
                cp.wait()
            _for_each_piece(plen_ref[e], TM_LOG2 - 1, put)

        def put_block(blk, carry):
            cp = pltpu.make_async_copy(
                zeros, xs_ref.at[_rows(pstart_ref[N_EXPERTS] + blk * (TM // 2), TM // 2), :], zsem)
            cp.start()
            cp.wait()
            return carry
        lax.fori_loop(0, plen_ref[N_EXPERTS], put_block, 0)

    @pl.when(j >= 2)
    def _():
        wait_tile(slot)

    h = h_ref[...]
    lp = lp_ref[...]
    for c in range(STAGE_ROWS // TT):
        r = _iota((TT, TT), 0) + c * TT
        hit = (r == lp[0:1]) | (r == lp[1:2]) | (r == lp[2:3]) | (r == lp[3:4])
        rows = _dot(jnp.where(hit, 1.0, 0.0).astype(BF16), h)
        for q in range(ROW_TILE):
            stage[slot, pl.ds(c * TT * ROW_TILE + q, TT, stride=ROW_TILE), :] = rows[:, q * LANES:(q + 1) * LANES]

    for e in range(N_EXPERTS):
        src = loff_ref[j * N_EXPERTS + e]
        dst = start_ref[j * N_EXPERTS + e]

        def put(offset, size, src=src, dst=dst):
            pltpu.make_async_copy(stage.at[slot, _rows(src + offset, size), :],
                                  xs_ref.at[_rows(dst + offset, size), :], sem.at[slot]).start()
        _for_each_piece(len_ref[j * N_EXPERTS + e], TT_LOG2, put)

    @pl.when(j == pl.num_programs(0) - 1)
    def _():
        wait_tile(1 - slot)
        wait_tile(slot)


def _dispatch(starts, lens, loffs, pad_starts, pad_lens, lp, h2, p_rows):
    t = h2.shape[0]
    return pl.pallas_call(
        _dispatch_kernel,
        grid_spec=pltpu.PrefetchScalarGridSpec(
            num_scalar_prefetch=5,
            grid=(t // TT,),
            in_specs=[
                pl.BlockSpec((8, TT), lambda j, *_: (0, j)),
                pl.BlockSpec((TT, D_MODEL), lambda j, *_: (j, 0)),
            ],
            out_specs=pl.BlockSpec(memory_space=pl.ANY),
            scratch_shapes=[
                pltpu.VMEM((2, STAGE_ROWS * ROW_TILE, LANES), F32),
                pltpu.VMEM((TM // 2 * ROW_TILE, LANES), F32),
                pltpu.SemaphoreType.DMA((2,)),
                pltpu.SemaphoreType.DMA(()),
            ],
        ),
        out_shape=jax.ShapeDtypeStruct((p_rows * ROW_TILE, LANES), F32),
        compiler_params=pltpu.CompilerParams(
            dimension_semantics=("arbitrary",), vmem_limit_bytes=VMEM_LIMIT),
        name="dispatch",
    )(starts, lens, loffs, pad_starts, pad_lens, lp, h2)


def _experts_kernel(te_ref, nv_ref, nxt_ref, par_ref, x_ref, wg_ref, bg_ref, wu_ref, bu_ref, wd_ref, bd_ref,
                    o_ref, act, wbuf, wg_b, wu_b, wd_b, wsem):
    i = pl.program_id(0)
    prev = jnp.maximum(i - 1, 0)

    def fetch(expert, s):
        return [pltpu.make_async_copy(w_ref.at[expert], wbuf.at[s, m], wsem.at[s])
                for m, w_ref in enumerate((wg_ref, wu_ref, wd_ref))]

    @pl.when(i >= nv_ref[0])
    def _():
        o_ref[...] = jnp.zeros_like(o_ref)

    @pl.when(i < nv_ref[0])
    def _():
        expert = te_ref[i]
        s = par_ref[i]

        @pl.when(i == 0)
        def _():
            for cp in fetch(expert, s):
                cp.start()

        @pl.when((i == 0) | (expert != te_ref[prev]))
        def _():
            for cp in fetch(expert, s):
                cp.wait()
            for c in range(N_CHUNKS):
                cs = slice(c * N_COLS, (c + 1) * N_COLS)
                wg_b[c] = wbuf[s, 0, :, cs].astype(BF16)
                wu_b[c] = wbuf[s, 1, :, cs].astype(BF16)
                wd_b[c] = wbuf[s, 2, :, cs].astype(BF16)

            @pl.when(nxt_ref[i] != expert)
            def _():
                for cp in fetch(nxt_ref[i], 1 - s):
                    cp.start()

        xb = _load_row_tiles(x_ref, TM).astype(BF16)
        for n in range(N_CHUNKS):
            gate = _dot(xb, wg_b[n]) + bg_ref[n:n + 1, :]
            up = _dot(xb, wu_b[n]) + bu_ref[n:n + 1, :]
            gate = jnp.minimum(gate, SWIGLU_LIMIT)
            up = jnp.clip(up, -SWIGLU_LIMIT, SWIGLU_LIMIT)
            act[n] = ((up + 1.0) * (gate * jax.nn.sigmoid(SWIGLU_ALPHA * gate))).astype(BF16)
        a = jnp.concatenate([act[c] for c in range(N_CHUNKS)], axis=1)
        for n in range(N_CHUNKS):
            out = _dot(a, wd_b[n]) + bd_ref[n:n + 1, :]
            for q in range(N_COLS // LANES):
                o_ref[pl.ds(n * (N_COLS // LANES) + q, TM, stride=ROW_TILE), :] = (
                    out[:, q * LANES:(q + 1) * LANES])


def _experts(tile_expert, n_valid, next_expert, parity, xs, wg, bg, wu, bu, wd, bd):
    n_tiles = xs.shape[0] // (TM * ROW_TILE)

    def x_map(i, te, nv, nx, pr):
        return (jnp.minimum(i, nv[0] - 1), 0)

    def b_map(i, te, nv, nx, pr):
        return (te[i], 0, 0)

    w_spec = pl.BlockSpec(memory_space=pl.ANY)
    b_spec = pl.BlockSpec((None, N_CHUNKS, N_COLS), b_map)
    return pl.pallas_call(
        _experts_kernel,
        grid_spec=pltpu.PrefetchScalarGridSpec(
            num_scalar_prefetch=4,
            grid=(n_tiles,),
            in_specs=[pl.BlockSpec((TM * ROW_TILE, LANES), x_map),
                      w_spec, b_spec, w_spec, b_spec, w_spec, b_spec],
            out_specs=pl.BlockSpec((TM * ROW_TILE, LANES), lambda i, te, nv, nx, pr: (i, 0)),
            scratch_shapes=[
                pltpu.VMEM((N_CHUNKS, TM, N_COLS), BF16),
                pltpu.VMEM((2, 3, D_MODEL, D_MODEL), F32),
                pltpu.VMEM((N_CHUNKS, D_MODEL, N_COLS), BF16),
                pltpu.VMEM((N_CHUNKS, D_MODEL, N_COLS), BF16),
                pltpu.VMEM((N_CHUNKS, D_MODEL, N_COLS), BF16),
                pltpu.SemaphoreType.DMA((2,)),
            ],
        ),
        out_shape=jax.ShapeDtypeStruct(xs.shape, F32),
        compiler_params=pltpu.CompilerParams(
            dimension_semantics=("arbitrary",), vmem_limit_bytes=VMEM_LIMIT),
        name="experts",
    )(tile_expert, n_valid, next_expert, parity, xs, wg, bg, wu, bu, wd, bd)


def _combine_kernel(start_ref, len_ref, loff_ref, info_ref, x1_ref, fw_ref, eo_ref, out_ref, stage, sem):
    j = pl.program_id(0)
    slot = j & 1
    last = pl.num_programs(0) - 1

    def fetch(tile, s):
        for e in range(N_EXPERTS):
            src = start_ref[tile * N_EXPERTS + e]
            dst = loff_ref[tile * N_EXPERTS + e]

            def get(offset, size, src=src, dst=dst):
                pltpu.make_async_copy(eo_ref.at[_rows(src + offset, size), :],
                                      stage.at[s, _rows(dst + offset, size), :], sem.at[s]).start()
            _for_each_piece(len_ref[tile * N_EXPERTS + e], TT_LOG2, get)

    @pl.when(j == 0)
    def _():
        fetch(0, 0)

    @pl.when(j < last)
    def _():
        fetch(j + 1, 1 - slot)

    _wait_rows(stage.at[slot], eo_ref, sem.at[slot], to_hbm=False)

    info = info_ref[...]
    y = x1_ref[...]
    for c in range(STAGE_ROWS // TT):
        r = (_iota((TT, TT), 1) + c * TT).astype(F32)
        g = jnp.zeros((TT, TT), F32)
        for k in range(TOP_K):
            g = g + jnp.where(r == info[:, TOP_K + k:TOP_K + k + 1], info[:, k:k + 1], 0.0)
        g_hi = g.astype(BF16)
        g_lo = (g - g_hi.astype(F32)).astype(BF16)
        rows = jnp.concatenate(
            [stage[slot, pl.ds(c * TT * ROW_TILE + q, TT, stride=ROW_TILE), :] for q in range(ROW_TILE)],
            axis=1).astype(BF16)
        y = y + _dot(g_hi, rows) + _dot(g_lo, rows)
    ms = jnp.mean(y * y, axis=-1, keepdims=True)
    out_ref[...] = y * lax.rsqrt(ms + EPS) * fw_ref[...]


def _combine(starts, lens, loffs, info, x1, final_w, eo):
    t = x1.shape[0]
    return pl.pallas_call(
        _combine_kernel,
        grid_spec=pltpu.PrefetchScalarGridSpec(
            num_scalar_prefetch=3,
            grid=(t // TT,),
            in_specs=[
                pl.BlockSpec((TT, LANES), lambda j, *_: (j, 0)),
                pl.BlockSpec((TT, D_MODEL), lambda j, *_: (j, 0)),
                pl.BlockSpec((1, D_MODEL), lambda j, *_: (0, 0)),
                pl.BlockSpec(memory_space=pl.ANY),
            ],
            out_specs=pl.BlockSpec((TT, D_MODEL), lambda j, *_: (j, 0)),
            scratch_shapes=[
                pltpu.VMEM((2, STAGE_ROWS * ROW_TILE, LANES), F32),
                pltpu.SemaphoreType.DMA((2,)),
            ],
        ),
        out_shape=jax.ShapeDtypeStruct((t, D_MODEL), F32),
        compiler_params=pltpu.CompilerParams(
            dimension_semantics=("arbitrary",), vmem_limit_bytes=VMEM_LIMIT),
        name="combine",
    )(starts, lens, loffs, info, x1, final_w, eo)


def _pad_lanes(v, offset, fill=0.0):
    row = jnp.full((1, LANES), fill, F32)
    return row.at[0, offset:offset + v.shape[0]].set(v.astype(F32))


def kernel(x, norm_mix_w, w_in, gla_w_alpha_up, gla_b_alpha, gla_norm_w, ssd_conv_w, ssd_conv_b,
           ssd_dt_bias, ssd_A_log, ssd_D, ssd_norm_w, w_out, norm_ffn_w, router_w, router_b,
           moe_w_gate, moe_b_gate, moe_w_up, moe_b_up, moe_w_down, moe_b_down, final_norm_w):
    bsz, seqlen, d = x.shape
    t = bsz * seqlen
    depth = w_in.shape[0]
    assert depth == 1, "the final RMSNorm is fused into the (single) layer's combine step"
    p_rows = t * TOP_K + N_EXPERTS * TM
    n_tiles = p_rows // TM
    for l in range(depth):
        w = w_in[l]
        w_all = jnp.concatenate(
            [w[:, 0:1536], w[:, 1552:3088], w[:, 1536:1552], w[:, 3088:3096],
             jnp.zeros((d, N_SMALL - GLA_GATE_RANK - SSD_HEADS), w.dtype)], axis=1).astype(BF16)
        wup = jnp.zeros((N_SMALL, GLA_KW), F32).at[0:GLA_GATE_RANK].set(gla_w_alpha_up[l])
        dtb = _pad_lanes(ssd_dt_bias[l], DT_COL)
        aneg = _pad_lanes(-jnp.exp(ssd_A_log[l].astype(F32)), DT_COL)
        dexp = jnp.repeat(ssd_D[l].astype(F32), SSD_HEADDIM)[None, :]
        rw = jnp.zeros((LANES, d), F32).at[0:N_EXPERTS].set(router_w[l].T)
        rb = jnp.zeros((LANES, TL), F32).at[0:N_EXPERTS].set(
            jnp.broadcast_to(router_b[l].astype(F32)[:, None], (N_EXPERTS, TL)))

        x1, h2, logits, counts = _mixer(
            x, norm_mix_w[l][None, :], w_all, wup, gla_b_alpha[l][None, :], gla_norm_w[l][None, :], ssd_conv_w[l],
            ssd_conv_b[l][None, :], dtb, aneg, dexp, ssd_norm_w[l][None, :],
            w_out[l].astype(BF16), norm_ffn_w[l][None, :], rw, rb)

        info, lp, starts, lens, loffs = _route(logits, counts)
        starts, lens, loffs = starts[:, 0], lens[:, 0], loffs[:, 0]

        cnt = counts[:, 0].astype(jnp.int32)
        padded = ((cnt + TM - 1) // TM) * TM
        ends = jnp.cumsum(padded)
        n_valid = (ends[-1] // TM).astype(jnp.int32)
        tile_starts = jnp.arange(n_tiles, dtype=jnp.int32) * TM
        tile_expert = jnp.sum(tile_starts[:, None] >= ends[None, :], axis=1).astype(jnp.int32)
        last_expert = tile_expert[jnp.maximum(n_valid - 1, 0)]
        tile_expert = jnp.where(tile_starts < ends[-1], tile_expert, last_expert)
        next_first = jnp.minimum(ends[tile_expert] // TM, n_valid - 1)
        next_expert = tile_expert[next_first]
        new_group = jnp.concatenate([jnp.ones((1,), jnp.int32),
                                     (tile_expert[1:] != tile_expert[:-1]).astype(jnp.int32)])
        parity = (jnp.cumsum(new_group) - 1) % 2
        pad_starts = jnp.concatenate([ends - padded + cnt, ends[-1:]])
        pad_lens = jnp.concatenate([padded - cnt, (p_rows - ends[-1:]) // (TM // 2)])

        xs = _dispatch(starts, lens, loffs, pad_starts, pad_lens, lp, h2.reshape(t, d), p_rows)
        eo = _experts(tile_expert, n_valid.reshape(1), next_expert, parity.astype(jnp.int32), xs,
                      moe_w_gate[l], moe_b_gate[l].reshape(N_EXPERTS, N_CHUNKS, N_COLS),
                      moe_w_up[l], moe_b_up[l].reshape(N_EXPERTS, N_CHUNKS, N_COLS),
                      moe_w_down[l], moe_b_down[l].reshape(N_EXPERTS, N_CHUNKS, N_COLS))
        x = _combine(starts, lens, loffs, info, x1.reshape(t, d), final_norm_w[None, :], eo
                     ).reshape(bsz, seqlen, d)
    return x
```

```python
import functools

import jax
import jax.numpy as jnp
from jax import lax
from jax.experimental import pallas as pl
from jax.experimental.pallas import tpu as pltpu

F32 = jnp.float32
BF16 = jnp.bfloat16

D_MODEL = 1024
GLA_WIDTH = 512
GLA_HEADS = 4
GLA_DV = 128
GLA_DK = 64
GLA_KW = 256
GLA_GATE_RANK = 16
GLA_GATE_NORM = 16.0
SSD_WIDTH = 512
SSD_HEADDIM = 64
SSD_HEADS = 8
SSD_GROUPS = 2
SSD_HPG = 4
SSD_STATE = 128
SSD_CONV = 4
SSD_CONV_CH = 1024
N_EXPERTS = 32
TOP_K = 4
SWIGLU_LIMIT = 7.0
SWIGLU_ALPHA = 1.702
EPS = 1e-6
GROUP_EPS = 1e-5

LANES = 128
ROW_TILE = D_MODEL // LANES
N_MAIN = 3072
N_SMALL = LANES
DT_COL = GLA_GATE_RANK

GLA_CHUNK = 64
SSD_CHUNK = 128
PROJ_COLS = 512
TL = 256
TT_LOG2 = 8
TT = 1 << TT_LOG2
STAGE_ROWS = 4 * TT
TM_LOG2 = 9
TM = 1 << TM_LOG2
WAIT_ROWS = 512
assert STAGE_ROWS * D_MODEL * 4 <= (1 << 17) * 32
N_COLS = 256
N_CHUNKS = D_MODEL // N_COLS
VMEM_LIMIT = 56 * 1024 * 1024


def _dot(a, b):
    return jnp.dot(a, b, preferred_element_type=F32)


def _dot_nt(a, b):
    return lax.dot_general(a, b, (((1,), (1,)), ((), ())), preferred_element_type=F32)


def _dot_tn(a, b):
    return lax.dot_general(a, b, (((0,), (0,)), ((), ())), preferred_element_type=F32)


def _split3(a):
    hi = a.astype(BF16)
    r1 = a - hi.astype(F32)
    mid = r1.astype(BF16)
    lo = (r1 - mid.astype(F32)).astype(BF16)
    return hi, mid, lo


def _dot_sel_lhs(sel, a):
    hi, mid, lo = _split3(a)
    return _dot(sel, hi) + _dot(sel, mid) + _dot(sel, lo)


def _dot_sel_rhs(a, sel, terms=3):
    parts = _split3(a)[:terms]
    out = _dot(parts[0], sel)
    for p in parts[1:]:
        out = out + _dot(p, sel)
    return out


def _dot_hi(a, b):
    a_hi = a.astype(BF16)
    a_lo = (a - a_hi.astype(F32)).astype(BF16)
    b_hi = b.astype(BF16)
    b_lo = (b - b_hi.astype(F32)).astype(BF16)
    return _dot(a_hi, b_hi) + _dot(a_lo, b_hi) + _dot(a_hi, b_lo)


def _dot_hi_nt(a, b):
    a_hi = a.astype(BF16)
    a_lo = (a - a_hi.astype(F32)).astype(BF16)
    b_hi = b.astype(BF16)
    b_lo = (b - b_hi.astype(F32)).astype(BF16)
    return _dot_nt(a_hi, b_hi) + _dot_nt(a_lo, b_hi) + _dot_nt(a_hi, b_lo)


def _softplus(x):
    return jnp.maximum(x, 0.0) + jnp.log1p(jnp.exp(-jnp.abs(x)))


def _silu(x):
    return x * jax.nn.sigmoid(x)


def _iota(shape, dim):
    return lax.broadcasted_iota(jnp.int32, shape, dim)


def _load_row_tiles(ref, rows):
    return jnp.concatenate([ref[pl.ds(j, rows, stride=ROW_TILE), :] for j in range(ROW_TILE)], axis=1)


def _project_parts(x_ref, nmw_ref, win_ref, pm_ref, small_ref):
    state = {}

    def norm():
        x_in = x_ref[...]
        ms = jnp.mean(x_in * x_in, axis=-1, keepdims=True)
        state["h"] = (x_in * lax.rsqrt(ms + EPS) * nmw_ref[...]).astype(BF16)

    def chunk(n0):
        def run():
            pm_ref[:, n0:n0 + PROJ_COLS] = _dot(state["h"], win_ref[:, n0:n0 + PROJ_COLS]).astype(BF16)
        return run

    def small():
        small_ref[...] = _dot(state["h"], win_ref[:, N_MAIN:N_MAIN + N_SMALL])

    return [norm] + [chunk(n0) for n0 in range(0, N_MAIN, PROJ_COLS)] + [small]


def _mixer_kernel(x_ref, xn_ref, nmw_ref, win_ref, *refs, tiles_per_row):
    params, outs = refs[:13], refs[13:17]
    gla_state, ssd_state, conv_tail, mix_scr, cnt_scr, pm_a, pm_b, small_a, small_b = refs[17:]
    g = pl.program_id(0)

    @pl.when(g == 0)
    def _():
        cnt_scr[...] = jnp.zeros_like(cnt_scr)
        for part in _project_parts(x_ref, nmw_ref, win_ref, pm_a, small_a):
            part()

    @pl.when(lax.rem(g, tiles_per_row) == 0)
    def _():
        gla_state[...] = jnp.zeros_like(gla_state)
        ssd_state[...] = jnp.zeros_like(ssd_state)
        conv_tail[...] = jnp.zeros_like(conv_tail)

    for parity, (pm_cur, small_cur, pm_nxt, small_nxt) in enumerate(
            ((pm_a, small_a, pm_b, small_b), (pm_b, small_b, pm_a, small_a))):
        @pl.when((g & 1) == parity)
        def _(pm_cur=pm_cur, small_cur=small_cur, pm_nxt=pm_nxt, small_nxt=small_nxt):
            _mixer_tile(_project_parts(xn_ref, nmw_ref, win_ref, pm_nxt, small_nxt),
                        pm_cur, small_cur, x_ref, *params, *outs,
                        gla_state, ssd_state, conv_tail, mix_scr, cnt_scr)


def _mixer_tile(side_work, pm_ref, small_ref, x_ref, wup_ref, balpha_ref, gnw_ref, convw_ref, convb_ref,
                dtb_ref, aneg_ref, dexp_ref, snw_ref, wout_ref, nfw_ref, rw_ref, rb_ref,
                x1_ref, h2_ref, lg_ref, cnt_ref,
                gla_state, ssd_state, conv_tail, mix_scr, cnt_scr):
    side_work = list(side_work)

    def run_side(n=1):
        for _ in range(min(n, len(side_work))):
            side_work.pop(0)()

    small = small_ref[...]

    row = _iota((TL, TL), 0)
    col = _iota((TL, TL), 1)
    causal = col <= row
    cum64 = jnp.where(causal & ((row // GLA_CHUNK) == (col // GLA_CHUNK)), 1.0, 0.0).astype(BF16)
    cum128 = jnp.where(causal & ((row // SSD_CHUNK) == (col // SSD_CHUNK)), 1.0, 0.0).astype(BF16)

    xa = _dot_hi(small, wup_ref[...]) + balpha_ref[...]
    log_a = (jnp.minimum(xa, 0.0) - jnp.log1p(jnp.exp(-jnp.abs(xa)))) * (1.0 / GLA_GATE_NORM)
    bcum = _dot_sel_lhs(cum64, log_a)

    lane_kw = _iota((GLA_CHUNK, GLA_KW), 1)
    head_masks = [(lane_kw // GLA_DK) == h for h in range(GLA_HEADS)]
    lane_kw_s = _iota((GLA_DV, GLA_KW), 1)
    head_masks_s = [(lane_kw_s // GLA_DK) == h for h in range(GLA_HEADS)]
    tril64 = _iota((GLA_CHUNK, GLA_CHUNK), 1) <= _iota((GLA_CHUNK, GLA_CHUNK), 0)
    q_scale = GLA_DK ** -0.5

    for c in range(TL // GLA_CHUNK):
        rs = slice(c * GLA_CHUNK, (c + 1) * GLA_CHUNK)
        bc = bcum[rs]
        b_mid = bc[GLA_CHUNK // 2:GLA_CHUNK // 2 + 1]
        b_last = bc[GLA_CHUNK - 1:GLA_CHUNK]
        qc = pm_ref[rs, 0:GLA_KW].astype(F32) * q_scale
        kc = pm_ref[rs, GLA_KW:2 * GLA_KW].astype(F32)
        vc = pm_ref[rs, 2 * GLA_KW:2 * GLA_KW + GLA_WIDTH]
        q_in = (qc * jnp.exp(bc - b_mid)).astype(BF16)
        k_in = (kc * jnp.exp(b_mid - bc)).astype(BF16)
        q_st = (qc * jnp.exp(bc)).astype(BF16)
        k_st = (kc * jnp.exp(b_last - bc)).astype(BF16)
        st = gla_state[...]
        st_b = st.astype(BF16)
        zero_b = jnp.zeros_like(q_in)
        for h in range(GLA_HEADS):
            scores = _dot_nt(jnp.where(head_masks[h], q_in, zero_b), k_in)
            scores = jnp.where(tril64, scores, 0.0).astype(BF16)
            o_h = _dot(scores, vc[:, h * GLA_DV:(h + 1) * GLA_DV])
            o_h = o_h + _dot_nt(jnp.where(head_masks[h], q_st, zero_b), st_b)
            mix_scr[rs, h * GLA_DV:(h + 1) * GLA_DV] = o_h
        upd = _dot_tn(vc, k_st)
        new_st = st * jnp.exp(b_last)
        for h in range(GLA_HEADS):
            new_st = new_st + jnp.where(head_masks_s[h], upd[h * GLA_DV:(h + 1) * GLA_DV], 0.0)
        gla_state[...] = new_st
        run_side()

    xbc = pm_ref[:, 2048:3072].astype(F32)
    tail = conv_tail[...]
    conv_tail[...] = xbc[TL - 8:TL]
    row8 = _iota((8, SSD_CONV_CH), 0)
    conv = xbc * convw_ref[SSD_CONV - 1:SSD_CONV, :]
    for s in range(1, SSD_CONV):
        shifted = pltpu.roll(xbc, s, 0)
        head = jnp.where(row8 < s, pltpu.roll(tail, s, 0), shifted[0:8])
        shifted = jnp.concatenate([head, shifted[8:]], axis=0)
        conv = conv + shifted * convw_ref[SSD_CONV - 1 - s:SSD_CONV - s, :]
    act = _silu(conv + convb_ref[...])
    run_side()
    xs = act[:, 0:SSD_WIDTH]
    bm = act[:, SSD_WIDTH:SSD_WIDTH + SSD_GROUPS * SSD_STATE].astype(BF16)
    cm = act[:, SSD_WIDTH + SSD_GROUPS * SSD_STATE:].astype(BF16)

    dt_full = _softplus(small + dtb_ref[...])
    a_full = dt_full * aneg_ref[...]
    acum = _dot_sel_lhs(cum128, a_full)
    acum_t = acum.T

    e_row = _iota((N_SMALL, SSD_WIDTH), 0)
    e_col = _iota((N_SMALL, SSD_WIDTH), 1)
    spread64 = jnp.where(e_row == DT_COL + e_col // SSD_HEADDIM, 1.0, 0.0).astype(BF16)
    e_row2 = _iota((N_SMALL, SSD_HEADS * LANES), 0)
    e_col2 = _iota((N_SMALL, SSD_HEADS * LANES), 1)
    spread128 = jnp.where(e_row2 == DT_COL + e_col2 // LANES, 1.0, 0.0).astype(BF16)
    dt_e = _dot_sel_rhs(dt_full, spread64, terms=1)
    ac_e = _dot_sel_rhs(acum, spread64, terms=2)
    ac_w = _dot_sel_rhs(acum, spread128, terms=2)

    tril128 = _iota((SSD_CHUNK, SSD_CHUNK), 1) <= _iota((SSD_CHUNK, SSD_CHUNK), 0)
    lane_g = _iota((SSD_CHUNK, SSD_HPG * SSD_HEADDIM), 1)
    for c in range(TL // SSD_CHUNK):
        rs = slice(c * SSD_CHUNK, (c + 1) * SSD_CHUNK)
        ac_c = ac_e[rs]
        a_last = ac_c[SSD_CHUNK - 1:SSD_CHUNK]
        dt_c = dt_e[rs]
        xs_c = xs[rs]
        x_dt = (xs_c * dt_c).astype(BF16)
        x_w = (xs_c * (jnp.exp(a_last - ac_c) * dt_c)).astype(BF16)
        e_ac = jnp.exp(ac_c)
        for g in range(SSD_GROUPS):
            gs = slice(g * SSD_STATE, (g + 1) * SSD_STATE)
            ws = slice(g * SSD_HPG * SSD_HEADDIM, (g + 1) * SSD_HPG * SSD_HEADDIM)
            c_g = cm[rs, gs]
            b_g = bm[rs, gs]
            cb = _dot_nt(c_g, b_g)
            x_dt_g = x_dt[:, ws]
            lhs_parts = []
            rhs_parts = []
            for hh in range(SSD_HPG):
                h = g * SSD_HPG + hh
                seg = ac_w[rs, h * LANES:(h + 1) * LANES] - acum_t[DT_COL + h:DT_COL + h + 1, rs]
                lmat = jnp.where(tril128, jnp.exp(jnp.where(tril128, seg, 0.0)), 0.0)
                lhs_parts.append((cb * lmat).astype(BF16))
                rhs_parts.append(jnp.where((lane_g // SSD_HEADDIM) == hh, x_dt_g,
                                           jnp.zeros_like(x_dt_g)))
            intra = _dot(jnp.concatenate(lhs_parts, axis=1), jnp.concatenate(rhs_parts, axis=0))
            st = ssd_state[g]
            inter = _dot(c_g, st.astype(BF16)) * e_ac[:, ws]
            mix_scr[rs, GLA_WIDTH + g * 256:GLA_WIDTH + (g + 1) * 256] = intra + inter
            ssd_state[g] = st * jnp.exp(a_last[:, ws]) + _dot_tn(b_g, x_w[:, ws])
        run_side()

    run_side(len(side_work))

    o = mix_scr[:, 0:GLA_WIDTH]
    g_gate = _silu(pm_ref[:, 1024:1536].astype(F32))
    gla_parts = []
    for h in range(GLA_HEADS):
        o_h = o[:, h * GLA_DV:(h + 1) * GLA_DV]
        ms = jnp.mean(o_h * o_h, axis=-1, keepdims=True)
        gla_parts.append(o_h * lax.rsqrt(ms + GROUP_EPS))
    gla_out = jnp.concatenate(gla_parts, axis=1) * gnw_ref[...] * g_gate

    y = mix_scr[:, GLA_WIDTH:] + dexp_ref[...] * xs
    y = y * _silu(pm_ref[:, 1536:2048].astype(F32))
    ssd_parts = []
    for g in range(SSD_GROUPS):
        y_g = y[:, g * 256:(g + 1) * 256]
        ms = jnp.mean(y_g * y_g, axis=-1, keepdims=True)
        ssd_parts.append(y_g * lax.rsqrt(ms + GROUP_EPS))
    ssd_out = jnp.concatenate(ssd_parts, axis=1) * snw_ref[...]

    mixed = jnp.concatenate([gla_out, ssd_out], axis=1).astype(BF16)
    x1 = x_ref[...] + _dot(mixed, wout_ref[...])
    x1_ref[...] = x1

    ms = jnp.mean(x1 * x1, axis=-1, keepdims=True)
    h2 = x1 * lax.rsqrt(ms + EPS) * nfw_ref[...]
    h2_ref[...] = h2.astype(BF16)
    lg = _dot_hi_nt(rw_ref[...], h2) + rb_ref[...]
    lg_ref[...] = lg
    cnt_scr[...] = cnt_scr[...] + _tile_counts(_top4(lg[0:N_EXPERTS])[0])[1]
    cnt_ref[...] = cnt_scr[...]


def _mixer(x, nmw, w_all, wup, balpha, gnw, convw, convb, dtb, aneg, dexp, snw, wout, nfw, rw, rb):
    bsz, seqlen, _ = x.shape
    per_row = seqlen // TL
    steps = bsz * per_row

    def full(a):
        return pl.BlockSpec(a.shape, lambda g: (0,) * a.ndim)

    def tile(ahead):
        def index(g):
            tile_id = jnp.minimum(g + ahead, steps - 1)
            return (tile_id // per_row, tile_id % per_row, 0)
        return pl.BlockSpec((None, TL, D_MODEL), index)

    params = (nmw, w_all, wup, balpha, gnw, convw, convb, dtb, aneg, dexp, snw, wout, nfw, rw, rb)
    return pl.pallas_call(
        functools.partial(_mixer_kernel, tiles_per_row=per_row),
        grid=(steps,),
        in_specs=[tile(0), tile(1)] + [full(p) for p in params],
        out_specs=[tile(0), tile(0),
                   pl.BlockSpec((LANES, TL), lambda g: (0, g)),
                   pl.BlockSpec((N_EXPERTS, LANES), lambda g: (0, 0))],
        out_shape=[
            jax.ShapeDtypeStruct((bsz, seqlen, D_MODEL), F32),
            jax.ShapeDtypeStruct((bsz, seqlen, D_MODEL), BF16),
            jax.ShapeDtypeStruct((LANES, bsz * seqlen), F32),
            jax.ShapeDtypeStruct((N_EXPERTS, LANES), F32),
        ],
        scratch_shapes=[
            pltpu.VMEM((GLA_DV, GLA_KW), F32),
            pltpu.VMEM((SSD_GROUPS, SSD_STATE, SSD_HPG * SSD_HEADDIM), F32),
            pltpu.VMEM((8, SSD_CONV_CH), F32),
            pltpu.VMEM((TL, D_MODEL), F32),
            pltpu.VMEM((N_EXPERTS, LANES), F32),
            pltpu.VMEM((TL, N_MAIN), BF16),
            pltpu.VMEM((TL, N_MAIN), BF16),
            pltpu.VMEM((TL, N_SMALL), F32),
            pltpu.VMEM((TL, N_SMALL), F32),
        ],
        compiler_params=pltpu.CompilerParams(
            dimension_semantics=("arbitrary",), vmem_limit_bytes=VMEM_LIMIT),
        name="mixer",
    )(x, x, *params)


def _top4(lg):
    n_e, n_t = lg.shape
    row = _iota((n_e, n_t), 0)
    work = lg
    onehots = []
    vals = []
    for _ in range(TOP_K):
        m = jnp.max(work, axis=0, keepdims=True)
        idx = jnp.min(jnp.where(work == m, row, n_e), axis=0, keepdims=True)
        oh = row == idx
        onehots.append(oh)
        vals.append(m)
        work = jnp.where(oh, -jnp.inf, work)
    return onehots, vals


def _tile_counts(onehots):
    multi = jnp.where(onehots[0] | onehots[1] | onehots[2] | onehots[3], 1.0, 0.0).astype(BF16)
    return multi, _dot(multi, jnp.ones((multi.shape[1], LANES), BF16))


def _route_kernel(lg_ref, cnt_ref, info_ref, lp_ref, start_ref, len_ref, loff_ref, run_scr):
    @pl.when(pl.program_id(0) == 0)
    def _():
        run_scr[...] = jnp.zeros_like(run_scr)

    onehots, vals = _top4(lg_ref[0:N_EXPERTS, :])
    multi, tile_cnt = _tile_counts(onehots)
    counts = cnt_ref[...]
    padded = jnp.floor((counts + (TM - 1)) * (1.0 / TM)) * TM
    lower = jnp.where(_iota((N_EXPERTS, N_EXPERTS), 1) < _iota((N_EXPERTS, N_EXPERTS), 0),
                      1.0, 0.0).astype(BF16)
    offs = _dot_sel_lhs(lower, padded)
    loff = _dot_sel_lhs(lower, tile_cnt)
    before = jnp.where(_iota((TT, TT), 0) < _iota((TT, TT), 1), 1.0, 0.0).astype(BF16)
    rank = _dot(multi, before)
    start_ref[...] = (offs + run_scr[...]).astype(jnp.int32)
    len_ref[...] = tile_cnt.astype(jnp.int32)
    loff_ref[...] = loff.astype(jnp.int32)
    run_scr[...] = run_scr[...] + tile_cnt
    local = rank + jnp.concatenate([loff] * (TT // LANES), axis=1)
    exps = [jnp.exp(v - vals[0]) for v in vals]
    den = exps[0] + exps[1] + exps[2] + exps[3]
    lp_rows = [jnp.sum(jnp.where(oh, local, 0.0), axis=0, keepdims=True) for oh in onehots]
    lp_ref[...] = jnp.concatenate(lp_rows + [jnp.zeros((8 - TOP_K, TT), F32)], axis=0).astype(jnp.int32)
    gate_rows = [e / den for e in exps]
    info = jnp.concatenate(gate_rows + lp_rows + [jnp.zeros((LANES - 2 * TOP_K, TT), F32)], axis=0)
    info_ref[...] = info.T


def _route(logits_t, counts):
    t = logits_t.shape[1]
    steps = t // TT
    run_spec = pl.BlockSpec((N_EXPERTS, LANES), lambda i: (i, 0))
    run_shape = jax.ShapeDtypeStruct((steps * N_EXPERTS, LANES), jnp.int32)
    return pl.pallas_call(
        _route_kernel,
        grid=(steps,),
        in_specs=[pl.BlockSpec((LANES, TT), lambda i: (0, i)),
                  pl.BlockSpec((N_EXPERTS, LANES), lambda i: (0, 0))],
        out_specs=[
            pl.BlockSpec((TT, LANES), lambda i: (i, 0)),
            pl.BlockSpec((8, TT), lambda i: (0, i)),
            run_spec, run_spec, run_spec,
        ],
        out_shape=[
            jax.ShapeDtypeStruct((t, LANES), F32),
            jax.ShapeDtypeStruct((8, t), jnp.int32),
            run_shape, run_shape, run_shape,
        ],
        scratch_shapes=[pltpu.VMEM((N_EXPERTS, LANES), F32)],
        compiler_params=pltpu.CompilerParams(
            dimension_semantics=("arbitrary",), vmem_limit_bytes=VMEM_LIMIT),
        name="route",
    )(logits_t, counts)


def _rows(first, count):
    return pl.ds(pl.multiple_of(first * ROW_TILE, ROW_TILE), count * ROW_TILE)


def _wait_rows(stage_slot, hbm_ref, sem, to_hbm):
    for w in range(STAGE_ROWS // WAIT_ROWS):
        part = stage_slot.at[pl.ds(w * WAIT_ROWS * ROW_TILE, WAIT_ROWS * ROW_TILE), :]
        hbm = hbm_ref.at[pl.ds(0, WAIT_ROWS * ROW_TILE), :]
        src, dst = (part, hbm) if to_hbm else (hbm, part)
        pltpu.make_async_copy(src, dst, sem).wait()


def _for_each_piece(length, max_log2, fn):
    for b in reversed(range(max_log2 + 1)):
        size = 1 << b
        offset = lax.shift_left(lax.shift_right_logical(length, b + 1), b + 1)

        @pl.when((length & size) != 0)
        def _(offset=offset, size=size):
            fn(offset, size)


def _dispatch_kernel(start_ref, len_ref, loff_ref, pstart_ref, plen_ref,
                     lp_ref, h_ref, xs_ref, stage, zeros, sem, zsem):
    j = pl.program_id(0)
    slot = j & 1

    def wait_tile(s):
        _wait_rows(stage.at[s], xs_ref, sem.at[s], to_hbm=True)

    @pl.when(j == 0)
    def _():
        zeros[...] = jnp.zeros_like(zeros)

        def pad_pieces(e, wait):
            def piece(offset, size):
                cp = pltpu.make_async_copy(zeros.at[pl.ds(0, size * ROW_TILE), :],
                                           xs_ref.at[_rows(pstart_ref[e] + offset, size), :], zsem.at[e % 2])
                cp.wait() if wait else cp.start()
            _for_each_piece(plen_ref[e], TM_LOG2 - 1, piece)

        for e in range(N_EXPERTS):
            pad_pieces(e, wait=False)
            if e >= 1:
                pad_pieces(e - 1, wait=True)
        pad_pieces(N_EXPERTS - 1, wait=True)

        n_blocks = plen_ref[N_EXPERTS]

        def block_copy(blk):
            return pltpu.make_async_copy(
                zeros, xs_ref.at[_rows(pstart_ref[N_EXPERTS] + blk * (TM // 2), TM // 2), :], zsem.at[blk & 1])

        def put_block(blk, carry):
            block_copy(blk).start()

            @pl.when(blk >= 2)
            def _():
                block_copy(blk - 2).wait()
            return carry
        lax.fori_loop(0, n_blocks, put_block, 0)
        for back in (2, 1):
            @pl.when(n_blocks >= back)
            def _(back=back):
                block_copy(n_blocks - back).wait()

    @pl.when(j >= 2)
    def _():
        wait_tile(slot)

    h = h_ref[...]
    lp = lp_ref[...]
    for c in range(STAGE_ROWS // TT):
        r = _iota((TT, TT), 0) + c * TT
        hit = (r == lp[0:1]) | (r == lp[1:2]) | (r == lp[2:3]) | (r == lp[3:4])
        rows = _dot(jnp.where(hit, 1.0, 0.0).astype(BF16), h)
        for q in range(ROW_TILE):
            stage[slot, pl.ds(c * TT * ROW_TILE + q, TT, stride=ROW_TILE), :] = rows[:, q * LANES:(q + 1) * LANES]

    for e in range(N_EXPERTS):
        src = loff_ref[j * N_EXPERTS + e]
        dst = start_ref[j * N_EXPERTS + e]

        def put(offset, size, src=src, dst=dst):
            pltpu.make_async_copy(stage.at[slot, _rows(src + offset, size), :],
                                  xs_ref.at[_rows(dst + offset, size), :], sem.at[slot]).start()
        _for_each_piece(len_ref[j * N_EXPERTS + e], TT_LOG2, put)

    @pl.when(j == pl.num_programs(0) - 1)
    def _():
        wait_tile(1 - slot)
        wait_tile(slot)


def _dispatch(starts, lens, loffs, pad_starts, pad_lens, lp, h2, p_rows):
    t = h2.shape[0]
    return pl.pallas_call(
        _dispatch_kernel,
        grid_spec=pltpu.PrefetchScalarGridSpec(
            num_scalar_prefetch=5,
            grid=(t // TT,),
            in_specs=[
                pl.BlockSpec((8, TT), lambda j, *_: (0, j)),
                pl.BlockSpec((TT, D_MODEL), lambda j, *_: (j, 0)),
            ],
            out_specs=pl.BlockSpec(memory_space=pl.ANY),
            scratch_shapes=[
                pltpu.VMEM((2, STAGE_ROWS * ROW_TILE, LANES), F32),
                pltpu.VMEM((TM // 2 * ROW_TILE, LANES), F32),
                pltpu.SemaphoreType.DMA((2,)),
                pltpu.SemaphoreType.DMA((2,)),
            ],
        ),
        out_shape=jax.ShapeDtypeStruct((p_rows * ROW_TILE, LANES), F32),
        compiler_params=pltpu.CompilerParams(
            dimension_semantics=("arbitrary",), vmem_limit_bytes=VMEM_LIMIT),
        name="dispatch",
    )(starts, lens, loffs, pad_starts, pad_lens, lp, h2)


def _experts_kernel(te_ref, nv_ref, nxt_ref, par_ref, x_ref, wg_ref, bg_ref, wu_ref, bu_ref, wd_ref, bd_ref,
                    o_ref, act, wbuf, wg_b, wu_b, wd_b, wsem):
    i = pl.program_id(0)
    prev = jnp.maximum(i - 1, 0)

    def fetch(expert, s):
        return [pltpu.make_async_copy(w_ref.at[expert], wbuf.at[s, m], wsem.at[s])
                for m, w_ref in enumerate((wg_ref, wu_ref, wd_ref))]

    @pl.when(i >= nv_ref[0])
    def _():
        o_ref[...] = jnp.zeros_like(o_ref)

    @pl.when(i < nv_ref[0])
    def _():
        expert = te_ref[i]
        s = par_ref[i]

        @pl.when(i == 0)
        def _():
            for cp in fetch(expert, s):
                cp.start()

        @pl.when((i == 0) | (expert != te_ref[prev]))
        def _():
            for cp in fetch(expert, s):
                cp.wait()
            for c in range(N_CHUNKS):
                cs = slice(c * N_COLS, (c + 1) * N_COLS)
                wg_b[c] = wbuf[s, 0, :, cs].astype(BF16)
                wu_b[c] = wbuf[s, 1, :, cs].astype(BF16)
                wd_b[c] = wbuf[s, 2, :, cs].astype(BF16)

            @pl.when(nxt_ref[i] != expert)
            def _():
                for cp in fetch(nxt_ref[i], 1 - s):
                    cp.start()

        xb = _load_row_tiles(x_ref, TM).astype(BF16)
        for n in range(N_CHUNKS):
            gate = _dot(xb, wg_b[n]) + bg_ref[n:n + 1, :]
            up = _dot(xb, wu_b[n]) + bu_ref[n:n + 1, :]
            gate = jnp.minimum(gate, SWIGLU_LIMIT)
            up = jnp.clip(up, -SWIGLU_LIMIT, SWIGLU_LIMIT)
            act[n] = ((up + 1.0) * (gate * jax.nn.sigmoid(SWIGLU_ALPHA * gate))).astype(BF16)
        a = jnp.concatenate([act[c] for c in range(N_CHUNKS)], axis=1)
        for n in range(N_CHUNKS):
            out = _dot(a, wd_b[n]) + bd_ref[n:n + 1, :]
            for q in range(N_COLS // LANES):
                o_ref[pl.ds(n * (N_COLS // LANES) + q, TM, stride=ROW_TILE), :] = (
                    out[:, q * LANES:(q + 1) * LANES])


def _experts(tile_expert, n_valid, next_expert, parity, xs, wg, bg, wu, bu, wd, bd):
    n_tiles = xs.shape[0] // (TM * ROW_TILE)

    def x_map(i, te, nv, nx, pr):
        return (jnp.minimum(i, nv[0] - 1), 0)

    def b_map(i, te, nv, nx, pr):
        return (te[i], 0, 0)

    w_spec = pl.BlockSpec(memory_space=pl.ANY)
    b_spec = pl.BlockSpec((None, N_CHUNKS, N_COLS), b_map)
    return pl.pallas_call(
        _experts_kernel,
        grid_spec=pltpu.PrefetchScalarGridSpec(
            num_scalar_prefetch=4,
            grid=(n_tiles,),
            in_specs=[pl.BlockSpec((TM * ROW_TILE, LANES), x_map),
                      w_spec, b_spec, w_spec, b_spec, w_spec, b_spec],
            out_specs=pl.BlockSpec((TM * ROW_TILE, LANES), lambda i, te, nv, nx, pr: (i, 0)),
            scratch_shapes=[
                pltpu.VMEM((N_CHUNKS, TM, N_COLS), BF16),
                pltpu.VMEM((2, 3, D_MODEL, D_MODEL), F32),
                pltpu.VMEM((N_CHUNKS, D_MODEL, N_COLS), BF16),
                pltpu.VMEM((N_CHUNKS, D_MODEL, N_COLS), BF16),
                pltpu.VMEM((N_CHUNKS, D_MODEL, N_COLS), BF16),
                pltpu.SemaphoreType.DMA((2,)),
            ],
        ),
        out_shape=jax.ShapeDtypeStruct(xs.shape, F32),
        compiler_params=pltpu.CompilerParams(
            dimension_semantics=("arbitrary",), vmem_limit_bytes=VMEM_LIMIT),
        name="experts",
    )(tile_expert, n_valid, next_expert, parity, xs, wg, bg, wu, bu, wd, bd)


def _combine_kernel(start_ref, len_ref, loff_ref, info_ref, x1_ref, fw_ref, eo_ref, out_ref, stage, sem):
    j = pl.program_id(0)
    slot = j & 1
    last = pl.num_programs(0) - 1

    def fetch(tile, s):
        for e in range(N_EXPERTS):
            src = start_ref[tile * N_EXPERTS + e]
            dst = loff_ref[tile * N_EXPERTS + e]

            def get(offset, size, src=src, dst=dst):
                pltpu.make_async_copy(eo_ref.at[_rows(src + offset, size), :],
                                      stage.at[s, _rows(dst + offset, size), :], sem.at[s]).start()
            _for_each_piece(len_ref[tile * N_EXPERTS + e], TT_LOG2, get)

    @pl.when(j == 0)
    def _():
        fetch(0, 0)

    @pl.when(j < last)
    def _():
        fetch(j + 1, 1 - slot)

    _wait_rows(stage.at[slot], eo_ref, sem.at[slot], to_hbm=False)

    info = info_ref[...]
    y = x1_ref[...]
    for c in range(STAGE_ROWS // TT):
        r = (_iota((TT, TT), 1) + c * TT).astype(F32)
        g = jnp.zeros((TT, TT), F32)
        for k in range(TOP_K):
            g = g + jnp.where(r == info[:, TOP_K + k:TOP_K + k + 1], info[:, k:k + 1], 0.0)
        g_hi = g.astype(BF16)
        g_lo = (g - g_hi.astype(F32)).astype(BF16)
        rows = jnp.concatenate(
            [stage[slot, pl.ds(c * TT * ROW_TILE + q, TT, stride=ROW_TILE), :] for q in range(ROW_TILE)],
            axis=1).astype(BF16)
        y = y + _dot(g_hi, rows) + _dot(g_lo, rows)
    ms = jnp.mean(y * y, axis=-1, keepdims=True)
    out_ref[...] = y * lax.rsqrt(ms + EPS) * fw_ref[...]


def _combine(starts, lens, loffs, info, x1, final_w, eo):
    t = x1.shape[0]
    return pl.pallas_call(
        _combine_kernel,
        grid_spec=pltpu.PrefetchScalarGridSpec(
            num_scalar_prefetch=3,
            grid=(t // TT,),
            in_specs=[
                pl.BlockSpec((TT, LANES), lambda j, *_: (j, 0)),
                pl.BlockSpec((TT, D_MODEL), lambda j, *_: (j, 0)),
                pl.BlockSpec((1, D_MODEL), lambda j, *_: (0, 0)),
                pl.BlockSpec(memory_space=pl.ANY),
            ],
            out_specs=pl.BlockSpec((TT, D_MODEL), lambda j, *_: (j, 0)),
            scratch_shapes=[
                pltpu.VMEM((2, STAGE_ROWS * ROW_TILE, LANES), F32),
                pltpu.SemaphoreType.DMA((2,)),
            ],
        ),
        out_shape=jax.ShapeDtypeStruct((t, D_MODEL), F32),
        compiler_params=pltpu.CompilerParams(
            dimension_semantics=("arbitrary",), vmem_limit_bytes=VMEM_LIMIT),
        name="combine",
    )(starts, lens, loffs, info, x1, final_w, eo)


def _pad_lanes(v, offset, fill=0.0):
    row = jnp.full((1, LANES), fill, F32)
    return row.at[0, offset:offset + v.shape[0]].set(v.astype(F32))


def kernel(x, norm_mix_w, w_in, gla_w_alpha_up, gla_b_alpha, gla_norm_w, ssd_conv_w, ssd_conv_b,
           ssd_dt_bias, ssd_A_log, ssd_D, ssd_norm_w, w_out, norm_ffn_w, router_w, router_b,
           moe_w_gate, moe_b_gate, moe_w_up, moe_b_up, moe_w_down, moe_b_down, final_norm_w):
    bsz, seqlen, d = x.shape
    t = bsz * seqlen
    depth = w_in.shape[0]
    assert depth == 1, "the final RMSNorm is fused into the (single) layer's combine step"
    p_rows = t * TOP_K + N_EXPERTS * TM
    n_tiles = p_rows // TM
    for l in range(depth):
        w = w_in[l]
        w_all = jnp.concatenate(
            [w[:, 0:1536], w[:, 1552:3088], w[:, 1536:1552], w[:, 3088:3096],
             jnp.zeros((d, N_SMALL - GLA_GATE_RANK - SSD_HEADS), w.dtype)], axis=1).astype(BF16)
        wup = jnp.zeros((N_SMALL, GLA_KW), F32).at[0:GLA_GATE_RANK].set(gla_w_alpha_up[l])
        dtb = _pad_lanes(ssd_dt_bias[l], DT_COL)
        aneg = _pad_lanes(-jnp.exp(ssd_A_log[l].astype(F32)), DT_COL)
        dexp = jnp.repeat(ssd_D[l].astype(F32), SSD_HEADDIM)[None, :]
        rw = jnp.zeros((LANES, d), F32).at[0:N_EXPERTS].set(router_w[l].T)
        rb = jnp.zeros((LANES, TL), F32).at[0:N_EXPERTS].set(
            jnp.broadcast_to(router_b[l].astype(F32)[:, None], (N_EXPERTS, TL)))

        x1, h2, logits, counts = _mixer(
            x, norm_mix_w[l][None, :], w_all, wup, gla_b_alpha[l][None, :], gla_norm_w[l][None, :], ssd_conv_w[l],
            ssd_conv_b[l][None, :], dtb, aneg, dexp, ssd_norm_w[l][None, :],
            w_out[l].astype(BF16), norm_ffn_w[l][None, :], rw, rb)

        info, lp, starts, lens, loffs = _route(logits, counts)
        starts, lens, loffs = starts[:, 0], lens[:, 0], loffs[:, 0]

        cnt = counts[:, 0].astype(jnp.int32)
        padded = ((cnt + TM - 1) // TM) * TM
        ends = jnp.cumsum(padded)
        n_valid = (ends[-1] // TM).astype(jnp.int32)
        tile_starts = jnp.arange(n_tiles, dtype=jnp.int32) * TM
        tile_expert = jnp.sum(tile_starts[:, None] >= ends[None, :], axis=1).astype(jnp.int32)
        last_expert = tile_expert[jnp.maximum(n_valid - 1, 0)]
        tile_expert = jnp.where(tile_starts < ends[-1], tile_expert, last_expert)
        next_first = jnp.minimum(ends[tile_expert] // TM, n_valid - 1)
        next_expert = tile_expert[next_first]
        new_group = jnp.concatenate([jnp.ones((1,), jnp.int32),
                                     (tile_expert[1:] != tile_expert[:-1]).astype(jnp.int32)])
        parity = (jnp.cumsum(new_group) - 1) % 2
        pad_starts = jnp.concatenate([ends - padded + cnt, ends[-1:]])
        pad_lens = jnp.concatenate([padded - cnt, (p_rows - ends[-1:]) // (TM // 2)])

        xs = _dispatch(starts, lens, loffs, pad_starts, pad_lens, lp, h2.reshape(t, d), p_rows)
        eo = _experts(tile_expert, n_valid.reshape(1), next_expert, parity.astype(jnp.int32), xs,
                      moe_w_gate[l], moe_b_gate[l].reshape(N_EXPERTS, N_CHUNKS, N_COLS),
                      moe_w_up[l], moe_b_up[l].reshape(N_EXPERTS, N_CHUNKS, N_COLS),
                      moe_w_down[l], moe_b_down[l].reshape(N_EXPERTS, N_CHUNKS, N_COLS))
        x = _combine(starts, lens, loffs, info, x1.reshape(t, d), final_norm_w[None, :], eo
                     ).reshape(bsz, seqlen, d)
    return x
```

```python
import functools

import jax
import jax.numpy as jnp
from jax import lax
from jax.experimental import pallas as pl
from jax.experimental.pallas import tpu as pltpu

F32 = jnp.float32
BF16 = jnp.bfloat16

D_MODEL = 1024
GLA_WIDTH = 512
GLA_HEADS = 4
GLA_DV = 128
GLA_DK = 64
GLA_KW = 256
GLA_GATE_RANK = 16
GLA_GATE_NORM = 16.0
SSD_WIDTH = 512
SSD_HEADDIM = 64
SSD_HEADS = 8
SSD_GROUPS = 2
SSD_HPG = 4
SSD_STATE = 128
SSD_CONV = 4
SSD_CONV_CH = 1024
N_EXPERTS = 32
TOP_K = 4
SWIGLU_LIMIT = 7.0
SWIGLU_ALPHA = 1.702
EPS = 1e-6
GROUP_EPS = 1e-5

LANES = 128
ROW_TILE = D_MODEL // LANES
N_MAIN = 3072
N_SMALL = LANES
DT_COL = GLA_GATE_RANK

GLA_CHUNK = 64
SSD_CHUNK = 128
PROJ_COLS = 512
TL = 256
TT_LOG2 = 8
TT = 1 << TT_LOG2
STAGE_ROWS = 4 * TT
TM_LOG2 = 9
TM = 1 << TM_LOG2
WAIT_ROWS = 512
assert STAGE_ROWS * D_MODEL * 4 <= (1 << 17) * 32
N_COLS = 256
N_CHUNKS = D_MODEL // N_COLS
VMEM_LIMIT = 56 * 1024 * 1024


def _dot(a, b):
    return jnp.dot(a, b, preferred_element_type=F32)


def _dot_nt(a, b):
    return lax.dot_general(a, b, (((1,), (1,)), ((), ())), preferred_element_type=F32)


def _dot_tn(a, b):
    return lax.dot_general(a, b, (((0,), (0,)), ((), ())), preferred_element_type=F32)


def _split3(a):
    hi = a.astype(BF16)
    r1 = a - hi.astype(F32)
    mid = r1.astype(BF16)
    lo = (r1 - mid.astype(F32)).astype(BF16)
    return hi, mid, lo


def _dot_sel_lhs(sel, a):
    hi, mid, lo = _split3(a)
    return _dot(sel, hi) + _dot(sel, mid) + _dot(sel, lo)


def _dot_sel_rhs(a, sel, terms=3):
    parts = _split3(a)[:terms]
    out = _dot(parts[0], sel)
    for p in parts[1:]:
        out = out + _dot(p, sel)
    return out


def _dot_hi(a, b):
    a_hi = a.astype(BF16)
    a_lo = (a - a_hi.astype(F32)).astype(BF16)
    b_hi = b.astype(BF16)
    b_lo = (b - b_hi.astype(F32)).astype(BF16)
    return _dot(a_hi, b_hi) + _dot(a_lo, b_hi) + _dot(a_hi, b_lo)


def _dot_hi_nt(a, b):
    a_hi = a.astype(BF16)
    a_lo = (a - a_hi.astype(F32)).astype(BF16)
    b_hi = b.astype(BF16)
    b_lo = (b - b_hi.astype(F32)).astype(BF16)
    return _dot_nt(a_hi, b_hi) + _dot_nt(a_lo, b_hi) + _dot_nt(a_hi, b_lo)


def _softplus(x):
    return jnp.maximum(x, 0.0) + jnp.log1p(jnp.exp(-jnp.abs(x)))


def _silu(x):
    return x * jax.nn.sigmoid(x)


def _iota(shape, dim):
    return lax.broadcasted_iota(jnp.int32, shape, dim)


def _load_row_tiles(ref, rows):
    return jnp.concatenate([ref[pl.ds(j, rows, stride=ROW_TILE), :] for j in range(ROW_TILE)], axis=1)


def _project_parts(x_ref, nmw_ref, win_ref, pm_ref, small_ref):
    state = {}

    def norm():
        x_in = x_ref[...]
        ms = jnp.mean(x_in * x_in, axis=-1, keepdims=True)
        state["h"] = (x_in * lax.rsqrt(ms + EPS) * nmw_ref[...]).astype(BF16)

    def chunk(n0):
        def run():
            pm_ref[:, n0:n0 + PROJ_COLS] = _dot(state["h"], win_ref[:, n0:n0 + PROJ_COLS]).astype(BF16)
        return run

    def small():
        small_ref[...] = _dot(state["h"], win_ref[:, N_MAIN:N_MAIN + N_SMALL])

    return [norm] + [chunk(n0) for n0 in range(0, N_MAIN, PROJ_COLS)] + [small]


def _mixer_kernel(x_ref, xn_ref, nmw_ref, win_ref, *refs, tiles_per_row):
    params, outs = refs[:13], refs[13:17]
    gla_state, ssd_state, conv_tail, mix_scr, cnt_scr, pm_a, pm_b, small_a, small_b = refs[17:]
    g = pl.program_id(0)

    @pl.when(g == 0)
    def _():
        cnt_scr[...] = jnp.zeros_like(cnt_scr)
        for part in _project_parts(x_ref, nmw_ref, win_ref, pm_a, small_a):
            part()

    @pl.when(lax.rem(g, tiles_per_row) == 0)
    def _():
        gla_state[...] = jnp.zeros_like(gla_state)
        ssd_state[...] = jnp.zeros_like(ssd_state)
        conv_tail[...] = jnp.zeros_like(conv_tail)

    for parity, (pm_cur, small_cur, pm_nxt, small_nxt) in enumerate(
            ((pm_a, small_a, pm_b, small_b), (pm_b, small_b, pm_a, small_a))):
        @pl.when((g & 1) == parity)
        def _(pm_cur=pm_cur, small_cur=small_cur, pm_nxt=pm_nxt, small_nxt=small_nxt):
            _mixer_tile(_project_parts(xn_ref, nmw_ref, win_ref, pm_nxt, small_nxt),
                        pm_cur, small_cur, x_ref, *params, *outs,
                        gla_state, ssd_state, conv_tail, mix_scr, cnt_scr)


def _mixer_tile(side_work, pm_ref, small_ref, x_ref, wup_ref, balpha_ref, gnw_ref, convw_ref, convb_ref,
                dtb_ref, aneg_ref, dexp_ref, snw_ref, wout_ref, nfw_ref, rw_ref, rb_ref,
                x1_ref, h2_ref, lg_ref, cnt_ref,
                gla_state, ssd_state, conv_tail, mix_scr, cnt_scr):
    side_work = list(side_work)

    def run_side(n=1):
        for _ in range(min(n, len(side_work))):
            side_work.pop(0)()

    small = small_ref[...]

    row = _iota((TL, TL), 0)
    col = _iota((TL, TL), 1)
    causal = col <= row
    cum64 = jnp.where(causal & ((row // GLA_CHUNK) == (col // GLA_CHUNK)), 1.0, 0.0).astype(BF16)
    cum128 = jnp.where(causal & ((row // SSD_CHUNK) == (col // SSD_CHUNK)), 1.0, 0.0).astype(BF16)

    xa = _dot_hi(small, wup_ref[...]) + balpha_ref[...]
    log_a = (jnp.minimum(xa, 0.0) - jnp.log1p(jnp.exp(-jnp.abs(xa)))) * (1.0 / GLA_GATE_NORM)
    bcum = _dot_sel_lhs(cum64, log_a)

    lane_kw = _iota((GLA_CHUNK, GLA_KW), 1)
    head_masks = [(lane_kw // GLA_DK) == h for h in range(GLA_HEADS)]
    lane_kw_s = _iota((GLA_DV, GLA_KW), 1)
    head_masks_s = [(lane_kw_s // GLA_DK) == h for h in range(GLA_HEADS)]
    tril64 = _iota((GLA_CHUNK, GLA_CHUNK), 1) <= _iota((GLA_CHUNK, GLA_CHUNK), 0)
    q_scale = GLA_DK ** -0.5

    for c in range(TL // GLA_CHUNK):
        rs = slice(c * GLA_CHUNK, (c + 1) * GLA_CHUNK)
        bc = bcum[rs]
        b_mid = bc[GLA_CHUNK // 2:GLA_CHUNK // 2 + 1]
        b_last = bc[GLA_CHUNK - 1:GLA_CHUNK]
        qc = pm_ref[rs, 0:GLA_KW].astype(F32) * q_scale
        kc = pm_ref[rs, GLA_KW:2 * GLA_KW].astype(F32)
        vc = pm_ref[rs, 2 * GLA_KW:2 * GLA_KW + GLA_WIDTH]
        q_in = (qc * jnp.exp(bc - b_mid)).astype(BF16)
        k_in = (kc * jnp.exp(b_mid - bc)).astype(BF16)
        q_st = (qc * jnp.exp(bc)).astype(BF16)
        k_st = (kc * jnp.exp(b_last - bc)).astype(BF16)
        st = gla_state[...]
        st_b = st.astype(BF16)
        zero_b = jnp.zeros_like(q_in)
        for h in range(GLA_HEADS):
            scores = _dot_nt(jnp.where(head_masks[h], q_in, zero_b), k_in)
            scores = jnp.where(tril64, scores, 0.0).astype(BF16)
            o_h = _dot(scores, vc[:, h * GLA_DV:(h + 1) * GLA_DV])
            o_h = o_h + _dot_nt(jnp.where(head_masks[h], q_st, zero_b), st_b)
            mix_scr[rs, h * GLA_DV:(h + 1) * GLA_DV] = o_h
        upd = _dot_tn(vc, k_st)
        new_st = st * jnp.exp(b_last)
        for h in range(GLA_HEADS):
            new_st = new_st + jnp.where(head_masks_s[h], upd[h * GLA_DV:(h + 1) * GLA_DV], 0.0)
        gla_state[...] = new_st
        run_side()

    xbc = pm_ref[:, 2048:3072].astype(F32)
    tail = conv_tail[...]
    conv_tail[...] = xbc[TL - 8:TL]
    row8 = _iota((8, SSD_CONV_CH), 0)
    conv = xbc * convw_ref[SSD_CONV - 1:SSD_CONV, :]
    for s in range(1, SSD_CONV):
        shifted = pltpu.roll(xbc, s, 0)
        head = jnp.where(row8 < s, pltpu.roll(tail, s, 0), shifted[0:8])
        shifted = jnp.concatenate([head, shifted[8:]], axis=0)
        conv = conv + shifted * convw_ref[SSD_CONV - 1 - s:SSD_CONV - s, :]
    act = _silu(conv + convb_ref[...])
    run_side()
    xs = act[:, 0:SSD_WIDTH]
    bm = act[:, SSD_WIDTH:SSD_WIDTH + SSD_GROUPS * SSD_STATE].astype(BF16)
    cm = act[:, SSD_WIDTH + SSD_GROUPS * SSD_STATE:].astype(BF16)

    dt_full = _softplus(small + dtb_ref[...])
    a_full = dt_full * aneg_ref[...]
    acum = _dot_sel_lhs(cum128, a_full)
    acum_t = acum.T

    e_row = _iota((N_SMALL, SSD_WIDTH), 0)
    e_col = _iota((N_SMALL, SSD_WIDTH), 1)
    spread64 = jnp.where(e_row == DT_COL + e_col // SSD_HEADDIM, 1.0, 0.0).astype(BF16)
    e_row2 = _iota((N_SMALL, SSD_HEADS * LANES), 0)
    e_col2 = _iota((N_SMALL, SSD_HEADS * LANES), 1)
    spread128 = jnp.where(e_row2 == DT_COL + e_col2 // LANES, 1.0, 0.0).astype(BF16)
    dt_e = _dot_sel_rhs(dt_full, spread64, terms=1)
    ac_e = _dot_sel_rhs(acum, spread64, terms=2)
    ac_w = _dot_sel_rhs(acum, spread128, terms=2)

    tril128 = _iota((SSD_CHUNK, SSD_CHUNK), 1) <= _iota((SSD_CHUNK, SSD_CHUNK), 0)
    lane_g = _iota((SSD_CHUNK, SSD_HPG * SSD_HEADDIM), 1)
    for c in range(TL // SSD_CHUNK):
        rs = slice(c * SSD_CHUNK, (c + 1) * SSD_CHUNK)
        ac_c = ac_e[rs]
        a_last = ac_c[SSD_CHUNK - 1:SSD_CHUNK]
        dt_c = dt_e[rs]
        xs_c = xs[rs]
        x_dt = (xs_c * dt_c).astype(BF16)
        x_w = (xs_c * (jnp.exp(a_last - ac_c) * dt_c)).astype(BF16)
        e_ac = jnp.exp(ac_c)
        for g in range(SSD_GROUPS):
            gs = slice(g * SSD_STATE, (g + 1) * SSD_STATE)
            ws = slice(g * SSD_HPG * SSD_HEADDIM, (g + 1) * SSD_HPG * SSD_HEADDIM)
            c_g = cm[rs, gs]
            b_g = bm[rs, gs]
            cb = _dot_nt(c_g, b_g)
            x_dt_g = x_dt[:, ws]
            lhs_parts = []
            rhs_parts = []
            for hh in range(SSD_HPG):
                h = g * SSD_HPG + hh
                seg = ac_w[rs, h * LANES:(h + 1) * LANES] - acum_t[DT_COL + h:DT_COL + h + 1, rs]
                lmat = jnp.where(tril128, jnp.exp(jnp.where(tril128, seg, 0.0)), 0.0)
                lhs_parts.append((cb * lmat).astype(BF16))
                rhs_parts.append(jnp.where((lane_g // SSD_HEADDIM) == hh, x_dt_g,
                                           jnp.zeros_like(x_dt_g)))
            intra = _dot(jnp.concatenate(lhs_parts, axis=1), jnp.concatenate(rhs_parts, axis=0))
            st = ssd_state[g]
            inter = _dot(c_g, st.astype(BF16)) * e_ac[:, ws]
            mix_scr[rs, GLA_WIDTH + g * 256:GLA_WIDTH + (g + 1) * 256] = intra + inter
            ssd_state[g] = st * jnp.exp(a_last[:, ws]) + _dot_tn(b_g, x_w[:, ws])
        run_side()

    run_side(len(side_work))

    o = mix_scr[:, 0:GLA_WIDTH]
    g_gate = _silu(pm_ref[:, 1024:1536].astype(F32))
    gla_parts = []
    for h in range(GLA_HEADS):
        o_h = o[:, h * GLA_DV:(h + 1) * GLA_DV]
        ms = jnp.mean(o_h * o_h, axis=-1, keepdims=True)
        gla_parts.append(o_h * lax.rsqrt(ms + GROUP_EPS))
    gla_out = jnp.concatenate(gla_parts, axis=1) * gnw_ref[...] * g_gate

    y = mix_scr[:, GLA_WIDTH:] + dexp_ref[...] * xs
    y = y * _silu(pm_ref[:, 1536:2048].astype(F32))
    ssd_parts = []
    for g in range(SSD_GROUPS):
        y_g = y[:, g * 256:(g + 1) * 256]
        ms = jnp.mean(y_g * y_g, axis=-1, keepdims=True)
        ssd_parts.append(y_g * lax.rsqrt(ms + GROUP_EPS))
    ssd_out = jnp.concatenate(ssd_parts, axis=1) * snw_ref[...]

    mixed = jnp.concatenate([gla_out, ssd_out], axis=1).astype(BF16)
    x1 = x_ref[...] + _dot(mixed, wout_ref[...])
    x1_ref[...] = x1

    ms = jnp.mean(x1 * x1, axis=-1, keepdims=True)
    h2 = x1 * lax.rsqrt(ms + EPS) * nfw_ref[...]
    h2_ref[...] = h2.astype(BF16)
    lg = _dot_hi_nt(rw_ref[...], h2) + rb_ref[...]
    lg_ref[...] = lg
    cnt_scr[...] = cnt_scr[...] + _tile_counts(_top4(lg[0:N_EXPERTS])[0])[1]
    cnt_ref[...] = cnt_scr[...]


def _mixer(x, nmw, w_all, wup, balpha, gnw, convw, convb, dtb, aneg, dexp, snw, wout, nfw, rw, rb):
    bsz, seqlen, _ = x.shape
    per_row = seqlen // TL
    steps = bsz * per_row

    def full(a):
        return pl.BlockSpec(a.shape, lambda g: (0,) * a.ndim)

    def tile(ahead):
        def index(g):
            tile_id = jnp.minimum(g + ahead, steps - 1)
            return (tile_id // per_row, tile_id % per_row, 0)
        return pl.BlockSpec((None, TL, D_MODEL), index)

    params = (nmw, w_all, wup, balpha, gnw, convw, convb, dtb, aneg, dexp, snw, wout, nfw, rw, rb)
    return pl.pallas_call(
        functools.partial(_mixer_kernel, tiles_per_row=per_row),
        grid=(steps,),
        in_specs=[tile(0), tile(1)] + [full(p) for p in params],
        out_specs=[tile(0), tile(0),
                   pl.BlockSpec((LANES, TL), lambda g: (0, g)),
                   pl.BlockSpec((N_EXPERTS, LANES), lambda g: (0, 0))],
        out_shape=[
            jax.ShapeDtypeStruct((bsz, seqlen, D_MODEL), F32),
            jax.ShapeDtypeStruct((bsz, seqlen, D_MODEL), BF16),
            jax.ShapeDtypeStruct((LANES, bsz * seqlen), F32),
            jax.ShapeDtypeStruct((N_EXPERTS, LANES), F32),
        ],
        scratch_shapes=[
            pltpu.VMEM((GLA_DV, GLA_KW), F32),
            pltpu.VMEM((SSD_GROUPS, SSD_STATE, SSD_HPG * SSD_HEADDIM), F32),
            pltpu.VMEM((8, SSD_CONV_CH), F32),
            pltpu.VMEM((TL, D_MODEL), F32),
            pltpu.VMEM((N_EXPERTS, LANES), F32),
            pltpu.VMEM((TL, N_MAIN), BF16),
            pltpu.VMEM((TL, N_MAIN), BF16),
            pltpu.VMEM((TL, N_SMALL), F32),
            pltpu.VMEM((TL, N_SMALL), F32),
        ],
        compiler_params=pltpu.CompilerParams(
            dimension_semantics=("arbitrary",), vmem_limit_bytes=VMEM_LIMIT),
        name="mixer",
    )(x, x, *params)


def _top4(lg):
    n_e, n_t = lg.shape
    row = _iota((n_e, n_t), 0)
    work = lg
    onehots = []
    vals = []
    for _ in range(TOP_K):
        m = jnp.max(work, axis=0, keepdims=True)
        idx = jnp.min(jnp.where(work == m, row, n_e), axis=0, keepdims=True)
        oh = row == idx
        onehots.append(oh)
        vals.append(m)
        work = jnp.where(oh, -jnp.inf, work)
    return onehots, vals


def _tile_counts(onehots):
    multi = jnp.where(onehots[0] | onehots[1] | onehots[2] | onehots[3], 1.0, 0.0).astype(BF16)
    return multi, _dot(multi, jnp.ones((multi.shape[1], LANES), BF16))


def _route_kernel(lg_ref, cnt_ref, info_ref, lp_ref, start_ref, len_ref, loff_ref, run_scr):
    @pl.when(pl.program_id(0) == 0)
    def _():
        run_scr[...] = jnp.zeros_like(run_scr)

    onehots, vals = _top4(lg_ref[0:N_EXPERTS, :])
    multi, tile_cnt = _tile_counts(onehots)
    counts = cnt_ref[...]
    padded = jnp.floor((counts + (TM - 1)) * (1.0 / TM)) * TM
    lower = jnp.where(_iota((N_EXPERTS, N_EXPERTS), 1) < _iota((N_EXPERTS, N_EXPERTS), 0),
                      1.0, 0.0).astype(BF16)
    offs = _dot_sel_lhs(lower, padded)
    loff = _dot_sel_lhs(lower, tile_cnt)
    before = jnp.where(_iota((TT, TT), 0) < _iota((TT, TT), 1), 1.0, 0.0).astype(BF16)
    rank = _dot(multi, before)
    start_ref[...] = (offs + run_scr[...]).astype(jnp.int32)
    len_ref[...] = tile_cnt.astype(jnp.int32)
    loff_ref[...] = loff.astype(jnp.int32)
    run_scr[...] = run_scr[...] + tile_cnt
    local = rank + jnp.concatenate([loff] * (TT // LANES), axis=1)
    exps = [jnp.exp(v - vals[0]) for v in vals]
    den = exps[0] + exps[1] + exps[2] + exps[3]
    lp_rows = [jnp.sum(jnp.where(oh, local, 0.0), axis=0, keepdims=True) for oh in onehots]
    lp_ref[...] = jnp.concatenate(lp_rows + [jnp.zeros((8 - TOP_K, TT), F32)], axis=0).astype(jnp.int32)
    gate_rows = [e / den for e in exps]
    info = jnp.concatenate(gate_rows + lp_rows + [jnp.zeros((LANES - 2 * TOP_K, TT), F32)], axis=0)
    info_ref[...] = info.T


def _route(logits_t, counts):
    t = logits_t.shape[1]
    steps = t // TT
    run_spec = pl.BlockSpec((N_EXPERTS, LANES), lambda i: (i, 0))
    run_shape = jax.ShapeDtypeStruct((steps * N_EXPERTS, LANES), jnp.int32)
    return pl.pallas_call(
        _route_kernel,
        grid=(steps,),
        in_specs=[pl.BlockSpec((LANES, TT), lambda i: (0, i)),
                  pl.BlockSpec((N_EXPERTS, LANES), lambda i: (0, 0))],
        out_specs=[
            pl.BlockSpec((TT, LANES), lambda i: (i, 0)),
            pl.BlockSpec((8, TT), lambda i: (0, i)),
            run_spec, run_spec, run_spec,
        ],
        out_shape=[
            jax.ShapeDtypeStruct((t, LANES), F32),
            jax.ShapeDtypeStruct((8, t), jnp.int32),
            run_shape, run_shape, run_shape,
        ],
        scratch_shapes=[pltpu.VMEM((N_EXPERTS, LANES), F32)],
        compiler_params=pltpu.CompilerParams(
            dimension_semantics=("arbitrary",), vmem_limit_bytes=VMEM_LIMIT),
        name="route",
    )(logits_t, counts)


def _rows(first, count):
    return pl.ds(pl.multiple_of(first * ROW_TILE, ROW_TILE), count * ROW_TILE)


def _wait_rows(stage_slot, hbm_ref, sem, to_hbm):
    for w in range(STAGE_ROWS // WAIT_ROWS):
        part = stage_slot.at[pl.ds(w * WAIT_ROWS * ROW_TILE, WAIT_ROWS * ROW_TILE), :]
        hbm = hbm_ref.at[pl.ds(0, WAIT_ROWS * ROW_TILE), :]
        src, dst = (part, hbm) if to_hbm else (hbm, part)
        pltpu.make_async_copy(src, dst, sem).wait()


def _for_each_piece(length, max_log2, fn):
    for b in reversed(range(max_log2 + 1)):
        size = 1 << b
        offset = lax.shift_left(lax.shift_right_logical(length, b + 1), b + 1)

        @pl.when((length & size) != 0)
        def _(offset=offset, size=size):
            fn(offset, size)


def _dispatch_kernel(start_ref, len_ref, loff_ref, pstart_ref, plen_ref,
                     lp_ref, h_ref, xs_ref, stage, zeros, sem, zsem):
    j = pl.program_id(0)
    slot = j & 1

    def wait_tile(s):
        _wait_rows(stage.at[s], xs_ref, sem.at[s], to_hbm=True)

    @pl.when(j == 0)
    def _():
        zeros[...] = jnp.zeros_like(zeros)

        def pad_pieces(e, wait):
            def piece(offset, size):
                cp = pltpu.make_async_copy(zeros.at[pl.ds(0, size * ROW_TILE), :],
                                           xs_ref.at[_rows(pstart_ref[e] + offset, size), :], zsem.at[e % 2])
                cp.wait() if wait else cp.start()
            _for_each_piece(plen_ref[e], TM_LOG2 - 1, piece)

        for e in range(N_EXPERTS):
            pad_pieces(e, wait=False)
            if e >= 1:
                pad_pieces(e - 1, wait=True)
        pad_pieces(N_EXPERTS - 1, wait=True)

        n_blocks = plen_ref[N_EXPERTS]

        def block_copy(blk):
            return pltpu.make_async_copy(
                zeros, xs_ref.at[_rows(pstart_ref[N_EXPERTS] + blk * (TM // 2), TM // 2), :], zsem.at[blk & 1])

        def put_block(blk, carry):
            block_copy(blk).start()

            @pl.when(blk >= 2)
            def _():
                block_copy(blk - 2).wait()
            return carry
        lax.fori_loop(0, n_blocks, put_block, 0)
        for back in (2, 1):
            @pl.when(n_blocks >= back)
            def _(back=back):
                block_copy(n_blocks - back).wait()

    @pl.when(j >= 2)
    def _():
        wait_tile(slot)

    h = h_ref[...]
    lp = lp_ref[...]
    for c in range(STAGE_ROWS // TT):
        r = _iota((TT, TT), 0) + c * TT
        hit = (r == lp[0:1]) | (r == lp[1:2]) | (r == lp[2:3]) | (r == lp[3:4])
        rows = _dot(jnp.where(hit, 1.0, 0.0).astype(BF16), h)
        for q in range(ROW_TILE):
            stage[slot, pl.ds(c * TT * ROW_TILE + q, TT, stride=ROW_TILE), :] = rows[:, q * LANES:(q + 1) * LANES]

    for e in range(N_EXPERTS):
        src = loff_ref[j * N_EXPERTS + e]
        dst = start_ref[j * N_EXPERTS + e]

        def put(offset, size, src=src, dst=dst):
            pltpu.make_async_copy(stage.at[slot, _rows(src + offset, size), :],
                                  xs_ref.at[_rows(dst + offset, size), :], sem.at[slot]).start()
        _for_each_piece(len_ref[j * N_EXPERTS + e], TT_LOG2, put)

    @pl.when(j == pl.num_programs(0) - 1)
    def _():
        wait_tile(1 - slot)
        wait_tile(slot)


def _dispatch(starts, lens, loffs, pad_starts, pad_lens, lp, h2, p_rows):
    t = h2.shape[0]
    return pl.pallas_call(
        _dispatch_kernel,
        grid_spec=pltpu.PrefetchScalarGridSpec(
            num_scalar_prefetch=5,
            grid=(t // TT,),
            in_specs=[
                pl.BlockSpec((8, TT), lambda j, *_: (0, j)),
                pl.BlockSpec((TT, D_MODEL), lambda j, *_: (j, 0)),
            ],
            out_specs=pl.BlockSpec(memory_space=pl.ANY),
            scratch_shapes=[
                pltpu.VMEM((2, STAGE_ROWS * ROW_TILE, LANES), F32),
                pltpu.VMEM((TM // 2 * ROW_TILE, LANES), F32),
                pltpu.SemaphoreType.DMA((2,)),
                pltpu.SemaphoreType.DMA((2,)),
            ],
        ),
        out_shape=jax.ShapeDtypeStruct((p_rows * ROW_TILE, LANES), F32),
        compiler_params=pltpu.CompilerParams(
            dimension_semantics=("arbitrary",), vmem_limit_bytes=VMEM_LIMIT),
        name="dispatch",
    )(starts, lens, loffs, pad_starts, pad_lens, lp, h2)


def _experts_kernel(te_ref, nv_ref, nxt_ref, par_ref, x_ref, wg_ref, bg_ref, wu_ref, bu_ref, wd_ref, bd_ref,
                    o_ref, act, wbuf, wg_b, wu_b, wd_b, wsem):
    i = pl.program_id(0)
    prev = jnp.maximum(i - 1, 0)

    def fetch(expert, s):
        return [pltpu.make_async_copy(w_ref.at[expert], wbuf.at[s, m], wsem.at[s])
                for m, w_ref in enumerate((wg_ref, wu_ref, wd_ref))]

    @pl.when(i >= nv_ref[0])
    def _():
        o_ref[...] = jnp.zeros_like(o_ref)

    @pl.when(i < nv_ref[0])
    def _():
        expert = te_ref[i]
        s = par_ref[i]

        @pl.when(i == 0)
        def _():
            for cp in fetch(expert, s):
                cp.start()

        @pl.when((i == 0) | (expert != te_ref[prev]))
        def _():
            for cp in fetch(expert, s):
                cp.wait()
            for c in range(N_CHUNKS):
                cs = slice(c * N_COLS, (c + 1) * N_COLS)
                wg_b[c] = wbuf[s, 0, :, cs].astype(BF16)
                wu_b[c] = wbuf[s, 1, :, cs].astype(BF16)
                wd_b[c] = wbuf[s, 2, :, cs].astype(BF16)

            @pl.when(nxt_ref[i] != expert)
            def _():
                for cp in fetch(nxt_ref[i], 1 - s):
                    cp.start()

        xb = _load_row_tiles(x_ref, TM).astype(BF16)
        for n in range(N_CHUNKS):
            gate = _dot(xb, wg_b[n]) + bg_ref[n:n + 1, :]
            up = _dot(xb, wu_b[n]) + bu_ref[n:n + 1, :]
            gate = jnp.minimum(gate, SWIGLU_LIMIT)
            up = jnp.clip(up, -SWIGLU_LIMIT, SWIGLU_LIMIT)
            act[n] = ((up + 1.0) * (gate * jax.nn.sigmoid(SWIGLU_ALPHA * gate))).astype(BF16)
        a = jnp.concatenate([act[c] for c in range(N_CHUNKS)], axis=1)
        for n in range(N_CHUNKS):
            out = _dot(a, wd_b[n]) + bd_ref[n:n + 1, :]
            for q in range(N_COLS // LANES):
                o_ref[pl.ds(n * (N_COLS // LANES) + q, TM, stride=ROW_TILE), :] = (
                    out[:, q * LANES:(q + 1) * LANES])


def _experts(tile_expert, n_valid, next_expert, parity, xs, wg, bg, wu, bu, wd, bd):
    n_tiles = xs.shape[0] // (TM * ROW_TILE)

    def x_map(i, te, nv, nx, pr):
        return (jnp.minimum(i, nv[0] - 1), 0)

    def b_map(i, te, nv, nx, pr):
        return (te[i], 0, 0)

    w_spec = pl.BlockSpec(memory_space=pl.ANY)
    b_spec = pl.BlockSpec((None, N_CHUNKS, N_COLS), b_map)
    return pl.pallas_call(
        _experts_kernel,
        grid_spec=pltpu.PrefetchScalarGridSpec(
            num_scalar_prefetch=4,
            grid=(n_tiles,),
            in_specs=[pl.BlockSpec((TM * ROW_TILE, LANES), x_map),
                      w_spec, b_spec, w_spec, b_spec, w_spec, b_spec],
            out_specs=pl.BlockSpec((TM * ROW_TILE, LANES), lambda i, te, nv, nx, pr: (i, 0)),
            scratch_shapes=[
                pltpu.VMEM((N_CHUNKS, TM, N_COLS), BF16),
                pltpu.VMEM((2, 3, D_MODEL, D_MODEL), F32),
                pltpu.VMEM((N_CHUNKS, D_MODEL, N_COLS), BF16),
                pltpu.VMEM((N_CHUNKS, D_MODEL, N_COLS), BF16),
                pltpu.VMEM((N_CHUNKS, D_MODEL, N_COLS), BF16),
                pltpu.SemaphoreType.DMA((2,)),
            ],
        ),
        out_shape=jax.ShapeDtypeStruct(xs.shape, F32),
        compiler_params=pltpu.CompilerParams(
            dimension_semantics=("arbitrary",), vmem_limit_bytes=VMEM_LIMIT),
        name="experts",
    )(tile_expert, n_valid, next_expert, parity, xs, wg, bg, wu, bu, wd, bd)


def _combine_kernel(start_ref, len_ref, loff_ref, info_ref, x1_ref, fw_ref, eo_ref, out_ref, stage, sem):
    j = pl.program_id(0)
    slot = lax.rem(j, 3)
    last = pl.num_programs(0) - 1

    def fetch(tile, s):
        for e in range(N_EXPERTS):
            src = start_ref[tile * N_EXPERTS + e]
            dst = loff_ref[tile * N_EXPERTS + e]

            def get(offset, size, src=src, dst=dst):
                pltpu.make_async_copy(eo_ref.at[_rows(src + offset, size), :],
                                      stage.at[s, _rows(dst + offset, size), :], sem.at[s]).start()
            _for_each_piece(len_ref[tile * N_EXPERTS + e], TT_LOG2, get)

    @pl.when(j == 0)
    def _():
        fetch(0, 0)
        fetch(jnp.minimum(1, last), 1)

    _wait_rows(stage.at[slot], eo_ref, sem.at[slot], to_hbm=False)

    info = info_ref[...]
    y = x1_ref[...]
    for c in range(STAGE_ROWS // TT):
        r = (_iota((TT, TT), 1) + c * TT).astype(F32)
        g = jnp.zeros((TT, TT), F32)
        for k in range(TOP_K):
            g = g + jnp.where(r == info[:, TOP_K + k:TOP_K + k + 1], info[:, k:k + 1], 0.0)
        g_hi = g.astype(BF16)
        g_lo = (g - g_hi.astype(F32)).astype(BF16)
        rows = jnp.concatenate(
            [stage[slot, pl.ds(c * TT * ROW_TILE + q, TT, stride=ROW_TILE), :] for q in range(ROW_TILE)],
            axis=1).astype(BF16)
        y = y + _dot(g_hi, rows) + _dot(g_lo, rows)
    ms = jnp.mean(y * y, axis=-1, keepdims=True)
    out_ref[...] = y * lax.rsqrt(ms + EPS) * fw_ref[...]

    ahead = lax.rem(j + 2, 3)
    fetch(jnp.minimum(j + 2, last), ahead)

    @pl.when(j == last)
    def _():
        _wait_rows(stage.at[lax.rem(j + 1, 3)], eo_ref, sem.at[lax.rem(j + 1, 3)], to_hbm=False)
        _wait_rows(stage.at[ahead], eo_ref, sem.at[ahead], to_hbm=False)


def _combine(starts, lens, loffs, info, x1, final_w, eo):
    t = x1.shape[0]
    return pl.pallas_call(
        _combine_kernel,
        grid_spec=pltpu.PrefetchScalarGridSpec(
            num_scalar_prefetch=3,
            grid=(t // TT,),
            in_specs=[
                pl.BlockSpec((TT, LANES), lambda j, *_: (j, 0)),
                pl.BlockSpec((TT, D_MODEL), lambda j, *_: (j, 0)),
                pl.BlockSpec((1, D_MODEL), lambda j, *_: (0, 0)),
                pl.BlockSpec(memory_space=pl.ANY),
            ],
            out_specs=pl.BlockSpec((TT, D_MODEL), lambda j, *_: (j, 0)),
            scratch_shapes=[
                pltpu.VMEM((3, STAGE_ROWS * ROW_TILE, LANES), F32),
                pltpu.SemaphoreType.DMA((3,)),
            ],
        ),
        out_shape=jax.ShapeDtypeStruct((t, D_MODEL), F32),
        compiler_params=pltpu.CompilerParams(
            dimension_semantics=("arbitrary",), vmem_limit_bytes=VMEM_LIMIT),
        name="combine",
    )(starts, lens, loffs, info, x1, final_w, eo)


def _pad_lanes(v, offset, fill=0.0):
    row = jnp.full((1, LANES), fill, F32)
    return row.at[0, offset:offset + v.shape[0]].set(v.astype(F32))


def kernel(x, norm_mix_w, w_in, gla_w_alpha_up, gla_b_alpha, gla_norm_w, ssd_conv_w, ssd_conv_b,
           ssd_dt_bias, ssd_A_log, ssd_D, ssd_norm_w, w_out, norm_ffn_w, router_w, router_b,
           moe_w_gate, moe_b_gate, moe_w_up, moe_b_up, moe_w_down, moe_b_down, final_norm_w):
    bsz, seqlen, d = x.shape
    t = bsz * seqlen
    depth = w_in.shape[0]
    assert depth == 1, "the final RMSNorm is fused into the (single) layer's combine step"
    p_rows = t * TOP_K + N_EXPERTS * TM
    n_tiles = p_rows // TM
    for l in range(depth):
        w = w_in[l]
        w_all = jnp.concatenate(
            [w[:, 0:1536], w[:, 1552:3088], w[:, 1536:1552], w[:, 3088:3096],
             jnp.zeros((d, N_SMALL - GLA_GATE_RANK - SSD_HEADS), w.dtype)], axis=1).astype(BF16)
        wup = jnp.zeros((N_SMALL, GLA_KW), F32).at[0:GLA_GATE_RANK].set(gla_w_alpha_up[l])
        dtb = _pad_lanes(ssd_dt_bias[l], DT_COL)
        aneg = _pad_lanes(-jnp.exp(ssd_A_log[l].astype(F32)), DT_COL)
        dexp = jnp.repeat(ssd_D[l].astype(F32), SSD_HEADDIM)[None, :]
        rw = jnp.zeros((LANES, d), F32).at[0:N_EXPERTS].set(router_w[l].T)
        rb = jnp.zeros((LANES, TL), F32).at[0:N_EXPERTS].set(
            jnp.broadcast_to(router_b[l].astype(F32)[:, None], (N_EXPERTS, TL)))

        x1, h2, logits, counts = _mixer(
            x, norm_mix_w[l][None, :], w_all, wup, gla_b_alpha[l][None, :], gla_norm_w[l][None, :], ssd_conv_w[l],
            ssd_conv_b[l][None, :], dtb, aneg, dexp, ssd_norm_w[l][None, :],
            w_out[l].astype(BF16), norm_ffn_w[l][None, :], rw, rb)

        info, lp, starts, lens, loffs = _route(logits, counts)
        starts, lens, loffs = starts[:, 0], lens[:, 0], loffs[:, 0]

        cnt = counts[:, 0].astype(jnp.int32)
        padded = ((cnt + TM - 1) // TM) * TM
        ends = jnp.cumsum(padded)
        n_valid = (ends[-1] // TM).astype(jnp.int32)
        tile_starts = jnp.arange(n_tiles, dtype=jnp.int32) * TM
        tile_expert = jnp.sum(tile_starts[:, None] >= ends[None, :], axis=1).astype(jnp.int32)
        last_expert = tile_expert[jnp.maximum(n_valid - 1, 0)]
        tile_expert = jnp.where(tile_starts < ends[-1], tile_expert, last_expert)
        next_first = jnp.minimum(ends[tile_expert] // TM, n_valid - 1)
        next_expert = tile_expert[next_first]
        new_group = jnp.concatenate([jnp.ones((1,), jnp.int32),
                                     (tile_expert[1:] != tile_expert[:-1]).astype(jnp.int32)])
        parity = (jnp.cumsum(new_group) - 1) % 2
        pad_starts = jnp.concatenate([ends - padded + cnt, ends[-1:]])
        pad_lens = jnp.concatenate([padded - cnt, (p_rows - ends[-1:]) // (TM // 2)])

        xs = _dispatch(starts, lens, loffs, pad_starts, pad_lens, lp, h2.reshape(t, d), p_rows)
        eo = _experts(tile_expert, n_valid.reshape(1), next_expert, parity.astype(jnp.int32), xs,
                      moe_w_gate[l], moe_b_gate[l].reshape(N_EXPERTS, N_CHUNKS, N_COLS),
                      moe_w_up[l], moe_b_up[l].reshape(N_EXPERTS, N_CHUNKS, N_COLS),
                      moe_w_down[l], moe_b_down[l].reshape(N_EXPERTS, N_CHUNKS, N_COLS))
        x = _combine(starts, lens, loffs, info, x1.reshape(t, d), final_norm_w[None, :], eo
                     ).reshape(bsz, seqlen, d)
    return x
```

```python
import functools

import jax
import jax.numpy as jnp
from jax import lax
from jax.experimental import pallas as pl
from jax.experimental.pallas import tpu as pltpu

F32 = jnp.float32
BF16 = jnp.bfloat16

D_MODEL = 1024
GLA_WIDTH = 512
GLA_HEADS = 4
GLA_DV = 128
GLA_DK = 64
GLA_KW = 256
GLA_GATE_RANK = 16
GLA_GATE_NORM = 16.0
SSD_WIDTH = 512
SSD_HEADDIM = 64
SSD_HEADS = 8
SSD_GROUPS = 2
SSD_HPG = 4
SSD_STATE = 128
SSD_CONV = 4
SSD_CONV_CH = 1024
N_EXPERTS = 32
TOP_K = 4
SWIGLU_LIMIT = 7.0
SWIGLU_ALPHA = 1.702
EPS = 1e-6
GROUP_EPS = 1e-5

LANES = 128
ROW_TILE = D_MODEL // LANES
N_MAIN = 3072
N_SMALL = LANES
DT_COL = GLA_GATE_RANK

GLA_CHUNK = 64
SSD_CHUNK = 128
PROJ_COLS = 512
TL = 256
TT_LOG2 = 8
TT = 1 << TT_LOG2
STAGE_ROWS = 4 * TT
TM_LOG2 = 8
TM = 1 << TM_LOG2
WAIT_ROWS = 512
assert STAGE_ROWS * D_MODEL * 4 <= (1 << 17) * 32
N_COLS = 256
N_CHUNKS = D_MODEL // N_COLS
VMEM_LIMIT = 56 * 1024 * 1024


def _dot(a, b):
    return jnp.dot(a, b, preferred_element_type=F32)


def _dot_nt(a, b):
    return lax.dot_general(a, b, (((1,), (1,)), ((), ())), preferred_element_type=F32)


def _dot_tn(a, b):
    return lax.dot_general(a, b, (((0,), (0,)), ((), ())), preferred_element_type=F32)


def _split3(a):
    hi = a.astype(BF16)
    r1 = a - hi.astype(F32)
    mid = r1.astype(BF16)
    lo = (r1 - mid.astype(F32)).astype(BF16)
    return hi, mid, lo


def _dot_sel_lhs(sel, a):
    hi, mid, lo = _split3(a)
    return _dot(sel, hi) + _dot(sel, mid) + _dot(sel, lo)


def _dot_sel_rhs(a, sel, terms=3):
    parts = _split3(a)[:terms]
    out = _dot(parts[0], sel)
    for p in parts[1:]:
        out = out + _dot(p, sel)
    return out


def _dot_hi(a, b):
    a_hi = a.astype(BF16)
    a_lo = (a - a_hi.astype(F32)).astype(BF16)
    b_hi = b.astype(BF16)
    b_lo = (b - b_hi.astype(F32)).astype(BF16)
    return _dot(a_hi, b_hi) + _dot(a_lo, b_hi) + _dot(a_hi, b_lo)


def _dot_hi_nt(a, b):
    a_hi = a.astype(BF16)
    a_lo = (a - a_hi.astype(F32)).astype(BF16)
    b_hi = b.astype(BF16)
    b_lo = (b - b_hi.astype(F32)).astype(BF16)
    return _dot_nt(a_hi, b_hi) + _dot_nt(a_lo, b_hi) + _dot_nt(a_hi, b_lo)


def _softplus(x):
    return jnp.maximum(x, 0.0) + jnp.log1p(jnp.exp(-jnp.abs(x)))


def _silu(x):
    return x * jax.nn.sigmoid(x)


def _iota(shape, dim):
    return lax.broadcasted_iota(jnp.int32, shape, dim)


def _load_row_tiles(ref, rows):
    return jnp.concatenate([ref[pl.ds(j, rows, stride=ROW_TILE), :] for j in range(ROW_TILE)], axis=1)


def _project_parts(x_ref, nmw_ref, win_ref, pm_ref, small_ref):
    state = {}

    def norm():
        x_in = x_ref[...]
        ms = jnp.mean(x_in * x_in, axis=-1, keepdims=True)
        state["h"] = (x_in * lax.rsqrt(ms + EPS) * nmw_ref[...]).astype(BF16)

    def chunk(n0):
        def run():
            pm_ref[:, n0:n0 + PROJ_COLS] = _dot(state["h"], win_ref[:, n0:n0 + PROJ_COLS]).astype(BF16)
        return run

    def small():
        small_ref[...] = _dot(state["h"], win_ref[:, N_MAIN:N_MAIN + N_SMALL])

    return [norm] + [chunk(n0) for n0 in range(0, N_MAIN, PROJ_COLS)] + [small]


def _mixer_kernel(x_ref, xn_ref, nmw_ref, win_ref, *refs, tiles_per_row):
    params, outs = refs[:13], refs[13:17]
    gla_state, ssd_state, conv_tail, mix_scr, cnt_scr, pm_a, pm_b, small_a, small_b = refs[17:]
    g = pl.program_id(0)

    @pl.when(g == 0)
    def _():
        cnt_scr[...] = jnp.zeros_like(cnt_scr)
        for part in _project_parts(x_ref, nmw_ref, win_ref, pm_a, small_a):
            part()

    @pl.when(lax.rem(g, tiles_per_row) == 0)
    def _():
        gla_state[...] = jnp.zeros_like(gla_state)
        ssd_state[...] = jnp.zeros_like(ssd_state)
        conv_tail[...] = jnp.zeros_like(conv_tail)

    for parity, (pm_cur, small_cur, pm_nxt, small_nxt) in enumerate(
            ((pm_a, small_a, pm_b, small_b), (pm_b, small_b, pm_a, small_a))):
        @pl.when((g & 1) == parity)
        def _(pm_cur=pm_cur, small_cur=small_cur, pm_nxt=pm_nxt, small_nxt=small_nxt):
            _mixer_tile(_project_parts(xn_ref, nmw_ref, win_ref, pm_nxt, small_nxt),
                        pm_cur, small_cur, x_ref, *params, *outs,
                        gla_state, ssd_state, conv_tail, mix_scr, cnt_scr)


def _mixer_tile(side_work, pm_ref, small_ref, x_ref, wup_ref, balpha_ref, gnw_ref, convw_ref, convb_ref,
                dtb_ref, aneg_ref, dexp_ref, snw_ref, wout_ref, nfw_ref, rw_ref, rb_ref,
                x1_ref, h2_ref, lg_ref, cnt_ref,
                gla_state, ssd_state, conv_tail, mix_scr, cnt_scr):
    side_work = list(side_work)

    def run_side(n=1):
        for _ in range(min(n, len(side_work))):
            side_work.pop(0)()

    small = small_ref[...]

    row = _iota((TL, TL), 0)
    col = _iota((TL, TL), 1)
    causal = col <= row
    cum64 = jnp.where(causal & ((row // GLA_CHUNK) == (col // GLA_CHUNK)), 1.0, 0.0).astype(BF16)
    cum128 = jnp.where(causal & ((row // SSD_CHUNK) == (col // SSD_CHUNK)), 1.0, 0.0).astype(BF16)

    xa = _dot_hi(small, wup_ref[...]) + balpha_ref[...]
    log_a = (jnp.minimum(xa, 0.0) - jnp.log1p(jnp.exp(-jnp.abs(xa)))) * (1.0 / GLA_GATE_NORM)
    bcum = _dot_sel_lhs(cum64, log_a)

    lane_kw = _iota((GLA_CHUNK, GLA_KW), 1)
    head_masks = [(lane_kw // GLA_DK) == h for h in range(GLA_HEADS)]
    lane_kw_s = _iota((GLA_DV, GLA_KW), 1)
    head_masks_s = [(lane_kw_s // GLA_DK) == h for h in range(GLA_HEADS)]
    tril64 = _iota((GLA_CHUNK, GLA_CHUNK), 1) <= _iota((GLA_CHUNK, GLA_CHUNK), 0)
    q_scale = GLA_DK ** -0.5

    for c in range(TL // GLA_CHUNK):
        rs = slice(c * GLA_CHUNK, (c + 1) * GLA_CHUNK)
        bc = bcum[rs]
        b_mid = bc[GLA_CHUNK // 2:GLA_CHUNK // 2 + 1]
        b_last = bc[GLA_CHUNK - 1:GLA_CHUNK]
        qc = pm_ref[rs, 0:GLA_KW].astype(F32) * q_scale
        kc = pm_ref[rs, GLA_KW:2 * GLA_KW].astype(F32)
        vc = pm_ref[rs, 2 * GLA_KW:2 * GLA_KW + GLA_WIDTH]
        q_in = (qc * jnp.exp(bc - b_mid)).astype(BF16)
        k_in = (kc * jnp.exp(b_mid - bc)).astype(BF16)
        q_st = (qc * jnp.exp(bc)).astype(BF16)
        k_st = (kc * jnp.exp(b_last - bc)).astype(BF16)
        st = gla_state[...]
        st_b = st.astype(BF16)
        zero_b = jnp.zeros_like(q_in)
        for h in range(GLA_HEADS):
            scores = _dot_nt(jnp.where(head_masks[h], q_in, zero_b), k_in)
            scores = jnp.where(tril64, scores, 0.0).astype(BF16)
            o_h = _dot(scores, vc[:, h * GLA_DV:(h + 1) * GLA_DV])
            o_h = o_h + _dot_nt(jnp.where(head_masks[h], q_st, zero_b), st_b)
            mix_scr[rs, h * GLA_DV:(h + 1) * GLA_DV] = o_h
        upd = _dot_tn(vc, k_st)
        new_st = st * jnp.exp(b_last)
        for h in range(GLA_HEADS):
            new_st = new_st + jnp.where(head_masks_s[h], upd[h * GLA_DV:(h + 1) * GLA_DV], 0.0)
        gla_state[...] = new_st
        run_side()

    xbc = pm_ref[:, 2048:3072].astype(F32)
    tail = conv_tail[...]
    conv_tail[...] = xbc[TL - 8:TL]
    row8 = _iota((8, SSD_CONV_CH), 0)
    conv = xbc * convw_ref[SSD_CONV - 1:SSD_CONV, :]
    for s in range(1, SSD_CONV):
        shifted = pltpu.roll(xbc, s, 0)
        head = jnp.where(row8 < s, pltpu.roll(tail, s, 0), shifted[0:8])
        shifted = jnp.concatenate([head, shifted[8:]], axis=0)
        conv = conv + shifted * convw_ref[SSD_CONV - 1 - s:SSD_CONV - s, :]
    act = _silu(conv + convb_ref[...])
    run_side()
    xs = act[:, 0:SSD_WIDTH]
    bm = act[:, SSD_WIDTH:SSD_WIDTH + SSD_GROUPS * SSD_STATE].astype(BF16)
    cm = act[:, SSD_WIDTH + SSD_GROUPS * SSD_STATE:].astype(BF16)

    dt_full = _softplus(small + dtb_ref[...])
    a_full = dt_full * aneg_ref[...]
    acum = _dot_sel_lhs(cum128, a_full)
    acum_t = acum.T

    e_row = _iota((N_SMALL, SSD_WIDTH), 0)
    e_col = _iota((N_SMALL, SSD_WIDTH), 1)
    spread64 = jnp.where(e_row == DT_COL + e_col // SSD_HEADDIM, 1.0, 0.0).astype(BF16)
    e_row2 = _iota((N_SMALL, SSD_HEADS * LANES), 0)
    e_col2 = _iota((N_SMALL, SSD_HEADS * LANES), 1)
    spread128 = jnp.where(e_row2 == DT_COL + e_col2 // LANES, 1.0, 0.0).astype(BF16)
    dt_e = _dot_sel_rhs(dt_full, spread64, terms=1)
    ac_e = _dot_sel_rhs(acum, spread64, terms=2)
    ac_w = _dot_sel_rhs(acum, spread128, terms=2)

    tril128 = _iota((SSD_CHUNK, SSD_CHUNK), 1) <= _iota((SSD_CHUNK, SSD_CHUNK), 0)
    lane_g = _iota((SSD_CHUNK, SSD_HPG * SSD_HEADDIM), 1)
    for c in range(TL // SSD_CHUNK):
        rs = slice(c * SSD_CHUNK, (c + 1) * SSD_CHUNK)
        ac_c = ac_e[rs]
        a_last = ac_c[SSD_CHUNK - 1:SSD_CHUNK]
        dt_c = dt_e[rs]
        xs_c = xs[rs]
        x_dt = (xs_c * dt_c).astype(BF16)
        x_w = (xs_c * (jnp.exp(a_last - ac_c) * dt_c)).astype(BF16)
        e_ac = jnp.exp(ac_c)
        for g in range(SSD_GROUPS):
            gs = slice(g * SSD_STATE, (g + 1) * SSD_STATE)
            ws = slice(g * SSD_HPG * SSD_HEADDIM, (g + 1) * SSD_HPG * SSD_HEADDIM)
            c_g = cm[rs, gs]
            b_g = bm[rs, gs]
            cb = _dot_nt(c_g, b_g)
            x_dt_g = x_dt[:, ws]
            lhs_parts = []
            rhs_parts = []
            for hh in range(SSD_HPG):
                h = g * SSD_HPG + hh
                seg = ac_w[rs, h * LANES:(h + 1) * LANES] - acum_t[DT_COL + h:DT_COL + h + 1, rs]
                lmat = jnp.where(tril128, jnp.exp(jnp.where(tril128, seg, 0.0)), 0.0)
                lhs_parts.append((cb * lmat).astype(BF16))
                rhs_parts.append(jnp.where((lane_g // SSD_HEADDIM) == hh, x_dt_g,
                                           jnp.zeros_like(x_dt_g)))
            intra = _dot(jnp.concatenate(lhs_parts, axis=1), jnp.concatenate(rhs_parts, axis=0))
            st = ssd_state[g]
            inter = _dot(c_g, st.astype(BF16)) * e_ac[:, ws]
            mix_scr[rs, GLA_WIDTH + g * 256:GLA_WIDTH + (g + 1) * 256] = intra + inter
            ssd_state[g] = st * jnp.exp(a_last[:, ws]) + _dot_tn(b_g, x_w[:, ws])
        run_side()

    run_side(len(side_work))

    o = mix_scr[:, 0:GLA_WIDTH]
    g_gate = _silu(pm_ref[:, 1024:1536].astype(F32))
    gla_parts = []
    for h in range(GLA_HEADS):
        o_h = o[:, h * GLA_DV:(h + 1) * GLA_DV]
        ms = jnp.mean(o_h * o_h, axis=-1, keepdims=True)
        gla_parts.append(o_h * lax.rsqrt(ms + GROUP_EPS))
    gla_out = jnp.concatenate(gla_parts, axis=1) * gnw_ref[...] * g_gate

    y = mix_scr[:, GLA_WIDTH:] + dexp_ref[...] * xs
    y = y * _silu(pm_ref[:, 1536:2048].astype(F32))
    ssd_parts = []
    for g in range(SSD_GROUPS):
        y_g = y[:, g * 256:(g + 1) * 256]
        ms = jnp.mean(y_g * y_g, axis=-1, keepdims=True)
        ssd_parts.append(y_g * lax.rsqrt(ms + GROUP_EPS))
    ssd_out = jnp.concatenate(ssd_parts, axis=1) * snw_ref[...]

    mixed = jnp.concatenate([gla_out, ssd_out], axis=1).astype(BF16)
    x1 = x_ref[...] + _dot(mixed, wout_ref[...])
    x1_ref[...] = x1

    ms = jnp.mean(x1 * x1, axis=-1, keepdims=True)
    h2 = x1 * lax.rsqrt(ms + EPS) * nfw_ref[...]
    h2_ref[...] = h2.astype(BF16)
    lg = _dot_hi_nt(rw_ref[...], h2) + rb_ref[...]
    lg_ref[...] = lg
    cnt_scr[...] = cnt_scr[...] + _tile_counts(_top4(lg[0:N_EXPERTS])[0])[1]
    cnt_ref[...] = cnt_scr[...]


def _mixer(x, nmw, w_all, wup, balpha, gnw, convw, convb, dtb, aneg, dexp, snw, wout, nfw, rw, rb):
    bsz, seqlen, _ = x.shape
    per_row = seqlen // TL
    steps = bsz * per_row

    def full(a):
        return pl.BlockSpec(a.shape, lambda g: (0,) * a.ndim)

    def tile(ahead):
        def index(g):
            tile_id = jnp.minimum(g + ahead, steps - 1)
            return (tile_id // per_row, tile_id % per_row, 0)
        return pl.BlockSpec((None, TL, D_MODEL), index)

    params = (nmw, w_all, wup, balpha, gnw, convw, convb, dtb, aneg, dexp, snw, wout, nfw, rw, rb)
    return pl.pallas_call(
        functools.partial(_mixer_kernel, tiles_per_row=per_row),
        grid=(steps,),
        in_specs=[tile(0), tile(1)] + [full(p) for p in params],
        out_specs=[tile(0), tile(0),
                   pl.BlockSpec((LANES, TL), lambda g: (0, g)),
                   pl.BlockSpec((N_EXPERTS, LANES), lambda g: (0, 0))],
        out_shape=[
            jax.ShapeDtypeStruct((bsz, seqlen, D_MODEL), F32),
            jax.ShapeDtypeStruct((bsz, seqlen, D_MODEL), BF16),
            jax.ShapeDtypeStruct((LANES, bsz * seqlen), F32),
            jax.ShapeDtypeStruct((N_EXPERTS, LANES), F32),
        ],
        scratch_shapes=[
            pltpu.VMEM((GLA_DV, GLA_KW), F32),
            pltpu.VMEM((SSD_GROUPS, SSD_STATE, SSD_HPG * SSD_HEADDIM), F32),
            pltpu.VMEM((8, SSD_CONV_CH), F32),
            pltpu.VMEM((TL, D_MODEL), F32),
            pltpu.VMEM((N_EXPERTS, LANES), F32),
            pltpu.VMEM((TL, N_MAIN), BF16),
            pltpu.VMEM((TL, N_MAIN), BF16),
            pltpu.VMEM((TL, N_SMALL), F32),
            pltpu.VMEM((TL, N_SMALL), F32),
        ],
        compiler_params=pltpu.CompilerParams(
            dimension_semantics=("arbitrary",), vmem_limit_bytes=VMEM_LIMIT),
        name="mixer",
    )(x, x, *params)


def _top4(lg):
    n_e, n_t = lg.shape
    row = _iota((n_e, n_t), 0)
    work = lg
    onehots = []
    vals = []
    for _ in range(TOP_K):
        m = jnp.max(work, axis=0, keepdims=True)
        idx = jnp.min(jnp.where(work == m, row, n_e), axis=0, keepdims=True)
        oh = row == idx
        onehots.append(oh)
        vals.append(m)
        work = jnp.where(oh, -jnp.inf, work)
    return onehots, vals


def _tile_counts(onehots):
    multi = jnp.where(onehots[0] | onehots[1] | onehots[2] | onehots[3], 1.0, 0.0).astype(BF16)
    return multi, _dot(multi, jnp.ones((multi.shape[1], LANES), BF16))


def _route_kernel(lg_ref, cnt_ref, info_ref, lp_ref, start_ref, len_ref, loff_ref, run_scr):
    @pl.when(pl.program_id(0) == 0)
    def _():
        run_scr[...] = jnp.zeros_like(run_scr)

    onehots, vals = _top4(lg_ref[0:N_EXPERTS, :])
    multi, tile_cnt = _tile_counts(onehots)
    counts = cnt_ref[...]
    padded = jnp.floor((counts + (TM - 1)) * (1.0 / TM)) * TM
    lower = jnp.where(_iota((N_EXPERTS, N_EXPERTS), 1) < _iota((N_EXPERTS, N_EXPERTS), 0),
                      1.0, 0.0).astype(BF16)
    offs = _dot_sel_lhs(lower, padded)
    loff = _dot_sel_lhs(lower, tile_cnt)
    before = jnp.where(_iota((TT, TT), 0) < _iota((TT, TT), 1), 1.0, 0.0).astype(BF16)
    rank = _dot(multi, before)
    start_ref[...] = (offs + run_scr[...]).astype(jnp.int32)
    len_ref[...] = tile_cnt.astype(jnp.int32)
    loff_ref[...] = loff.astype(jnp.int32)
    run_scr[...] = run_scr[...] + tile_cnt
    local = rank + jnp.concatenate([loff] * (TT // LANES), axis=1)
    exps = [jnp.exp(v - vals[0]) for v in vals]
    den = exps[0] + exps[1] + exps[2] + exps[3]
    lp_rows = [jnp.sum(jnp.where(oh, local, 0.0), axis=0, keepdims=True) for oh in onehots]
    lp_ref[...] = jnp.concatenate(lp_rows + [jnp.zeros((8 - TOP_K, TT), F32)], axis=0).astype(jnp.int32)
    gate_rows = [e / den for e in exps]
    info = jnp.concatenate(gate_rows + lp_rows + [jnp.zeros((LANES - 2 * TOP_K, TT), F32)], axis=0)
    info_ref[...] = info.T


def _route(logits_t, counts):
    t = logits_t.shape[1]
    steps = t // TT
    run_spec = pl.BlockSpec((N_EXPERTS, LANES), lambda i: (i, 0))
    run_shape = jax.ShapeDtypeStruct((steps * N_EXPERTS, LANES), jnp.int32)
    return pl.pallas_call(
        _route_kernel,
        grid=(steps,),
        in_specs=[pl.BlockSpec((LANES, TT), lambda i: (0, i)),
                  pl.BlockSpec((N_EXPERTS, LANES), lambda i: (0, 0))],
        out_specs=[
            pl.BlockSpec((TT, LANES), lambda i: (i, 0)),
            pl.BlockSpec((8, TT), lambda i: (0, i)),
            run_spec, run_spec, run_spec,
        ],
        out_shape=[
            jax.ShapeDtypeStruct((t, LANES), F32),
            jax.ShapeDtypeStruct((8, t), jnp.int32),
            run_shape, run_shape, run_shape,
        ],
        scratch_shapes=[pltpu.VMEM((N_EXPERTS, LANES), F32)],
        compiler_params=pltpu.CompilerParams(
            dimension_semantics=("arbitrary",), vmem_limit_bytes=VMEM_LIMIT),
        name="route",
    )(logits_t, counts)


def _rows(first, count):
    return pl.ds(pl.multiple_of(first * ROW_TILE, ROW_TILE), count * ROW_TILE)


def _wait_rows(stage_slot, hbm_ref, sem, to_hbm):
    for w in range(STAGE_ROWS // WAIT_ROWS):
        part = stage_slot.at[pl.ds(w * WAIT_ROWS * ROW_TILE, WAIT_ROWS * ROW_TILE), :]
        hbm = hbm_ref.at[pl.ds(0, WAIT_ROWS * ROW_TILE), :]
        src, dst = (part, hbm) if to_hbm else (hbm, part)
        pltpu.make_async_copy(src, dst, sem).wait()


def _for_each_piece(length, max_log2, fn):
    for b in reversed(range(max_log2 + 1)):
        size = 1 << b
        offset = lax.shift_left(lax.shift_right_logical(length, b + 1), b + 1)

        @pl.when((length & size) != 0)
        def _(offset=offset, size=size):
            fn(offset, size)


def _dispatch_kernel(start_ref, len_ref, loff_ref, pstart_ref, plen_ref,
                     lp_ref, h_ref, xs_ref, stage, zeros, sem, zsem):
    j = pl.program_id(0)
    slot = j & 1

    def wait_tile(s):
        _wait_rows(stage.at[s], xs_ref, sem.at[s], to_hbm=True)

    @pl.when(j == 0)
    def _():
        zeros[...] = jnp.zeros_like(zeros)

        def pad_pieces(e, wait):
            def piece(offset, size):
                cp = pltpu.make_async_copy(zeros.at[pl.ds(0, size * ROW_TILE), :],
                                           xs_ref.at[_rows(pstart_ref[e] + offset, size), :], zsem.at[e % 2])
                cp.wait() if wait else cp.start()
            _for_each_piece(plen_ref[e], TM_LOG2 - 1, piece)

        for e in range(N_EXPERTS):
            pad_pieces(e, wait=False)
            if e >= 1:
                pad_pieces(e - 1, wait=True)
        pad_pieces(N_EXPERTS - 1, wait=True)

        n_blocks = plen_ref[N_EXPERTS]

        def block_copy(blk):
            return pltpu.make_async_copy(
                zeros, xs_ref.at[_rows(pstart_ref[N_EXPERTS] + blk * (TM // 2), TM // 2), :], zsem.at[blk & 1])

        def put_block(blk, carry):
            block_copy(blk).start()

            @pl.when(blk >= 2)
            def _():
                block_copy(blk - 2).wait()
            return carry
        lax.fori_loop(0, n_blocks, put_block, 0)
        for back in (2, 1):
            @pl.when(n_blocks >= back)
            def _(back=back):
                block_copy(n_blocks - back).wait()

    @pl.when(j >= 2)
    def _():
        wait_tile(slot)

    h = h_ref[...]
    lp = lp_ref[...]
    for c in range(STAGE_ROWS // TT):
        r = _iota((TT, TT), 0) + c * TT
        hit = (r == lp[0:1]) | (r == lp[1:2]) | (r == lp[2:3]) | (r == lp[3:4])
        rows = _dot(jnp.where(hit, 1.0, 0.0).astype(BF16), h)
        for q in range(ROW_TILE):
            stage[slot, pl.ds(c * TT * ROW_TILE + q, TT, stride=ROW_TILE), :] = rows[:, q * LANES:(q + 1) * LANES]

    for e in range(N_EXPERTS):
        src = loff_ref[j * N_EXPERTS + e]
        dst = start_ref[j * N_EXPERTS + e]

        def put(offset, size, src=src, dst=dst):
            pltpu.make_async_copy(stage.at[slot, _rows(src + offset, size), :],
                                  xs_ref.at[_rows(dst + offset, size), :], sem.at[slot]).start()
        _for_each_piece(len_ref[j * N_EXPERTS + e], TT_LOG2, put)

    @pl.when(j == pl.num_programs(0) - 1)
    def _():
        wait_tile(1 - slot)
        wait_tile(slot)


def _dispatch(starts, lens, loffs, pad_starts, pad_lens, lp, h2, p_rows):
    t = h2.shape[0]
    return pl.pallas_call(
        _dispatch_kernel,
        grid_spec=pltpu.PrefetchScalarGridSpec(
            num_scalar_prefetch=5,
            grid=(t // TT,),
            in_specs=[
                pl.BlockSpec((8, TT), lambda j, *_: (0, j)),
                pl.BlockSpec((TT, D_MODEL), lambda j, *_: (j, 0)),
            ],
            out_specs=pl.BlockSpec(memory_space=pl.ANY),
            scratch_shapes=[
                pltpu.VMEM((2, STAGE_ROWS * ROW_TILE, LANES), F32),
                pltpu.VMEM((TM // 2 * ROW_TILE, LANES), F32),
                pltpu.SemaphoreType.DMA((2,)),
                pltpu.SemaphoreType.DMA((2,)),
            ],
        ),
        out_shape=jax.ShapeDtypeStruct((p_rows * ROW_TILE, LANES), F32),
        compiler_params=pltpu.CompilerParams(
            dimension_semantics=("arbitrary",), vmem_limit_bytes=VMEM_LIMIT),
        name="dispatch",
    )(starts, lens, loffs, pad_starts, pad_lens, lp, h2)


def _experts_kernel(te_ref, nv_ref, nxt_ref, par_ref, x_ref, wg_ref, bg_ref, wu_ref, bu_ref, wd_ref, bd_ref,
                    o_ref, act, wbuf, wg_b, wu_b, wd_b, wsem):
    i = pl.program_id(0)
    prev = jnp.maximum(i - 1, 0)

    def fetch(expert, s):
        return [pltpu.make_async_copy(w_ref.at[expert], wbuf.at[s, m], wsem.at[s])
                for m, w_ref in enumerate((wg_ref, wu_ref, wd_ref))]

    @pl.when(i >= nv_ref[0])
    def _():
        o_ref[...] = jnp.zeros_like(o_ref)

    @pl.when(i < nv_ref[0])
    def _():
        expert = te_ref[i]
        s = par_ref[i]

        @pl.when(i == 0)
        def _():
            for cp in fetch(expert, s):
                cp.start()

        @pl.when((i == 0) | (expert != te_ref[prev]))
        def _():
            for cp in fetch(expert, s):
                cp.wait()
            for c in range(N_CHUNKS):
                cs = slice(c * N_COLS, (c + 1) * N_COLS)
                wg_b[c] = wbuf[s, 0, :, cs].astype(BF16)
                wu_b[c] = wbuf[s, 1, :, cs].astype(BF16)
                wd_b[c] = wbuf[s, 2, :, cs].astype(BF16)

            @pl.when(nxt_ref[i] != expert)
            def _():
                for cp in fetch(nxt_ref[i], 1 - s):
                    cp.start()

        xb = _load_row_tiles(x_ref, TM).astype(BF16)
        for n in range(N_CHUNKS):
            gate = _dot(xb, wg_b[n]) + bg_ref[n:n + 1, :]
            up = _dot(xb, wu_b[n]) + bu_ref[n:n + 1, :]
            gate = jnp.minimum(gate, SWIGLU_LIMIT)
            up = jnp.clip(up, -SWIGLU_LIMIT, SWIGLU_LIMIT)
            act[n] = ((up + 1.0) * (gate * jax.nn.sigmoid(SWIGLU_ALPHA * gate))).astype(BF16)
        a = jnp.concatenate([act[c] for c in range(N_CHUNKS)], axis=1)
        for n in range(N_CHUNKS):
            out = _dot(a, wd_b[n]) + bd_ref[n:n + 1, :]
            for q in range(N_COLS // LANES):
                o_ref[pl.ds(n * (N_COLS // LANES) + q, TM, stride=ROW_TILE), :] = (
                    out[:, q * LANES:(q + 1) * LANES])


def _experts(tile_expert, n_valid, next_expert, parity, xs, wg, bg, wu, bu, wd, bd):
    n_tiles = xs.shape[0] // (TM * ROW_TILE)

    def x_map(i, te, nv, nx, pr):
        return (jnp.minimum(i, nv[0] - 1), 0)

    def b_map(i, te, nv, nx, pr):
        return (te[i], 0, 0)

    w_spec = pl.BlockSpec(memory_space=pl.ANY)
    b_spec = pl.BlockSpec((None, N_CHUNKS, N_COLS), b_map)
    return pl.pallas_call(
        _experts_kernel,
        grid_spec=pltpu.PrefetchScalarGridSpec(
            num_scalar_prefetch=4,
            grid=(n_tiles,),
            in_specs=[pl.BlockSpec((TM * ROW_TILE, LANES), x_map),
                      w_spec, b_spec, w_spec, b_spec, w_spec, b_spec],
            out_specs=pl.BlockSpec((TM * ROW_TILE, LANES), lambda i, te, nv, nx, pr: (i, 0)),
            scratch_shapes=[
                pltpu.VMEM((N_CHUNKS, TM, N_COLS), BF16),
                pltpu.VMEM((2, 3, D_MODEL, D_MODEL), F32),
                pltpu.VMEM((N_CHUNKS, D_MODEL, N_COLS), BF16),
                pltpu.VMEM((N_CHUNKS, D_MODEL, N_COLS), BF16),
                pltpu.VMEM((N_CHUNKS, D_MODEL, N_COLS), BF16),
                pltpu.SemaphoreType.DMA((2,)),
            ],
        ),
        out_shape=jax.ShapeDtypeStruct(xs.shape, F32),
        compiler_params=pltpu.CompilerParams(
            dimension_semantics=("arbitrary",), vmem_limit_bytes=VMEM_LIMIT),
        name="experts",
    )(tile_expert, n_valid, next_expert, parity, xs, wg, bg, wu, bu, wd, bd)


def _combine_kernel(start_ref, len_ref, loff_ref, info_ref, x1_ref, fw_ref, eo_ref, out_ref, stage, sem):
    j = pl.program_id(0)
    slot = lax.rem(j, 3)
    last = pl.num_programs(0) - 1

    def fetch(tile, s):
        for e in range(N_EXPERTS):
            src = start_ref[tile * N_EXPERTS + e]
            dst = loff_ref[tile * N_EXPERTS + e]

            def get(offset, size, src=src, dst=dst):
                pltpu.make_async_copy(eo_ref.at[_rows(src + offset, size), :],
                                      stage.at[s, _rows(dst + offset, size), :], sem.at[s]).start()
            _for_each_piece(len_ref[tile * N_EXPERTS + e], TT_LOG2, get)

    @pl.when(j == 0)
    def _():
        fetch(0, 0)
        fetch(jnp.minimum(1, last), 1)

    _wait_rows(stage.at[slot], eo_ref, sem.at[slot], to_hbm=False)

    info = info_ref[...]
    y = x1_ref[...]
    for c in range(STAGE_ROWS // TT):
        r = (_iota((TT, TT), 1) + c * TT).astype(F32)
        g = jnp.zeros((TT, TT), F32)
        for k in range(TOP_K):
            g = g + jnp.where(r == info[:, TOP_K + k:TOP_K + k + 1], info[:, k:k + 1], 0.0)
        g_hi = g.astype(BF16)
        g_lo = (g - g_hi.astype(F32)).astype(BF16)
        rows = jnp.concatenate(
            [stage[slot, pl.ds(c * TT * ROW_TILE + q, TT, stride=ROW_TILE), :] for q in range(ROW_TILE)],
            axis=1).astype(BF16)
        y = y + _dot(g_hi, rows) + _dot(g_lo, rows)
    ms = jnp.mean(y * y, axis=-1, keepdims=True)
    out_ref[...] = y * lax.rsqrt(ms + EPS) * fw_ref[...]

    ahead = lax.rem(j + 2, 3)
    fetch(jnp.minimum(j + 2, last), ahead)

    @pl.when(j == last)
    def _():
        _wait_rows(stage.at[lax.rem(j + 1, 3)], eo_ref, sem.at[lax.rem(j + 1, 3)], to_hbm=False)
        _wait_rows(stage.at[ahead], eo_ref, sem.at[ahead], to_hbm=False)


def _combine(starts, lens, loffs, info, x1, final_w, eo):
    t = x1.shape[0]
    return pl.pallas_call(
        _combine_kernel,
        grid_spec=pltpu.PrefetchScalarGridSpec(
            num_scalar_prefetch=3,
            grid=(t // TT,),
            in_specs=[
                pl.BlockSpec((TT, LANES), lambda j, *_: (j, 0)),
                pl.BlockSpec((TT, D_MODEL), lambda j, *_: (j, 0)),
                pl.BlockSpec((1, D_MODEL), lambda j, *_: (0, 0)),
                pl.BlockSpec(memory_space=pl.ANY),
            ],
            out_specs=pl.BlockSpec((TT, D_MODEL), lambda j, *_: (j, 0)),
            scratch_shapes=[
                pltpu.VMEM((3, STAGE_ROWS * ROW_TILE, LANES), F32),
                pltpu.SemaphoreType.DMA((3,)),
            ],
        ),
        out_shape=jax.ShapeDtypeStruct((t, D_MODEL), F32),
        compiler_params=pltpu.CompilerParams(
            dimension_semantics=("arbitrary",), vmem_limit_bytes=VMEM_LIMIT),
        name="combine",
    )(starts, lens, loffs, info, x1, final_w, eo)


def _pad_lanes(v, offset, fill=0.0):
    row = jnp.full((1, LANES), fill, F32)
    return row.at[0, offset:offset + v.shape[0]].set(v.astype(F32))


def kernel(x, norm_mix_w, w_in, gla_w_alpha_up, gla_b_alpha, gla_norm_w, ssd_conv_w, ssd_conv_b,
           ssd_dt_bias, ssd_A_log, ssd_D, ssd_norm_w, w_out, norm_ffn_w, router_w, router_b,
           moe_w_gate, moe_b_gate, moe_w_up, moe_b_up, moe_w_down, moe_b_down, final_norm_w):
    bsz, seqlen, d = x.shape
    t = bsz * seqlen
    depth = w_in.shape[0]
    assert depth == 1, "the final RMSNorm is fused into the (single) layer's combine step"
    p_rows = t * TOP_K + N_EXPERTS * TM
    n_tiles = p_rows // TM
    for l in range(depth):
        w = w_in[l]
        w_all = jnp.concatenate(
            [w[:, 0:1536], w[:, 1552:3088], w[:, 1536:1552], w[:, 3088:3096],
             jnp.zeros((d, N_SMALL - GLA_GATE_RANK - SSD_HEADS), w.dtype)], axis=1).astype(BF16)
        wup = jnp.zeros((N_SMALL, GLA_KW), F32).at[0:GLA_GATE_RANK].set(gla_w_alpha_up[l])
        dtb = _pad_lanes(ssd_dt_bias[l], DT_COL)
        aneg = _pad_lanes(-jnp.exp(ssd_A_log[l].astype(F32)), DT_COL)
        dexp = jnp.repeat(ssd_D[l].astype(F32), SSD_HEADDIM)[None, :]
        rw = jnp.zeros((LANES, d), F32).at[0:N_EXPERTS].set(router_w[l].T)
        rb = jnp.zeros((LANES, TL), F32).at[0:N_EXPERTS].set(
            jnp.broadcast_to(router_b[l].astype(F32)[:, None], (N_EXPERTS, TL)))

        x1, h2, logits, counts = _mixer(
            x, norm_mix_w[l][None, :], w_all, wup, gla_b_alpha[l][None, :], gla_norm_w[l][None, :], ssd_conv_w[l],
            ssd_conv_b[l][None, :], dtb, aneg, dexp, ssd_norm_w[l][None, :],
            w_out[l].astype(BF16), norm_ffn_w[l][None, :], rw, rb)

        info, lp, starts, lens, loffs = _route(logits, counts)
        starts, lens, loffs = starts[:, 0], lens[:, 0], loffs[:, 0]

        cnt = counts[:, 0].astype(jnp.int32)
        padded = ((cnt + TM - 1) // TM) * TM
        ends = jnp.cumsum(padded)
        n_valid = (ends[-1] // TM).astype(jnp.int32)
        tile_starts = jnp.arange(n_tiles, dtype=jnp.int32) * TM
        tile_expert = jnp.sum(tile_starts[:, None] >= ends[None, :], axis=1).astype(jnp.int32)
        last_expert = tile_expert[jnp.maximum(n_valid - 1, 0)]
        tile_expert = jnp.where(tile_starts < ends[-1], tile_expert, last_expert)
        next_first = jnp.minimum(ends[tile_expert] // TM, n_valid - 1)
        next_expert = tile_expert[next_first]
        new_group = jnp.concatenate([jnp.ones((1,), jnp.int32),
                                     (tile_expert[1:] != tile_expert[:-1]).astype(jnp.int32)])
        parity = (jnp.cumsum(new_group) - 1) % 2
        pad_starts = jnp.concatenate([ends - padded + cnt, ends[-1:]])
        pad_lens = jnp.concatenate([padded - cnt, (p_rows - ends[-1:]) // (TM // 2)])

        xs = _dispatch(starts, lens, loffs, pad_starts, pad_lens, lp, h2.reshape(t, d), p_rows)
        eo = _experts(tile_expert, n_valid.reshape(1), next_expert, parity.astype(jnp.int32), xs,
                      moe_w_gate[l], moe_b_gate[l].reshape(N_EXPERTS, N_CHUNKS, N_COLS),
                      moe_w_up[l], moe_b_up[l].reshape(N_EXPERTS, N_CHUNKS, N_COLS),
                      moe_w_down[l], moe_b_down[l].reshape(N_EXPERTS, N_CHUNKS, N_COLS))
        x = _combine(starts, lens, loffs, info, x1.reshape(t, d), final_norm_w[None, :], eo
                     ).reshape(bsz, seqlen, d)
    return x
```

```python
import functools

import jax
import jax.numpy as jnp
from jax import lax
from jax.experimental import pallas as pl
from jax.experimental.pallas import tpu as pltpu

F32 = jnp.float32
BF16 = jnp.bfloat16

D_MODEL = 1024
GLA_WIDTH = 512
GLA_HEADS = 4
GLA_DV = 128
GLA_DK = 64
GLA_KW = 256
GLA_GATE_RANK = 16
GLA_GATE_NORM = 16.0
SSD_WIDTH = 512
SSD_HEADDIM = 64
SSD_HEADS = 8
SSD_GROUPS = 2
SSD_HPG = 4
SSD_STATE = 128
SSD_CONV = 4
SSD_CONV_CH = 1024
N_EXPERTS = 32
TOP_K = 4
SWIGLU_LIMIT = 7.0
SWIGLU_ALPHA = 1.702
EPS = 1e-6
GROUP_EPS = 1e-5

LANES = 128
ROW_TILE = D_MODEL // LANES
N_MAIN = 3072
N_SMALL = LANES
DT_COL = GLA_GATE_RANK

GLA_CHUNK = 64
SSD_CHUNK = 128
PROJ_COLS = 512
TL = 256
TT_LOG2 = 8
TT = 1 << TT_LOG2
STAGE_ROWS = 4 * TT
TM_LOG2 = 9
TM = 1 << TM_LOG2
WAIT_ROWS = 512
RUN_START, RUN_LEN, RUN_LOFF = 0, 1, 2
PLAN_EXPERT, PLAN_NEXT, PLAN_PARITY, PLAN_NVALID, PLAN_PAD_START, PLAN_PAD_LEN = 0, 1, 2, 3, 4, 5
PLAN_COLS = 512
assert STAGE_ROWS * D_MODEL * 4 <= (1 << 17) * 32
N_COLS = 256
N_CHUNKS = D_MODEL // N_COLS
VMEM_LIMIT = 56 * 1024 * 1024


def _dot(a, b):
    return jnp.dot(a, b, preferred_element_type=F32)


def _dot_nt(a, b):
    return lax.dot_general(a, b, (((1,), (1,)), ((), ())), preferred_element_type=F32)


def _dot_tn(a, b):
    return lax.dot_general(a, b, (((0,), (0,)), ((), ())), preferred_element_type=F32)


def _split3(a):
    hi = a.astype(BF16)
    r1 = a - hi.astype(F32)
    mid = r1.astype(BF16)
    lo = (r1 - mid.astype(F32)).astype(BF16)
    return hi, mid, lo


def _dot_sel_lhs(sel, a):
    hi, mid, lo = _split3(a)
    return _dot(sel, hi) + _dot(sel, mid) + _dot(sel, lo)


def _dot_sel_rhs(a, sel, terms=3):
    parts = _split3(a)[:terms]
    out = _dot(parts[0], sel)
    for p in parts[1:]:
        out = out + _dot(p, sel)
    return out


def _dot_hi(a, b):
    a_hi = a.astype(BF16)
    a_lo = (a - a_hi.astype(F32)).astype(BF16)
    b_hi = b.astype(BF16)
    b_lo = (b - b_hi.astype(F32)).astype(BF16)
    return _dot(a_hi, b_hi) + _dot(a_lo, b_hi) + _dot(a_hi, b_lo)


def _dot_hi_nt(a, b):
    a_hi = a.astype(BF16)
    a_lo = (a - a_hi.astype(F32)).astype(BF16)
    b_hi = b.astype(BF16)
    b_lo = (b - b_hi.astype(F32)).astype(BF16)
    return _dot_nt(a_hi, b_hi) + _dot_nt(a_lo, b_hi) + _dot_nt(a_hi, b_lo)


def _softplus(x):
    return jnp.maximum(x, 0.0) + jnp.log1p(jnp.exp(-jnp.abs(x)))


def _silu(x):
    return x * jax.nn.sigmoid(x)


def _iota(shape, dim):
    return lax.broadcasted_iota(jnp.int32, shape, dim)


def _load_row_tiles(ref, rows):
    return jnp.concatenate([ref[pl.ds(j, rows, stride=ROW_TILE), :] for j in range(ROW_TILE)], axis=1)


def _project_parts(x_ref, nmw_ref, win_ref, pm_ref, small_ref):
    state = {}

    def norm():
        x_in = x_ref[...]
        ms = jnp.mean(x_in * x_in, axis=-1, keepdims=True)
        state["h"] = (x_in * lax.rsqrt(ms + EPS) * nmw_ref[...]).astype(BF16)

    def chunk(n0):
        def run():
            pm_ref[:, n0:n0 + PROJ_COLS] = _dot(state["h"], win_ref[:, n0:n0 + PROJ_COLS]).astype(BF16)
        return run

    def small():
        small_ref[...] = _dot(state["h"], win_ref[:, N_MAIN:N_MAIN + N_SMALL])

    return [norm] + [chunk(n0) for n0 in range(0, N_MAIN, PROJ_COLS)] + [small]


def _mixer_kernel(x_ref, xn_ref, nmw_ref, win_ref, *refs, tiles_per_row):
    params, outs = refs[:13], refs[13:17]
    gla_state, ssd_state, conv_tail, mix_scr, cnt_scr, pm_a, pm_b, small_a, small_b = refs[17:]
    g = pl.program_id(0)

    @pl.when(g == 0)
    def _():
        cnt_scr[...] = jnp.zeros_like(cnt_scr)
        for part in _project_parts(x_ref, nmw_ref, win_ref, pm_a, small_a):
            part()

    @pl.when(lax.rem(g, tiles_per_row) == 0)
    def _():
        gla_state[...] = jnp.zeros_like(gla_state)
        ssd_state[...] = jnp.zeros_like(ssd_state)
        conv_tail[...] = jnp.zeros_like(conv_tail)

    for parity, (pm_cur, small_cur, pm_nxt, small_nxt) in enumerate(
            ((pm_a, small_a, pm_b, small_b), (pm_b, small_b, pm_a, small_a))):
        @pl.when((g & 1) == parity)
        def _(pm_cur=pm_cur, small_cur=small_cur, pm_nxt=pm_nxt, small_nxt=small_nxt):
            _mixer_tile(_project_parts(xn_ref, nmw_ref, win_ref, pm_nxt, small_nxt),
                        pm_cur, small_cur, x_ref, *params, *outs,
                        gla_state, ssd_state, conv_tail, mix_scr, cnt_scr)


def _mixer_tile(side_work, pm_ref, small_ref, x_ref, wup_ref, balpha_ref, gnw_ref, convw_ref, convb_ref,
                dtb_ref, aneg_ref, dexp_ref, snw_ref, wout_ref, nfw_ref, rw_ref, rb_ref,
                x1_ref, h2_ref, lg_ref, cnt_ref,
                gla_state, ssd_state, conv_tail, mix_scr, cnt_scr):
    side_work = list(side_work)

    def run_side(n=1):
        for _ in range(min(n, len(side_work))):
            side_work.pop(0)()

    small = small_ref[...]

    row = _iota((TL, TL), 0)
    col = _iota((TL, TL), 1)
    causal = col <= row
    cum64 = jnp.where(causal & ((row // GLA_CHUNK) == (col // GLA_CHUNK)), 1.0, 0.0).astype(BF16)
    cum128 = jnp.where(causal & ((row // SSD_CHUNK) == (col // SSD_CHUNK)), 1.0, 0.0).astype(BF16)

    xa = _dot_hi(small, wup_ref[...]) + balpha_ref[...]
    log_a = (jnp.minimum(xa, 0.0) - jnp.log1p(jnp.exp(-jnp.abs(xa)))) * (1.0 / GLA_GATE_NORM)
    bcum = _dot_sel_lhs(cum64, log_a)

    lane_kw = _iota((GLA_CHUNK, GLA_KW), 1)
    head_masks = [(lane_kw // GLA_DK) == h for h in range(GLA_HEADS)]
    lane_kw_s = _iota((GLA_DV, GLA_KW), 1)
    head_masks_s = [(lane_kw_s // GLA_DK) == h for h in range(GLA_HEADS)]
    tril64 = _iota((GLA_CHUNK, GLA_CHUNK), 1) <= _iota((GLA_CHUNK, GLA_CHUNK), 0)
    q_scale = GLA_DK ** -0.5

    for c in range(TL // GLA_CHUNK):
        rs = slice(c * GLA_CHUNK, (c + 1) * GLA_CHUNK)
        bc = bcum[rs]
        b_mid = bc[GLA_CHUNK // 2:GLA_CHUNK // 2 + 1]
        b_last = bc[GLA_CHUNK - 1:GLA_CHUNK]
        qc = pm_ref[rs, 0:GLA_KW].astype(F32) * q_scale
        kc = pm_ref[rs, GLA_KW:2 * GLA_KW].astype(F32)
        vc = pm_ref[rs, 2 * GLA_KW:2 * GLA_KW + GLA_WIDTH]
        q_in = (qc * jnp.exp(bc - b_mid)).astype(BF16)
        k_in = (kc * jnp.exp(b_mid - bc)).astype(BF16)
        q_st = (qc * jnp.exp(bc)).astype(BF16)
        k_st = (kc * jnp.exp(b_last - bc)).astype(BF16)
        st = gla_state[...]
        st_b = st.astype(BF16)
        zero_b = jnp.zeros_like(q_in)
        for h in range(GLA_HEADS):
            scores = _dot_nt(jnp.where(head_masks[h], q_in, zero_b), k_in)
            scores = jnp.where(tril64, scores, 0.0).astype(BF16)
            o_h = _dot(scores, vc[:, h * GLA_DV:(h + 1) * GLA_DV])
            o_h = o_h + _dot_nt(jnp.where(head_masks[h], q_st, zero_b), st_b)
            mix_scr[rs, h * GLA_DV:(h + 1) * GLA_DV] = o_h
        upd = _dot_tn(vc, k_st)
        new_st = st * jnp.exp(b_last)
        for h in range(GLA_HEADS):
            new_st = new_st + jnp.where(head_masks_s[h], upd[h * GLA_DV:(h + 1) * GLA_DV], 0.0)
        gla_state[...] = new_st
        run_side()

    xbc = pm_ref[:, 2048:3072].astype(F32)
    tail = conv_tail[...]
    conv_tail[...] = xbc[TL - 8:TL]
    row8 = _iota((8, SSD_CONV_CH), 0)
    conv = xbc * convw_ref[SSD_CONV - 1:SSD_CONV, :]
    for s in range(1, SSD_CONV):
        shifted = pltpu.roll(xbc, s, 0)
        head = jnp.where(row8 < s, pltpu.roll(tail, s, 0), shifted[0:8])
        shifted = jnp.concatenate([head, shifted[8:]], axis=0)
        conv = conv + shifted * convw_ref[SSD_CONV - 1 - s:SSD_CONV - s, :]
    act = _silu(conv + convb_ref[...])
    run_side()
    xs = act[:, 0:SSD_WIDTH]
    bm = act[:, SSD_WIDTH:SSD_WIDTH + SSD_GROUPS * SSD_STATE].astype(BF16)
    cm = act[:, SSD_WIDTH + SSD_GROUPS * SSD_STATE:].astype(BF16)

    dt_full = _softplus(small + dtb_ref[...])
    a_full = dt_full * aneg_ref[...]
    acum = _dot_sel_lhs(cum128, a_full)
    acum_t = acum.T

    e_row = _iota((N_SMALL, SSD_WIDTH), 0)
    e_col = _iota((N_SMALL, SSD_WIDTH), 1)
    spread64 = jnp.where(e_row == DT_COL + e_col // SSD_HEADDIM, 1.0, 0.0).astype(BF16)
    e_row2 = _iota((N_SMALL, SSD_HEADS * LANES), 0)
    e_col2 = _iota((N_SMALL, SSD_HEADS * LANES), 1)
    spread128 = jnp.where(e_row2 == DT_COL + e_col2 // LANES, 1.0, 0.0).astype(BF16)
    dt_e = _dot_sel_rhs(dt_full, spread64, terms=1)
    ac_e = _dot_sel_rhs(acum, spread64, terms=2)
    ac_w = _dot_sel_rhs(acum, spread128, terms=2)

    tril128 = _iota((SSD_CHUNK, SSD_CHUNK), 1) <= _iota((SSD_CHUNK, SSD_CHUNK), 0)
    lane_g = _iota((SSD_CHUNK, SSD_HPG * SSD_HEADDIM), 1)
    for c in range(TL // SSD_CHUNK):
        rs = slice(c * SSD_CHUNK, (c + 1) * SSD_CHUNK)
        ac_c = ac_e[rs]
        a_last = ac_c[SSD_CHUNK - 1:SSD_CHUNK]
        dt_c = dt_e[rs]
        xs_c = xs[rs]
        x_dt = (xs_c * dt_c).astype(BF16)
        x_w = (xs_c * (jnp.exp(a_last - ac_c) * dt_c)).astype(BF16)
        e_ac = jnp.exp(ac_c)
        for g in range(SSD_GROUPS):
            gs = slice(g * SSD_STATE, (g + 1) * SSD_STATE)
            ws = slice(g * SSD_HPG * SSD_HEADDIM, (g + 1) * SSD_HPG * SSD_HEADDIM)
            c_g = cm[rs, gs]
            b_g = bm[rs, gs]
            cb = _dot_nt(c_g, b_g)
            x_dt_g = x_dt[:, ws]
            lhs_parts = []
            rhs_parts = []
            for hh in range(SSD_HPG):
                h = g * SSD_HPG + hh
                seg = ac_w[rs, h * LANES:(h + 1) * LANES] - acum_t[DT_COL + h:DT_COL + h + 1, rs]
                lmat = jnp.where(tril128, jnp.exp(jnp.where(tril128, seg, 0.0)), 0.0)
                lhs_parts.append((cb * lmat).astype(BF16))
                rhs_parts.append(jnp.where((lane_g // SSD_HEADDIM) == hh, x_dt_g,
                                           jnp.zeros_like(x_dt_g)))
            intra = _dot(jnp.concatenate(lhs_parts, axis=1), jnp.concatenate(rhs_parts, axis=0))
            st = ssd_state[g]
            inter = _dot(c_g, st.astype(BF16)) * e_ac[:, ws]
            mix_scr[rs, GLA_WIDTH + g * 256:GLA_WIDTH + (g + 1) * 256] = intra + inter
            ssd_state[g] = st * jnp.exp(a_last[:, ws]) + _dot_tn(b_g, x_w[:, ws])
        run_side()

    run_side(len(side_work))

    o = mix_scr[:, 0:GLA_WIDTH]
    g_gate = _silu(pm_ref[:, 1024:1536].astype(F32))
    gla_parts = []
    for h in range(GLA_HEADS):
        o_h = o[:, h * GLA_DV:(h + 1) * GLA_DV]
        ms = jnp.mean(o_h * o_h, axis=-1, keepdims=True)
        gla_parts.append(o_h * lax.rsqrt(ms + GROUP_EPS))
    gla_out = jnp.concatenate(gla_parts, axis=1) * gnw_ref[...] * g_gate

    y = mix_scr[:, GLA_WIDTH:] + dexp_ref[...] * xs
    y = y * _silu(pm_ref[:, 1536:2048].astype(F32))
    ssd_parts = []
    for g in range(SSD_GROUPS):
        y_g = y[:, g * 256:(g + 1) * 256]
        ms = jnp.mean(y_g * y_g, axis=-1, keepdims=True)
        ssd_parts.append(y_g * lax.rsqrt(ms + GROUP_EPS))
    ssd_out = jnp.concatenate(ssd_parts, axis=1) * snw_ref[...]

    mixed = jnp.concatenate([gla_out, ssd_out], axis=1).astype(BF16)
    x1 = x_ref[...] + _dot(mixed, wout_ref[...])
    x1_ref[...] = x1

    ms = jnp.mean(x1 * x1, axis=-1, keepdims=True)
    h2 = x1 * lax.rsqrt(ms + EPS) * nfw_ref[...]
    h2_ref[...] = h2.astype(BF16)
    lg = _dot_hi_nt(rw_ref[...], h2) + rb_ref[...]
    lg_ref[...] = lg
    cnt_scr[...] = cnt_scr[...] + _tile_counts(_top4(lg[0:N_EXPERTS])[0])[1]
    cnt_ref[...] = cnt_scr[...]


def _mixer(x, nmw, w_all, wup, balpha, gnw, convw, convb, dtb, aneg, dexp, snw, wout, nfw, rw, rb):
    bsz, seqlen, _ = x.shape
    per_row = seqlen // TL
    steps = bsz * per_row

    def full(a):
        return pl.BlockSpec(a.shape, lambda g: (0,) * a.ndim)

    def tile(ahead):
        def index(g):
            tile_id = jnp.minimum(g + ahead, steps - 1)
            return (tile_id // per_row, tile_id % per_row, 0)
        return pl.BlockSpec((None, TL, D_MODEL), index)

    params = (nmw, w_all, wup, balpha, gnw, convw, convb, dtb, aneg, dexp, snw, wout, nfw, rw, rb)
    return pl.pallas_call(
        functools.partial(_mixer_kernel, tiles_per_row=per_row),
        grid=(steps,),
        in_specs=[tile(0), tile(1)] + [full(p) for p in params],
        out_specs=[tile(0), tile(0),
                   pl.BlockSpec((LANES, TL), lambda g: (0, g)),
                   pl.BlockSpec((N_EXPERTS, LANES), lambda g: (0, 0))],
        out_shape=[
            jax.ShapeDtypeStruct((bsz, seqlen, D_MODEL), F32),
            jax.ShapeDtypeStruct((bsz, seqlen, D_MODEL), BF16),
            jax.ShapeDtypeStruct((LANES, bsz * seqlen), F32),
            jax.ShapeDtypeStruct((N_EXPERTS, LANES), F32),
        ],
        scratch_shapes=[
            pltpu.VMEM((GLA_DV, GLA_KW), F32),
            pltpu.VMEM((SSD_GROUPS, SSD_STATE, SSD_HPG * SSD_HEADDIM), F32),
            pltpu.VMEM((8, SSD_CONV_CH), F32),
            pltpu.VMEM((TL, D_MODEL), F32),
            pltpu.VMEM((N_EXPERTS, LANES), F32),
            pltpu.VMEM((TL, N_MAIN), BF16),
            pltpu.VMEM((TL, N_MAIN), BF16),
            pltpu.VMEM((TL, N_SMALL), F32),
            pltpu.VMEM((TL, N_SMALL), F32),
        ],
        compiler_params=pltpu.CompilerParams(
            dimension_semantics=("arbitrary",), vmem_limit_bytes=VMEM_LIMIT),
        name="mixer",
    )(x, x, *params)


def _top4(lg):
    n_e, n_t = lg.shape
    row = _iota((n_e, n_t), 0)
    work = lg
    onehots = []
    vals = []
    for _ in range(TOP_K):
        m = jnp.max(work, axis=0, keepdims=True)
        idx = jnp.min(jnp.where(work == m, row, n_e), axis=0, keepdims=True)
        oh = row == idx
        onehots.append(oh)
        vals.append(m)
        work = jnp.where(oh, -jnp.inf, work)
    return onehots, vals


def _tile_counts(onehots):
    multi = jnp.where(onehots[0] | onehots[1] | onehots[2] | onehots[3], 1.0, 0.0).astype(BF16)
    return multi, _dot(multi, jnp.ones((multi.shape[1], LANES), BF16))


def _route_kernel(lg_ref, cnt_ref, info_ref, lp_ref, runs_ref, run_scr):
    @pl.when(pl.program_id(0) == 0)
    def _():
        run_scr[...] = jnp.zeros_like(run_scr)

    onehots, vals = _top4(lg_ref[0:N_EXPERTS, :])
    multi, tile_cnt = _tile_counts(onehots)
    counts = cnt_ref[...]
    padded = jnp.floor((counts + (TM - 1)) * (1.0 / TM)) * TM
    lower = jnp.where(_iota((N_EXPERTS, N_EXPERTS), 1) < _iota((N_EXPERTS, N_EXPERTS), 0),
                      1.0, 0.0).astype(BF16)
    offs = _dot_sel_lhs(lower, padded)
    loff = _dot_sel_lhs(lower, tile_cnt)
    before = jnp.where(_iota((TT, TT), 0) < _iota((TT, TT), 1), 1.0, 0.0).astype(BF16)
    rank = _dot(multi, before)
    lane = _iota((N_EXPERTS, LANES), 1)
    table = jnp.where(lane == RUN_START, offs + run_scr[...],
                      jnp.where(lane == RUN_LEN, tile_cnt, jnp.where(lane == RUN_LOFF, loff, 0.0)))
    table = jnp.concatenate([table, jnp.zeros((LANES - N_EXPERTS, LANES), F32)], axis=0)
    runs_ref[...] = table.T[0:8].astype(jnp.int32)
    run_scr[...] = run_scr[...] + tile_cnt
    local = rank + jnp.concatenate([loff] * (TT // LANES), axis=1)
    exps = [jnp.exp(v - vals[0]) for v in vals]
    den = exps[0] + exps[1] + exps[2] + exps[3]
    lp_rows = [jnp.sum(jnp.where(oh, local, 0.0), axis=0, keepdims=True) for oh in onehots]
    lp_ref[...] = jnp.concatenate(lp_rows + [jnp.zeros((8 - TOP_K, TT), F32)], axis=0).astype(jnp.int32)
    gate_rows = [e / den for e in exps]
    info = jnp.concatenate(gate_rows + lp_rows + [jnp.zeros((LANES - 2 * TOP_K, TT), F32)], axis=0)
    info_ref[...] = info.T


def _route(logits_t, counts):
    t = logits_t.shape[1]
    steps = t // TT
    return pl.pallas_call(
        _route_kernel,
        grid=(steps,),
        in_specs=[pl.BlockSpec((LANES, TT), lambda i: (0, i)),
                  pl.BlockSpec((N_EXPERTS, LANES), lambda i: (0, 0))],
        out_specs=[
            pl.BlockSpec((TT, LANES), lambda i: (i, 0)),
            pl.BlockSpec((8, TT), lambda i: (0, i)),
            pl.BlockSpec((8, LANES), lambda i: (i, 0)),
        ],
        out_shape=[
            jax.ShapeDtypeStruct((t, LANES), F32),
            jax.ShapeDtypeStruct((8, t), jnp.int32),
            jax.ShapeDtypeStruct((steps * 8, LANES), jnp.int32),
        ],
        scratch_shapes=[pltpu.VMEM((N_EXPERTS, LANES), F32)],
        compiler_params=pltpu.CompilerParams(
            dimension_semantics=("arbitrary",), vmem_limit_bytes=VMEM_LIMIT),
        name="route",
    )(logits_t, counts)


def _rows(first, count):
    return pl.ds(pl.multiple_of(first * ROW_TILE, ROW_TILE), count * ROW_TILE)


def _wait_rows(stage_slot, hbm_ref, sem, to_hbm):
    for w in range(STAGE_ROWS // WAIT_ROWS):
        part = stage_slot.at[pl.ds(w * WAIT_ROWS * ROW_TILE, WAIT_ROWS * ROW_TILE), :]
        hbm = hbm_ref.at[pl.ds(0, WAIT_ROWS * ROW_TILE), :]
        src, dst = (part, hbm) if to_hbm else (hbm, part)
        pltpu.make_async_copy(src, dst, sem).wait()


def _for_each_piece(length, max_log2, fn):
    for b in reversed(range(max_log2 + 1)):
        size = 1 << b
        offset = lax.shift_left(lax.shift_right_logical(length, b + 1), b + 1)

        @pl.when((length & size) != 0)
        def _(offset=offset, size=size):
            fn(offset, size)


def _plan_kernel(cnt_ref, plan_ref, *, n_tiles, p_rows):
    def clear(i, c):
        for r in range(8):
            plan_ref[r, i] = 0
        return c
    lax.fori_loop(0, PLAN_COLS, clear, 0)

    def per_expert(e, carry):
        first_row, group = carry
        count = cnt_ref[e, 0].astype(jnp.int32)
        tiles = lax.shift_right_logical(count + (TM - 1), TM_LOG2)
        first_tile = lax.shift_right_logical(first_row, TM_LOG2)

        def mark(i, c):
            plan_ref[PLAN_EXPERT, i] = e
            plan_ref[PLAN_PARITY, i] = group & 1
            return c
        lax.fori_loop(first_tile, first_tile + tiles, mark, 0)
        plan_ref[PLAN_PAD_START, e] = first_row + count
        plan_ref[PLAN_PAD_LEN, e] = tiles * TM - count
        return first_row + tiles * TM, group + jnp.minimum(tiles, 1)

    used_rows, _ = lax.fori_loop(0, N_EXPERTS, per_expert, (jnp.int32(0), jnp.int32(0)))
    n_valid = lax.shift_right_logical(used_rows, TM_LOG2)
    plan_ref[PLAN_NVALID, 0] = n_valid
    plan_ref[PLAN_PAD_START, N_EXPERTS] = used_rows
    plan_ref[PLAN_PAD_LEN, N_EXPERTS] = lax.shift_right_logical(p_rows - used_rows, TM_LOG2 - 1)

    last_expert = plan_ref[PLAN_EXPERT, n_valid - 1]
    last_parity = plan_ref[PLAN_PARITY, n_valid - 1]

    def mark_unused(i, c):
        plan_ref[PLAN_EXPERT, i] = last_expert
        plan_ref[PLAN_PARITY, i] = last_parity
        return c
    lax.fori_loop(n_valid, n_tiles, mark_unused, 0)

    def next_expert(k, carry):
        expert_after, next_after = carry
        i = n_tiles - 1 - k
        e = plan_ref[PLAN_EXPERT, i]
        nxt = jnp.where(expert_after != e, expert_after, next_after)
        plan_ref[PLAN_NEXT, i] = nxt
        return e, nxt
    lax.fori_loop(0, n_tiles, next_expert, (last_expert, last_expert))


def _plan(counts, n_tiles, p_rows):
    assert n_tiles <= PLAN_COLS and N_EXPERTS < PLAN_COLS
    return pl.pallas_call(
        functools.partial(_plan_kernel, n_tiles=n_tiles, p_rows=p_rows),
        in_specs=[pl.BlockSpec(memory_space=pltpu.SMEM)],
        out_specs=pl.BlockSpec(memory_space=pltpu.SMEM),
        out_shape=jax.ShapeDtypeStruct((8, PLAN_COLS), jnp.int32),
        name="plan",
    )(counts)


def _dispatch_kernel(runs_ref, plan_ref, lp_ref, h_ref, xs_ref, stage, zeros, sem, zsem):
    j = pl.program_id(0)
    slot = j & 1

    def wait_tile(s):
        _wait_rows(stage.at[s], xs_ref, sem.at[s], to_hbm=True)

    @pl.when(j == 0)
    def _():
        zeros[...] = jnp.zeros_like(zeros)

        def pad_pieces(e, wait):
            def piece(offset, size):
                cp = pltpu.make_async_copy(
                    zeros.at[pl.ds(0, size * ROW_TILE), :],
                    xs_ref.at[_rows(plan_ref[PLAN_PAD_START, e] + offset, size), :], zsem.at[e % 2])
                cp.wait() if wait else cp.start()
            _for_each_piece(plan_ref[PLAN_PAD_LEN, e], TM_LOG2 - 1, piece)

        for e in range(N_EXPERTS):
            pad_pieces(e, wait=False)
            if e >= 1:
                pad_pieces(e - 1, wait=True)
        pad_pieces(N_EXPERTS - 1, wait=True)

        n_blocks = plan_ref[PLAN_PAD_LEN, N_EXPERTS]

        def block_copy(blk):
            first = plan_ref[PLAN_PAD_START, N_EXPERTS] + blk * (TM // 2)
            return pltpu.make_async_copy(zeros, xs_ref.at[_rows(first, TM // 2), :], zsem.at[blk & 1])

        def put_block(blk, carry):
            block_copy(blk).start()

            @pl.when(blk >= 2)
            def _():
                block_copy(blk - 2).wait()
            return carry
        lax.fori_loop(0, n_blocks, put_block, 0)
        for back in (2, 1):
            @pl.when(n_blocks >= back)
            def _(back=back):
                block_copy(n_blocks - back).wait()

    @pl.when(j >= 2)
    def _():
        wait_tile(slot)

    h = h_ref[...]
    lp = lp_ref[...]
    for c in range(STAGE_ROWS // TT):
        r = _iota((TT, TT), 0) + c * TT
        hit = (r == lp[0:1]) | (r == lp[1:2]) | (r == lp[2:3]) | (r == lp[3:4])
        rows = _dot(jnp.where(hit, 1.0, 0.0).astype(BF16), h)
        for q in range(ROW_TILE):
            stage[slot, pl.ds(c * TT * ROW_TILE + q, TT, stride=ROW_TILE), :] = rows[:, q * LANES:(q + 1) * LANES]

    for e in range(N_EXPERTS):
        src = runs_ref[j * 8 + RUN_LOFF, e]
        dst = runs_ref[j * 8 + RUN_START, e]

        def put(offset, size, src=src, dst=dst):
            pltpu.make_async_copy(stage.at[slot, _rows(src + offset, size), :],
                                  xs_ref.at[_rows(dst + offset, size), :], sem.at[slot]).start()
        _for_each_piece(runs_ref[j * 8 + RUN_LEN, e], TT_LOG2, put)

    @pl.when(j == pl.num_programs(0) - 1)
    def _():
        wait_tile(1 - slot)
        wait_tile(slot)


def _dispatch(runs, plan, lp, h2, p_rows):
    t = h2.shape[0]
    return pl.pallas_call(
        _dispatch_kernel,
        grid_spec=pltpu.PrefetchScalarGridSpec(
            num_scalar_prefetch=2,
            grid=(t // TT,),
            in_specs=[
                pl.BlockSpec((8, TT), lambda j, *_: (0, j)),
                pl.BlockSpec((TT, D_MODEL), lambda j, *_: (j, 0)),
            ],
            out_specs=pl.BlockSpec(memory_space=pl.ANY),
            scratch_shapes=[
                pltpu.VMEM((2, STAGE_ROWS * ROW_TILE, LANES), F32),
                pltpu.VMEM((TM // 2 * ROW_TILE, LANES), F32),
                pltpu.SemaphoreType.DMA((2,)),
                pltpu.SemaphoreType.DMA((2,)),
            ],
        ),
        out_shape=jax.ShapeDtypeStruct((p_rows * ROW_TILE, LANES), F32),
        compiler_params=pltpu.CompilerParams(
            dimension_semantics=("arbitrary",), vmem_limit_bytes=VMEM_LIMIT),
        name="dispatch",
    )(runs, plan, lp, h2)


def _experts_kernel(plan_ref, x_ref, wg_ref, bg_ref, wu_ref, bu_ref, wd_ref, bd_ref,
                    o_ref, act, wbuf, wg_b, wu_b, wd_b, wsem):
    i = pl.program_id(0)
    prev = jnp.maximum(i - 1, 0)

    def fetch(expert, s):
        return [pltpu.make_async_copy(w_ref.at[expert], wbuf.at[s, m], wsem.at[s])
                for m, w_ref in enumerate((wg_ref, wu_ref, wd_ref))]

    n_valid = plan_ref[PLAN_NVALID, 0]

    @pl.when(i >= n_valid)
    def _():
        o_ref[...] = jnp.zeros_like(o_ref)

    @pl.when(i < n_valid)
    def _():
        expert = plan_ref[PLAN_EXPERT, i]
        s = plan_ref[PLAN_PARITY, i]
        nxt = plan_ref[PLAN_NEXT, i]

        @pl.when(i == 0)
        def _():
            for cp in fetch(expert, s):
                cp.start()

        @pl.when((i == 0) | (expert != plan_ref[PLAN_EXPERT, prev]))
        def _():
            for cp in fetch(expert, s):
                cp.wait()
            for c in range(N_CHUNKS):
                cs = slice(c * N_COLS, (c + 1) * N_COLS)
                wg_b[c] = wbuf[s, 0, :, cs].astype(BF16)
                wu_b[c] = wbuf[s, 1, :, cs].astype(BF16)
                wd_b[c] = wbuf[s, 2, :, cs].astype(BF16)

            @pl.when(nxt != expert)
            def _():
                for cp in fetch(nxt, 1 - s):
                    cp.start()

        xb = _load_row_tiles(x_ref, TM).astype(BF16)
        for n in range(N_CHUNKS):
            gate = _dot(xb, wg_b[n]) + bg_ref[n:n + 1, :]
            up = _dot(xb, wu_b[n]) + bu_ref[n:n + 1, :]
            gate = jnp.minimum(gate, SWIGLU_LIMIT)
            up = jnp.clip(up, -SWIGLU_LIMIT, SWIGLU_LIMIT)
            act[n] = ((up + 1.0) * (gate * jax.nn.sigmoid(SWIGLU_ALPHA * gate))).astype(BF16)
        a = jnp.concatenate([act[c] for c in range(N_CHUNKS)], axis=1)
        for n in range(N_CHUNKS):
            out = _dot(a, wd_b[n]) + bd_ref[n:n + 1, :]
            for q in range(N_COLS // LANES):
                o_ref[pl.ds(n * (N_COLS // LANES) + q, TM, stride=ROW_TILE), :] = (
                    out[:, q * LANES:(q + 1) * LANES])


def _experts(plan, xs, wg, bg, wu, bu, wd, bd):
    n_tiles = xs.shape[0] // (TM * ROW_TILE)

    def x_map(i, plan_ref):
        return (jnp.minimum(i, plan_ref[PLAN_NVALID, 0] - 1), 0)

    def b_map(i, plan_ref):
        return (plan_ref[PLAN_EXPERT, i], 0, 0)

    w_spec = pl.BlockSpec(memory_space=pl.ANY)
    b_spec = pl.BlockSpec((None, N_CHUNKS, N_COLS), b_map)
    return pl.pallas_call(
        _experts_kernel,
        grid_spec=pltpu.PrefetchScalarGridSpec(
            num_scalar_prefetch=1,
            grid=(n_tiles,),
            in_specs=[pl.BlockSpec((TM * ROW_TILE, LANES), x_map),
                      w_spec, b_spec, w_spec, b_spec, w_spec, b_spec],
            out_specs=pl.BlockSpec((TM * ROW_TILE, LANES), lambda i, plan_ref: (i, 0)),
            scratch_shapes=[
                pltpu.VMEM((N_CHUNKS, TM, N_COLS), BF16),
                pltpu.VMEM((2, 3, D_MODEL, D_MODEL), F32),
                pltpu.VMEM((N_CHUNKS, D_MODEL, N_COLS), BF16),
                pltpu.VMEM((N_CHUNKS, D_MODEL, N_COLS), BF16),
                pltpu.VMEM((N_CHUNKS, D_MODEL, N_COLS), BF16),
                pltpu.SemaphoreType.DMA((2,)),
            ],
        ),
        out_shape=jax.ShapeDtypeStruct(xs.shape, F32),
        compiler_params=pltpu.CompilerParams(
            dimension_semantics=("arbitrary",), vmem_limit_bytes=VMEM_LIMIT),
        name="experts",
    )(plan, xs, wg, bg, wu, bu, wd, bd)


def _combine_kernel(runs_ref, info_ref, x1_ref, fw_ref, eo_ref, out_ref, stage, sem):
    j = pl.program_id(0)
    slot = lax.rem(j, 3)
    last = pl.num_programs(0) - 1

    def fetch(tile, s):
        for e in range(N_EXPERTS):
            src = runs_ref[tile * 8 + RUN_START, e]
            dst = runs_ref[tile * 8 + RUN_LOFF, e]

            def get(offset, size, src=src, dst=dst):
                pltpu.make_async_copy(eo_ref.at[_rows(src + offset, size), :],
                                      stage.at[s, _rows(dst + offset, size), :], sem.at[s]).start()
            _for_each_piece(runs_ref[tile * 8 + RUN_LEN, e], TT_LOG2, get)

    @pl.when(j == 0)
    def _():
        fetch(0, 0)
        fetch(jnp.minimum(1, last), 1)

    _wait_rows(stage.at[slot], eo_ref, sem.at[slot], to_hbm=False)

    info = info_ref[...]
    y = x1_ref[...]
    for c in range(STAGE_ROWS // TT):
        r = (_iota((TT, TT), 1) + c * TT).astype(F32)
        g = jnp.zeros((TT, TT), F32)
        for k in range(TOP_K):
            g = g + jnp.where(r == info[:, TOP_K + k:TOP_K + k + 1], info[:, k:k + 1], 0.0)
        g_hi = g.astype(BF16)
        g_lo = (g - g_hi.astype(F32)).astype(BF16)
        rows = jnp.concatenate(
            [stage[slot, pl.ds(c * TT * ROW_TILE + q, TT, stride=ROW_TILE), :] for q in range(ROW_TILE)],
            axis=1).astype(BF16)
        y = y + _dot(g_hi, rows) + _dot(g_lo, rows)
    ms = jnp.mean(y * y, axis=-1, keepdims=True)
    out_ref[...] = y * lax.rsqrt(ms + EPS) * fw_ref[...]

    ahead = lax.rem(j + 2, 3)
    fetch(jnp.minimum(j + 2, last), ahead)

    @pl.when(j == last)
    def _():
        _wait_rows(stage.at[lax.rem(j + 1, 3)], eo_ref, sem.at[lax.rem(j + 1, 3)], to_hbm=False)
        _wait_rows(stage.at[ahead], eo_ref, sem.at[ahead], to_hbm=False)


def _combine(runs, info, x1, final_w, eo):
    t = x1.shape[0]
    return pl.pallas_call(
        _combine_kernel,
        grid_spec=pltpu.PrefetchScalarGridSpec(
            num_scalar_prefetch=1,
            grid=(t // TT,),
            in_specs=[
                pl.BlockSpec((TT, LANES), lambda j, *_: (j, 0)),
                pl.BlockSpec((TT, D_MODEL), lambda j, *_: (j, 0)),
                pl.BlockSpec((1, D_MODEL), lambda j, *_: (0, 0)),
                pl.BlockSpec(memory_space=pl.ANY),
            ],
            out_specs=pl.BlockSpec((TT, D_MODEL), lambda j, *_: (j, 0)),
            scratch_shapes=[
                pltpu.VMEM((3, STAGE_ROWS * ROW_TILE, LANES), F32),
                pltpu.SemaphoreType.DMA((3,)),
            ],
        ),
        out_shape=jax.ShapeDtypeStruct((t, D_MODEL), F32),
        compiler_params=pltpu.CompilerParams(
            dimension_semantics=("arbitrary",), vmem_limit_bytes=VMEM_LIMIT),
        name="combine",
    )(runs, info, x1, final_w, eo)


def _pad_lanes(v, offset, fill=0.0):
    row = jnp.full((1, LANES), fill, F32)
    return row.at[0, offset:offset + v.shape[0]].set(v.astype(F32))


def kernel(x, norm_mix_w, w_in, gla_w_alpha_up, gla_b_alpha, gla_norm_w, ssd_conv_w, ssd_conv_b,
           ssd_dt_bias, ssd_A_log, ssd_D, ssd_norm_w, w_out, norm_ffn_w, router_w, router_b,
           moe_w_gate, moe_b_gate, moe_w_up, moe_b_up, moe_w_down, moe_b_down, final_norm_w):
    bsz, seqlen, d = x.shape
    t = bsz * seqlen
    depth = w_in.shape[0]
    assert depth == 1, "the final RMSNorm is fused into the (single) layer's combine step"
    p_rows = t * TOP_K + N_EXPERTS * TM
    n_tiles = p_rows // TM
    for l in range(depth):
        w = w_in[l]
        w_all = jnp.concatenate(
            [w[:, 0:1536], w[:, 1552:3088], w[:, 1536:1552], w[:, 3088:3096],
             jnp.zeros((d, N_SMALL - GLA_GATE_RANK - SSD_HEADS), w.dtype)], axis=1).astype(BF16)
        wup = jnp.zeros((N_SMALL, GLA_KW), F32).at[0:GLA_GATE_RANK].set(gla_w_alpha_up[l])
        dtb = _pad_lanes(ssd_dt_bias[l], DT_COL)
        aneg = _pad_lanes(-jnp.exp(ssd_A_log[l].astype(F32)), DT_COL)
        dexp = jnp.repeat(ssd_D[l].astype(F32), SSD_HEADDIM)[None, :]
        rw = jnp.zeros((LANES, d), F32).at[0:N_EXPERTS].set(router_w[l].T)
        rb = jnp.zeros((LANES, TL), F32).at[0:N_EXPERTS].set(
            jnp.broadcast_to(router_b[l].astype(F32)[:, None], (N_EXPERTS, TL)))

        x1, h2, logits, counts = _mixer(
            x, norm_mix_w[l][None, :], w_all, wup, gla_b_alpha[l][None, :], gla_norm_w[l][None, :], ssd_conv_w[l],
            ssd_conv_b[l][None, :], dtb, aneg, dexp, ssd_norm_w[l][None, :],
            w_out[l].astype(BF16), norm_ffn_w[l][None, :], rw, rb)

        info, lp, runs = _route(logits, counts)
        plan = _plan(counts, n_tiles, p_rows)
        xs = _dispatch(runs, plan, lp, h2.reshape(t, d), p_rows)
        eo = _experts(plan, xs,
                      moe_w_gate[l], moe_b_gate[l].reshape(N_EXPERTS, N_CHUNKS, N_COLS),
                      moe_w_up[l], moe_b_up[l].reshape(N_EXPERTS, N_CHUNKS, N_COLS),
                      moe_w_down[l], moe_b_down[l].reshape(N_EXPERTS, N_CHUNKS, N_COLS))
        x = _combine(runs, info, x1.reshape(t, d), final_norm_w[None, :], eo).reshape(bsz, seqlen, d)
    return x
```

```python
import functools

import jax
import jax.numpy as jnp
from jax import lax
from jax.experimental import pallas as pl
from jax.experimental.pallas import tpu as pltpu

F32 = jnp.float32
BF16 = jnp.bfloat16

D_MODEL = 1024
GLA_WIDTH = 512
GLA_HEADS = 4
GLA_DV = 128
GLA_DK = 64
GLA_KW = 256
GLA_GATE_RANK = 16
GLA_GATE_NORM = 16.0
SSD_WIDTH = 512
SSD_HEADDIM = 64
SSD_HEADS = 8
SSD_GROUPS = 2
SSD_HPG = 4
SSD_STATE = 128
SSD_CONV = 4
SSD_CONV_CH = 1024
N_EXPERTS = 32
TOP_K = 4
SWIGLU_LIMIT = 7.0
SWIGLU_ALPHA = 1.702
EPS = 1e-6
GROUP_EPS = 1e-5

LANES = 128
ROW_TILE = D_MODEL // LANES
N_MAIN = 3072
N_SMALL = LANES
DT_COL = GLA_GATE_RANK

GLA_CHUNK = 64
SSD_CHUNK = 128
PROJ_COLS = 512
TL = 256
TT_LOG2 = 8
TT = 1 << TT_LOG2
STAGE_ROWS = 4 * TT
TM_LOG2 = 9
TM = 1 << TM_LOG2
WAIT_ROWS = 512
RUN_START, RUN_LEN, RUN_LOFF = 0, 1, 2
PLAN_EXPERT, PLAN_NEXT, PLAN_PARITY, PLAN_NVALID, PLAN_PAD_START, PLAN_PAD_LEN = 0, 1, 2, 3, 4, 5
PLAN_COLS = 512
assert STAGE_ROWS * D_MODEL * 4 <= (1 << 17) * 32
N_COLS = 256
N_CHUNKS = D_MODEL // N_COLS
VMEM_LIMIT = 56 * 1024 * 1024


def _dot(a, b):
    return jnp.dot(a, b, preferred_element_type=F32)


def _dot_nt(a, b):
    return lax.dot_general(a, b, (((1,), (1,)), ((), ())), preferred_element_type=F32)


def _dot_tn(a, b):
    return lax.dot_general(a, b, (((0,), (0,)), ((), ())), preferred_element_type=F32)


def _split3(a):
    hi = a.astype(BF16)
    r1 = a - hi.astype(F32)
    mid = r1.astype(BF16)
    lo = (r1 - mid.astype(F32)).astype(BF16)
    return hi, mid, lo


def _dot_sel_lhs(sel, a):
    hi, mid, lo = _split3(a)
    return _dot(sel, hi) + _dot(sel, mid) + _dot(sel, lo)


def _dot_sel_rhs(a, sel, terms=3):
    parts = _split3(a)[:terms]
    out = _dot(parts[0], sel)
    for p in parts[1:]:
        out = out + _dot(p, sel)
    return out


def _dot_hi(a, b):
    a_hi = a.astype(BF16)
    a_lo = (a - a_hi.astype(F32)).astype(BF16)
    b_hi = b.astype(BF16)
    b_lo = (b - b_hi.astype(F32)).astype(BF16)
    return _dot(a_hi, b_hi) + _dot(a_lo, b_hi) + _dot(a_hi, b_lo)


def _dot_hi_nt(a, b):
    a_hi = a.astype(BF16)
    a_lo = (a - a_hi.astype(F32)).astype(BF16)
    b_hi = b.astype(BF16)
    b_lo = (b - b_hi.astype(F32)).astype(BF16)
    return _dot_nt(a_hi, b_hi) + _dot_nt(a_lo, b_hi) + _dot_nt(a_hi, b_lo)


def _softplus(x):
    return jnp.maximum(x, 0.0) + jnp.log1p(jnp.exp(-jnp.abs(x)))


def _silu(x):
    return x * jax.nn.sigmoid(x)


def _iota(shape, dim):
    return lax.broadcasted_iota(jnp.int32, shape, dim)


def _load_row_tiles(ref, rows):
    return jnp.concatenate([ref[pl.ds(j, rows, stride=ROW_TILE), :] for j in range(ROW_TILE)], axis=1)


def _project_parts(x_ref, nmw_ref, win_ref, pm_ref, small_ref):
    state = {}

    def norm():
        x_in = x_ref[...]
        ms = jnp.mean(x_in * x_in, axis=-1, keepdims=True)
        state["h"] = (x_in * lax.rsqrt(ms + EPS) * nmw_ref[...]).astype(BF16)

    def chunk(n0):
        def run():
            pm_ref[:, n0:n0 + PROJ_COLS] = _dot(state["h"], win_ref[:, n0:n0 + PROJ_COLS]).astype(BF16)
        return run

    def small():
        small_ref[...] = _dot(state["h"], win_ref[:, N_MAIN:N_MAIN + N_SMALL])

    return [norm] + [chunk(n0) for n0 in range(0, N_MAIN, PROJ_COLS)] + [small]


def _mixer_kernel(x_ref, xn_ref, nmw_ref, win_ref, *refs, tiles_per_row):
    params, outs = refs[:13], refs[13:17]
    gla_state, ssd_state, conv_tail, mix_scr, cnt_scr, pm_a, pm_b, small_a, small_b = refs[17:]
    g = pl.program_id(0)

    @pl.when(g == 0)
    def _():
        cnt_scr[...] = jnp.zeros_like(cnt_scr)
        for part in _project_parts(x_ref, nmw_ref, win_ref, pm_a, small_a):
            part()

    @pl.when(lax.rem(g, tiles_per_row) == 0)
    def _():
        gla_state[...] = jnp.zeros_like(gla_state)
        ssd_state[...] = jnp.zeros_like(ssd_state)
        conv_tail[...] = jnp.zeros_like(conv_tail)

    for parity, (pm_cur, small_cur, pm_nxt, small_nxt) in enumerate(
            ((pm_a, small_a, pm_b, small_b), (pm_b, small_b, pm_a, small_a))):
        @pl.when((g & 1) == parity)
        def _(pm_cur=pm_cur, small_cur=small_cur, pm_nxt=pm_nxt, small_nxt=small_nxt):
            _mixer_tile(_project_parts(xn_ref, nmw_ref, win_ref, pm_nxt, small_nxt),
                        pm_cur, small_cur, x_ref, *params, *outs,
                        gla_state, ssd_state, conv_tail, mix_scr, cnt_scr)


def _mixer_tile(side_work, pm_ref, small_ref, x_ref, wup_ref, balpha_ref, gnw_ref, convw_ref, convb_ref,
                dtb_ref, aneg_ref, dexp_ref, snw_ref, wout_ref, nfw_ref, rw_ref, rb_ref,
                x1_ref, h2_ref, lg_ref, cnt_ref,
                gla_state, ssd_state, conv_tail, mix_scr, cnt_scr):
    side_work = list(side_work)

    def run_side(n=1):
        for _ in range(min(n, len(side_work))):
            side_work.pop(0)()

    small = small_ref[...]

    row = _iota((TL, TL), 0)
    col = _iota((TL, TL), 1)
    causal = col <= row
    cum64 = jnp.where(causal & ((row // GLA_CHUNK) == (col // GLA_CHUNK)), 1.0, 0.0).astype(BF16)
    cum128 = jnp.where(causal & ((row // SSD_CHUNK) == (col // SSD_CHUNK)), 1.0, 0.0).astype(BF16)

    xa = _dot_hi(small, wup_ref[...]) + balpha_ref[...]
    log_a = (jnp.minimum(xa, 0.0) - jnp.log1p(jnp.exp(-jnp.abs(xa)))) * (1.0 / GLA_GATE_NORM)
    bcum = _dot_sel_lhs(cum64, log_a)

    lane_kw = _iota((GLA_CHUNK, GLA_KW), 1)
    head_masks = [(lane_kw // GLA_DK) == h for h in range(GLA_HEADS)]
    lane_kw_s = _iota((GLA_DV, GLA_KW), 1)
    head_masks_s = [(lane_kw_s // GLA_DK) == h for h in range(GLA_HEADS)]
    tril64 = _iota((GLA_CHUNK, GLA_CHUNK), 1) <= _iota((GLA_CHUNK, GLA_CHUNK), 0)
    q_scale = GLA_DK ** -0.5

    def gla_chunk(c):
        rs = slice(c * GLA_CHUNK, (c + 1) * GLA_CHUNK)
        bc = bcum[rs]
        b_mid = bc[GLA_CHUNK // 2:GLA_CHUNK // 2 + 1]
        b_last = bc[GLA_CHUNK - 1:GLA_CHUNK]
        qc = pm_ref[rs, 0:GLA_KW].astype(F32) * q_scale
        kc = pm_ref[rs, GLA_KW:2 * GLA_KW].astype(F32)
        vc = pm_ref[rs, 2 * GLA_KW:2 * GLA_KW + GLA_WIDTH]
        q_in = (qc * jnp.exp(bc - b_mid)).astype(BF16)
        k_in = (kc * jnp.exp(b_mid - bc)).astype(BF16)
        q_st = (qc * jnp.exp(bc)).astype(BF16)
        k_st = (kc * jnp.exp(b_last - bc)).astype(BF16)
        st = gla_state[...]
        st_b = st.astype(BF16)
        zero_b = jnp.zeros_like(q_in)
        for h in range(GLA_HEADS):
            scores = _dot_nt(jnp.where(head_masks[h], q_in, zero_b), k_in)
            scores = jnp.where(tril64, scores, 0.0).astype(BF16)
            o_h = _dot(scores, vc[:, h * GLA_DV:(h + 1) * GLA_DV])
            o_h = o_h + _dot_nt(jnp.where(head_masks[h], q_st, zero_b), st_b)
            mix_scr[rs, h * GLA_DV:(h + 1) * GLA_DV] = o_h
        upd = _dot_tn(vc, k_st)
        new_st = st * jnp.exp(b_last)
        for h in range(GLA_HEADS):
            new_st = new_st + jnp.where(head_masks_s[h], upd[h * GLA_DV:(h + 1) * GLA_DV], 0.0)
        gla_state[...] = new_st
        run_side()

    xbc = pm_ref[:, 2048:3072].astype(F32)
    tail = conv_tail[...]
    conv_tail[...] = xbc[TL - 8:TL]
    row8 = _iota((8, SSD_CONV_CH), 0)
    conv = xbc * convw_ref[SSD_CONV - 1:SSD_CONV, :]
    for s in range(1, SSD_CONV):
        shifted = pltpu.roll(xbc, s, 0)
        head = jnp.where(row8 < s, pltpu.roll(tail, s, 0), shifted[0:8])
        shifted = jnp.concatenate([head, shifted[8:]], axis=0)
        conv = conv + shifted * convw_ref[SSD_CONV - 1 - s:SSD_CONV - s, :]
    act = _silu(conv + convb_ref[...])
    run_side()
    xs = act[:, 0:SSD_WIDTH]
    bm = act[:, SSD_WIDTH:SSD_WIDTH + SSD_GROUPS * SSD_STATE].astype(BF16)
    cm = act[:, SSD_WIDTH + SSD_GROUPS * SSD_STATE:].astype(BF16)

    dt_full = _softplus(small + dtb_ref[...])
    a_full = dt_full * aneg_ref[...]
    acum = _dot_sel_lhs(cum128, a_full)
    acum_t = acum.T

    e_row = _iota((N_SMALL, SSD_WIDTH), 0)
    e_col = _iota((N_SMALL, SSD_WIDTH), 1)
    spread64 = jnp.where(e_row == DT_COL + e_col // SSD_HEADDIM, 1.0, 0.0).astype(BF16)
    e_row2 = _iota((N_SMALL, SSD_HEADS * LANES), 0)
    e_col2 = _iota((N_SMALL, SSD_HEADS * LANES), 1)
    spread128 = jnp.where(e_row2 == DT_COL + e_col2 // LANES, 1.0, 0.0).astype(BF16)
    dt_e = _dot_sel_rhs(dt_full, spread64, terms=1)
    ac_e = _dot_sel_rhs(acum, spread64, terms=2)
    ac_w = _dot_sel_rhs(acum, spread128, terms=2)

    tril128 = _iota((SSD_CHUNK, SSD_CHUNK), 1) <= _iota((SSD_CHUNK, SSD_CHUNK), 0)
    lane_g = _iota((SSD_CHUNK, SSD_HPG * SSD_HEADDIM), 1)
    def ssd_chunk(c):
        rs = slice(c * SSD_CHUNK, (c + 1) * SSD_CHUNK)
        ac_c = ac_e[rs]
        a_last = ac_c[SSD_CHUNK - 1:SSD_CHUNK]
        dt_c = dt_e[rs]
        xs_c = xs[rs]
        x_dt = (xs_c * dt_c).astype(BF16)
        x_w = (xs_c * (jnp.exp(a_last - ac_c) * dt_c)).astype(BF16)
        e_ac = jnp.exp(ac_c)
        for g in range(SSD_GROUPS):
            gs = slice(g * SSD_STATE, (g + 1) * SSD_STATE)
            ws = slice(g * SSD_HPG * SSD_HEADDIM, (g + 1) * SSD_HPG * SSD_HEADDIM)
            c_g = cm[rs, gs]
            b_g = bm[rs, gs]
            cb = _dot_nt(c_g, b_g)
            x_dt_g = x_dt[:, ws]
            lhs_parts = []
            rhs_parts = []
            for hh in range(SSD_HPG):
                h = g * SSD_HPG + hh
                seg = ac_w[rs, h * LANES:(h + 1) * LANES] - acum_t[DT_COL + h:DT_COL + h + 1, rs]
                lmat = jnp.where(tril128, jnp.exp(jnp.where(tril128, seg, 0.0)), 0.0)
                lhs_parts.append((cb * lmat).astype(BF16))
                rhs_parts.append(jnp.where((lane_g // SSD_HEADDIM) == hh, x_dt_g,
                                           jnp.zeros_like(x_dt_g)))
            intra = _dot(jnp.concatenate(lhs_parts, axis=1), jnp.concatenate(rhs_parts, axis=0))
            st = ssd_state[g]
            inter = _dot(c_g, st.astype(BF16)) * e_ac[:, ws]
            mix_scr[rs, GLA_WIDTH + g * 256:GLA_WIDTH + (g + 1) * 256] = intra + inter
            ssd_state[g] = st * jnp.exp(a_last[:, ws]) + _dot_tn(b_g, x_w[:, ws])
        run_side()

    gla_per_ssd = SSD_CHUNK // GLA_CHUNK
    for c in range(TL // SSD_CHUNK):
        for cc in range(gla_per_ssd):
            gla_chunk(c * gla_per_ssd + cc)
        ssd_chunk(c)
    run_side(len(side_work))

    o = mix_scr[:, 0:GLA_WIDTH]
    g_gate = _silu(pm_ref[:, 1024:1536].astype(F32))
    gla_parts = []
    for h in range(GLA_HEADS):
        o_h = o[:, h * GLA_DV:(h + 1) * GLA_DV]
        ms = jnp.mean(o_h * o_h, axis=-1, keepdims=True)
        gla_parts.append(o_h * lax.rsqrt(ms + GROUP_EPS))
    gla_out = jnp.concatenate(gla_parts, axis=1) * gnw_ref[...] * g_gate

    y = mix_scr[:, GLA_WIDTH:] + dexp_ref[...] * xs
    y = y * _silu(pm_ref[:, 1536:2048].astype(F32))
    ssd_parts = []
    for g in range(SSD_GROUPS):
        y_g = y[:, g * 256:(g + 1) * 256]
        ms = jnp.mean(y_g * y_g, axis=-1, keepdims=True)
        ssd_parts.append(y_g * lax.rsqrt(ms + GROUP_EPS))
    ssd_out = jnp.concatenate(ssd_parts, axis=1) * snw_ref[...]

    mixed = jnp.concatenate([gla_out, ssd_out], axis=1).astype(BF16)
    x1 = x_ref[...] + _dot(mixed, wout_ref[...])
    x1_ref[...] = x1

    ms = jnp.mean(x1 * x1, axis=-1, keepdims=True)
    h2 = x1 * lax.rsqrt(ms + EPS) * nfw_ref[...]
    h2_ref[...] = h2.astype(BF16)
    lg = _dot_hi_nt(rw_ref[...], h2) + rb_ref[...]
    lg_ref[...] = lg
    cnt_scr[...] = cnt_scr[...] + _tile_counts(_top4(lg[0:N_EXPERTS])[0])[1]
    cnt_ref[...] = cnt_scr[...]


def _mixer(x, nmw, w_all, wup, balpha, gnw, convw, convb, dtb, aneg, dexp, snw, wout, nfw, rw, rb):
    bsz, seqlen, _ = x.shape
    per_row = seqlen // TL
    steps = bsz * per_row

    def full(a):
        return pl.BlockSpec(a.shape, lambda g: (0,) * a.ndim)

    def tile(ahead):
        def index(g):
            tile_id = jnp.minimum(g + ahead, steps - 1)
            return (tile_id // per_row, tile_id % per_row, 0)
        return pl.BlockSpec((None, TL, D_MODEL), index)

    params = (nmw, w_all, wup, balpha, gnw, convw, convb, dtb, aneg, dexp, snw, wout, nfw, rw, rb)
    return pl.pallas_call(
        functools.partial(_mixer_kernel, tiles_per_row=per_row),
        grid=(steps,),
        in_specs=[tile(0), tile(1)] + [full(p) for p in params],
        out_specs=[tile(0), tile(0),
                   pl.BlockSpec((LANES, TL), lambda g: (0, g)),
                   pl.BlockSpec((N_EXPERTS, LANES), lambda g: (0, 0))],
        out_shape=[
            jax.ShapeDtypeStruct((bsz, seqlen, D_MODEL), F32),
            jax.ShapeDtypeStruct((bsz, seqlen, D_MODEL), BF16),
            jax.ShapeDtypeStruct((LANES, bsz * seqlen), F32),
            jax.ShapeDtypeStruct((N_EXPERTS, LANES), F32),
        ],
        scratch_shapes=[
            pltpu.VMEM((GLA_DV, GLA_KW), F32),
            pltpu.VMEM((SSD_GROUPS, SSD_STATE, SSD_HPG * SSD_HEADDIM), F32),
            pltpu.VMEM((8, SSD_CONV_CH), F32),
            pltpu.VMEM((TL, D_MODEL), F32),
            pltpu.VMEM((N_EXPERTS, LANES), F32),
            pltpu.VMEM((TL, N_MAIN), BF16),
            pltpu.VMEM((TL, N_MAIN), BF16),
            pltpu.VMEM((TL, N_SMALL), F32),
            pltpu.VMEM((TL, N_SMALL), F32),
        ],
        compiler_params=pltpu.CompilerParams(
            dimension_semantics=("arbitrary",), vmem_limit_bytes=VMEM_LIMIT),
        name="mixer",
    )(x, x, *params)


def _top4(lg):
    n_e, n_t = lg.shape
    row = _iota((n_e, n_t), 0)
    work = lg
    onehots = []
    vals = []
    for _ in range(TOP_K):
        m = jnp.max(work, axis=0, keepdims=True)
        idx = jnp.min(jnp.where(work == m, row, n_e), axis=0, keepdims=True)
        oh = row == idx
        onehots.append(oh)
        vals.append(m)
        work = jnp.where(oh, -jnp.inf, work)
    return onehots, vals


def _tile_counts(onehots):
    multi = jnp.where(onehots[0] | onehots[1] | onehots[2] | onehots[3], 1.0, 0.0).astype(BF16)
    return multi, _dot(multi, jnp.ones((multi.shape[1], LANES), BF16))


def _route_kernel(lg_ref, cnt_ref, info_ref, lp_ref, runs_ref, run_scr):
    @pl.when(pl.program_id(0) == 0)
    def _():
        run_scr[...] = jnp.zeros_like(run_scr)

    onehots, vals = _top4(lg_ref[0:N_EXPERTS, :])
    multi, tile_cnt = _tile_counts(onehots)
    counts = cnt_ref[...]
    padded = jnp.floor((counts + (TM - 1)) * (1.0 / TM)) * TM
    lower = jnp.where(_iota((N_EXPERTS, N_EXPERTS), 1) < _iota((N_EXPERTS, N_EXPERTS), 0),
                      1.0, 0.0).astype(BF16)
    offs = _dot_sel_lhs(lower, padded)
    loff = _dot_sel_lhs(lower, tile_cnt)
    before = jnp.where(_iota((TT, TT), 0) < _iota((TT, TT), 1), 1.0, 0.0).astype(BF16)
    rank = _dot(multi, before)
    lane = _iota((N_EXPERTS, LANES), 1)
    table = jnp.where(lane == RUN_START, offs + run_scr[...],
                      jnp.where(lane == RUN_LEN, tile_cnt, jnp.where(lane == RUN_LOFF, loff, 0.0)))
    table = jnp.concatenate([table, jnp.zeros((LANES - N_EXPERTS, LANES), F32)], axis=0)
    runs_ref[...] = table.T[0:8].astype(jnp.int32)
    run_scr[...] = run_scr[...] + tile_cnt
    local = rank + jnp.concatenate([loff] * (TT // LANES), axis=1)
    exps = [jnp.exp(v - vals[0]) for v in vals]
    den = exps[0] + exps[1] + exps[2] + exps[3]
    lp_rows = [jnp.sum(jnp.where(oh, local, 0.0), axis=0, keepdims=True) for oh in onehots]
    lp_ref[...] = jnp.concatenate(lp_rows + [jnp.zeros((8 - TOP_K, TT), F32)], axis=0).astype(jnp.int32)
    gate_rows = [e / den for e in exps]
    info = jnp.concatenate(gate_rows + lp_rows + [jnp.zeros((LANES - 2 * TOP_K, TT), F32)], axis=0)
    info_ref[...] = info.T


def _route(logits_t, counts):
    t = logits_t.shape[1]
    steps = t // TT
    return pl.pallas_call(
        _route_kernel,
        grid=(steps,),
        in_specs=[pl.BlockSpec((LANES, TT), lambda i: (0, i)),
                  pl.BlockSpec((N_EXPERTS, LANES), lambda i: (0, 0))],
        out_specs=[
            pl.BlockSpec((TT, LANES), lambda i: (i, 0)),
            pl.BlockSpec((8, TT), lambda i: (0, i)),
            pl.BlockSpec((8, LANES), lambda i: (i, 0)),
        ],
        out_shape=[
            jax.ShapeDtypeStruct((t, LANES), F32),
            jax.ShapeDtypeStruct((8, t), jnp.int32),
            jax.ShapeDtypeStruct((steps * 8, LANES), jnp.int32),
        ],
        scratch_shapes=[pltpu.VMEM((N_EXPERTS, LANES), F32)],
        compiler_params=pltpu.CompilerParams(
            dimension_semantics=("arbitrary",), vmem_limit_bytes=VMEM_LIMIT),
        name="route",
    )(logits_t, counts)


def _rows(first, count):
    return pl.ds(pl.multiple_of(first * ROW_TILE, ROW_TILE), count * ROW_TILE)


def _wait_rows(stage_slot, hbm_ref, sem, to_hbm):
    for w in range(STAGE_ROWS // WAIT_ROWS):
        part = stage_slot.at[pl.ds(w * WAIT_ROWS * ROW_TILE, WAIT_ROWS * ROW_TILE), :]
        hbm = hbm_ref.at[pl.ds(0, WAIT_ROWS * ROW_TILE), :]
        src, dst = (part, hbm) if to_hbm else (hbm, part)
        pltpu.make_async_copy(src, dst, sem).wait()


def _for_each_piece(length, max_log2, fn):
    for b in reversed(range(max_log2 + 1)):
        size = 1 << b
        offset = lax.shift_left(lax.shift_right_logical(length, b + 1), b + 1)

        @pl.when((length & size) != 0)
        def _(offset=offset, size=size):
            fn(offset, size)


def _plan_kernel(cnt_ref, plan_ref, *, n_tiles, p_rows):
    def clear(i, c):
        for r in range(8):
            plan_ref[r, i] = 0
        return c
    lax.fori_loop(0, PLAN_COLS, clear, 0)

    def per_expert(e, carry):
        first_row, group = carry
        count = cnt_ref[e, 0].astype(jnp.int32)
        tiles = lax.shift_right_logical(count + (TM - 1), TM_LOG2)
        first_tile = lax.shift_right_logical(first_row, TM_LOG2)

        def mark(i, c):
            plan_ref[PLAN_EXPERT, i] = e
            plan_ref[PLAN_PARITY, i] = group & 1
            return c
        lax.fori_loop(first_tile, first_tile + tiles, mark, 0)
        plan_ref[PLAN_PAD_START, e] = first_row + count
        plan_ref[PLAN_PAD_LEN, e] = tiles * TM - count
        return first_row + tiles * TM, group + jnp.minimum(tiles, 1)

    used_rows, _ = lax.fori_loop(0, N_EXPERTS, per_expert, (jnp.int32(0), jnp.int32(0)))
    n_valid = lax.shift_right_logical(used_rows, TM_LOG2)
    plan_ref[PLAN_NVALID, 0] = n_valid
    plan_ref[PLAN_PAD_START, N_EXPERTS] = used_rows
    plan_ref[PLAN_PAD_LEN, N_EXPERTS] = lax.shift_right_logical(p_rows - used_rows, TM_LOG2 - 1)

    last_expert = plan_ref[PLAN_EXPERT, n_valid - 1]
    last_parity = plan_ref[PLAN_PARITY, n_valid - 1]

    def mark_unused(i, c):
        plan_ref[PLAN_EXPERT, i] = last_expert
        plan_ref[PLAN_PARITY, i] = last_parity
        return c
    lax.fori_loop(n_valid, n_tiles, mark_unused, 0)

    def next_expert(k, carry):
        expert_after, next_after = carry
        i = n_tiles - 1 - k
        e = plan_ref[PLAN_EXPERT, i]
        nxt = jnp.where(expert_after != e, expert_after, next_after)
        plan_ref[PLAN_NEXT, i] = nxt
        return e, nxt
    lax.fori_loop(0, n_tiles, next_expert, (last_expert, last_expert))


def _plan(counts, n_tiles, p_rows):
    assert n_tiles <= PLAN_COLS and N_EXPERTS < PLAN_COLS
    return pl.pallas_call(
        functools.partial(_plan_kernel, n_tiles=n_tiles, p_rows=p_rows),
        in_specs=[pl.BlockSpec(memory_space=pltpu.SMEM)],
        out_specs=pl.BlockSpec(memory_space=pltpu.SMEM),
        out_shape=jax.ShapeDtypeStruct((8, PLAN_COLS), jnp.int32),
        name="plan",
    )(counts)


def _dispatch_kernel(runs_ref, plan_ref, lp_ref, h_ref, xs_ref, stage, zeros, sem, zsem):
    j = pl.program_id(0)
    slot = j & 1

    def wait_tile(s):
        _wait_rows(stage.at[s], xs_ref, sem.at[s], to_hbm=True)

    @pl.when(j == 0)
    def _():
        zeros[...] = jnp.zeros_like(zeros)

        def pad_pieces(e, wait):
            def piece(offset, size):
                cp = pltpu.make_async_copy(
                    zeros.at[pl.ds(0, size * ROW_TILE), :],
                    xs_ref.at[_rows(plan_ref[PLAN_PAD_START, e] + offset, size), :], zsem.at[e % 2])
                cp.wait() if wait else cp.start()
            _for_each_piece(plan_ref[PLAN_PAD_LEN, e], TM_LOG2 - 1, piece)

        for e in range(N_EXPERTS):
            pad_pieces(e, wait=False)
            if e >= 1:
                pad_pieces(e - 1, wait=True)
        pad_pieces(N_EXPERTS - 1, wait=True)

        n_blocks = plan_ref[PLAN_PAD_LEN, N_EXPERTS]

        def block_copy(blk):
            first = plan_ref[PLAN_PAD_START, N_EXPERTS] + blk * (TM // 2)
            return pltpu.make_async_copy(zeros, xs_ref.at[_rows(first, TM // 2), :], zsem.at[blk & 1])

        def put_block(blk, carry):
            block_copy(blk).start()

            @pl.when(blk >= 2)
            def _():
                block_copy(blk - 2).wait()
            return carry
        lax.fori_loop(0, n_blocks, put_block, 0)
        for back in (2, 1):
            @pl.when(n_blocks >= back)
            def _(back=back):
                block_copy(n_blocks - back).wait()

    @pl.when(j >= 2)
    def _():
        wait_tile(slot)

    h = h_ref[...]
    lp = lp_ref[...]
    for c in range(STAGE_ROWS // TT):
        r = _iota((TT, TT), 0) + c * TT
        hit = (r == lp[0:1]) | (r == lp[1:2]) | (r == lp[2:3]) | (r == lp[3:4])
        rows = _dot(jnp.where(hit, 1.0, 0.0).astype(BF16), h)
        for q in range(ROW_TILE):
            stage[slot, pl.ds(c * TT * ROW_TILE + q, TT, stride=ROW_TILE), :] = rows[:, q * LANES:(q + 1) * LANES]

    for e in range(N_EXPERTS):
        src = runs_ref[j * 8 + RUN_LOFF, e]
        dst = runs_ref[j * 8 + RUN_START, e]

        def put(offset, size, src=src, dst=dst):
            pltpu.make_async_copy(stage.at[slot, _rows(src + offset, size), :],
                                  xs_ref.at[_rows(dst + offset, size), :], sem.at[slot]).start()
        _for_each_piece(runs_ref[j * 8 + RUN_LEN, e], TT_LOG2, put)

    @pl.when(j == pl.num_programs(0) - 1)
    def _():
        wait_tile(1 - slot)
        wait_tile(slot)


def _dispatch(runs, plan, lp, h2, p_rows):
    t = h2.shape[0]
    return pl.pallas_call(
        _dispatch_kernel,
        grid_spec=pltpu.PrefetchScalarGridSpec(
            num_scalar_prefetch=2,
            grid=(t // TT,),
            in_specs=[
                pl.BlockSpec((8, TT), lambda j, *_: (0, j)),
                pl.BlockSpec((TT, D_MODEL), lambda j, *_: (j, 0)),
            ],
            out_specs=pl.BlockSpec(memory_space=pl.ANY),
            scratch_shapes=[
                pltpu.VMEM((2, STAGE_ROWS * ROW_TILE, LANES), F32),
                pltpu.VMEM((TM // 2 * ROW_TILE, LANES), F32),
                pltpu.SemaphoreType.DMA((2,)),
                pltpu.SemaphoreType.DMA((2,)),
            ],
        ),
        out_shape=jax.ShapeDtypeStruct((p_rows * ROW_TILE, LANES), F32),
        compiler_params=pltpu.CompilerParams(
            dimension_semantics=("arbitrary",), vmem_limit_bytes=VMEM_LIMIT),
        name="dispatch",
    )(runs, plan, lp, h2)


def _experts_kernel(plan_ref, x_ref, wg_ref, bg_ref, wu_ref, bu_ref, wd_ref, bd_ref,
                    o_ref, act, wbuf, wg_b, wu_b, wd_b, wsem):
    i = pl.program_id(0)
    prev = jnp.maximum(i - 1, 0)

    def fetch(expert, s):
        return [pltpu.make_async_copy(w_ref.at[expert], wbuf.at[s, m], wsem.at[s])
                for m, w_ref in enumerate((wg_ref, wu_ref, wd_ref))]

    n_valid = plan_ref[PLAN_NVALID, 0]

    @pl.when(i >= n_valid)
    def _():
        o_ref[...] = jnp.zeros_like(o_ref)

    @pl.when(i < n_valid)
    def _():
        expert = plan_ref[PLAN_EXPERT, i]
        s = plan_ref[PLAN_PARITY, i]
        nxt = plan_ref[PLAN_NEXT, i]

        @pl.when(i == 0)
        def _():
            for cp in fetch(expert, s):
                cp.start()

        @pl.when((i == 0) | (expert != plan_ref[PLAN_EXPERT, prev]))
        def _():
            for cp in fetch(expert, s):
                cp.wait()
            for c in range(N_CHUNKS):
                cs = slice(c * N_COLS, (c + 1) * N_COLS)
                wg_b[c] = wbuf[s, 0, :, cs].astype(BF16)
                wu_b[c] = wbuf[s, 1, :, cs].astype(BF16)
                wd_b[c] = wbuf[s, 2, :, cs].astype(BF16)

            @pl.when(nxt != expert)
            def _():
                for cp in fetch(nxt, 1 - s):
                    cp.start()

        xb = _load_row_tiles(x_ref, TM).astype(BF16)
        for n in range(N_CHUNKS):
            gate = _dot(xb, wg_b[n]) + bg_ref[n:n + 1, :]
            up = _dot(xb, wu_b[n]) + bu_ref[n:n + 1, :]
            gate = jnp.minimum(gate, SWIGLU_LIMIT)
            up = jnp.clip(up, -SWIGLU_LIMIT, SWIGLU_LIMIT)
            act[n] = ((up + 1.0) * (gate * jax.nn.sigmoid(SWIGLU_ALPHA * gate))).astype(BF16)
        a = jnp.concatenate([act[c] for c in range(N_CHUNKS)], axis=1)
        for n in range(N_CHUNKS):
            out = _dot(a, wd_b[n]) + bd_ref[n:n + 1, :]
            for q in range(N_COLS // LANES):
                o_ref[pl.ds(n * (N_COLS // LANES) + q, TM, stride=ROW_TILE), :] = (
                    out[:, q * LANES:(q + 1) * LANES])


def _experts(plan, xs, wg, bg, wu, bu, wd, bd):
    n_tiles = xs.shape[0] // (TM * ROW_TILE)

    def x_map(i, plan_ref):
        return (jnp.minimum(i, plan_ref[PLAN_NVALID, 0] - 1), 0)

    def b_map(i, plan_ref):
        return (plan_ref[PLAN_EXPERT, i], 0, 0)

    w_spec = pl.BlockSpec(memory_space=pl.ANY)
    b_spec = pl.BlockSpec((None, N_CHUNKS, N_COLS), b_map)
    return pl.pallas_call(
        _experts_kernel,
        grid_spec=pltpu.PrefetchScalarGridSpec(
            num_scalar_prefetch=1,
            grid=(n_tiles,),
            in_specs=[pl.BlockSpec((TM * ROW_TILE, LANES), x_map),
                      w_spec, b_spec, w_spec, b_spec, w_spec, b_spec],
            out_specs=pl.BlockSpec((TM * ROW_TILE, LANES), lambda i, plan_ref: (i, 0)),
            scratch_shapes=[
                pltpu.VMEM((N_CHUNKS, TM, N_COLS), BF16),
                pltpu.VMEM((2, 3, D_MODEL, D_MODEL), F32),
                pltpu.VMEM((N_CHUNKS, D_MODEL, N_COLS), BF16),
                pltpu.VMEM((N_CHUNKS, D_MODEL, N_COLS), BF16),
                pltpu.VMEM((N_CHUNKS, D_MODEL, N_COLS), BF16),
                pltpu.SemaphoreType.DMA((2,)),
            ],
        ),
        out_shape=jax.ShapeDtypeStruct(xs.shape, F32),
        compiler_params=pltpu.CompilerParams(
            dimension_semantics=("arbitrary",), vmem_limit_bytes=VMEM_LIMIT),
        name="experts",
    )(plan, xs, wg, bg, wu, bu, wd, bd)


def _combine_kernel(runs_ref, info_ref, x1_ref, fw_ref, eo_ref, out_ref, stage, sem):
    j = pl.program_id(0)
    slot = lax.rem(j, 3)
    last = pl.num_programs(0) - 1

    def fetch(tile, s):
        for e in range(N_EXPERTS):
            src = runs_ref[tile * 8 + RUN_START, e]
            dst = runs_ref[tile * 8 + RUN_LOFF, e]

            def get(offset, size, src=src, dst=dst):
                pltpu.make_async_copy(eo_ref.at[_rows(src + offset, size), :],
                                      stage.at[s, _rows(dst + offset, size), :], sem.at[s]).start()
            _for_each_piece(runs_ref[tile * 8 + RUN_LEN, e], TT_LOG2, get)

    @pl.when(j == 0)
    def _():
        fetch(0, 0)
        fetch(jnp.minimum(1, last), 1)

    _wait_rows(stage.at[slot], eo_ref, sem.at[slot], to_hbm=False)

    info = info_ref[...]
    y = x1_ref[...]
    for c in range(STAGE_ROWS // TT):
        r = (_iota((TT, TT), 1) + c * TT).astype(F32)
        g = jnp.zeros((TT, TT), F32)
        for k in range(TOP_K):
            g = g + jnp.where(r == info[:, TOP_K + k:TOP_K + k + 1], info[:, k:k + 1], 0.0)
        g_hi = g.astype(BF16)
        g_lo = (g - g_hi.astype(F32)).astype(BF16)
        rows = jnp.concatenate(
            [stage[slot, pl.ds(c * TT * ROW_TILE + q, TT, stride=ROW_TILE), :] for q in range(ROW_TILE)],
            axis=1).astype(BF16)
        y = y + _dot(g_hi, rows) + _dot(g_lo, rows)
    ms = jnp.mean(y * y, axis=-1, keepdims=True)
    out_ref[...] = y * lax.rsqrt(ms + EPS) * fw_ref[...]

    ahead = lax.rem(j + 2, 3)
    fetch(jnp.minimum(j + 2, last), ahead)

    @pl.when(j == last)
    def _():
        _wait_rows(stage.at[lax.rem(j + 1, 3)], eo_ref, sem.at[lax.rem(j + 1, 3)], to_hbm=False)
        _wait_rows(stage.at[ahead], eo_ref, sem.at[ahead], to_hbm=False)


def _combine(runs, info, x1, final_w, eo):
    t = x1.shape[0]
    return pl.pallas_call(
        _combine_kernel,
        grid_spec=pltpu.PrefetchScalarGridSpec(
            num_scalar_prefetch=1,
            grid=(t // TT,),
            in_specs=[
                pl.BlockSpec((TT, LANES), lambda j, *_: (j, 0)),
                pl.BlockSpec((TT, D_MODEL), lambda j, *_: (j, 0)),
                pl.BlockSpec((1, D_MODEL), lambda j, *_: (0, 0)),
                pl.BlockSpec(memory_space=pl.ANY),
            ],
            out_specs=pl.BlockSpec((TT, D_MODEL), lambda j, *_: (j, 0)),
            scratch_shapes=[
                pltpu.VMEM((3, STAGE_ROWS * ROW_TILE, LANES), F32),
                pltpu.SemaphoreType.DMA((3,)),
            ],
        ),
        out_shape=jax.ShapeDtypeStruct((t, D_MODEL), F32),
        compiler_params=pltpu.CompilerParams(
            dimension_semantics=("arbitrary",), vmem_limit_bytes=VMEM_LIMIT),
        name="combine",
    )(runs, info, x1, final_w, eo)


def _pad_lanes(v, offset, fill=0.0):
    row = jnp.full((1, LANES), fill, F32)
    return row.at[0, offset:offset + v.shape[0]].set(v.astype(F32))


def kernel(x, norm_mix_w, w_in, gla_w_alpha_up, gla_b_alpha, gla_norm_w, ssd_conv_w, ssd_conv_b,
           ssd_dt_bias, ssd_A_log, ssd_D, ssd_norm_w, w_out, norm_ffn_w, router_w, router_b,
           moe_w_gate, moe_b_gate, moe_w_up, moe_b_up, moe_w_down, moe_b_down, final_norm_w):
    bsz, seqlen, d = x.shape
    t = bsz * seqlen
    depth = w_in.shape[0]
    assert depth == 1, "the final RMSNorm is fused into the (single) layer's combine step"
    p_rows = t * TOP_K + N_EXPERTS * TM
    n_tiles = p_rows // TM
    for l in range(depth):
        w = w_in[l]
        w_all = jnp.concatenate(
            [w[:, 0:1536], w[:, 1552:3088], w[:, 1536:1552], w[:, 3088:3096],
             jnp.zeros((d, N_SMALL - GLA_GATE_RANK - SSD_HEADS), w.dtype)], axis=1).astype(BF16)
        wup = jnp.zeros((N_SMALL, GLA_KW), F32).at[0:GLA_GATE_RANK].set(gla_w_alpha_up[l])
        dtb = _pad_lanes(ssd_dt_bias[l], DT_COL)
        aneg = _pad_lanes(-jnp.exp(ssd_A_log[l].astype(F32)), DT_COL)
        dexp = jnp.repeat(ssd_D[l].astype(F32), SSD_HEADDIM)[None, :]
        rw = jnp.zeros((LANES, d), F32).at[0:N_EXPERTS].set(router_w[l].T)
        rb = jnp.zeros((LANES, TL), F32).at[0:N_EXPERTS].set(
            jnp.broadcast_to(router_b[l].astype(F32)[:, None], (N_EXPERTS, TL)))

        x1, h2, logits, counts = _mixer(
            x, norm_mix_w[l][None, :], w_all, wup, gla_b_alpha[l][None, :], gla_norm_w[l][None, :], ssd_conv_w[l],
            ssd_conv_b[l][None, :], dtb, aneg, dexp, ssd_norm_w[l][None, :],
            w_out[l].astype(BF16), norm_ffn_w[l][None, :], rw, rb)

        info, lp, runs = _route(logits, counts)
        plan = _plan(counts, n_tiles, p_rows)
        xs = _dispatch(runs, plan, lp, h2.reshape(t, d), p_rows)
        eo = _experts(plan, xs,
                      moe_w_gate[l], moe_b_gate[l].reshape(N_EXPERTS, N_CHUNKS, N_COLS),
                      moe_w_up[l], moe_b_up[l].reshape(N_EXPERTS, N_CHUNKS, N_COLS),
                      moe_w_down[l], moe_b_down[l].reshape(N_EXPERTS, N_CHUNKS, N_COLS))
        x = _combine(runs, info, x1.reshape(t, d), final_norm_w[None, :], eo).reshape(bsz, seqlen, d)
    return x
```

```python
import functools

import jax
import jax.numpy as jnp
from jax import lax
from jax.experimental import pallas as pl
from jax.experimental.pallas import tpu as pltpu

F32 = jnp.float32
BF16 = jnp.bfloat16

D_MODEL = 1024
GLA_WIDTH = 512
GLA_HEADS = 4
GLA_DV = 128
GLA_DK = 64
GLA_KW = 256
GLA_GATE_RANK = 16
GLA_GATE_NORM = 16.0
SSD_WIDTH = 512
SSD_HEADDIM = 64
SSD_HEADS = 8
SSD_GROUPS = 2
SSD_HPG = 4
SSD_STATE = 128
SSD_CONV = 4
SSD_CONV_CH = 1024
N_EXPERTS = 32
TOP_K = 4
SWIGLU_LIMIT = 7.0
SWIGLU_ALPHA = 1.702
EPS = 1e-6
GROUP_EPS = 1e-5

LANES = 128
ROW_TILE = D_MODEL // LANES
N_MAIN = 3072
N_SMALL = LANES
DT_COL = GLA_GATE_RANK

GLA_CHUNK = 64
SSD_CHUNK = 128
PROJ_COLS = 512
TL = 256
TT_LOG2 = 8
TT = 1 << TT_LOG2
STAGE_ROWS = 4 * TT
TM_LOG2 = 9
TM = 1 << TM_LOG2
WAIT_ROWS = 512
ZERO_DEPTH = 4
RUN_START, RUN_LEN, RUN_LOFF = 0, 1, 2
PLAN_EXPERT, PLAN_NEXT, PLAN_PARITY, PLAN_NVALID, PLAN_PAD_START, PLAN_PAD_LEN = 0, 1, 2, 3, 4, 5
PLAN_COLS = 512
assert STAGE_ROWS * D_MODEL * 4 <= (1 << 17) * 32
N_COLS = 256
N_CHUNKS = D_MODEL // N_COLS
VMEM_LIMIT = 56 * 1024 * 1024


def _dot(a, b):
    return jnp.dot(a, b, preferred_element_type=F32)


def _dot_nt(a, b):
    return lax.dot_general(a, b, (((1,), (1,)), ((), ())), preferred_element_type=F32)


def _dot_tn(a, b):
    return lax.dot_general(a, b, (((0,), (0,)), ((), ())), preferred_element_type=F32)


def _split3(a):
    hi = a.astype(BF16)
    r1 = a - hi.astype(F32)
    mid = r1.astype(BF16)
    lo = (r1 - mid.astype(F32)).astype(BF16)
    return hi, mid, lo


def _dot_sel_lhs(sel, a):
    hi, mid, lo = _split3(a)
    return _dot(sel, hi) + _dot(sel, mid) + _dot(sel, lo)


def _dot_sel_rhs(a, sel, terms=3):
    parts = _split3(a)[:terms]
    out = _dot(parts[0], sel)
    for p in parts[1:]:
        out = out + _dot(p, sel)
    return out


def _dot_hi(a, b):
    a_hi = a.astype(BF16)
    a_lo = (a - a_hi.astype(F32)).astype(BF16)
    b_hi = b.astype(BF16)
    b_lo = (b - b_hi.astype(F32)).astype(BF16)
    return _dot(a_hi, b_hi) + _dot(a_lo, b_hi) + _dot(a_hi, b_lo)


def _dot_hi_nt(a, b):
    a_hi = a.astype(BF16)
    a_lo = (a - a_hi.astype(F32)).astype(BF16)
    b_hi = b.astype(BF16)
    b_lo = (b - b_hi.astype(F32)).astype(BF16)
    return _dot_nt(a_hi, b_hi) + _dot_nt(a_lo, b_hi) + _dot_nt(a_hi, b_lo)


def _softplus(x):
    return jnp.maximum(x, 0.0) + jnp.log1p(jnp.exp(-jnp.abs(x)))


def _silu(x):
    return x * jax.nn.sigmoid(x)


def _iota(shape, dim):
    return lax.broadcasted_iota(jnp.int32, shape, dim)


def _load_row_tiles(ref, rows):
    return jnp.concatenate([ref[pl.ds(j, rows, stride=ROW_TILE), :] for j in range(ROW_TILE)], axis=1)


def _project_parts(x_ref, nmw_ref, win_ref, pm_ref, small_ref):
    state = {}

    def norm():
        x_in = x_ref[...]
        ms = jnp.mean(x_in * x_in, axis=-1, keepdims=True)
        state["h"] = (x_in * lax.rsqrt(ms + EPS) * nmw_ref[...]).astype(BF16)

    def chunk(n0):
        def run():
            pm_ref[:, n0:n0 + PROJ_COLS] = _dot(state["h"], win_ref[:, n0:n0 + PROJ_COLS]).astype(BF16)
        return run

    def small():
        small_ref[...] = _dot(state["h"], win_ref[:, N_MAIN:N_MAIN + N_SMALL])

    return [norm] + [chunk(n0) for n0 in range(0, N_MAIN, PROJ_COLS)] + [small]


def _mixer_kernel(x_ref, xn_ref, nmw_ref, win_ref, *refs, tiles_per_row):
    params, outs = refs[:13], refs[13:17]
    gla_state, ssd_state, conv_tail, mix_scr, cnt_scr, pm_a, pm_b, small_a, small_b = refs[17:]
    g = pl.program_id(0)

    @pl.when(g == 0)
    def _():
        cnt_scr[...] = jnp.zeros_like(cnt_scr)
        for part in _project_parts(x_ref, nmw_ref, win_ref, pm_a, small_a):
            part()

    @pl.when(lax.rem(g, tiles_per_row) == 0)
    def _():
        gla_state[...] = jnp.zeros_like(gla_state)
        ssd_state[...] = jnp.zeros_like(ssd_state)
        conv_tail[...] = jnp.zeros_like(conv_tail)

    for parity, (pm_cur, small_cur, pm_nxt, small_nxt) in enumerate(
            ((pm_a, small_a, pm_b, small_b), (pm_b, small_b, pm_a, small_a))):
        @pl.when((g & 1) == parity)
        def _(pm_cur=pm_cur, small_cur=small_cur, pm_nxt=pm_nxt, small_nxt=small_nxt):
            _mixer_tile(_project_parts(xn_ref, nmw_ref, win_ref, pm_nxt, small_nxt),
                        pm_cur, small_cur, x_ref, *params, *outs,
                        gla_state, ssd_state, conv_tail, mix_scr, cnt_scr)


def _mixer_tile(side_work, pm_ref, small_ref, x_ref, wup_ref, balpha_ref, gnw_ref, convw_ref, convb_ref,
                dtb_ref, aneg_ref, dexp_ref, snw_ref, wout_ref, nfw_ref, rw_ref, rb_ref,
                x1_ref, h2_ref, lg_ref, cnt_ref,
                gla_state, ssd_state, conv_tail, mix_scr, cnt_scr):
    side_work = list(side_work)

    def run_side(n=1):
        for _ in range(min(n, len(side_work))):
            side_work.pop(0)()

    small = small_ref[...]

    row = _iota((TL, TL), 0)
    col = _iota((TL, TL), 1)
    causal = col <= row
    cum64 = jnp.where(causal & ((row // GLA_CHUNK) == (col // GLA_CHUNK)), 1.0, 0.0).astype(BF16)
    cum128 = jnp.where(causal & ((row // SSD_CHUNK) == (col // SSD_CHUNK)), 1.0, 0.0).astype(BF16)

    xa = _dot_hi(small, wup_ref[...]) + balpha_ref[...]
    log_a = (jnp.minimum(xa, 0.0) - jnp.log1p(jnp.exp(-jnp.abs(xa)))) * (1.0 / GLA_GATE_NORM)
    bcum = _dot_sel_lhs(cum64, log_a)

    lane_kw = _iota((GLA_CHUNK, GLA_KW), 1)
    head_masks = [(lane_kw // GLA_DK) == h for h in range(GLA_HEADS)]
    lane_kw_s = _iota((GLA_DV, GLA_KW), 1)
    head_masks_s = [(lane_kw_s // GLA_DK) == h for h in range(GLA_HEADS)]
    tril64 = _iota((GLA_CHUNK, GLA_CHUNK), 1) <= _iota((GLA_CHUNK, GLA_CHUNK), 0)
    q_scale = GLA_DK ** -0.5

    def gla_chunk(c):
        rs = slice(c * GLA_CHUNK, (c + 1) * GLA_CHUNK)
        bc = bcum[rs]
        b_mid = bc[GLA_CHUNK // 2:GLA_CHUNK // 2 + 1]
        b_last = bc[GLA_CHUNK - 1:GLA_CHUNK]
        qc = pm_ref[rs, 0:GLA_KW].astype(F32) * q_scale
        kc = pm_ref[rs, GLA_KW:2 * GLA_KW].astype(F32)
        vc = pm_ref[rs, 2 * GLA_KW:2 * GLA_KW + GLA_WIDTH]
        q_in = (qc * jnp.exp(bc - b_mid)).astype(BF16)
        k_in = (kc * jnp.exp(b_mid - bc)).astype(BF16)
        q_st = (qc * jnp.exp(bc)).astype(BF16)
        k_st = (kc * jnp.exp(b_last - bc)).astype(BF16)
        st = gla_state[...]
        st_b = st.astype(BF16)
        zero_b = jnp.zeros_like(q_in)
        for h in range(GLA_HEADS):
            scores = _dot_nt(jnp.where(head_masks[h], q_in, zero_b), k_in)
            scores = jnp.where(tril64, scores, 0.0).astype(BF16)
            o_h = _dot(scores, vc[:, h * GLA_DV:(h + 1) * GLA_DV])
            o_h = o_h + _dot_nt(jnp.where(head_masks[h], q_st, zero_b), st_b)
            mix_scr[rs, h * GLA_DV:(h + 1) * GLA_DV] = o_h
        upd = _dot_tn(vc, k_st)
        new_st = st * jnp.exp(b_last)
        for h in range(GLA_HEADS):
            new_st = new_st + jnp.where(head_masks_s[h], upd[h * GLA_DV:(h + 1) * GLA_DV], 0.0)
        gla_state[...] = new_st
        run_side()

    xbc = pm_ref[:, 2048:3072].astype(F32)
    tail = conv_tail[...]
    conv_tail[...] = xbc[TL - 8:TL]
    row8 = _iota((8, SSD_CONV_CH), 0)
    conv = xbc * convw_ref[SSD_CONV - 1:SSD_CONV, :]
    for s in range(1, SSD_CONV):
        shifted = pltpu.roll(xbc, s, 0)
        head = jnp.where(row8 < s, pltpu.roll(tail, s, 0), shifted[0:8])
        shifted = jnp.concatenate([head, shifted[8:]], axis=0)
        conv = conv + shifted * convw_ref[SSD_CONV - 1 - s:SSD_CONV - s, :]
    act = _silu(conv + convb_ref[...])
    run_side()
    xs = act[:, 0:SSD_WIDTH]
    bm = act[:, SSD_WIDTH:SSD_WIDTH + SSD_GROUPS * SSD_STATE].astype(BF16)
    cm = act[:, SSD_WIDTH + SSD_GROUPS * SSD_STATE:].astype(BF16)

    dt_full = _softplus(small + dtb_ref[...])
    a_full = dt_full * aneg_ref[...]
    acum = _dot_sel_lhs(cum128, a_full)
    acum_t = acum.T

    e_row = _iota((N_SMALL, SSD_WIDTH), 0)
    e_col = _iota((N_SMALL, SSD_WIDTH), 1)
    spread64 = jnp.where(e_row == DT_COL + e_col // SSD_HEADDIM, 1.0, 0.0).astype(BF16)
    e_row2 = _iota((N_SMALL, SSD_HEADS * LANES), 0)
    e_col2 = _iota((N_SMALL, SSD_HEADS * LANES), 1)
    spread128 = jnp.where(e_row2 == DT_COL + e_col2 // LANES, 1.0, 0.0).astype(BF16)
    dt_e = _dot_sel_rhs(dt_full, spread64, terms=1)
    ac_e = _dot_sel_rhs(acum, spread64, terms=2)
    ac_w = _dot_sel_rhs(acum, spread128, terms=2)

    tril128 = _iota((SSD_CHUNK, SSD_CHUNK), 1) <= _iota((SSD_CHUNK, SSD_CHUNK), 0)
    lane_g = _iota((SSD_CHUNK, SSD_HPG * SSD_HEADDIM), 1)
    def ssd_chunk(c):
        rs = slice(c * SSD_CHUNK, (c + 1) * SSD_CHUNK)
        ac_c = ac_e[rs]
        a_last = ac_c[SSD_CHUNK - 1:SSD_CHUNK]
        dt_c = dt_e[rs]
        xs_c = xs[rs]
        x_dt = (xs_c * dt_c).astype(BF16)
        x_w = (xs_c * (jnp.exp(a_last - ac_c) * dt_c)).astype(BF16)
        e_ac = jnp.exp(ac_c)
        for g in range(SSD_GROUPS):
            gs = slice(g * SSD_STATE, (g + 1) * SSD_STATE)
            ws = slice(g * SSD_HPG * SSD_HEADDIM, (g + 1) * SSD_HPG * SSD_HEADDIM)
            c_g = cm[rs, gs]
            b_g = bm[rs, gs]
            cb = _dot_nt(c_g, b_g)
            x_dt_g = x_dt[:, ws]
            lhs_parts = []
            rhs_parts = []
            for hh in range(SSD_HPG):
                h = g * SSD_HPG + hh
                seg = ac_w[rs, h * LANES:(h + 1) * LANES] - acum_t[DT_COL + h:DT_COL + h + 1, rs]
                lmat = jnp.where(tril128, jnp.exp(jnp.where(tril128, seg, 0.0)), 0.0)
                lhs_parts.append((cb * lmat).astype(BF16))
                rhs_parts.append(jnp.where((lane_g // SSD_HEADDIM) == hh, x_dt_g,
                                           jnp.zeros_like(x_dt_g)))
            intra = _dot(jnp.concatenate(lhs_parts, axis=1), jnp.concatenate(rhs_parts, axis=0))
            st = ssd_state[g]
            inter = _dot(c_g, st.astype(BF16)) * e_ac[:, ws]
            mix_scr[rs, GLA_WIDTH + g * 256:GLA_WIDTH + (g + 1) * 256] = intra + inter
            ssd_state[g] = st * jnp.exp(a_last[:, ws]) + _dot_tn(b_g, x_w[:, ws])
        run_side()

    gla_per_ssd = SSD_CHUNK // GLA_CHUNK
    for c in range(TL // SSD_CHUNK):
        for cc in range(gla_per_ssd):
            gla_chunk(c * gla_per_ssd + cc)
        ssd_chunk(c)
    run_side(len(side_work))

    o = mix_scr[:, 0:GLA_WIDTH]
    g_gate = _silu(pm_ref[:, 1024:1536].astype(F32))
    gla_parts = []
    for h in range(GLA_HEADS):
        o_h = o[:, h * GLA_DV:(h + 1) * GLA_DV]
        ms = jnp.mean(o_h * o_h, axis=-1, keepdims=True)
        gla_parts.append(o_h * lax.rsqrt(ms + GROUP_EPS))
    gla_out = jnp.concatenate(gla_parts, axis=1) * gnw_ref[...] * g_gate

    y = mix_scr[:, GLA_WIDTH:] + dexp_ref[...] * xs
    y = y * _silu(pm_ref[:, 1536:2048].astype(F32))
    ssd_parts = []
    for g in range(SSD_GROUPS):
        y_g = y[:, g * 256:(g + 1) * 256]
        ms = jnp.mean(y_g * y_g, axis=-1, keepdims=True)
        ssd_parts.append(y_g * lax.rsqrt(ms + GROUP_EPS))
    ssd_out = jnp.concatenate(ssd_parts, axis=1) * snw_ref[...]

    mixed = jnp.concatenate([gla_out, ssd_out], axis=1).astype(BF16)
    x1 = x_ref[...] + _dot(mixed, wout_ref[...])
    x1_ref[...] = x1

    ms = jnp.mean(x1 * x1, axis=-1, keepdims=True)
    h2 = x1 * lax.rsqrt(ms + EPS) * nfw_ref[...]
    h2_ref[...] = h2.astype(BF16)
    lg = _dot_hi_nt(rw_ref[...], h2) + rb_ref[...]
    lg_ref[...] = lg
    cnt_scr[...] = cnt_scr[...] + _tile_counts(_top4(lg[0:N_EXPERTS])[0])[1]
    cnt_ref[...] = cnt_scr[...]


def _mixer(x, nmw, w_all, wup, balpha, gnw, convw, convb, dtb, aneg, dexp, snw, wout, nfw, rw, rb):
    bsz, seqlen, _ = x.shape
    per_row = seqlen // TL
    steps = bsz * per_row

    def full(a):
        return pl.BlockSpec(a.shape, lambda g: (0,) * a.ndim)

    def tile(ahead):
        def index(g):
            tile_id = jnp.minimum(g + ahead, steps - 1)
            return (tile_id // per_row, tile_id % per_row, 0)
        return pl.BlockSpec((None, TL, D_MODEL), index)

    params = (nmw, w_all, wup, balpha, gnw, convw, convb, dtb, aneg, dexp, snw, wout, nfw, rw, rb)
    return pl.pallas_call(
        functools.partial(_mixer_kernel, tiles_per_row=per_row),
        grid=(steps,),
        in_specs=[tile(0), tile(1)] + [full(p) for p in params],
        out_specs=[tile(0), tile(0),
                   pl.BlockSpec((LANES, TL), lambda g: (0, g)),
                   pl.BlockSpec((N_EXPERTS, LANES), lambda g: (0, 0))],
        out_shape=[
            jax.ShapeDtypeStruct((bsz, seqlen, D_MODEL), F32),
            jax.ShapeDtypeStruct((bsz, seqlen, D_MODEL), BF16),
            jax.ShapeDtypeStruct((LANES, bsz * seqlen), F32),
            jax.ShapeDtypeStruct((N_EXPERTS, LANES), F32),
        ],
        scratch_shapes=[
            pltpu.VMEM((GLA_DV, GLA_KW), F32),
            pltpu.VMEM((SSD_GROUPS, SSD_STATE, SSD_HPG * SSD_HEADDIM), F32),
            pltpu.VMEM((8, SSD_CONV_CH), F32),
            pltpu.VMEM((TL, D_MODEL), F32),
            pltpu.VMEM((N_EXPERTS, LANES), F32),
            pltpu.VMEM((TL, N_MAIN), BF16),
            pltpu.VMEM((TL, N_MAIN), BF16),
            pltpu.VMEM((TL, N_SMALL), F32),
            pltpu.VMEM((TL, N_SMALL), F32),
        ],
        compiler_params=pltpu.CompilerParams(
            dimension_semantics=("arbitrary",), vmem_limit_bytes=VMEM_LIMIT),
        name="mixer",
    )(x, x, *params)


def _top4(lg):
    n_e, n_t = lg.shape
    row = _iota((n_e, n_t), 0)
    work = lg
    onehots = []
    vals = []
    for _ in range(TOP_K):
        m = jnp.max(work, axis=0, keepdims=True)
        idx = jnp.min(jnp.where(work == m, row, n_e), axis=0, keepdims=True)
        oh = row == idx
        onehots.append(oh)
        vals.append(m)
        work = jnp.where(oh, -jnp.inf, work)
    return onehots, vals


def _tile_counts(onehots):
    multi = jnp.where(onehots[0] | onehots[1] | onehots[2] | onehots[3], 1.0, 0.0).astype(BF16)
    return multi, _dot(multi, jnp.ones((multi.shape[1], LANES), BF16))


def _route_kernel(lg_ref, cnt_ref, info_ref, lp_ref, runs_ref, run_scr):
    @pl.when(pl.program_id(0) == 0)
    def _():
        run_scr[...] = jnp.zeros_like(run_scr)

    onehots, vals = _top4(lg_ref[0:N_EXPERTS, :])
    multi, tile_cnt = _tile_counts(onehots)
    counts = cnt_ref[...]
    padded = jnp.floor((counts + (TM - 1)) * (1.0 / TM)) * TM
    lower = jnp.where(_iota((N_EXPERTS, N_EXPERTS), 1) < _iota((N_EXPERTS, N_EXPERTS), 0),
                      1.0, 0.0).astype(BF16)
    offs = _dot_sel_lhs(lower, padded)
    loff = _dot_sel_lhs(lower, tile_cnt)
    before = jnp.where(_iota((TT, TT), 0) < _iota((TT, TT), 1), 1.0, 0.0).astype(BF16)
    rank = _dot(multi, before)
    lane = _iota((N_EXPERTS, LANES), 1)
    table = jnp.where(lane == RUN_START, offs + run_scr[...],
                      jnp.where(lane == RUN_LEN, tile_cnt, jnp.where(lane == RUN_LOFF, loff, 0.0)))
    table = jnp.concatenate([table, jnp.zeros((LANES - N_EXPERTS, LANES), F32)], axis=0)
    runs_ref[...] = table.T[0:8].astype(jnp.int32)
    run_scr[...] = run_scr[...] + tile_cnt
    local = rank + jnp.concatenate([loff] * (TT // LANES), axis=1)
    exps = [jnp.exp(v - vals[0]) for v in vals]
    den = exps[0] + exps[1] + exps[2] + exps[3]
    lp_rows = [jnp.sum(jnp.where(oh, local, 0.0), axis=0, keepdims=True) for oh in onehots]
    lp_ref[...] = jnp.concatenate(lp_rows + [jnp.zeros((8 - TOP_K, TT), F32)], axis=0).astype(jnp.int32)
    gate_rows = [e / den for e in exps]
    info = jnp.concatenate(gate_rows + lp_rows + [jnp.zeros((LANES - 2 * TOP_K, TT), F32)], axis=0)
    info_ref[...] = info.T


def _route(logits_t, counts):
    t = logits_t.shape[1]
    steps = t // TT
    return pl.pallas_call(
        _route_kernel,
        grid=(steps,),
        in_specs=[pl.BlockSpec((LANES, TT), lambda i: (0, i)),
                  pl.BlockSpec((N_EXPERTS, LANES), lambda i: (0, 0))],
        out_specs=[
            pl.BlockSpec((TT, LANES), lambda i: (i, 0)),
            pl.BlockSpec((8, TT), lambda i: (0, i)),
            pl.BlockSpec((8, LANES), lambda i: (i, 0)),
        ],
        out_shape=[
            jax.ShapeDtypeStruct((t, LANES), F32),
            jax.ShapeDtypeStruct((8, t), jnp.int32),
            jax.ShapeDtypeStruct((steps * 8, LANES), jnp.int32),
        ],
        scratch_shapes=[pltpu.VMEM((N_EXPERTS, LANES), F32)],
        compiler_params=pltpu.CompilerParams(
            dimension_semantics=("arbitrary",), vmem_limit_bytes=VMEM_LIMIT),
        name="route",
    )(logits_t, counts)


def _rows(first, count):
    return pl.ds(pl.multiple_of(first * ROW_TILE, ROW_TILE), count * ROW_TILE)


def _wait_rows(stage_slot, hbm_ref, sem, to_hbm):
    for w in range(STAGE_ROWS // WAIT_ROWS):
        part = stage_slot.at[pl.ds(w * WAIT_ROWS * ROW_TILE, WAIT_ROWS * ROW_TILE), :]
        hbm = hbm_ref.at[pl.ds(0, WAIT_ROWS * ROW_TILE), :]
        src, dst = (part, hbm) if to_hbm else (hbm, part)
        pltpu.make_async_copy(src, dst, sem).wait()


def _for_each_piece(length, max_log2, fn):
    for b in reversed(range(max_log2 + 1)):
        size = 1 << b
        offset = lax.shift_left(lax.shift_right_logical(length, b + 1), b + 1)

        @pl.when((length & size) != 0)
        def _(offset=offset, size=size):
            fn(offset, size)


def _plan_kernel(cnt_ref, plan_ref, *, n_tiles, p_rows):
    def clear(i, c):
        for r in range(8):
            plan_ref[r, i] = 0
        return c
    lax.fori_loop(0, PLAN_COLS, clear, 0)

    def per_expert(e, carry):
        first_row, group = carry
        count = cnt_ref[e, 0].astype(jnp.int32)
        tiles = lax.shift_right_logical(count + (TM - 1), TM_LOG2)
        first_tile = lax.shift_right_logical(first_row, TM_LOG2)

        def mark(i, c):
            plan_ref[PLAN_EXPERT, i] = e
            plan_ref[PLAN_PARITY, i] = group & 1
            return c
        lax.fori_loop(first_tile, first_tile + tiles, mark, 0)
        plan_ref[PLAN_PAD_START, e] = first_row + count
        plan_ref[PLAN_PAD_LEN, e] = tiles * TM - count
        return first_row + tiles * TM, group + jnp.minimum(tiles, 1)

    used_rows, _ = lax.fori_loop(0, N_EXPERTS, per_expert, (jnp.int32(0), jnp.int32(0)))
    n_valid = lax.shift_right_logical(used_rows, TM_LOG2)
    plan_ref[PLAN_NVALID, 0] = n_valid
    plan_ref[PLAN_PAD_START, N_EXPERTS] = used_rows
    plan_ref[PLAN_PAD_LEN, N_EXPERTS] = lax.shift_right_logical(p_rows - used_rows, TM_LOG2 - 1)

    last_expert = plan_ref[PLAN_EXPERT, n_valid - 1]
    last_parity = plan_ref[PLAN_PARITY, n_valid - 1]

    def mark_unused(i, c):
        plan_ref[PLAN_EXPERT, i] = last_expert
        plan_ref[PLAN_PARITY, i] = last_parity
        return c
    lax.fori_loop(n_valid, n_tiles, mark_unused, 0)

    def next_expert(k, carry):
        expert_after, next_after = carry
        i = n_tiles - 1 - k
        e = plan_ref[PLAN_EXPERT, i]
        nxt = jnp.where(expert_after != e, expert_after, next_after)
        plan_ref[PLAN_NEXT, i] = nxt
        return e, nxt
    lax.fori_loop(0, n_tiles, next_expert, (last_expert, last_expert))


def _plan(counts, n_tiles, p_rows):
    assert n_tiles <= PLAN_COLS and N_EXPERTS < PLAN_COLS
    return pl.pallas_call(
        functools.partial(_plan_kernel, n_tiles=n_tiles, p_rows=p_rows),
        in_specs=[pl.BlockSpec(memory_space=pltpu.SMEM)],
        out_specs=pl.BlockSpec(memory_space=pltpu.SMEM),
        out_shape=jax.ShapeDtypeStruct((8, PLAN_COLS), jnp.int32),
        name="plan",
    )(counts)


def _dispatch_kernel(runs_ref, plan_ref, lp_ref, h_ref, xs_ref, stage, zeros, sem, zsem):
    j = pl.program_id(0)
    slot = j & 1

    def wait_tile(s):
        _wait_rows(stage.at[s], xs_ref, sem.at[s], to_hbm=True)

    @pl.when(j == 0)
    def _():
        zeros[...] = jnp.zeros_like(zeros)

        def pad_pieces(e, wait):
            def piece(offset, size):
                cp = pltpu.make_async_copy(
                    zeros.at[pl.ds(0, size * ROW_TILE), :],
                    xs_ref.at[_rows(plan_ref[PLAN_PAD_START, e] + offset, size), :],
                    zsem.at[e % ZERO_DEPTH])
                cp.wait() if wait else cp.start()
            _for_each_piece(plan_ref[PLAN_PAD_LEN, e], TM_LOG2 - 1, piece)

        for e in range(N_EXPERTS + ZERO_DEPTH - 1):
            if e < N_EXPERTS:
                pad_pieces(e, wait=False)
            if e >= ZERO_DEPTH - 1:
                pad_pieces(e - (ZERO_DEPTH - 1), wait=True)

        n_blocks = plan_ref[PLAN_PAD_LEN, N_EXPERTS]

        def block_copy(blk):
            first = plan_ref[PLAN_PAD_START, N_EXPERTS] + blk * (TM // 2)
            return pltpu.make_async_copy(zeros, xs_ref.at[_rows(first, TM // 2), :],
                                         zsem.at[lax.rem(blk, ZERO_DEPTH)])

        def put_block(blk, carry):
            block_copy(blk).start()

            @pl.when(blk >= ZERO_DEPTH - 1)
            def _():
                block_copy(blk - (ZERO_DEPTH - 1)).wait()
            return carry
        lax.fori_loop(0, n_blocks, put_block, 0)
        for back in range(ZERO_DEPTH - 1, 0, -1):
            @pl.when(n_blocks >= back)
            def _(back=back):
                block_copy(n_blocks - back).wait()

    @pl.when(j >= 2)
    def _():
        wait_tile(slot)

    h = h_ref[...]
    lp = lp_ref[...]
    for c in range(STAGE_ROWS // TT):
        r = _iota((TT, TT), 0) + c * TT
        hit = (r == lp[0:1]) | (r == lp[1:2]) | (r == lp[2:3]) | (r == lp[3:4])
        rows = _dot(jnp.where(hit, 1.0, 0.0).astype(BF16), h)
        for q in range(ROW_TILE):
            stage[slot, pl.ds(c * TT * ROW_TILE + q, TT, stride=ROW_TILE), :] = rows[:, q * LANES:(q + 1) * LANES]

    for e in range(N_EXPERTS):
        src = runs_ref[j * 8 + RUN_LOFF, e]
        dst = runs_ref[j * 8 + RUN_START, e]

        def put(offset, size, src=src, dst=dst):
            pltpu.make_async_copy(stage.at[slot, _rows(src + offset, size), :],
                                  xs_ref.at[_rows(dst + offset, size), :], sem.at[slot]).start()
        _for_each_piece(runs_ref[j * 8 + RUN_LEN, e], TT_LOG2, put)

    @pl.when(j == pl.num_programs(0) - 1)
    def _():
        wait_tile(1 - slot)
        wait_tile(slot)


def _dispatch(runs, plan, lp, h2, p_rows):
    t = h2.shape[0]
    return pl.pallas_call(
        _dispatch_kernel,
        grid_spec=pltpu.PrefetchScalarGridSpec(
            num_scalar_prefetch=2,
            grid=(t // TT,),
            in_specs=[
                pl.BlockSpec((8, TT), lambda j, *_: (0, j)),
                pl.BlockSpec((TT, D_MODEL), lambda j, *_: (j, 0)),
            ],
            out_specs=pl.BlockSpec(memory_space=pl.ANY),
            scratch_shapes=[
                pltpu.VMEM((2, STAGE_ROWS * ROW_TILE, LANES), F32),
                pltpu.VMEM((TM // 2 * ROW_TILE, LANES), F32),
                pltpu.SemaphoreType.DMA((2,)),
                pltpu.SemaphoreType.DMA((ZERO_DEPTH,)),
            ],
        ),
        out_shape=jax.ShapeDtypeStruct((p_rows * ROW_TILE, LANES), F32),
        compiler_params=pltpu.CompilerParams(
            dimension_semantics=("arbitrary",), vmem_limit_bytes=VMEM_LIMIT),
        name="dispatch",
    )(runs, plan, lp, h2)


def _experts_kernel(plan_ref, x_ref, wg_ref, bg_ref, wu_ref, bu_ref, wd_ref, bd_ref,
                    o_ref, act, wbuf, wg_b, wu_b, wd_b, wsem):
    i = pl.program_id(0)
    prev = jnp.maximum(i - 1, 0)

    def fetch(expert, s):
        return [pltpu.make_async_copy(w_ref.at[expert], wbuf.at[s, m], wsem.at[s])
                for m, w_ref in enumerate((wg_ref, wu_ref, wd_ref))]

    n_valid = plan_ref[PLAN_NVALID, 0]

    @pl.when(i >= n_valid)
    def _():
        o_ref[...] = jnp.zeros_like(o_ref)

    @pl.when(i < n_valid)
    def _():
        expert = plan_ref[PLAN_EXPERT, i]
        s = plan_ref[PLAN_PARITY, i]
        nxt = plan_ref[PLAN_NEXT, i]

        @pl.when(i == 0)
        def _():
            for cp in fetch(expert, s):
                cp.start()

        @pl.when((i == 0) | (expert != plan_ref[PLAN_EXPERT, prev]))
        def _():
            for cp in fetch(expert, s):
                cp.wait()
            for c in range(N_CHUNKS):
                cs = slice(c * N_COLS, (c + 1) * N_COLS)
                wg_b[c] = wbuf[s, 0, :, cs].astype(BF16)
                wu_b[c] = wbuf[s, 1, :, cs].astype(BF16)
                wd_b[c] = wbuf[s, 2, :, cs].astype(BF16)

            @pl.when(nxt != expert)
            def _():
                for cp in fetch(nxt, 1 - s):
                    cp.start()

        xb = _load_row_tiles(x_ref, TM).astype(BF16)
        for n in range(N_CHUNKS):
            gate = _dot(xb, wg_b[n]) + bg_ref[n:n + 1, :]
            up = _dot(xb, wu_b[n]) + bu_ref[n:n + 1, :]
            gate = jnp.minimum(gate, SWIGLU_LIMIT)
            up = jnp.clip(up, -SWIGLU_LIMIT, SWIGLU_LIMIT)
            act[n] = ((up + 1.0) * (gate * jax.nn.sigmoid(SWIGLU_ALPHA * gate))).astype(BF16)
        a = jnp.concatenate([act[c] for c in range(N_CHUNKS)], axis=1)
        for n in range(N_CHUNKS):
            out = _dot(a, wd_b[n]) + bd_ref[n:n + 1, :]
            for q in range(N_COLS // LANES):
                o_ref[pl.ds(n * (N_COLS // LANES) + q, TM, stride=ROW_TILE), :] = (
                    out[:, q * LANES:(q + 1) * LANES])


def _experts(plan, xs, wg, bg, wu, bu, wd, bd):
    n_tiles = xs.shape[0] // (TM * ROW_TILE)

    def x_map(i, plan_ref):
        return (jnp.minimum(i, plan_ref[PLAN_NVALID, 0] - 1), 0)

    def b_map(i, plan_ref):
        return (plan_ref[PLAN_EXPERT, i], 0, 0)

    w_spec = pl.BlockSpec(memory_space=pl.ANY)
    b_spec = pl.BlockSpec((None, N_CHUNKS, N_COLS), b_map)
    return pl.pallas_call(
        _experts_kernel,
        grid_spec=pltpu.PrefetchScalarGridSpec(
            num_scalar_prefetch=1,
            grid=(n_tiles,),
            in_specs=[pl.BlockSpec((TM * ROW_TILE, LANES), x_map),
                      w_spec, b_spec, w_spec, b_spec, w_spec, b_spec],
            out_specs=pl.BlockSpec((TM * ROW_TILE, LANES), lambda i, plan_ref: (i, 0)),
            scratch_shapes=[
                pltpu.VMEM((N_CHUNKS, TM, N_COLS), BF16),
                pltpu.VMEM((2, 3, D_MODEL, D_MODEL), F32),
                pltpu.VMEM((N_CHUNKS, D_MODEL, N_COLS), BF16),
                pltpu.VMEM((N_CHUNKS, D_MODEL, N_COLS), BF16),
                pltpu.VMEM((N_CHUNKS, D_MODEL, N_COLS), BF16),
                pltpu.SemaphoreType.DMA((2,)),
            ],
        ),
        out_shape=jax.ShapeDtypeStruct(xs.shape, F32),
        compiler_params=pltpu.CompilerParams(
            dimension_semantics=("arbitrary",), vmem_limit_bytes=VMEM_LIMIT),
        name="experts",
    )(plan, xs, wg, bg, wu, bu, wd, bd)


def _combine_kernel(runs_ref, info_ref, x1_ref, fw_ref, eo_ref, out_ref, stage, sem):
    j = pl.program_id(0)
    slot = lax.rem(j, 3)
    last = pl.num_programs(0) - 1

    def fetch(tile, s):
        for e in range(N_EXPERTS):
            src = runs_ref[tile * 8 + RUN_START, e]
            dst = runs_ref[tile * 8 + RUN_LOFF, e]

            def get(offset, size, src=src, dst=dst):
                pltpu.make_async_copy(eo_ref.at[_rows(src + offset, size), :],
                                      stage.at[s, _rows(dst + offset, size), :], sem.at[s]).start()
            _for_each_piece(runs_ref[tile * 8 + RUN_LEN, e], TT_LOG2, get)

    @pl.when(j == 0)
    def _():
        fetch(0, 0)
        fetch(jnp.minimum(1, last), 1)

    _wait_rows(stage.at[slot], eo_ref, sem.at[slot], to_hbm=False)

    info = info_ref[...]
    y = x1_ref[...]
    for c in range(STAGE_ROWS // TT):
        r = (_iota((TT, TT), 1) + c * TT).astype(F32)
        g = jnp.zeros((TT, TT), F32)
        for k in range(TOP_K):
            g = g + jnp.where(r == info[:, TOP_K + k:TOP_K + k + 1], info[:, k:k + 1], 0.0)
        g_hi = g.astype(BF16)
        g_lo = (g - g_hi.astype(F32)).astype(BF16)
        rows = jnp.concatenate(
            [stage[slot, pl.ds(c * TT * ROW_TILE + q, TT, stride=ROW_TILE), :] for q in range(ROW_TILE)],
            axis=1).astype(BF16)
        y = y + _dot(g_hi, rows) + _dot(g_lo, rows)
    ms = jnp.mean(y * y, axis=-1, keepdims=True)
    out_ref[...] = y * lax.rsqrt(ms + EPS) * fw_ref[...]

    ahead = lax.rem(j + 2, 3)
    fetch(jnp.minimum(j + 2, last), ahead)

    @pl.when(j == last)
    def _():
        _wait_rows(stage.at[lax.rem(j + 1, 3)], eo_ref, sem.at[lax.rem(j + 1, 3)], to_hbm=False)
        _wait_rows(stage.at[ahead], eo_ref, sem.at[ahead], to_hbm=False)


def _combine(runs, info, x1, final_w, eo):
    t = x1.shape[0]
    return pl.pallas_call(
        _combine_kernel,
        grid_spec=pltpu.PrefetchScalarGridSpec(
            num_scalar_prefetch=1,
            grid=(t // TT,),
            in_specs=[
                pl.BlockSpec((TT, LANES), lambda j, *_: (j, 0)),
                pl.BlockSpec((TT, D_MODEL), lambda j, *_: (j, 0)),
                pl.BlockSpec((1, D_MODEL), lambda j, *_: (0, 0)),
                pl.BlockSpec(memory_space=pl.ANY),
            ],
            out_specs=pl.BlockSpec((TT, D_MODEL), lambda j, *_: (j, 0)),
            scratch_shapes=[
                pltpu.VMEM((3, STAGE_ROWS * ROW_TILE, LANES), F32),
                pltpu.SemaphoreType.DMA((3,)),
            ],
        ),
        out_shape=jax.ShapeDtypeStruct((t, D_MODEL), F32),
        compiler_params=pltpu.CompilerParams(
            dimension_semantics=("arbitrary",), vmem_limit_bytes=VMEM_LIMIT),
        name="combine",
    )(runs, info, x1, final_w, eo)


def _pad_lanes(v, offset, fill=0.0):
    row = jnp.full((1, LANES), fill, F32)
    return row.at[0, offset:offset + v.shape[0]].set(v.astype(F32))


def kernel(x, norm_mix_w, w_in, gla_w_alpha_up, gla_b_alpha, gla_norm_w, ssd_conv_w, ssd_conv_b,
           ssd_dt_bias, ssd_A_log, ssd_D, ssd_norm_w, w_out, norm_ffn_w, router_w, router_b,
           moe_w_gate, moe_b_gate, moe_w_up, moe_b_up, moe_w_down, moe_b_down, final_norm_w):
    bsz, seqlen, d = x.shape
    t = bsz * seqlen
    depth = w_in.shape[0]
    assert depth == 1, "the final RMSNorm is fused into the (single) layer's combine step"
    p_rows = t * TOP_K + N_EXPERTS * TM
    n_tiles = p_rows // TM
    for l in range(depth):
        w = w_in[l]
        w_all = jnp.concatenate(
            [w[:, 0:1536], w[:, 1552:3088], w[:, 1536:1552], w[:, 3088:3096],
             jnp.zeros((d, N_SMALL - GLA_GATE_RANK - SSD_HEADS), w.dtype)], axis=1).astype(BF16)
        wup = jnp.zeros((N_SMALL, GLA_KW), F32).at[0:GLA_GATE_RANK].set(gla_w_alpha_up[l])
        dtb = _pad_lanes(ssd_dt_bias[l], DT_COL)
        aneg = _pad_lanes(-jnp.exp(ssd_A_log[l].astype(F32)), DT_COL)
        dexp = jnp.repeat(ssd_D[l].astype(F32), SSD_HEADDIM)[None, :]
        rw = jnp.zeros((LANES, d), F32).at[0:N_EXPERTS].set(router_w[l].T)
        rb = jnp.zeros((LANES, TL), F32).at[0:N_EXPERTS].set(
            jnp.broadcast_to(router_b[l].astype(F32)[:, None], (N_EXPERTS, TL)))

        x1, h2, logits, counts = _mixer(
            x, norm_mix_w[l][None, :], w_all, wup, gla_b_alpha[l][None, :], gla_norm_w[l][None, :], ssd_conv_w[l],
            ssd_conv_b[l][None, :], dtb, aneg, dexp, ssd_norm_w[l][None, :],
            w_out[l].astype(BF16), norm_ffn_w[l][None, :], rw, rb)

        info, lp, runs = _route(logits, counts)
        plan = _plan(counts, n_tiles, p_rows)
        xs = _dispatch(runs, plan, lp, h2.reshape(t, d), p_rows)
        eo = _experts(plan, xs,
                      moe_w_gate[l], moe_b_gate[l].reshape(N_EXPERTS, N_CHUNKS, N_COLS),
                      moe_w_up[l], moe_b_up[l].reshape(N_EXPERTS, N_CHUNKS, N_COLS),
                      moe_w_down[l], moe_b_down[l].reshape(N_EXPERTS, N_CHUNKS, N_COLS))
        x = _combine(runs, info, x1.reshape(t, d), final_norm_w[None, :], eo).reshape(bsz, seqlen, d)
    return x
```

```python
import functools

import jax
import jax.numpy as jnp
from jax import lax
from jax.experimental import pallas as pl
from jax.experimental.pallas import tpu as pltpu

F32 = jnp.float32
BF16 = jnp.bfloat16

D_MODEL = 1024
GLA_WIDTH = 512
GLA_HEADS = 4
GLA_DV = 128
GLA_DK = 64
GLA_KW = 256
GLA_GATE_RANK = 16
GLA_GATE_NORM = 16.0
SSD_WIDTH = 512
SSD_HEADDIM = 64
SSD_HEADS = 8
SSD_GROUPS = 2
SSD_HPG = 4
SSD_STATE = 128
SSD_CONV = 4
SSD_CONV_CH = 1024
N_EXPERTS = 32
TOP_K = 4
SWIGLU_LIMIT = 7.0
SWIGLU_ALPHA = 1.702
EPS = 1e-6
GROUP_EPS = 1e-5

LANES = 128
ROW_TILE = D_MODEL // LANES
N_MAIN = 3072
N_SMALL = LANES
DT_COL = GLA_GATE_RANK

GLA_CHUNK = 64
SSD_CHUNK = 128
PROJ_COLS = 512
TL = 256
TT_LOG2 = 8
TT = 1 << TT_LOG2
STAGE_ROWS = 4 * TT
TM_LOG2 = 9
TM = 1 << TM_LOG2
WAIT_ROWS = 512
ZERO_DEPTH = 4
RUN_START, RUN_LEN, RUN_LOFF = 0, 1, 2
PLAN_EXPERT, PLAN_NEXT, PLAN_PARITY, PLAN_NVALID, PLAN_PAD_START, PLAN_PAD_LEN = 0, 1, 2, 3, 4, 5
PLAN_COLS = 256
MAX_TILE_COPY_BYTES = 4 * 1024 * 1024
assert STAGE_ROWS * D_MODEL * 4 <= MAX_TILE_COPY_BYTES
N_COLS = 256
N_CHUNKS = D_MODEL // N_COLS
VMEM_LIMIT = 56 * 1024 * 1024


def _dot(a, b):
    return jnp.dot(a, b, preferred_element_type=F32)


def _dot_nt(a, b):
    return lax.dot_general(a, b, (((1,), (1,)), ((), ())), preferred_element_type=F32)


def _dot_tn(a, b):
    return lax.dot_general(a, b, (((0,), (0,)), ((), ())), preferred_element_type=F32)


def _split3(a):
    hi = a.astype(BF16)
    r1 = a - hi.astype(F32)
    mid = r1.astype(BF16)
    lo = (r1 - mid.astype(F32)).astype(BF16)
    return hi, mid, lo


def _dot_sel_lhs(sel, a):
    hi, mid, lo = _split3(a)
    return _dot(sel, hi) + _dot(sel, mid) + _dot(sel, lo)


def _dot_sel_rhs(a, sel, terms=3):
    parts = _split3(a)[:terms]
    out = _dot(parts[0], sel)
    for p in parts[1:]:
        out = out + _dot(p, sel)
    return out


def _dot_hi(a, b):
    a_hi = a.astype(BF16)
    a_lo = (a - a_hi.astype(F32)).astype(BF16)
    b_hi = b.astype(BF16)
    b_lo = (b - b_hi.astype(F32)).astype(BF16)
    return _dot(a_hi, b_hi) + _dot(a_lo, b_hi) + _dot(a_hi, b_lo)


def _dot_hi_nt(a, b):
    a_hi = a.astype(BF16)
    a_lo = (a - a_hi.astype(F32)).astype(BF16)
    b_hi = b.astype(BF16)
    b_lo = (b - b_hi.astype(F32)).astype(BF16)
    return _dot_nt(a_hi, b_hi) + _dot_nt(a_lo, b_hi) + _dot_nt(a_hi, b_lo)


def _softplus(x):
    return jnp.maximum(x, 0.0) + jnp.log1p(jnp.exp(-jnp.abs(x)))


def _silu(x):
    return x * jax.nn.sigmoid(x)


def _iota(shape, dim):
    return lax.broadcasted_iota(jnp.int32, shape, dim)


def _load_row_tiles(ref, rows):
    return jnp.concatenate([ref[pl.ds(j, rows, stride=ROW_TILE), :] for j in range(ROW_TILE)], axis=1)


def _project_parts(x_ref, nmw_ref, win_ref, pm_ref, small_ref):
    state = {}

    def norm():
        x_in = x_ref[...]
        ms = jnp.mean(x_in * x_in, axis=-1, keepdims=True)
        state["h"] = (x_in * lax.rsqrt(ms + EPS) * nmw_ref[...]).astype(BF16)

    def chunk(n0):
        def run():
            pm_ref[:, n0:n0 + PROJ_COLS] = _dot(state["h"], win_ref[:, n0:n0 + PROJ_COLS]).astype(BF16)
        return run

    def small():
        small_ref[...] = _dot(state["h"], win_ref[:, N_MAIN:N_MAIN + N_SMALL])

    return [norm] + [chunk(n0) for n0 in range(0, N_MAIN, PROJ_COLS)] + [small]


def _mixer_kernel(x_ref, xn_ref, nmw_ref, win_ref, *refs, tiles_per_row):
    params, outs = refs[:13], refs[13:17]
    gla_state, ssd_state, conv_tail, mix_scr, cnt_scr, pm_a, pm_b, small_a, small_b = refs[17:]
    g = pl.program_id(0)

    @pl.when(g == 0)
    def _():
        cnt_scr[...] = jnp.zeros_like(cnt_scr)
        for part in _project_parts(x_ref, nmw_ref, win_ref, pm_a, small_a):
            part()

    @pl.when(lax.rem(g, tiles_per_row) == 0)
    def _():
        gla_state[...] = jnp.zeros_like(gla_state)
        ssd_state[...] = jnp.zeros_like(ssd_state)
        conv_tail[...] = jnp.zeros_like(conv_tail)

    for parity, (pm_cur, small_cur, pm_nxt, small_nxt) in enumerate(
            ((pm_a, small_a, pm_b, small_b), (pm_b, small_b, pm_a, small_a))):
        @pl.when((g & 1) == parity)
        def _(pm_cur=pm_cur, small_cur=small_cur, pm_nxt=pm_nxt, small_nxt=small_nxt):
            _mixer_tile(_project_parts(xn_ref, nmw_ref, win_ref, pm_nxt, small_nxt),
                        pm_cur, small_cur, x_ref, *params, *outs,
                        gla_state, ssd_state, conv_tail, mix_scr, cnt_scr)


def _mixer_tile(side_work, pm_ref, small_ref, x_ref, wup_ref, balpha_ref, gnw_ref, convw_ref, convb_ref,
                dtb_ref, aneg_ref, dexp_ref, snw_ref, wout_ref, nfw_ref, rw_ref, rb_ref,
                x1_ref, h2_ref, lg_ref, cnt_ref,
                gla_state, ssd_state, conv_tail, mix_scr, cnt_scr):
    side_work = list(side_work)

    def run_side(n=1):
        for _ in range(min(n, len(side_work))):
            side_work.pop(0)()

    small = small_ref[...]

    row = _iota((TL, TL), 0)
    col = _iota((TL, TL), 1)
    causal = col <= row
    cum64 = jnp.where(causal & ((row // GLA_CHUNK) == (col // GLA_CHUNK)), 1.0, 0.0).astype(BF16)
    cum128 = jnp.where(causal & ((row // SSD_CHUNK) == (col // SSD_CHUNK)), 1.0, 0.0).astype(BF16)

    xa = _dot_hi(small, wup_ref[...]) + balpha_ref[...]
    log_a = (jnp.minimum(xa, 0.0) - jnp.log1p(jnp.exp(-jnp.abs(xa)))) * (1.0 / GLA_GATE_NORM)
    bcum = _dot_sel_lhs(cum64, log_a)

    lane_kw = _iota((GLA_CHUNK, GLA_KW), 1)
    head_masks = [(lane_kw // GLA_DK) == h for h in range(GLA_HEADS)]
    lane_kw_s = _iota((GLA_DV, GLA_KW), 1)
    head_masks_s = [(lane_kw_s // GLA_DK) == h for h in range(GLA_HEADS)]
    tril64 = _iota((GLA_CHUNK, GLA_CHUNK), 1) <= _iota((GLA_CHUNK, GLA_CHUNK), 0)
    q_scale = GLA_DK ** -0.5

    def gla_chunk(c):
        rs = slice(c * GLA_CHUNK, (c + 1) * GLA_CHUNK)
        bc = bcum[rs]
        b_mid = bc[GLA_CHUNK // 2:GLA_CHUNK // 2 + 1]
        b_last = bc[GLA_CHUNK - 1:GLA_CHUNK]
        qc = pm_ref[rs, 0:GLA_KW].astype(F32) * q_scale
        kc = pm_ref[rs, GLA_KW:2 * GLA_KW].astype(F32)
        vc = pm_ref[rs, 2 * GLA_KW:2 * GLA_KW + GLA_WIDTH]
        q_in = (qc * jnp.exp(bc - b_mid)).astype(BF16)
        k_in = (kc * jnp.exp(b_mid - bc)).astype(BF16)
        q_st = (qc * jnp.exp(bc)).astype(BF16)
        k_st = (kc * jnp.exp(b_last - bc)).astype(BF16)
        st = gla_state[...]
        st_b = st.astype(BF16)
        zero_b = jnp.zeros_like(q_in)
        for h in range(GLA_HEADS):
            scores = _dot_nt(jnp.where(head_masks[h], q_in, zero_b), k_in)
            scores = jnp.where(tril64, scores, 0.0).astype(BF16)
            o_h = _dot(scores, vc[:, h * GLA_DV:(h + 1) * GLA_DV])
            o_h = o_h + _dot_nt(jnp.where(head_masks[h], q_st, zero_b), st_b)
            mix_scr[rs, h * GLA_DV:(h + 1) * GLA_DV] = o_h
        upd = _dot_tn(vc, k_st)
        new_st = st * jnp.exp(b_last)
        for h in range(GLA_HEADS):
            new_st = new_st + jnp.where(head_masks_s[h], upd[h * GLA_DV:(h + 1) * GLA_DV], 0.0)
        gla_state[...] = new_st
        run_side()

    xbc = pm_ref[:, 2048:3072].astype(F32)
    tail = conv_tail[...]
    conv_tail[...] = xbc[TL - 8:TL]
    row8 = _iota((8, SSD_CONV_CH), 0)
    conv = xbc * convw_ref[SSD_CONV - 1:SSD_CONV, :]
    for s in range(1, SSD_CONV):
        shifted = pltpu.roll(xbc, s, 0)
        head = jnp.where(row8 < s, pltpu.roll(tail, s, 0), shifted[0:8])
        shifted = jnp.concatenate([head, shifted[8:]], axis=0)
        conv = conv + shifted * convw_ref[SSD_CONV - 1 - s:SSD_CONV - s, :]
    act = _silu(conv + convb_ref[...])
    run_side()
    xs = act[:, 0:SSD_WIDTH]
    bm = act[:, SSD_WIDTH:SSD_WIDTH + SSD_GROUPS * SSD_STATE].astype(BF16)
    cm = act[:, SSD_WIDTH + SSD_GROUPS * SSD_STATE:].astype(BF16)

    dt_full = _softplus(small + dtb_ref[...])
    a_full = dt_full * aneg_ref[...]
    acum = _dot_sel_lhs(cum128, a_full)
    acum_t = acum.T

    e_row = _iota((N_SMALL, SSD_WIDTH), 0)
    e_col = _iota((N_SMALL, SSD_WIDTH), 1)
    spread64 = jnp.where(e_row == DT_COL + e_col // SSD_HEADDIM, 1.0, 0.0).astype(BF16)
    e_row2 = _iota((N_SMALL, SSD_HEADS * LANES), 0)
    e_col2 = _iota((N_SMALL, SSD_HEADS * LANES), 1)
    spread128 = jnp.where(e_row2 == DT_COL + e_col2 // LANES, 1.0, 0.0).astype(BF16)
    dt_e = _dot_sel_rhs(dt_full, spread64, terms=1)
    ac_e = _dot_sel_rhs(acum, spread64, terms=2)
    ac_w = _dot_sel_rhs(acum, spread128, terms=2)

    tril128 = _iota((SSD_CHUNK, SSD_CHUNK), 1) <= _iota((SSD_CHUNK, SSD_CHUNK), 0)
    lane_g = _iota((SSD_CHUNK, SSD_HPG * SSD_HEADDIM), 1)
    def ssd_chunk(c):
        rs = slice(c * SSD_CHUNK, (c + 1) * SSD_CHUNK)
        ac_c = ac_e[rs]
        a_last = ac_c[SSD_CHUNK - 1:SSD_CHUNK]
        dt_c = dt_e[rs]
        xs_c = xs[rs]
        x_dt = (xs_c * dt_c).astype(BF16)
        x_w = (xs_c * (jnp.exp(a_last - ac_c) * dt_c)).astype(BF16)
        e_ac = jnp.exp(ac_c)
        for g in range(SSD_GROUPS):
            gs = slice(g * SSD_STATE, (g + 1) * SSD_STATE)
            ws = slice(g * SSD_HPG * SSD_HEADDIM, (g + 1) * SSD_HPG * SSD_HEADDIM)
            c_g = cm[rs, gs]
            b_g = bm[rs, gs]
            cb = _dot_nt(c_g, b_g)
            x_dt_g = x_dt[:, ws]
            lhs_parts = []
            rhs_parts = []
            for hh in range(SSD_HPG):
                h = g * SSD_HPG + hh
                seg = ac_w[rs, h * LANES:(h + 1) * LANES] - acum_t[DT_COL + h:DT_COL + h + 1, rs]
                lmat = jnp.where(tril128, jnp.exp(jnp.where(tril128, seg, 0.0)), 0.0)
                lhs_parts.append((cb * lmat).astype(BF16))
                rhs_parts.append(jnp.where((lane_g // SSD_HEADDIM) == hh, x_dt_g,
                                           jnp.zeros_like(x_dt_g)))
            intra = _dot(jnp.concatenate(lhs_parts, axis=1), jnp.concatenate(rhs_parts, axis=0))
            st = ssd_state[g]
            inter = _dot(c_g, st.astype(BF16)) * e_ac[:, ws]
            mix_scr[rs, GLA_WIDTH + g * 256:GLA_WIDTH + (g + 1) * 256] = intra + inter
            ssd_state[g] = st * jnp.exp(a_last[:, ws]) + _dot_tn(b_g, x_w[:, ws])
        run_side()

    gla_per_ssd = SSD_CHUNK // GLA_CHUNK
    for c in range(TL // SSD_CHUNK):
        for cc in range(gla_per_ssd):
            gla_chunk(c * gla_per_ssd + cc)
        ssd_chunk(c)
    run_side(len(side_work))

    o = mix_scr[:, 0:GLA_WIDTH]
    g_gate = _silu(pm_ref[:, 1024:1536].astype(F32))
    gla_parts = []
    for h in range(GLA_HEADS):
        o_h = o[:, h * GLA_DV:(h + 1) * GLA_DV]
        ms = jnp.mean(o_h * o_h, axis=-1, keepdims=True)
        gla_parts.append(o_h * lax.rsqrt(ms + GROUP_EPS))
    gla_out = jnp.concatenate(gla_parts, axis=1) * gnw_ref[...] * g_gate

    y = mix_scr[:, GLA_WIDTH:] + dexp_ref[...] * xs
    y = y * _silu(pm_ref[:, 1536:2048].astype(F32))
    ssd_parts = []
    for g in range(SSD_GROUPS):
        y_g = y[:, g * 256:(g + 1) * 256]
        ms = jnp.mean(y_g * y_g, axis=-1, keepdims=True)
        ssd_parts.append(y_g * lax.rsqrt(ms + GROUP_EPS))
    ssd_out = jnp.concatenate(ssd_parts, axis=1) * snw_ref[...]

    mixed = jnp.concatenate([gla_out, ssd_out], axis=1).astype(BF16)
    x1 = x_ref[...] + _dot(mixed, wout_ref[...])
    x1_ref[...] = x1

    ms = jnp.mean(x1 * x1, axis=-1, keepdims=True)
    h2 = x1 * lax.rsqrt(ms + EPS) * nfw_ref[...]
    h2_ref[...] = h2.astype(BF16)
    lg = _dot_hi_nt(rw_ref[...], h2) + rb_ref[...]
    lg_ref[...] = lg
    cnt_scr[...] = cnt_scr[...] + _tile_counts(_top4(lg[0:N_EXPERTS])[0])[1]
    cnt_ref[...] = cnt_scr[...]


def _mixer(x, nmw, w_all, wup, balpha, gnw, convw, convb, dtb, aneg, dexp, snw, wout, nfw, rw, rb):
    bsz, seqlen, _ = x.shape
    per_row = seqlen // TL
    steps = bsz * per_row

    def full(a):
        return pl.BlockSpec(a.shape, lambda g: (0,) * a.ndim)

    def tile(ahead):
        def index(g):
            tile_id = jnp.minimum(g + ahead, steps - 1)
            return (tile_id // per_row, tile_id % per_row, 0)
        return pl.BlockSpec((None, TL, D_MODEL), index)

    params = (nmw, w_all, wup, balpha, gnw, convw, convb, dtb, aneg, dexp, snw, wout, nfw, rw, rb)
    return pl.pallas_call(
        functools.partial(_mixer_kernel, tiles_per_row=per_row),
        grid=(steps,),
        in_specs=[tile(0), tile(1)] + [full(p) for p in params],
        out_specs=[tile(0), tile(0),
                   pl.BlockSpec((LANES, TL), lambda g: (0, g)),
                   pl.BlockSpec((N_EXPERTS, LANES), lambda g: (0, 0))],
        out_shape=[
            jax.ShapeDtypeStruct((bsz, seqlen, D_MODEL), F32),
            jax.ShapeDtypeStruct((bsz, seqlen, D_MODEL), BF16),
            jax.ShapeDtypeStruct((LANES, bsz * seqlen), F32),
            jax.ShapeDtypeStruct((N_EXPERTS, LANES), F32),
        ],
        scratch_shapes=[
            pltpu.VMEM((GLA_DV, GLA_KW), F32),
            pltpu.VMEM((SSD_GROUPS, SSD_STATE, SSD_HPG * SSD_HEADDIM), F32),
            pltpu.VMEM((8, SSD_CONV_CH), F32),
            pltpu.VMEM((TL, D_MODEL), F32),
            pltpu.VMEM((N_EXPERTS, LANES), F32),
            pltpu.VMEM((TL, N_MAIN), BF16),
            pltpu.VMEM((TL, N_MAIN), BF16),
            pltpu.VMEM((TL, N_SMALL), F32),
            pltpu.VMEM((TL, N_SMALL), F32),
        ],
        compiler_params=pltpu.CompilerParams(
            dimension_semantics=("arbitrary",), vmem_limit_bytes=VMEM_LIMIT),
        name="mixer",
    )(x, x, *params)


def _top4(lg):
    n_e, n_t = lg.shape
    row = _iota((n_e, n_t), 0)
    work = lg
    onehots = []
    vals = []
    for _ in range(TOP_K):
        m = jnp.max(work, axis=0, keepdims=True)
        idx = jnp.min(jnp.where(work == m, row, n_e), axis=0, keepdims=True)
        oh = row == idx
        onehots.append(oh)
        vals.append(m)
        work = jnp.where(oh, -jnp.inf, work)
    return onehots, vals


def _tile_counts(onehots):
    multi = jnp.where(onehots[0] | onehots[1] | onehots[2] | onehots[3], 1.0, 0.0).astype(BF16)
    return multi, _dot(multi, jnp.ones((multi.shape[1], LANES), BF16))


def _route_kernel(lg_ref, cnt_ref, info_ref, lp_ref, runs_ref, run_scr):
    @pl.when(pl.program_id(0) == 0)
    def _():
        run_scr[...] = jnp.zeros_like(run_scr)

    onehots, vals = _top4(lg_ref[0:N_EXPERTS, :])
    multi, tile_cnt = _tile_counts(onehots)
    counts = cnt_ref[...]
    padded = jnp.floor((counts + (TM - 1)) * (1.0 / TM)) * TM
    lower = jnp.where(_iota((N_EXPERTS, N_EXPERTS), 1) < _iota((N_EXPERTS, N_EXPERTS), 0),
                      1.0, 0.0).astype(BF16)
    offs = _dot_sel_lhs(lower, padded)
    loff = _dot_sel_lhs(lower, tile_cnt)
    before = jnp.where(_iota((TT, TT), 0) < _iota((TT, TT), 1), 1.0, 0.0).astype(BF16)
    rank = _dot(multi, before)
    lane = _iota((N_EXPERTS, LANES), 1)
    table = jnp.where(lane == RUN_START, offs + run_scr[...],
                      jnp.where(lane == RUN_LEN, tile_cnt, jnp.where(lane == RUN_LOFF, loff, 0.0)))
    table = jnp.concatenate([table, jnp.zeros((LANES - N_EXPERTS, LANES), F32)], axis=0)
    runs_ref[...] = table.T[0:8].astype(jnp.int32)
    run_scr[...] = run_scr[...] + tile_cnt
    local = rank + jnp.concatenate([loff] * (TT // LANES), axis=1)
    exps = [jnp.exp(v - vals[0]) for v in vals]
    den = exps[0] + exps[1] + exps[2] + exps[3]
    lp_rows = [jnp.sum(jnp.where(oh, local, 0.0), axis=0, keepdims=True) for oh in onehots]
    lp_ref[...] = jnp.concatenate(lp_rows + [jnp.zeros((8 - TOP_K, TT), F32)], axis=0).astype(jnp.int32)
    gate_rows = [e / den for e in exps]
    info = jnp.concatenate(gate_rows + lp_rows + [jnp.zeros((LANES - 2 * TOP_K, TT), F32)], axis=0)
    info_ref[...] = info.T


def _route(logits_t, counts):
    t = logits_t.shape[1]
    steps = t // TT
    return pl.pallas_call(
        _route_kernel,
        grid=(steps,),
        in_specs=[pl.BlockSpec((LANES, TT), lambda i: (0, i)),
                  pl.BlockSpec((N_EXPERTS, LANES), lambda i: (0, 0))],
        out_specs=[
            pl.BlockSpec((TT, LANES), lambda i: (i, 0)),
            pl.BlockSpec((8, TT), lambda i: (0, i)),
            pl.BlockSpec((8, LANES), lambda i: (i, 0)),
        ],
        out_shape=[
            jax.ShapeDtypeStruct((t, LANES), F32),
            jax.ShapeDtypeStruct((8, t), jnp.int32),
            jax.ShapeDtypeStruct((steps * 8, LANES), jnp.int32),
        ],
        scratch_shapes=[pltpu.VMEM((N_EXPERTS, LANES), F32)],
        compiler_params=pltpu.CompilerParams(
            dimension_semantics=("arbitrary",), vmem_limit_bytes=VMEM_LIMIT),
        name="route",
    )(logits_t, counts)


def _rows(first, count):
    return pl.ds(pl.multiple_of(first * ROW_TILE, ROW_TILE), count * ROW_TILE)


def _wait_rows(stage_slot, hbm_ref, sem, to_hbm):
    for w in range(STAGE_ROWS // WAIT_ROWS):
        part = stage_slot.at[pl.ds(w * WAIT_ROWS * ROW_TILE, WAIT_ROWS * ROW_TILE), :]
        hbm = hbm_ref.at[pl.ds(0, WAIT_ROWS * ROW_TILE), :]
        src, dst = (part, hbm) if to_hbm else (hbm, part)
        pltpu.make_async_copy(src, dst, sem).wait()


def _for_each_piece(length, max_log2, fn):
    for b in reversed(range(max_log2 + 1)):
        size = 1 << b
        offset = lax.shift_left(lax.shift_right_logical(length, b + 1), b + 1)

        @pl.when((length & size) != 0)
        def _(offset=offset, size=size):
            fn(offset, size)


def _plan_kernel(cnt_ref, plan_ref, *, n_tiles, p_rows):
    def clear(i, c):
        for r in range(8):
            plan_ref[r, i] = 0
        return c
    lax.fori_loop(0, PLAN_COLS, clear, 0)

    def per_expert(e, carry):
        first_row, group = carry
        count = cnt_ref[e, 0].astype(jnp.int32)
        tiles = lax.shift_right_logical(count + (TM - 1), TM_LOG2)
        first_tile = lax.shift_right_logical(first_row, TM_LOG2)

        def mark(i, c):
            plan_ref[PLAN_EXPERT, i] = e
            plan_ref[PLAN_PARITY, i] = group & 1
            return c
        lax.fori_loop(first_tile, first_tile + tiles, mark, 0)
        plan_ref[PLAN_PAD_START, e] = first_row + count
        plan_ref[PLAN_PAD_LEN, e] = tiles * TM - count
        return first_row + tiles * TM, group + jnp.minimum(tiles, 1)

    used_rows, _ = lax.fori_loop(0, N_EXPERTS, per_expert, (jnp.int32(0), jnp.int32(0)))
    n_valid = lax.shift_right_logical(used_rows, TM_LOG2)
    plan_ref[PLAN_NVALID, 0] = n_valid
    plan_ref[PLAN_PAD_START, N_EXPERTS] = used_rows
    plan_ref[PLAN_PAD_LEN, N_EXPERTS] = lax.shift_right_logical(p_rows - used_rows, TM_LOG2 - 1)

    last_expert = plan_ref[PLAN_EXPERT, n_valid - 1]
    last_parity = plan_ref[PLAN_PARITY, n_valid - 1]

    def mark_unused(i, c):
        plan_ref[PLAN_EXPERT, i] = last_expert
        plan_ref[PLAN_PARITY, i] = last_parity
        return c
    lax.fori_loop(n_valid, n_tiles, mark_unused, 0)

    def next_expert(k, carry):
        expert_after, next_after = carry
        i = n_tiles - 1 - k
        e = plan_ref[PLAN_EXPERT, i]
        nxt = jnp.where(expert_after != e, expert_after, next_after)
        plan_ref[PLAN_NEXT, i] = nxt
        return e, nxt
    lax.fori_loop(0, n_tiles, next_expert, (last_expert, last_expert))


def _plan(counts, n_tiles, p_rows):
    assert n_tiles <= PLAN_COLS and N_EXPERTS < PLAN_COLS
    return pl.pallas_call(
        functools.partial(_plan_kernel, n_tiles=n_tiles, p_rows=p_rows),
        in_specs=[pl.BlockSpec(memory_space=pltpu.SMEM)],
        out_specs=pl.BlockSpec(memory_space=pltpu.SMEM),
        out_shape=jax.ShapeDtypeStruct((8, PLAN_COLS), jnp.int32),
        name="plan",
    )(counts)


def _dispatch_kernel(runs_ref, plan_ref, lp_ref, h_ref, xs_ref, stage, zeros, sem, zsem):
    j = pl.program_id(0)
    slot = j & 1

    def wait_tile(s):
        _wait_rows(stage.at[s], xs_ref, sem.at[s], to_hbm=True)

    @pl.when(j == 0)
    def _():
        zeros[...] = jnp.zeros_like(zeros)

        def pad_pieces(e, wait):
            def piece(offset, size):
                cp = pltpu.make_async_copy(
                    zeros.at[pl.ds(0, size * ROW_TILE), :],
                    xs_ref.at[_rows(plan_ref[PLAN_PAD_START, e] + offset, size), :],
                    zsem.at[e % ZERO_DEPTH])
                cp.wait() if wait else cp.start()
            _for_each_piece(plan_ref[PLAN_PAD_LEN, e], TM_LOG2 - 1, piece)

        for e in range(N_EXPERTS + ZERO_DEPTH - 1):
            if e < N_EXPERTS:
                pad_pieces(e, wait=False)
            if e >= ZERO_DEPTH - 1:
                pad_pieces(e - (ZERO_DEPTH - 1), wait=True)

        n_blocks = plan_ref[PLAN_PAD_LEN, N_EXPERTS]

        def block_copy(blk):
            first = plan_ref[PLAN_PAD_START, N_EXPERTS] + blk * (TM // 2)
            return pltpu.make_async_copy(zeros, xs_ref.at[_rows(first, TM // 2), :],
                                         zsem.at[lax.rem(blk, ZERO_DEPTH)])

        def put_block(blk, carry):
            block_copy(blk).start()

            @pl.when(blk >= ZERO_DEPTH - 1)
            def _():
                block_copy(blk - (ZERO_DEPTH - 1)).wait()
            return carry
        lax.fori_loop(0, n_blocks, put_block, 0)
        for back in range(ZERO_DEPTH - 1, 0, -1):
            @pl.when(n_blocks >= back)
            def _(back=back):
                block_copy(n_blocks - back).wait()

    @pl.when(j >= 2)
    def _():
        wait_tile(slot)

    h = h_ref[...]
    lp = lp_ref[...]
    for c in range(STAGE_ROWS // TT):
        r = _iota((TT, TT), 0) + c * TT
        hit = (r == lp[0:1]) | (r == lp[1:2]) | (r == lp[2:3]) | (r == lp[3:4])
        rows = _dot(jnp.where(hit, 1.0, 0.0).astype(BF16), h)
        for q in range(ROW_TILE):
            stage[slot, pl.ds(c * TT * ROW_TILE + q, TT, stride=ROW_TILE), :] = rows[:, q * LANES:(q + 1) * LANES]

    for e in range(N_EXPERTS):
        src = runs_ref[j * 8 + RUN_LOFF, e]
        dst = runs_ref[j * 8 + RUN_START, e]

        def put(offset, size, src=src, dst=dst):
            pltpu.make_async_copy(stage.at[slot, _rows(src + offset, size), :],
                                  xs_ref.at[_rows(dst + offset, size), :], sem.at[slot]).start()
        _for_each_piece(runs_ref[j * 8 + RUN_LEN, e], TT_LOG2, put)

    @pl.when(j == pl.num_programs(0) - 1)
    def _():
        wait_tile(1 - slot)
        wait_tile(slot)


def _dispatch(runs, plan, lp, h2, p_rows):
    t = h2.shape[0]
    return pl.pallas_call(
        _dispatch_kernel,
        grid_spec=pltpu.PrefetchScalarGridSpec(
            num_scalar_prefetch=2,
            grid=(t // TT,),
            in_specs=[
                pl.BlockSpec((8, TT), lambda j, *_: (0, j)),
                pl.BlockSpec((TT, D_MODEL), lambda j, *_: (j, 0)),
            ],
            out_specs=pl.BlockSpec(memory_space=pl.ANY),
            scratch_shapes=[
                pltpu.VMEM((2, STAGE_ROWS * ROW_TILE, LANES), F32),
                pltpu.VMEM((TM // 2 * ROW_TILE, LANES), F32),
                pltpu.SemaphoreType.DMA((2,)),
                pltpu.SemaphoreType.DMA((ZERO_DEPTH,)),
            ],
        ),
        out_shape=jax.ShapeDtypeStruct((p_rows * ROW_TILE, LANES), F32),
        compiler_params=pltpu.CompilerParams(
            dimension_semantics=("arbitrary",), vmem_limit_bytes=VMEM_LIMIT),
        name="dispatch",
    )(runs, plan, lp, h2)


def _experts_kernel(plan_ref, x_ref, wg_ref, bg_ref, wu_ref, bu_ref, wd_ref, bd_ref,
                    o_ref, act, wbuf, wg_b, wu_b, wd_b, wsem):
    i = pl.program_id(0)
    prev = jnp.maximum(i - 1, 0)

    def fetch(expert, s):
        return [pltpu.make_async_copy(w_ref.at[expert], wbuf.at[s, m], wsem.at[s])
                for m, w_ref in enumerate((wg_ref, wu_ref, wd_ref))]

    n_valid = plan_ref[PLAN_NVALID, 0]

    @pl.when(i >= n_valid)
    def _():
        o_ref[...] = jnp.zeros_like(o_ref)

    @pl.when(i < n_valid)
    def _():
        expert = plan_ref[PLAN_EXPERT, i]
        s = plan_ref[PLAN_PARITY, i]
        nxt = plan_ref[PLAN_NEXT, i]

        @pl.when(i == 0)
        def _():
            for cp in fetch(expert, s):
                cp.start()

        @pl.when((i == 0) | (expert != plan_ref[PLAN_EXPERT, prev]))
        def _():
            for cp in fetch(expert, s):
                cp.wait()
            for c in range(N_CHUNKS):
                cs = slice(c * N_COLS, (c + 1) * N_COLS)
                wg_b[c] = wbuf[s, 0, :, cs].astype(BF16)
                wu_b[c] = wbuf[s, 1, :, cs].astype(BF16)
                wd_b[c] = wbuf[s, 2, :, cs].astype(BF16)

            @pl.when(nxt != expert)
            def _():
                for cp in fetch(nxt, 1 - s):
                    cp.start()

        xb = _load_row_tiles(x_ref, TM).astype(BF16)
        for n in range(N_CHUNKS):
            gate = _dot(xb, wg_b[n]) + bg_ref[n:n + 1, :]
            up = _dot(xb, wu_b[n]) + bu_ref[n:n + 1, :]
            gate = jnp.minimum(gate, SWIGLU_LIMIT)
            up = jnp.clip(up, -SWIGLU_LIMIT, SWIGLU_LIMIT)
            act[n] = ((up + 1.0) * (gate * jax.nn.sigmoid(SWIGLU_ALPHA * gate))).astype(BF16)
        a = jnp.concatenate([act[c] for c in range(N_CHUNKS)], axis=1)
        for n in range(N_CHUNKS):
            out = _dot(a, wd_b[n]) + bd_ref[n:n + 1, :]
            for q in range(N_COLS // LANES):
                o_ref[pl.ds(n * (N_COLS // LANES) + q, TM, stride=ROW_TILE), :] = (
                    out[:, q * LANES:(q + 1) * LANES])


def _experts(plan, xs, wg, bg, wu, bu, wd, bd):
    n_tiles = xs.shape[0] // (TM * ROW_TILE)

    def x_map(i, plan_ref):
        return (jnp.minimum(i, plan_ref[PLAN_NVALID, 0] - 1), 0)

    def b_map(i, plan_ref):
        return (plan_ref[PLAN_EXPERT, i], 0, 0)

    w_spec = pl.BlockSpec(memory_space=pl.ANY)
    b_spec = pl.BlockSpec((None, N_CHUNKS, N_COLS), b_map)
    return pl.pallas_call(
        _experts_kernel,
        grid_spec=pltpu.PrefetchScalarGridSpec(
            num_scalar_prefetch=1,
            grid=(n_tiles,),
            in_specs=[pl.BlockSpec((TM * ROW_TILE, LANES), x_map),
                      w_spec, b_spec, w_spec, b_spec, w_spec, b_spec],
            out_specs=pl.BlockSpec((TM * ROW_TILE, LANES), lambda i, plan_ref: (i, 0)),
            scratch_shapes=[
                pltpu.VMEM((N_CHUNKS, TM, N_COLS), BF16),
                pltpu.VMEM((2, 3, D_MODEL, D_MODEL), F32),
                pltpu.VMEM((N_CHUNKS, D_MODEL, N_COLS), BF16),
                pltpu.VMEM((N_CHUNKS, D_MODEL, N_COLS), BF16),
                pltpu.VMEM((N_CHUNKS, D_MODEL, N_COLS), BF16),
                pltpu.SemaphoreType.DMA((2,)),
            ],
        ),
        out_shape=jax.ShapeDtypeStruct(xs.shape, F32),
        compiler_params=pltpu.CompilerParams(
            dimension_semantics=("arbitrary",), vmem_limit_bytes=VMEM_LIMIT),
        name="experts",
    )(plan, xs, wg, bg, wu, bu, wd, bd)


def _combine_kernel(runs_ref, info_ref, x1_ref, fw_ref, eo_ref, out_ref, stage, sem):
    j = pl.program_id(0)
    slot = lax.rem(j, 3)
    last = pl.num_programs(0) - 1

    def fetch(tile, s):
        for e in range(N_EXPERTS):
            src = runs_ref[tile * 8 + RUN_START, e]
            dst = runs_ref[tile * 8 + RUN_LOFF, e]

            def get(offset, size, src=src, dst=dst):
                pltpu.make_async_copy(eo_ref.at[_rows(src + offset, size), :],
                                      stage.at[s, _rows(dst + offset, size), :], sem.at[s]).start()
            _for_each_piece(runs_ref[tile * 8 + RUN_LEN, e], TT_LOG2, get)

    @pl.when(j == 0)
    def _():
        fetch(0, 0)
        fetch(jnp.minimum(1, last), 1)

    _wait_rows(stage.at[slot], eo_ref, sem.at[slot], to_hbm=False)

    info = info_ref[...]
    y = x1_ref[...]
    for c in range(STAGE_ROWS // TT):
        r = (_iota((TT, TT), 1) + c * TT).astype(F32)
        g = jnp.zeros((TT, TT), F32)
        for k in range(TOP_K):
            g = g + jnp.where(r == info[:, TOP_K + k:TOP_K + k + 1], info[:, k:k + 1], 0.0)
        g_hi = g.astype(BF16)
        g_lo = (g - g_hi.astype(F32)).astype(BF16)
        rows = jnp.concatenate(
            [stage[slot, pl.ds(c * TT * ROW_TILE + q, TT, stride=ROW_TILE), :] for q in range(ROW_TILE)],
            axis=1).astype(BF16)
        y = y + _dot(g_hi, rows) + _dot(g_lo, rows)
    ms = jnp.mean(y * y, axis=-1, keepdims=True)
    out_ref[...] = y * lax.rsqrt(ms + EPS) * fw_ref[...]

    ahead = lax.rem(j + 2, 3)
    fetch(jnp.minimum(j + 2, last), ahead)

    @pl.when(j == last)
    def _():
        _wait_rows(stage.at[lax.rem(j + 1, 3)], eo_ref, sem.at[lax.rem(j + 1, 3)], to_hbm=False)
        _wait_rows(stage.at[ahead], eo_ref, sem.at[ahead], to_hbm=False)


def _combine(runs, info, x1, final_w, eo):
    t = x1.shape[0]
    return pl.pallas_call(
        _combine_kernel,
        grid_spec=pltpu.PrefetchScalarGridSpec(
            num_scalar_prefetch=1,
            grid=(t // TT,),
            in_specs=[
                pl.BlockSpec((TT, LANES), lambda j, *_: (j, 0)),
                pl.BlockSpec((TT, D_MODEL), lambda j, *_: (j, 0)),
                pl.BlockSpec((1, D_MODEL), lambda j, *_: (0, 0)),
                pl.BlockSpec(memory_space=pl.ANY),
            ],
            out_specs=pl.BlockSpec((TT, D_MODEL), lambda j, *_: (j, 0)),
            scratch_shapes=[
                pltpu.VMEM((3, STAGE_ROWS * ROW_TILE, LANES), F32),
                pltpu.SemaphoreType.DMA((3,)),
            ],
        ),
        out_shape=jax.ShapeDtypeStruct((t, D_MODEL), F32),
        compiler_params=pltpu.CompilerParams(
            dimension_semantics=("arbitrary",), vmem_limit_bytes=VMEM_LIMIT),
        name="combine",
    )(runs, info, x1, final_w, eo)


def _pad_lanes(v, offset, fill=0.0):
    row = jnp.full((1, LANES), fill, F32)
    return row.at[0, offset:offset + v.shape[0]].set(v.astype(F32))


def kernel(x, norm_mix_w, w_in, gla_w_alpha_up, gla_b_alpha, gla_norm_w, ssd_conv_w, ssd_conv_b,
           ssd_dt_bias, ssd_A_log, ssd_D, ssd_norm_w, w_out, norm_ffn_w, router_w, router_b,
           moe_w_gate, moe_b_gate, moe_w_up, moe_b_up, moe_w_down, moe_b_down, final_norm_w):
    bsz, seqlen, d = x.shape
    t = bsz * seqlen
    depth = w_in.shape[0]
    assert depth == 1, "the final RMSNorm is fused into the (single) layer's combine step"
    p_rows = t * TOP_K + N_EXPERTS * TM
    n_tiles = p_rows // TM
    for l in range(depth):
        w = w_in[l]
        w_all = jnp.concatenate(
            [w[:, 0:1536], w[:, 1552:3088], w[:, 1536:1552], w[:, 3088:3096],
             jnp.zeros((d, N_SMALL - GLA_GATE_RANK - SSD_HEADS), w.dtype)], axis=1).astype(BF16)
        wup = jnp.zeros((N_SMALL, GLA_KW), F32).at[0:GLA_GATE_RANK].set(gla_w_alpha_up[l])
        dtb = _pad_lanes(ssd_dt_bias[l], DT_COL)
        aneg = _pad_lanes(-jnp.exp(ssd_A_log[l].astype(F32)), DT_COL)
        dexp = jnp.repeat(ssd_D[l].astype(F32), SSD_HEADDIM)[None, :]
        rw = jnp.zeros((LANES, d), F32).at[0:N_EXPERTS].set(router_w[l].T)
        rb = jnp.zeros((LANES, TL), F32).at[0:N_EXPERTS].set(
            jnp.broadcast_to(router_b[l].astype(F32)[:, None], (N_EXPERTS, TL)))

        x1, h2, logits, counts = _mixer(
            x, norm_mix_w[l][None, :], w_all, wup, gla_b_alpha[l][None, :], gla_norm_w[l][None, :], ssd_conv_w[l],
            ssd_conv_b[l][None, :], dtb, aneg, dexp, ssd_norm_w[l][None, :],
            w_out[l].astype(BF16), norm_ffn_w[l][None, :], rw, rb)

        info, lp, runs = _route(logits, counts)
        plan = _plan(counts, n_tiles, p_rows)
        xs = _dispatch(runs, plan, lp, h2.reshape(t, d), p_rows)
        eo = _experts(plan, xs,
                      moe_w_gate[l], moe_b_gate[l].reshape(N_EXPERTS, N_CHUNKS, N_COLS),
                      moe_w_up[l], moe_b_up[l].reshape(N_EXPERTS, N_CHUNKS, N_COLS),
                      moe_w_down[l], moe_b_down[l].reshape(N_EXPERTS, N_CHUNKS, N_COLS))
        x = _combine(runs, info, x1.reshape(t, d), final_norm_w[None, :], eo).reshape(bsz, seqlen, d)
    return x
```

```python
import functools

import jax
import jax.numpy as jnp
from jax import lax
from jax.experimental import pallas as pl
from jax.experimental.pallas import tpu as pltpu

F32 = jnp.float32
BF16 = jnp.bfloat16

D_MODEL = 1024
GLA_WIDTH = 512
GLA_HEADS = 4
GLA_DV = 128
GLA_DK = 64
GLA_KW = 256
GLA_GATE_RANK = 16
GLA_GATE_NORM = 16.0
SSD_WIDTH = 512
SSD_HEADDIM = 64
SSD_HEADS = 8
SSD_GROUPS = 2
SSD_HPG = 4
SSD_STATE = 128
SSD_CONV = 4
SSD_CONV_CH = 1024
N_EXPERTS = 32
TOP_K = 4
SWIGLU_LIMIT = 7.0
SWIGLU_ALPHA = 1.702
EPS = 1e-6
GROUP_EPS = 1e-5

LANES = 128
ROW_TILE = D_MODEL // LANES
N_MAIN = 3072
N_SMALL = LANES
DT_COL = GLA_GATE_RANK

GLA_CHUNK = 64
SSD_CHUNK = 128
PROJ_COLS = 512
TL = 256
TT_LOG2 = 8
TT = 1 << TT_LOG2
STAGE_ROWS = 4 * TT
TM_LOG2 = 9
TM = 1 << TM_LOG2
WAIT_ROWS = 512
ZERO_DEPTH = 4
RUN_START, RUN_LEN, RUN_LOFF = 0, 1, 2
PLAN_EXPERT, PLAN_NEXT, PLAN_PARITY, PLAN_NVALID, PLAN_PAD_START, PLAN_PAD_LEN = 0, 1, 2, 3, 4, 5
PLAN_COLS = 256
MAX_TILE_COPY_BYTES = 4 * 1024 * 1024
assert STAGE_ROWS * D_MODEL * 4 <= MAX_TILE_COPY_BYTES
N_COLS = 256
N_CHUNKS = D_MODEL // N_COLS
VMEM_LIMIT = 56 * 1024 * 1024


def _dot(a, b):
    return jnp.dot(a, b, preferred_element_type=F32)


def _dot_nt(a, b):
    return lax.dot_general(a, b, (((1,), (1,)), ((), ())), preferred_element_type=F32)


def _dot_tn(a, b):
    return lax.dot_general(a, b, (((0,), (0,)), ((), ())), preferred_element_type=F32)


def _split3(a):
    hi = a.astype(BF16)
    r1 = a - hi.astype(F32)
    mid = r1.astype(BF16)
    lo = (r1 - mid.astype(F32)).astype(BF16)
    return hi, mid, lo


def _dot_sel_lhs(sel, a):
    hi, mid, lo = _split3(a)
    return _dot(sel, hi) + _dot(sel, mid) + _dot(sel, lo)


def _dot_sel_rhs(a, sel, terms=3):
    parts = _split3(a)[:terms]
    out = _dot(parts[0], sel)
    for p in parts[1:]:
        out = out + _dot(p, sel)
    return out


def _dot_hi(a, b):
    a_hi = a.astype(BF16)
    a_lo = (a - a_hi.astype(F32)).astype(BF16)
    b_hi = b.astype(BF16)
    b_lo = (b - b_hi.astype(F32)).astype(BF16)
    return _dot(a_hi, b_hi) + _dot(a_lo, b_hi) + _dot(a_hi, b_lo)


def _dot_hi_nt(a, b):
    a_hi = a.astype(BF16)
    a_lo = (a - a_hi.astype(F32)).astype(BF16)
    b_hi = b.astype(BF16)
    b_lo = (b - b_hi.astype(F32)).astype(BF16)
    return _dot_nt(a_hi, b_hi) + _dot_nt(a_lo, b_hi) + _dot_nt(a_hi, b_lo)


def _softplus(x):
    return jnp.maximum(x, 0.0) + jnp.log1p(jnp.exp(-jnp.abs(x)))


def _silu(x):
    return x * jax.nn.sigmoid(x)


def _iota(shape, dim):
    return lax.broadcasted_iota(jnp.int32, shape, dim)


def _load_row_tiles(ref, rows):
    return jnp.concatenate([ref[pl.ds(j, rows, stride=ROW_TILE), :] for j in range(ROW_TILE)], axis=1)


def _project_parts(x_ref, nmw_ref, win_ref, pm_ref, small_ref):
    state = {}

    def norm():
        x_in = x_ref[...]
        ms = jnp.mean(x_in * x_in, axis=-1, keepdims=True)
        state["h"] = (x_in * lax.rsqrt(ms + EPS) * nmw_ref[...]).astype(BF16)

    def chunk(n0):
        def run():
            pm_ref[:, n0:n0 + PROJ_COLS] = _dot(state["h"], win_ref[:, n0:n0 + PROJ_COLS]).astype(BF16)
        return run

    def small():
        small_ref[...] = _dot(state["h"], win_ref[:, N_MAIN:N_MAIN + N_SMALL])

    return [norm] + [chunk(n0) for n0 in range(0, N_MAIN, PROJ_COLS)] + [small]


def _mixer_kernel(x_ref, xn_ref, nmw_ref, win_ref, *refs, tiles_per_row):
    params, outs = refs[:13], refs[13:17]
    gla_state, ssd_state, conv_tail, mix_scr, cnt_scr, pm_a, pm_b, small_a, small_b = refs[17:]
    g = pl.program_id(0)

    @pl.when(g == 0)
    def _():
        cnt_scr[...] = jnp.zeros_like(cnt_scr)
        for part in _project_parts(x_ref, nmw_ref, win_ref, pm_a, small_a):
            part()

    @pl.when(lax.rem(g, tiles_per_row) == 0)
    def _():
        gla_state[...] = jnp.zeros_like(gla_state)
        ssd_state[...] = jnp.zeros_like(ssd_state)
        conv_tail[...] = jnp.zeros_like(conv_tail)

    for parity, (pm_cur, small_cur, pm_nxt, small_nxt) in enumerate(
            ((pm_a, small_a, pm_b, small_b), (pm_b, small_b, pm_a, small_a))):
        @pl.when((g & 1) == parity)
        def _(pm_cur=pm_cur, small_cur=small_cur, pm_nxt=pm_nxt, small_nxt=small_nxt):
            _mixer_tile(_project_parts(xn_ref, nmw_ref, win_ref, pm_nxt, small_nxt),
                        pm_cur, small_cur, x_ref, *params, *outs,
                        gla_state, ssd_state, conv_tail, mix_scr, cnt_scr)


def _mixer_tile(side_work, pm_ref, small_ref, x_ref, wup_ref, balpha_ref, gnw_ref, convw_ref, convb_ref,
                dtb_ref, aneg_ref, dexp_ref, snw_ref, wout_ref, nfw_ref, rw_ref, rb_ref,
                x1_ref, h2_ref, lg_ref, cnt_ref,
                gla_state, ssd_state, conv_tail, mix_scr, cnt_scr):
    side_work = list(side_work)

    def run_side(n=1):
        for _ in range(min(n, len(side_work))):
            side_work.pop(0)()

    small = small_ref[...]

    row = _iota((TL, TL), 0)
    col = _iota((TL, TL), 1)
    causal = col <= row
    cum64 = jnp.where(causal & ((row // GLA_CHUNK) == (col // GLA_CHUNK)), 1.0, 0.0).astype(BF16)
    cum128 = jnp.where(causal & ((row // SSD_CHUNK) == (col // SSD_CHUNK)), 1.0, 0.0).astype(BF16)

    xa = _dot_hi(small, wup_ref[...]) + balpha_ref[...]
    log_a = (jnp.minimum(xa, 0.0) - jnp.log1p(jnp.exp(-jnp.abs(xa)))) * (1.0 / GLA_GATE_NORM)
    bcum = _dot_sel_lhs(cum64, log_a)

    lane_kw = _iota((GLA_CHUNK, GLA_KW), 1)
    head_masks = [(lane_kw // GLA_DK) == h for h in range(GLA_HEADS)]
    lane_kw_s = _iota((GLA_DV, GLA_KW), 1)
    head_masks_s = [(lane_kw_s // GLA_DK) == h for h in range(GLA_HEADS)]
    tril64 = _iota((GLA_CHUNK, GLA_CHUNK), 1) <= _iota((GLA_CHUNK, GLA_CHUNK), 0)
    q_scale = GLA_DK ** -0.5

    def gla_chunk(c):
        rs = slice(c * GLA_CHUNK, (c + 1) * GLA_CHUNK)
        bc = bcum[rs]
        b_mid = bc[GLA_CHUNK // 2:GLA_CHUNK // 2 + 1]
        b_last = bc[GLA_CHUNK - 1:GLA_CHUNK]
        qc = pm_ref[rs, 0:GLA_KW].astype(F32) * q_scale
        kc = pm_ref[rs, GLA_KW:2 * GLA_KW].astype(F32)
        vc = pm_ref[rs, 2 * GLA_KW:2 * GLA_KW + GLA_WIDTH]
        q_in = (qc * jnp.exp(bc - b_mid)).astype(BF16)
        k_in = (kc * jnp.exp(b_mid - bc)).astype(BF16)
        q_st = (qc * jnp.exp(bc)).astype(BF16)
        k_st = (kc * jnp.exp(b_last - bc)).astype(BF16)
        st = gla_state[...]
        st_b = st.astype(BF16)
        zero_b = jnp.zeros_like(q_in)
        for h in range(GLA_HEADS):
            scores = _dot_nt(jnp.where(head_masks[h], q_in, zero_b), k_in)
            scores = jnp.where(tril64, scores, 0.0).astype(BF16)
            o_h = _dot(scores, vc[:, h * GLA_DV:(h + 1) * GLA_DV])
            o_h = o_h + _dot_nt(jnp.where(head_masks[h], q_st, zero_b), st_b)
            mix_scr[rs, h * GLA_DV:(h + 1) * GLA_DV] = o_h
        upd = _dot_tn(vc, k_st)
        new_st = st * jnp.exp(b_last)
        for h in range(GLA_HEADS):
            new_st = new_st + jnp.where(head_masks_s[h], upd[h * GLA_DV:(h + 1) * GLA_DV], 0.0)
        gla_state[...] = new_st
        run_side()

    xbc = pm_ref[:, 2048:3072].astype(F32)
    tail = conv_tail[...]
    conv_tail[...] = xbc[TL - 8:TL]
    row8 = _iota((8, SSD_CONV_CH), 0)
    conv = xbc * convw_ref[SSD_CONV - 1:SSD_CONV, :]
    for s in range(1, SSD_CONV):
        shifted = pltpu.roll(xbc, s, 0)
        head = jnp.where(row8 < s, pltpu.roll(tail, s, 0), shifted[0:8])
        shifted = jnp.concatenate([head, shifted[8:]], axis=0)
        conv = conv + shifted * convw_ref[SSD_CONV - 1 - s:SSD_CONV - s, :]
    act = _silu(conv + convb_ref[...])
    run_side()
    xs = act[:, 0:SSD_WIDTH]
    bm = act[:, SSD_WIDTH:SSD_WIDTH + SSD_GROUPS * SSD_STATE].astype(BF16)
    cm = act[:, SSD_WIDTH + SSD_GROUPS * SSD_STATE:].astype(BF16)

    dt_full = _softplus(small + dtb_ref[...])
    a_full = dt_full * aneg_ref[...]
    acum = _dot_sel_lhs(cum128, a_full)
    acum_t = acum.T

    e_row = _iota((N_SMALL, SSD_WIDTH), 0)
    e_col = _iota((N_SMALL, SSD_WIDTH), 1)
    spread64 = jnp.where(e_row == DT_COL + e_col // SSD_HEADDIM, 1.0, 0.0).astype(BF16)
    e_row2 = _iota((N_SMALL, SSD_HEADS * LANES), 0)
    e_col2 = _iota((N_SMALL, SSD_HEADS * LANES), 1)
    spread128 = jnp.where(e_row2 == DT_COL + e_col2 // LANES, 1.0, 0.0).astype(BF16)
    dt_e = _dot_sel_rhs(dt_full, spread64, terms=1)
    ac_e = _dot_sel_rhs(acum, spread64, terms=2)
    ac_w = _dot_sel_rhs(acum, spread128, terms=2)

    tril128 = _iota((SSD_CHUNK, SSD_CHUNK), 1) <= _iota((SSD_CHUNK, SSD_CHUNK), 0)
    lane_g = _iota((SSD_CHUNK, SSD_HPG * SSD_HEADDIM), 1)
    def ssd_chunk(c):
        rs = slice(c * SSD_CHUNK, (c + 1) * SSD_CHUNK)
        ac_c = ac_e[rs]
        a_last = ac_c[SSD_CHUNK - 1:SSD_CHUNK]
        dt_c = dt_e[rs]
        xs_c = xs[rs]
        x_dt = (xs_c * dt_c).astype(BF16)
        x_w = (xs_c * (jnp.exp(a_last - ac_c) * dt_c)).astype(BF16)
        e_ac = jnp.exp(ac_c)
        for g in range(SSD_GROUPS):
            gs = slice(g * SSD_STATE, (g + 1) * SSD_STATE)
            ws = slice(g * SSD_HPG * SSD_HEADDIM, (g + 1) * SSD_HPG * SSD_HEADDIM)
            c_g = cm[rs, gs]
            b_g = bm[rs, gs]
            cb = _dot_nt(c_g, b_g)
            x_dt_g = x_dt[:, ws]
            lhs_parts = []
            rhs_parts = []
            for hh in range(SSD_HPG):
                h = g * SSD_HPG + hh
                seg = ac_w[rs, h * LANES:(h + 1) * LANES] - acum_t[DT_COL + h:DT_COL + h + 1, rs]
                lmat = jnp.where(tril128, jnp.exp(jnp.where(tril128, seg, 0.0)), 0.0)
                lhs_parts.append((cb * lmat).astype(BF16))
                rhs_parts.append(jnp.where((lane_g // SSD_HEADDIM) == hh, x_dt_g,
                                           jnp.zeros_like(x_dt_g)))
            intra = _dot(jnp.concatenate(lhs_parts, axis=1), jnp.concatenate(rhs_parts, axis=0))
            st = ssd_state[g]
            inter = _dot(c_g, st.astype(BF16)) * e_ac[:, ws]
            mix_scr[rs, GLA_WIDTH + g * 256:GLA_WIDTH + (g + 1) * 256] = intra + inter
            ssd_state[g] = st * jnp.exp(a_last[:, ws]) + _dot_tn(b_g, x_w[:, ws])
        run_side()

    gla_per_ssd = SSD_CHUNK // GLA_CHUNK
    for c in range(TL // SSD_CHUNK):
        for cc in range(gla_per_ssd):
            gla_chunk(c * gla_per_ssd + cc)
        ssd_chunk(c)
    run_side(len(side_work))

    o = mix_scr[:, 0:GLA_WIDTH]
    g_gate = _silu(pm_ref[:, 1024:1536].astype(F32))
    gla_parts = []
    for h in range(GLA_HEADS):
        o_h = o[:, h * GLA_DV:(h + 1) * GLA_DV]
        ms = jnp.mean(o_h * o_h, axis=-1, keepdims=True)
        gla_parts.append(o_h * lax.rsqrt(ms + GROUP_EPS))
    gla_out = jnp.concatenate(gla_parts, axis=1) * gnw_ref[...] * g_gate

    y = mix_scr[:, GLA_WIDTH:] + dexp_ref[...] * xs
    y = y * _silu(pm_ref[:, 1536:2048].astype(F32))
    ssd_parts = []
    for g in range(SSD_GROUPS):
        y_g = y[:, g * 256:(g + 1) * 256]
        ms = jnp.mean(y_g * y_g, axis=-1, keepdims=True)
        ssd_parts.append(y_g * lax.rsqrt(ms + GROUP_EPS))
    ssd_out = jnp.concatenate(ssd_parts, axis=1) * snw_ref[...]

    mixed = jnp.concatenate([gla_out, ssd_out], axis=1).astype(BF16)
    x1 = x_ref[...] + _dot(mixed, wout_ref[...])
    x1_ref[...] = x1

    ms = jnp.mean(x1 * x1, axis=-1, keepdims=True)
    h2 = x1 * lax.rsqrt(ms + EPS) * nfw_ref[...]
    h2_ref[...] = h2.astype(BF16)
    lg = _dot_hi_nt(rw_ref[...], h2) + rb_ref[...]
    lg_ref[...] = lg
    cnt_scr[...] = cnt_scr[...] + _tile_counts(_top4(lg[0:N_EXPERTS])[0])[1]
    cnt_ref[...] = cnt_scr[...]


def _mixer(x, nmw, w_all, wup, balpha, gnw, convw, convb, dtb, aneg, dexp, snw, wout, nfw, rw, rb):
    bsz, seqlen, _ = x.shape
    per_row = seqlen // TL
    steps = bsz * per_row

    def full(a):
        return pl.BlockSpec(a.shape, lambda g: (0,) * a.ndim)

    def tile(ahead):
        def index(g):
            tile_id = jnp.minimum(g + ahead, steps - 1)
            return (tile_id // per_row, tile_id % per_row, 0)
        return pl.BlockSpec((None, TL, D_MODEL), index)

    params = (nmw, w_all, wup, balpha, gnw, convw, convb, dtb, aneg, dexp, snw, wout, nfw, rw, rb)
    return pl.pallas_call(
        functools.partial(_mixer_kernel, tiles_per_row=per_row),
        grid=(steps,),
        in_specs=[tile(0), tile(1)] + [full(p) for p in params],
        out_specs=[tile(0), tile(0),
                   pl.BlockSpec((LANES, TL), lambda g: (0, g)),
                   pl.BlockSpec((N_EXPERTS, LANES), lambda g: (0, 0))],
        out_shape=[
            jax.ShapeDtypeStruct((bsz, seqlen, D_MODEL), F32),
            jax.ShapeDtypeStruct((bsz, seqlen, D_MODEL), BF16),
            jax.ShapeDtypeStruct((LANES, bsz * seqlen), F32),
            jax.ShapeDtypeStruct((N_EXPERTS, LANES), F32),
        ],
        scratch_shapes=[
            pltpu.VMEM((GLA_DV, GLA_KW), F32),
            pltpu.VMEM((SSD_GROUPS, SSD_STATE, SSD_HPG * SSD_HEADDIM), F32),
            pltpu.VMEM((8, SSD_CONV_CH), F32),
            pltpu.VMEM((TL, D_MODEL), F32),
            pltpu.VMEM((N_EXPERTS, LANES), F32),
            pltpu.VMEM((TL, N_MAIN), BF16),
            pltpu.VMEM((TL, N_MAIN), BF16),
            pltpu.VMEM((TL, N_SMALL), F32),
            pltpu.VMEM((TL, N_SMALL), F32),
        ],
        compiler_params=pltpu.CompilerParams(
            dimension_semantics=("arbitrary",), vmem_limit_bytes=VMEM_LIMIT),
        name="mixer",
    )(x, x, *params)


def _top4(lg):
    n_e, n_t = lg.shape
    row = _iota((n_e, n_t), 0)
    work = lg
    onehots = []
    vals = []
    for _ in range(TOP_K):
        m = jnp.max(work, axis=0, keepdims=True)
        idx = jnp.min(jnp.where(work == m, row, n_e), axis=0, keepdims=True)
        oh = row == idx
        onehots.append(oh)
        vals.append(m)
        work = jnp.where(oh, -jnp.inf, work)
    return onehots, vals


def _tile_counts(onehots):
    multi = jnp.where(onehots[0] | onehots[1] | onehots[2] | onehots[3], 1.0, 0.0).astype(BF16)
    return multi, _dot(multi, jnp.ones((multi.shape[1], LANES), BF16))


def _route_kernel(lg_ref, cnt_ref, info_ref, lp_ref, runs_ref, run_scr):
    @pl.when(pl.program_id(0) == 0)
    def _():
        run_scr[...] = jnp.zeros_like(run_scr)

    onehots, vals = _top4(lg_ref[0:N_EXPERTS, :])
    multi, tile_cnt = _tile_counts(onehots)
    counts = cnt_ref[...]
    padded = jnp.floor((counts + (TM - 1)) * (1.0 / TM)) * TM
    lower = jnp.where(_iota((N_EXPERTS, N_EXPERTS), 1) < _iota((N_EXPERTS, N_EXPERTS), 0),
                      1.0, 0.0).astype(BF16)
    offs = _dot_sel_lhs(lower, padded)
    loff = _dot_sel_lhs(lower, tile_cnt)
    before = jnp.where(_iota((TT, TT), 0) < _iota((TT, TT), 1), 1.0, 0.0).astype(BF16)
    rank = _dot(multi, before)
    lane = _iota((N_EXPERTS, LANES), 1)
    table = jnp.where(lane == RUN_START, offs + run_scr[...],
                      jnp.where(lane == RUN_LEN, tile_cnt, jnp.where(lane == RUN_LOFF, loff, 0.0)))
    table = jnp.concatenate([table, jnp.zeros((LANES - N_EXPERTS, LANES), F32)], axis=0)
    runs_ref[...] = table.T[0:8].astype(jnp.int32)
    run_scr[...] = run_scr[...] + tile_cnt
    local = rank + jnp.concatenate([loff] * (TT // LANES), axis=1)
    exps = [jnp.exp(v - vals[0]) for v in vals]
    den = exps[0] + exps[1] + exps[2] + exps[3]
    lp_rows = [jnp.sum(jnp.where(oh, local, 0.0), axis=0, keepdims=True) for oh in onehots]
    lp_ref[...] = jnp.concatenate(lp_rows + [jnp.zeros((8 - TOP_K, TT), F32)], axis=0).astype(jnp.int32)
    gate_rows = [e / den for e in exps]
    info = jnp.concatenate(gate_rows + lp_rows + [jnp.zeros((LANES - 2 * TOP_K, TT), F32)], axis=0)
    info_ref[...] = info.T


def _route(logits_t, counts):
    t = logits_t.shape[1]
    steps = t // TT
    return pl.pallas_call(
        _route_kernel,
        grid=(steps,),
        in_specs=[pl.BlockSpec((LANES, TT), lambda i: (0, i)),
                  pl.BlockSpec((N_EXPERTS, LANES), lambda i: (0, 0))],
        out_specs=[
            pl.BlockSpec((TT, LANES), lambda i: (i, 0)),
            pl.BlockSpec((8, TT), lambda i: (0, i)),
            pl.BlockSpec((8, LANES), lambda i: (i, 0)),
        ],
        out_shape=[
            jax.ShapeDtypeStruct((t, LANES), F32),
            jax.ShapeDtypeStruct((8, t), jnp.int32),
            jax.ShapeDtypeStruct((steps * 8, LANES), jnp.int32),
        ],
        scratch_shapes=[pltpu.VMEM((N_EXPERTS, LANES), F32)],
        compiler_params=pltpu.CompilerParams(
            dimension_semantics=("arbitrary",), vmem_limit_bytes=VMEM_LIMIT),
        name="route",
    )(logits_t, counts)


def _rows(first, count):
    return pl.ds(pl.multiple_of(first * ROW_TILE, ROW_TILE), count * ROW_TILE)


def _wait_rows(stage_slot, hbm_ref, sem, to_hbm):
    for w in range(STAGE_ROWS // WAIT_ROWS):
        part = stage_slot.at[pl.ds(w * WAIT_ROWS * ROW_TILE, WAIT_ROWS * ROW_TILE), :]
        hbm = hbm_ref.at[pl.ds(0, WAIT_ROWS * ROW_TILE), :]
        src, dst = (part, hbm) if to_hbm else (hbm, part)
        pltpu.make_async_copy(src, dst, sem).wait()


def _for_each_piece(length, max_log2, fn):
    for b in reversed(range(max_log2 + 1)):
        size = 1 << b
        offset = lax.shift_left(lax.shift_right_logical(length, b + 1), b + 1)

        @pl.when((length & size) != 0)
        def _(offset=offset, size=size):
            fn(offset, size)


def _plan_kernel(cnt_ref, plan_ref, *, n_tiles, p_rows):
    def clear(i, c):
        for r in range(8):
            plan_ref[r, i] = 0
        return c
    lax.fori_loop(0, PLAN_COLS, clear, 0)

    def per_expert(e, carry):
        first_row, group = carry
        count = cnt_ref[e, 0].astype(jnp.int32)
        tiles = lax.shift_right_logical(count + (TM - 1), TM_LOG2)
        first_tile = lax.shift_right_logical(first_row, TM_LOG2)

        def mark(i, c):
            plan_ref[PLAN_EXPERT, i] = e
            plan_ref[PLAN_PARITY, i] = group & 1
            return c
        lax.fori_loop(first_tile, first_tile + tiles, mark, 0)
        plan_ref[PLAN_PAD_START, e] = first_row + count
        plan_ref[PLAN_PAD_LEN, e] = tiles * TM - count
        return first_row + tiles * TM, group + jnp.minimum(tiles, 1)

    used_rows, _ = lax.fori_loop(0, N_EXPERTS, per_expert, (jnp.int32(0), jnp.int32(0)))
    n_valid = lax.shift_right_logical(used_rows, TM_LOG2)
    plan_ref[PLAN_NVALID, 0] = n_valid
    plan_ref[PLAN_PAD_START, N_EXPERTS] = used_rows
    plan_ref[PLAN_PAD_LEN, N_EXPERTS] = lax.shift_right_logical(p_rows - used_rows, TM_LOG2 - 1)

    last_expert = plan_ref[PLAN_EXPERT, n_valid - 1]
    last_parity = plan_ref[PLAN_PARITY, n_valid - 1]

    def mark_unused(i, c):
        plan_ref[PLAN_EXPERT, i] = last_expert
        plan_ref[PLAN_PARITY, i] = last_parity
        return c
    lax.fori_loop(n_valid, n_tiles, mark_unused, 0)

    def next_expert(k, carry):
        expert_after, next_after = carry
        i = n_tiles - 1 - k
        e = plan_ref[PLAN_EXPERT, i]
        nxt = jnp.where(expert_after != e, expert_after, next_after)
        plan_ref[PLAN_NEXT, i] = nxt
        return e, nxt
    lax.fori_loop(0, n_tiles, next_expert, (last_expert, last_expert))


def _plan(counts, n_tiles, p_rows):
    assert n_tiles <= PLAN_COLS and N_EXPERTS < PLAN_COLS
    return pl.pallas_call(
        functools.partial(_plan_kernel, n_tiles=n_tiles, p_rows=p_rows),
        in_specs=[pl.BlockSpec(memory_space=pltpu.SMEM)],
        out_specs=pl.BlockSpec(memory_space=pltpu.SMEM),
        out_shape=jax.ShapeDtypeStruct((8, PLAN_COLS), jnp.int32),
        name="plan",
    )(counts)


def _dispatch_kernel(runs_ref, plan_ref, lp_ref, h_ref, xs_ref, stage, zeros, sem, zsem):
    j = pl.program_id(0)
    slot = j & 1

    def wait_tile(s):
        _wait_rows(stage.at[s], xs_ref, sem.at[s], to_hbm=True)

    @pl.when(j == 0)
    def _():
        zeros[...] = jnp.zeros_like(zeros)

        def pad_pieces(e, wait):
            def piece(offset, size):
                cp = pltpu.make_async_copy(
                    zeros.at[pl.ds(0, size * ROW_TILE), :],
                    xs_ref.at[_rows(plan_ref[PLAN_PAD_START, e] + offset, size), :],
                    zsem.at[e % ZERO_DEPTH])
                cp.wait() if wait else cp.start()
            _for_each_piece(plan_ref[PLAN_PAD_LEN, e], TM_LOG2 - 1, piece)

        for e in range(N_EXPERTS + ZERO_DEPTH - 1):
            if e < N_EXPERTS:
                pad_pieces(e, wait=False)
            if e >= ZERO_DEPTH - 1:
                pad_pieces(e - (ZERO_DEPTH - 1), wait=True)

        n_blocks = plan_ref[PLAN_PAD_LEN, N_EXPERTS]

        def block_copy(blk):
            first = plan_ref[PLAN_PAD_START, N_EXPERTS] + blk * (TM // 2)
            return pltpu.make_async_copy(zeros, xs_ref.at[_rows(first, TM // 2), :],
                                         zsem.at[lax.rem(blk, ZERO_DEPTH)])

        def put_block(blk, carry):
            block_copy(blk).start()

            @pl.when(blk >= ZERO_DEPTH - 1)
            def _():
                block_copy(blk - (ZERO_DEPTH - 1)).wait()
            return carry
        lax.fori_loop(0, n_blocks, put_block, 0)
        for back in range(ZERO_DEPTH - 1, 0, -1):
            @pl.when(n_blocks >= back)
            def _(back=back):
                block_copy(n_blocks - back).wait()

    @pl.when(j >= 2)
    def _():
        wait_tile(slot)

    h = h_ref[...]
    lp = lp_ref[...]
    for c in range(STAGE_ROWS // TT):
        r = _iota((TT, TT), 0) + c * TT
        hit = (r == lp[0:1]) | (r == lp[1:2]) | (r == lp[2:3]) | (r == lp[3:4])
        rows = _dot(jnp.where(hit, 1.0, 0.0).astype(BF16), h)
        for q in range(ROW_TILE):
            stage[slot, pl.ds(c * TT * ROW_TILE + q, TT, stride=ROW_TILE), :] = rows[:, q * LANES:(q + 1) * LANES]

    for e in range(N_EXPERTS):
        src = runs_ref[j * 8 + RUN_LOFF, e]
        dst = runs_ref[j * 8 + RUN_START, e]

        def put(offset, size, src=src, dst=dst):
            pltpu.make_async_copy(stage.at[slot, _rows(src + offset, size), :],
                                  xs_ref.at[_rows(dst + offset, size), :], sem.at[slot]).start()
        _for_each_piece(runs_ref[j * 8 + RUN_LEN, e], TT_LOG2, put)

    @pl.when(j == pl.num_programs(0) - 1)
    def _():
        wait_tile(1 - slot)
        wait_tile(slot)


def _dispatch(runs, plan, lp, h2, p_rows):
    t = h2.shape[0]
    return pl.pallas_call(
        _dispatch_kernel,
        grid_spec=pltpu.PrefetchScalarGridSpec(
            num_scalar_prefetch=2,
            grid=(t // TT,),
            in_specs=[
                pl.BlockSpec((8, TT), lambda j, *_: (0, j)),
                pl.BlockSpec((TT, D_MODEL), lambda j, *_: (j, 0)),
            ],
            out_specs=pl.BlockSpec(memory_space=pl.ANY),
            scratch_shapes=[
                pltpu.VMEM((2, STAGE_ROWS * ROW_TILE, LANES), F32),
                pltpu.VMEM((TM // 2 * ROW_TILE, LANES), F32),
                pltpu.SemaphoreType.DMA((2,)),
                pltpu.SemaphoreType.DMA((ZERO_DEPTH,)),
            ],
        ),
        out_shape=jax.ShapeDtypeStruct((p_rows * ROW_TILE, LANES), F32),
        compiler_params=pltpu.CompilerParams(
            dimension_semantics=("arbitrary",), vmem_limit_bytes=VMEM_LIMIT),
        name="dispatch",
    )(runs, plan, lp, h2)


def _experts_kernel(plan_ref, x_ref, wg_ref, bg_ref, wu_ref, bu_ref, wd_ref, bd_ref,
                    o_ref, act, wbuf, wg_b, wu_b, wd_b, wsem):
    i = pl.program_id(0)
    prev = jnp.maximum(i - 1, 0)

    def fetch(expert, s):
        return [pltpu.make_async_copy(w_ref.at[expert], wbuf.at[s, m], wsem.at[s])
                for m, w_ref in enumerate((wg_ref, wu_ref, wd_ref))]

    n_valid = plan_ref[PLAN_NVALID, 0]

    @pl.when(i >= n_valid)
    def _():
        o_ref[...] = jnp.zeros_like(o_ref)

    @pl.when(i < n_valid)
    def _():
        expert = plan_ref[PLAN_EXPERT, i]
        s = plan_ref[PLAN_PARITY, i]
        nxt = plan_ref[PLAN_NEXT, i]

        @pl.when(i == 0)
        def _():
            for cp in fetch(expert, s):
                cp.start()

        @pl.when((i == 0) | (expert != plan_ref[PLAN_EXPERT, prev]))
        def _():
            for cp in fetch(expert, s):
                cp.wait()
            for c in range(N_CHUNKS):
                cs = slice(c * N_COLS, (c + 1) * N_COLS)
                wg_b[c] = wbuf[s, 0, :, cs].astype(BF16)
                wu_b[c] = wbuf[s, 1, :, cs].astype(BF16)
                wd_b[c] = wbuf[s, 2, :, cs].astype(BF16)

            @pl.when(nxt != expert)
            def _():
                for cp in fetch(nxt, 1 - s):
                    cp.start()

        xb = _load_row_tiles(x_ref, TM).astype(BF16)
        for n in range(N_CHUNKS):
            gate = _dot(xb, wg_b[n]) + bg_ref[n:n + 1, :]
            up = _dot(xb, wu_b[n]) + bu_ref[n:n + 1, :]
            gate = jnp.minimum(gate, SWIGLU_LIMIT)
            up = jnp.clip(up, -SWIGLU_LIMIT, SWIGLU_LIMIT)
            act[n] = ((up + 1.0) * (gate * jax.nn.sigmoid(SWIGLU_ALPHA * gate))).astype(BF16)
        a = jnp.concatenate([act[c] for c in range(N_CHUNKS)], axis=1)
        for n in range(N_CHUNKS):
            out = _dot(a, wd_b[n]) + bd_ref[n:n + 1, :]
            for q in range(N_COLS // LANES):
                o_ref[pl.ds(n * (N_COLS // LANES) + q, TM, stride=ROW_TILE), :] = (
                    out[:, q * LANES:(q + 1) * LANES])


def _experts(plan, xs, wg, bg, wu, bu, wd, bd):
    n_tiles = xs.shape[0] // (TM * ROW_TILE)

    def x_map(i, plan_ref):
        return (jnp.minimum(i, plan_ref[PLAN_NVALID, 0] - 1), 0)

    def b_map(i, plan_ref):
        return (plan_ref[PLAN_EXPERT, i], 0, 0)

    w_spec = pl.BlockSpec(memory_space=pl.ANY)
    b_spec = pl.BlockSpec((None, N_CHUNKS, N_COLS), b_map)
    return pl.pallas_call(
        _experts_kernel,
        grid_spec=pltpu.PrefetchScalarGridSpec(
            num_scalar_prefetch=1,
            grid=(n_tiles,),
            in_specs=[pl.BlockSpec((TM * ROW_TILE, LANES), x_map),
                      w_spec, b_spec, w_spec, b_spec, w_spec, b_spec],
            out_specs=pl.BlockSpec((TM * ROW_TILE, LANES), lambda i, plan_ref: (i, 0)),
            scratch_shapes=[
                pltpu.VMEM((N_CHUNKS, TM, N_COLS), BF16),
                pltpu.VMEM((2, 3, D_MODEL, D_MODEL), F32),
                pltpu.VMEM((N_CHUNKS, D_MODEL, N_COLS), BF16),
                pltpu.VMEM((N_CHUNKS, D_MODEL, N_COLS), BF16),
                pltpu.VMEM((N_CHUNKS, D_MODEL, N_COLS), BF16),
                pltpu.SemaphoreType.DMA((2,)),
            ],
        ),
        out_shape=jax.ShapeDtypeStruct(xs.shape, F32),
        compiler_params=pltpu.CompilerParams(
            dimension_semantics=("arbitrary",), vmem_limit_bytes=VMEM_LIMIT),
        name="experts",
    )(plan, xs, wg, bg, wu, bu, wd, bd)


def _combine_kernel(runs_ref, info_ref, x1_ref, fw_ref, eo_ref, out_ref, stage, sem):
    j = pl.program_id(0)
    slot = lax.rem(j, 3)
    last = pl.num_programs(0) - 1

    def fetch(tile, s):
        for e in range(N_EXPERTS):
            src = runs_ref[tile * 8 + RUN_START, e]
            dst = runs_ref[tile * 8 + RUN_LOFF, e]

            def get(offset, size, src=src, dst=dst):
                pltpu.make_async_copy(eo_ref.at[_rows(src + offset, size), :],
                                      stage.at[s, _rows(dst + offset, size), :], sem.at[s]).start()
            _for_each_piece(runs_ref[tile * 8 + RUN_LEN, e], TT_LOG2, get)

    @pl.when(j == 0)
    def _():
        fetch(0, 0)
        fetch(jnp.minimum(1, last), 1)

    _wait_rows(stage.at[slot], eo_ref, sem.at[slot], to_hbm=False)

    info = info_ref[...]
    y = x1_ref[...]
    for c in range(STAGE_ROWS // TT):
        r = (_iota((TT, TT), 1) + c * TT).astype(F32)
        g = jnp.zeros((TT, TT), F32)
        for k in range(TOP_K):
            g = g + jnp.where(r == info[:, TOP_K + k:TOP_K + k + 1], info[:, k:k + 1], 0.0)
        g_hi = g.astype(BF16)
        g_lo = (g - g_hi.astype(F32)).astype(BF16)
        rows = jnp.concatenate(
            [stage[slot, pl.ds(c * TT * ROW_TILE + q, TT, stride=ROW_TILE), :] for q in range(ROW_TILE)],
            axis=1).astype(BF16)
        y = y + _dot(g_hi, rows) + _dot(g_lo, rows)
    ms = jnp.mean(y * y, axis=-1, keepdims=True)
    out_ref[...] = y * lax.rsqrt(ms + EPS) * fw_ref[...]

    ahead = lax.rem(j + 2, 3)
    fetch(jnp.minimum(j + 2, last), ahead)

    @pl.when(j == last)
    def _():
        _wait_rows(stage.at[lax.rem(j + 1, 3)], eo_ref, sem.at[lax.rem(j + 1, 3)], to_hbm=False)
        _wait_rows(stage.at[ahead], eo_ref, sem.at[ahead], to_hbm=False)


def _combine(runs, info, x1, final_w, eo):
    t = x1.shape[0]
    return pl.pallas_call(
        _combine_kernel,
        grid_spec=pltpu.PrefetchScalarGridSpec(
            num_scalar_prefetch=1,
            grid=(t // TT,),
            in_specs=[
                pl.BlockSpec((TT, LANES), lambda j, *_: (j, 0)),
                pl.BlockSpec((TT, D_MODEL), lambda j, *_: (j, 0)),
                pl.BlockSpec((1, D_MODEL), lambda j, *_: (0, 0)),
                pl.BlockSpec(memory_space=pl.ANY),
            ],
            out_specs=pl.BlockSpec((TT, D_MODEL), lambda j, *_: (j, 0)),
            scratch_shapes=[
                pltpu.VMEM((3, STAGE_ROWS * ROW_TILE, LANES), F32),
                pltpu.SemaphoreType.DMA((3,)),
            ],
        ),
        out_shape=jax.ShapeDtypeStruct((t, D_MODEL), F32),
        compiler_params=pltpu.CompilerParams(
            dimension_semantics=("arbitrary",), vmem_limit_bytes=VMEM_LIMIT),
        name="combine",
    )(runs, info, x1, final_w, eo)


def _pad_lanes(v, offset, fill=0.0):
    row = jnp.full((1, LANES), fill, F32)
    return row.at[0, offset:offset + v.shape[0]].set(v.astype(F32))


def kernel(x, norm_mix_w, w_in, gla_w_alpha_up, gla_b_alpha, gla_norm_w, ssd_conv_w, ssd_conv_b,
           ssd_dt_bias, ssd_A_log, ssd_D, ssd_norm_w, w_out, norm_ffn_w, router_w, router_b,
           moe_w_gate, moe_b_gate, moe_w_up, moe_b_up, moe_w_down, moe_b_down, final_norm_w):
    bsz, seqlen, d = x.shape
    t = bsz * seqlen
    depth = w_in.shape[0]
    assert depth == 1, "the final RMSNorm is fused into the (single) layer's combine step"
    p_rows = t * TOP_K + N_EXPERTS * TM
    n_tiles = p_rows // TM
    for l in range(depth):
        w = w_in[l]
        wb = w.astype(BF16)
        w_all = jnp.concatenate(
            [wb[:, 0:1536], wb[:, 1552:3088], wb[:, 1536:1552], wb[:, 3088:3096],
             jnp.zeros((d, N_SMALL - GLA_GATE_RANK - SSD_HEADS), BF16)], axis=1)
        wup = jnp.zeros((N_SMALL, GLA_KW), F32).at[0:GLA_GATE_RANK].set(gla_w_alpha_up[l])
        dtb = _pad_lanes(ssd_dt_bias[l], DT_COL)
        aneg = _pad_lanes(-jnp.exp(ssd_A_log[l].astype(F32)), DT_COL)
        dexp = jnp.repeat(ssd_D[l].astype(F32), SSD_HEADDIM)[None, :]
        rw = jnp.zeros((LANES, d), F32).at[0:N_EXPERTS].set(router_w[l].T)
        rb = jnp.zeros((LANES, TL), F32).at[0:N_EXPERTS].set(
            jnp.broadcast_to(router_b[l].astype(F32)[:, None], (N_EXPERTS, TL)))

        x1, h2, logits, counts = _mixer(
            x, norm_mix_w[l][None, :], w_all, wup, gla_b_alpha[l][None, :], gla_norm_w[l][None, :], ssd_conv_w[l],
            ssd_conv_b[l][None, :], dtb, aneg, dexp, ssd_norm_w[l][None, :],
            w_out[l].astype(BF16), norm_ffn_w[l][None, :], rw, rb)

        info, lp, runs = _route(logits, counts)
        plan = _plan(counts, n_tiles, p_rows)
        xs = _dispatch(runs, plan, lp, h2.reshape(t, d), p_rows)
        eo = _experts(plan, xs,
                      moe_w_gate[l], moe_b_gate[l].reshape(N_EXPERTS, N_CHUNKS, N_COLS),
                      moe_w_up[l], moe_b_up[l].reshape(N_EXPERTS, N_CHUNKS, N_COLS),
                      moe_w_down[l], moe_b_down[l].reshape(N_EXPERTS, N_CHUNKS, N_COLS))
        x = _combine(runs, info, x1.reshape(t, d), final_norm_w[None, :], eo).reshape(bsz, seqlen, d)
    return x
```

```python
import functools

import jax
import jax.numpy as jnp
from jax import lax
from jax.experimental import pallas as pl
from jax.experimental.pallas import tpu as pltpu

F32 = jnp.float32
BF16 = jnp.bfloat16

D_MODEL = 1024
GLA_WIDTH = 512
GLA_HEADS = 4
GLA_DV = 128
GLA_DK = 64
GLA_KW = 256
GLA_GATE_RANK = 16
GLA_GATE_NORM = 16.0
SSD_WIDTH = 512
SSD_HEADDIM = 64
SSD_HEADS = 8
SSD_GROUPS = 2
SSD_HPG = 4
SSD_STATE = 128
SSD_CONV = 4
SSD_CONV_CH = 1024
N_EXPERTS = 32
TOP_K = 4
SWIGLU_LIMIT = 7.0
SWIGLU_ALPHA = 1.702
EPS = 1e-6
GROUP_EPS = 1e-5

LANES = 128
ROW_TILE = D_MODEL // LANES
N_MAIN = 3072
N_SMALL = LANES
DT_COL = GLA_GATE_RANK

GLA_CHUNK = 64
GLA_SPAN = 128
SSD_CHUNK = 128
assert GLA_SPAN == 2 * GLA_CHUNK == SSD_CHUNK
PROJ_COLS = 512
TL = 256
TT_LOG2 = 8
TT = 1 << TT_LOG2
STAGE_ROWS = 4 * TT
TM_LOG2 = 9
TM = 1 << TM_LOG2
WAIT_ROWS = 512
ZERO_DEPTH = 4
RUN_START, RUN_LEN, RUN_LOFF = 0, 1, 2
PLAN_EXPERT, PLAN_NEXT, PLAN_PARITY, PLAN_NVALID, PLAN_PAD_START, PLAN_PAD_LEN = 0, 1, 2, 3, 4, 5
PLAN_COLS = 256
MAX_TILE_COPY_BYTES = 4 * 1024 * 1024
assert STAGE_ROWS * D_MODEL * 4 <= MAX_TILE_COPY_BYTES
N_COLS = 256
N_CHUNKS = D_MODEL // N_COLS
VMEM_LIMIT = 56 * 1024 * 1024


def _dot(a, b):
    return jnp.dot(a, b, preferred_element_type=F32)


def _dot_nt(a, b):
    return lax.dot_general(a, b, (((1,), (1,)), ((), ())), preferred_element_type=F32)


def _dot_tn(a, b):
    return lax.dot_general(a, b, (((0,), (0,)), ((), ())), preferred_element_type=F32)


def _split3(a):
    hi = a.astype(BF16)
    r1 = a - hi.astype(F32)
    mid = r1.astype(BF16)
    lo = (r1 - mid.astype(F32)).astype(BF16)
    return hi, mid, lo


def _dot_sel_lhs(sel, a):
    hi, mid, lo = _split3(a)
    return _dot(sel, hi) + _dot(sel, mid) + _dot(sel, lo)


def _dot_sel_rhs(a, sel, terms=3):
    parts = _split3(a)[:terms]
    out = _dot(parts[0], sel)
    for p in parts[1:]:
        out = out + _dot(p, sel)
    return out


def _dot_hi(a, b):
    a_hi = a.astype(BF16)
    a_lo = (a - a_hi.astype(F32)).astype(BF16)
    b_hi = b.astype(BF16)
    b_lo = (b - b_hi.astype(F32)).astype(BF16)
    return _dot(a_hi, b_hi) + _dot(a_lo, b_hi) + _dot(a_hi, b_lo)


def _dot_hi_nt(a, b):
    a_hi = a.astype(BF16)
    a_lo = (a - a_hi.astype(F32)).astype(BF16)
    b_hi = b.astype(BF16)
    b_lo = (b - b_hi.astype(F32)).astype(BF16)
    return _dot_nt(a_hi, b_hi) + _dot_nt(a_lo, b_hi) + _dot_nt(a_hi, b_lo)


def _softplus(x):
    return jnp.maximum(x, 0.0) + jnp.log1p(jnp.exp(-jnp.abs(x)))


def _silu(x):
    return x * jax.nn.sigmoid(x)


def _iota(shape, dim):
    return lax.broadcasted_iota(jnp.int32, shape, dim)


def _load_row_tiles(ref, rows):
    return jnp.concatenate([ref[pl.ds(j, rows, stride=ROW_TILE), :] for j in range(ROW_TILE)], axis=1)


def _project_parts(x_ref, nmw_ref, win_ref, pm_ref, small_ref):
    state = {}

    def norm():
        x_in = x_ref[...]
        ms = jnp.mean(x_in * x_in, axis=-1, keepdims=True)
        state["h"] = (x_in * lax.rsqrt(ms + EPS) * nmw_ref[...]).astype(BF16)

    def chunk(n0):
        def run():
            pm_ref[:, n0:n0 + PROJ_COLS] = _dot(state["h"], win_ref[:, n0:n0 + PROJ_COLS]).astype(BF16)
        return run

    def small():
        small_ref[...] = _dot(state["h"], win_ref[:, N_MAIN:N_MAIN + N_SMALL])

    return [norm] + [chunk(n0) for n0 in range(0, N_MAIN, PROJ_COLS)] + [small]


def _mixer_kernel(x_ref, xn_ref, nmw_ref, win_ref, *refs, tiles_per_row):
    params, outs = refs[:13], refs[13:17]
    gla_state, ssd_state, conv_tail, mix_scr, cnt_scr, pm_a, pm_b, small_a, small_b = refs[17:]
    g = pl.program_id(0)

    @pl.when(g == 0)
    def _():
        cnt_scr[...] = jnp.zeros_like(cnt_scr)
        for part in _project_parts(x_ref, nmw_ref, win_ref, pm_a, small_a):
            part()

    @pl.when(lax.rem(g, tiles_per_row) == 0)
    def _():
        gla_state[...] = jnp.zeros_like(gla_state)
        ssd_state[...] = jnp.zeros_like(ssd_state)
        conv_tail[...] = jnp.zeros_like(conv_tail)

    for parity, (pm_cur, small_cur, pm_nxt, small_nxt) in enumerate(
            ((pm_a, small_a, pm_b, small_b), (pm_b, small_b, pm_a, small_a))):
        @pl.when((g & 1) == parity)
        def _(pm_cur=pm_cur, small_cur=small_cur, pm_nxt=pm_nxt, small_nxt=small_nxt):
            _mixer_tile(_project_parts(xn_ref, nmw_ref, win_ref, pm_nxt, small_nxt),
                        pm_cur, small_cur, x_ref, *params, *outs,
                        gla_state, ssd_state, conv_tail, mix_scr, cnt_scr)


def _mixer_tile(side_work, pm_ref, small_ref, x_ref, wup_ref, balpha_ref, gnw_ref, convw_ref, convb_ref,
                dtb_ref, aneg_ref, dexp_ref, snw_ref, wout_ref, nfw_ref, rw_ref, rb_ref,
                x1_ref, h2_ref, lg_ref, cnt_ref,
                gla_state, ssd_state, conv_tail, mix_scr, cnt_scr):
    side_work = list(side_work)

    def run_side(n=1):
        for _ in range(min(n, len(side_work))):
            side_work.pop(0)()

    small = small_ref[...]

    row = _iota((TL, TL), 0)
    col = _iota((TL, TL), 1)
    causal = col <= row
    cum128 = jnp.where(causal & ((row // SSD_CHUNK) == (col // SSD_CHUNK)), 1.0, 0.0).astype(BF16)

    xa = _dot_hi(small, wup_ref[...]) + balpha_ref[...]
    log_a = (jnp.minimum(xa, 0.0) - jnp.log1p(jnp.exp(-jnp.abs(xa)))) * (1.0 / GLA_GATE_NORM)
    bcum = _dot_sel_lhs(cum128, log_a)

    lane_kw = _iota((GLA_CHUNK, GLA_KW), 1)
    head_masks = [(lane_kw // GLA_DK) == h for h in range(GLA_HEADS)]
    lane_kw2 = _iota((GLA_SPAN, GLA_KW), 1)
    head_masks2 = [(lane_kw2 // GLA_DK) == h for h in range(GLA_HEADS)]
    lane_kw_s = _iota((GLA_DV, GLA_KW), 1)
    head_masks_s = [(lane_kw_s // GLA_DK) == h for h in range(GLA_HEADS)]
    tril64 = _iota((GLA_CHUNK, GLA_CHUNK), 1) <= _iota((GLA_CHUNK, GLA_CHUNK), 0)
    q_scale = GLA_DK ** -0.5

    def gla_chunk(c):
        rs = slice(c * GLA_SPAN, (c + 1) * GLA_SPAN)
        top, bot = slice(0, GLA_CHUNK), slice(GLA_CHUNK, GLA_SPAN)
        bc = bcum[rs]
        b_last = bc[GLA_SPAN - 1:GLA_SPAN]
        b_edge = bc[GLA_CHUNK - 1:GLA_CHUNK]
        qc = pm_ref[rs, 0:GLA_KW].astype(F32) * q_scale
        kc = pm_ref[rs, GLA_KW:2 * GLA_KW].astype(F32)
        vc = pm_ref[rs, 2 * GLA_KW:2 * GLA_KW + GLA_WIDTH]
        q_st = (qc * jnp.exp(bc)).astype(BF16)
        k_st = (kc * jnp.exp(b_last - bc)).astype(BF16)
        blocks = []
        for part in (top, bot):
            b_part = bc[part]
            b_mid = b_part[GLA_CHUNK // 2:GLA_CHUNK // 2 + 1]
            blocks.append(((qc[part] * jnp.exp(b_part - b_mid)).astype(BF16),
                           (kc[part] * jnp.exp(b_mid - b_part)).astype(BF16)))
        q_off = (qc[bot] * jnp.exp(bc[bot] - b_edge)).astype(BF16)
        k_off = (kc[top] * jnp.exp(b_edge - bc[top])).astype(BF16)
        st = gla_state[...]
        st_b = st.astype(BF16)
        zero_h = jnp.zeros((GLA_CHUNK, GLA_KW), BF16)
        zero_s = jnp.zeros_like(q_st)
        for h in range(GLA_HEADS):
            v_h = vc[:, h * GLA_DV:(h + 1) * GLA_DV]
            diag = [jnp.where(tril64, _dot_nt(jnp.where(head_masks[h], q_in, zero_h), k_in), 0.0).astype(BF16)
                    for q_in, k_in in blocks]
            off = _dot_nt(jnp.where(head_masks[h], q_off, zero_h), k_off).astype(BF16)
            inter = _dot_nt(jnp.where(head_masks2[h], q_st, zero_s), st_b)
            o_top = _dot(diag[0], v_h[top]) + inter[top]
            o_bot = _dot(off, v_h[top]) + _dot(diag[1], v_h[bot]) + inter[bot]
            mix_scr[rs, h * GLA_DV:(h + 1) * GLA_DV] = jnp.concatenate([o_top, o_bot], axis=0)
        upd = _dot_tn(vc, k_st)
        new_st = st * jnp.exp(b_last)
        for h in range(GLA_HEADS):
            new_st = new_st + jnp.where(head_masks_s[h], upd[h * GLA_DV:(h + 1) * GLA_DV], 0.0)
        gla_state[...] = new_st
        run_side()

    xbc = pm_ref[:, 2048:3072].astype(F32)
    tail = conv_tail[...]
    conv_tail[...] = xbc[TL - 8:TL]
    row8 = _iota((8, SSD_CONV_CH), 0)
    conv = xbc * convw_ref[SSD_CONV - 1:SSD_CONV, :]
    for s in range(1, SSD_CONV):
        shifted = pltpu.roll(xbc, s, 0)
        head = jnp.where(row8 < s, pltpu.roll(tail, s, 0), shifted[0:8])
        shifted = jnp.concatenate([head, shifted[8:]], axis=0)
        conv = conv + shifted * convw_ref[SSD_CONV - 1 - s:SSD_CONV - s, :]
    act = _silu(conv + convb_ref[...])
    run_side()
    xs = act[:, 0:SSD_WIDTH]
    bm = act[:, SSD_WIDTH:SSD_WIDTH + SSD_GROUPS * SSD_STATE].astype(BF16)
    cm = act[:, SSD_WIDTH + SSD_GROUPS * SSD_STATE:].astype(BF16)

    dt_full = _softplus(small + dtb_ref[...])
    a_full = dt_full * aneg_ref[...]
    acum = _dot_sel_lhs(cum128, a_full)
    acum_t = acum.T

    e_row = _iota((N_SMALL, SSD_WIDTH), 0)
    e_col = _iota((N_SMALL, SSD_WIDTH), 1)
    spread64 = jnp.where(e_row == DT_COL + e_col // SSD_HEADDIM, 1.0, 0.0).astype(BF16)
    e_row2 = _iota((N_SMALL, SSD_HEADS * LANES), 0)
    e_col2 = _iota((N_SMALL, SSD_HEADS * LANES), 1)
    spread128 = jnp.where(e_row2 == DT_COL + e_col2 // LANES, 1.0, 0.0).astype(BF16)
    dt_e = _dot_sel_rhs(dt_full, spread64, terms=1)
    ac_e = _dot_sel_rhs(acum, spread64, terms=2)
    ac_w = _dot_sel_rhs(acum, spread128, terms=2)

    tril128 = _iota((SSD_CHUNK, SSD_CHUNK), 1) <= _iota((SSD_CHUNK, SSD_CHUNK), 0)
    lane_g = _iota((SSD_CHUNK, SSD_HPG * SSD_HEADDIM), 1)
    def ssd_chunk(c):
        rs = slice(c * SSD_CHUNK, (c + 1) * SSD_CHUNK)
        ac_c = ac_e[rs]
        a_last = ac_c[SSD_CHUNK - 1:SSD_CHUNK]
        dt_c = dt_e[rs]
        xs_c = xs[rs]
        x_dt = (xs_c * dt_c).astype(BF16)
        x_w = (xs_c * (jnp.exp(a_last - ac_c) * dt_c)).astype(BF16)
        e_ac = jnp.exp(ac_c)
        for g in range(SSD_GROUPS):
            gs = slice(g * SSD_STATE, (g + 1) * SSD_STATE)
            ws = slice(g * SSD_HPG * SSD_HEADDIM, (g + 1) * SSD_HPG * SSD_HEADDIM)
            c_g = cm[rs, gs]
            b_g = bm[rs, gs]
            cb = _dot_nt(c_g, b_g)
            x_dt_g = x_dt[:, ws]
            lhs_parts = []
            rhs_parts = []
            for hh in range(SSD_HPG):
                h = g * SSD_HPG + hh
                seg = ac_w[rs, h * LANES:(h + 1) * LANES] - acum_t[DT_COL + h:DT_COL + h + 1, rs]
                lmat = jnp.where(tril128, jnp.exp(jnp.where(tril128, seg, 0.0)), 0.0)
                lhs_parts.append((cb * lmat).astype(BF16))
                rhs_parts.append(jnp.where((lane_g // SSD_HEADDIM) == hh, x_dt_g,
                                           jnp.zeros_like(x_dt_g)))
            intra = _dot(jnp.concatenate(lhs_parts, axis=1), jnp.concatenate(rhs_parts, axis=0))
            st = ssd_state[g]
            inter = _dot(c_g, st.astype(BF16)) * e_ac[:, ws]
            mix_scr[rs, GLA_WIDTH + g * 256:GLA_WIDTH + (g + 1) * 256] = intra + inter
            ssd_state[g] = st * jnp.exp(a_last[:, ws]) + _dot_tn(b_g, x_w[:, ws])
        run_side()

    gla_per_ssd = SSD_CHUNK // GLA_SPAN
    for c in range(TL // SSD_CHUNK):
        for cc in range(gla_per_ssd):
            gla_chunk(c * gla_per_ssd + cc)
        ssd_chunk(c)
    run_side(len(side_work))

    o = mix_scr[:, 0:GLA_WIDTH]
    g_gate = _silu(pm_ref[:, 1024:1536].astype(F32))
    gla_parts = []
    for h in range(GLA_HEADS):
        o_h = o[:, h * GLA_DV:(h + 1) * GLA_DV]
        ms = jnp.mean(o_h * o_h, axis=-1, keepdims=True)
        gla_parts.append(o_h * lax.rsqrt(ms + GROUP_EPS))
    gla_out = jnp.concatenate(gla_parts, axis=1) * gnw_ref[...] * g_gate

    y = mix_scr[:, GLA_WIDTH:] + dexp_ref[...] * xs
    y = y * _silu(pm_ref[:, 1536:2048].astype(F32))
    ssd_parts = []
    for g in range(SSD_GROUPS):
        y_g = y[:, g * 256:(g + 1) * 256]
        ms = jnp.mean(y_g * y_g, axis=-1, keepdims=True)
        ssd_parts.append(y_g * lax.rsqrt(ms + GROUP_EPS))
    ssd_out = jnp.concatenate(ssd_parts, axis=1) * snw_ref[...]

    mixed = jnp.concatenate([gla_out, ssd_out], axis=1).astype(BF16)
    x1 = x_ref[...] + _dot(mixed, wout_ref[...])
    x1_ref[...] = x1

    ms = jnp.mean(x1 * x1, axis=-1, keepdims=True)
    h2 = x1 * lax.rsqrt(ms + EPS) * nfw_ref[...]
    h2_ref[...] = h2.astype(BF16)
    lg = _dot_hi_nt(rw_ref[...], h2) + rb_ref[...]
    lg_ref[...] = lg
    cnt_scr[...] = cnt_scr[...] + _tile_counts(_top4(lg[0:N_EXPERTS])[0])[1]
    cnt_ref[...] = cnt_scr[...]


def _mixer(x, nmw, w_all, wup, balpha, gnw, convw, convb, dtb, aneg, dexp, snw, wout, nfw, rw, rb):
    bsz, seqlen, _ = x.shape
    per_row = seqlen // TL
    steps = bsz * per_row

    def full(a):
        return pl.BlockSpec(a.shape, lambda g: (0,) * a.ndim)

    def tile(ahead):
        def index(g):
            tile_id = jnp.minimum(g + ahead, steps - 1)
            return (tile_id // per_row, tile_id % per_row, 0)
        return pl.BlockSpec((None, TL, D_MODEL), index)

    params = (nmw, w_all, wup, balpha, gnw, convw, convb, dtb, aneg, dexp, snw, wout, nfw, rw, rb)
    return pl.pallas_call(
        functools.partial(_mixer_kernel, tiles_per_row=per_row),
        grid=(steps,),
        in_specs=[tile(0), tile(1)] + [full(p) for p in params],
        out_specs=[tile(0), tile(0),
                   pl.BlockSpec((LANES, TL), lambda g: (0, g)),
                   pl.BlockSpec((N_EXPERTS, LANES), lambda g: (0, 0))],
        out_shape=[
            jax.ShapeDtypeStruct((bsz, seqlen, D_MODEL), F32),
            jax.ShapeDtypeStruct((bsz, seqlen, D_MODEL), BF16),
            jax.ShapeDtypeStruct((LANES, bsz * seqlen), F32),
            jax.ShapeDtypeStruct((N_EXPERTS, LANES), F32),
        ],
        scratch_shapes=[
            pltpu.VMEM((GLA_DV, GLA_KW), F32),
            pltpu.VMEM((SSD_GROUPS, SSD_STATE, SSD_HPG * SSD_HEADDIM), F32),
            pltpu.VMEM((8, SSD_CONV_CH), F32),
            pltpu.VMEM((TL, D_MODEL), F32),
            pltpu.VMEM((N_EXPERTS, LANES), F32),
            pltpu.VMEM((TL, N_MAIN), BF16),
            pltpu.VMEM((TL, N_MAIN), BF16),
            pltpu.VMEM((TL, N_SMALL), F32),
            pltpu.VMEM((TL, N_SMALL), F32),
        ],
        compiler_params=pltpu.CompilerParams(
            dimension_semantics=("arbitrary",), vmem_limit_bytes=VMEM_LIMIT),
        name="mixer",
    )(x, x, *params)


def _top4(lg):
    n_e, n_t = lg.shape
    row = _iota((n_e, n_t), 0)
    work = lg
    onehots = []
    vals = []
    for _ in range(TOP_K):
        m = jnp.max(work, axis=0, keepdims=True)
        idx = jnp.min(jnp.where(work == m, row, n_e), axis=0, keepdims=True)
        oh = row == idx
        onehots.append(oh)
        vals.append(m)
        work = jnp.where(oh, -jnp.inf, work)
    return onehots, vals


def _tile_counts(onehots):
    multi = jnp.where(onehots[0] | onehots[1] | onehots[2] | onehots[3], 1.0, 0.0).astype(BF16)
    return multi, _dot(multi, jnp.ones((multi.shape[1], LANES), BF16))


def _route_kernel(lg_ref, cnt_ref, info_ref, lp_ref, runs_ref, run_scr):
    @pl.when(pl.program_id(0) == 0)
    def _():
        run_scr[...] = jnp.zeros_like(run_scr)

    onehots, vals = _top4(lg_ref[0:N_EXPERTS, :])
    multi, tile_cnt = _tile_counts(onehots)
    counts = cnt_ref[...]
    padded = jnp.floor((counts + (TM - 1)) * (1.0 / TM)) * TM
    lower = jnp.where(_iota((N_EXPERTS, N_EXPERTS), 1) < _iota((N_EXPERTS, N_EXPERTS), 0),
                      1.0, 0.0).astype(BF16)
    offs = _dot_sel_lhs(lower, padded)
    loff = _dot_sel_lhs(lower, tile_cnt)
    before = jnp.where(_iota((TT, TT), 0) < _iota((TT, TT), 1), 1.0, 0.0).astype(BF16)
    rank = _dot(multi, before)
    lane = _iota((N_EXPERTS, LANES), 1)
    table = jnp.where(lane == RUN_START, offs + run_scr[...],
                      jnp.where(lane == RUN_LEN, tile_cnt, jnp.where(lane == RUN_LOFF, loff, 0.0)))
    table = jnp.concatenate([table, jnp.zeros((LANES - N_EXPERTS, LANES), F32)], axis=0)
    runs_ref[...] = table.T[0:8].astype(jnp.int32)
    run_scr[...] = run_scr[...] + tile_cnt
    local = rank + jnp.concatenate([loff] * (TT // LANES), axis=1)
    exps = [jnp.exp(v - vals[0]) for v in vals]
    den = exps[0] + exps[1] + exps[2] + exps[3]
    lp_rows = [jnp.sum(jnp.where(oh, local, 0.0), axis=0, keepdims=True) for oh in onehots]
    lp_ref[...] = jnp.concatenate(lp_rows + [jnp.zeros((8 - TOP_K, TT), F32)], axis=0).astype(jnp.int32)
    gate_rows = [e / den for e in exps]
    info = jnp.concatenate(gate_rows + lp_rows + [jnp.zeros((LANES - 2 * TOP_K, TT), F32)], axis=0)
    info_ref[...] = info.T


def _route(logits_t, counts):
    t = logits_t.shape[1]
    steps = t // TT
    return pl.pallas_call(
        _route_kernel,
        grid=(steps,),
        in_specs=[pl.BlockSpec((LANES, TT), lambda i: (0, i)),
                  pl.BlockSpec((N_EXPERTS, LANES), lambda i: (0, 0))],
        out_specs=[
            pl.BlockSpec((TT, LANES), lambda i: (i, 0)),
            pl.BlockSpec((8, TT), lambda i: (0, i)),
            pl.BlockSpec((8, LANES), lambda i: (i, 0)),
        ],
        out_shape=[
            jax.ShapeDtypeStruct((t, LANES), F32),
            jax.ShapeDtypeStruct((8, t), jnp.int32),
            jax.ShapeDtypeStruct((steps * 8, LANES), jnp.int32),
        ],
        scratch_shapes=[pltpu.VMEM((N_EXPERTS, LANES), F32)],
        compiler_params=pltpu.CompilerParams(
            dimension_semantics=("arbitrary",), vmem_limit_bytes=VMEM_LIMIT),
        name="route",
    )(logits_t, counts)


def _rows(first, count):
    return pl.ds(pl.multiple_of(first * ROW_TILE, ROW_TILE), count * ROW_TILE)


def _wait_rows(stage_slot, hbm_ref, sem, to_hbm):
    for w in range(STAGE_ROWS // WAIT_ROWS):
        part = stage_slot.at[pl.ds(w * WAIT_ROWS * ROW_TILE, WAIT_ROWS * ROW_TILE), :]
        hbm = hbm_ref.at[pl.ds(0, WAIT_ROWS * ROW_TILE), :]
        src, dst = (part, hbm) if to_hbm else (hbm, part)
        pltpu.make_async_copy(src, dst, sem).wait()


def _for_each_piece(length, max_log2, fn):
    for b in reversed(range(max_log2 + 1)):
        size = 1 << b
        offset = lax.shift_left(lax.shift_right_logical(length, b + 1), b + 1)

        @pl.when((length & size) != 0)
        def _(offset=offset, size=size):
            fn(offset, size)


def _plan_kernel(cnt_ref, plan_ref, *, n_tiles, p_rows):
    def clear(i, c):
        for r in range(8):
            plan_ref[r, i] = 0
        return c
    lax.fori_loop(0, PLAN_COLS, clear, 0)

    def per_expert(e, carry):
        first_row, group = carry
        count = cnt_ref[e, 0].astype(jnp.int32)
        tiles = lax.shift_right_logical(count + (TM - 1), TM_LOG2)
        first_tile = lax.shift_right_logical(first_row, TM_LOG2)

        def mark(i, c):
            plan_ref[PLAN_EXPERT, i] = e
            plan_ref[PLAN_PARITY, i] = group & 1
            return c
        lax.fori_loop(first_tile, first_tile + tiles, mark, 0)
        plan_ref[PLAN_PAD_START, e] = first_row + count
        plan_ref[PLAN_PAD_LEN, e] = tiles * TM - count
        return first_row + tiles * TM, group + jnp.minimum(tiles, 1)

    used_rows, _ = lax.fori_loop(0, N_EXPERTS, per_expert, (jnp.int32(0), jnp.int32(0)))
    n_valid = lax.shift_right_logical(used_rows, TM_LOG2)
    plan_ref[PLAN_NVALID, 0] = n_valid
    plan_ref[PLAN_PAD_START, N_EXPERTS] = used_rows
    plan_ref[PLAN_PAD_LEN, N_EXPERTS] = lax.shift_right_logical(p_rows - used_rows, TM_LOG2 - 1)

    last_expert = plan_ref[PLAN_EXPERT, n_valid - 1]
    last_parity = plan_ref[PLAN_PARITY, n_valid - 1]

    def mark_unused(i, c):
        plan_ref[PLAN_EXPERT, i] = last_expert
        plan_ref[PLAN_PARITY, i] = last_parity
        return c
    lax.fori_loop(n_valid, n_tiles, mark_unused, 0)

    def next_expert(k, carry):
        expert_after, next_after = carry
        i = n_tiles - 1 - k
        e = plan_ref[PLAN_EXPERT, i]
        nxt = jnp.where(expert_after != e, expert_after, next_after)
        plan_ref[PLAN_NEXT, i] = nxt
        return e, nxt
    lax.fori_loop(0, n_tiles, next_expert, (last_expert, last_expert))


def _plan(counts, n_tiles, p_rows):
    assert n_tiles <= PLAN_COLS and N_EXPERTS < PLAN_COLS
    return pl.pallas_call(
        functools.partial(_plan_kernel, n_tiles=n_tiles, p_rows=p_rows),
        in_specs=[pl.BlockSpec(memory_space=pltpu.SMEM)],
        out_specs=pl.BlockSpec(memory_space=pltpu.SMEM),
        out_shape=jax.ShapeDtypeStruct((8, PLAN_COLS), jnp.int32),
        name="plan",
    )(counts)


def _dispatch_kernel(runs_ref, plan_ref, lp_ref, h_ref, xs_ref, stage, zeros, sem, zsem):
    j = pl.program_id(0)
    slot = j & 1

    def wait_tile(s):
        _wait_rows(stage.at[s], xs_ref, sem.at[s], to_hbm=True)

    @pl.when(j == 0)
    def _():
        zeros[...] = jnp.zeros_like(zeros)

        def pad_pieces(e, wait):
            def piece(offset, size):
                cp = pltpu.make_async_copy(
                    zeros.at[pl.ds(0, size * ROW_TILE), :],
                    xs_ref.at[_rows(plan_ref[PLAN_PAD_START, e] + offset, size), :],
                    zsem.at[e % ZERO_DEPTH])
                cp.wait() if wait else cp.start()
            _for_each_piece(plan_ref[PLAN_PAD_LEN, e], TM_LOG2 - 1, piece)

        for e in range(N_EXPERTS + ZERO_DEPTH - 1):
            if e < N_EXPERTS:
                pad_pieces(e, wait=False)
            if e >= ZERO_DEPTH - 1:
                pad_pieces(e - (ZERO_DEPTH - 1), wait=True)

        n_blocks = plan_ref[PLAN_PAD_LEN, N_EXPERTS]

        def block_copy(blk):
            first = plan_ref[PLAN_PAD_START, N_EXPERTS] + blk * (TM // 2)
            return pltpu.make_async_copy(zeros, xs_ref.at[_rows(first, TM // 2), :],
                                         zsem.at[lax.rem(blk, ZERO_DEPTH)])

        def put_block(blk, carry):
            block_copy(blk).start()

            @pl.when(blk >= ZERO_DEPTH - 1)
            def _():
                block_copy(blk - (ZERO_DEPTH - 1)).wait()
            return carry
        lax.fori_loop(0, n_blocks, put_block, 0)
        for back in range(ZERO_DEPTH - 1, 0, -1):
            @pl.when(n_blocks >= back)
            def _(back=back):
                block_copy(n_blocks - back).wait()

    @pl.when(j >= 2)
    def _():
        wait_tile(slot)

    h = h_ref[...]
    lp = lp_ref[...]
    for c in range(STAGE_ROWS // TT):
        r = _iota((TT, TT), 0) + c * TT
        hit = (r == lp[0:1]) | (r == lp[1:2]) | (r == lp[2:3]) | (r == lp[3:4])
        rows = _dot(jnp.where(hit, 1.0, 0.0).astype(BF16), h)
        for q in range(ROW_TILE):
            stage[slot, pl.ds(c * TT * ROW_TILE + q, TT, stride=ROW_TILE), :] = rows[:, q * LANES:(q + 1) * LANES]

    for e in range(N_EXPERTS):
        src = runs_ref[j * 8 + RUN_LOFF, e]
        dst = runs_ref[j * 8 + RUN_START, e]

        def put(offset, size, src=src, dst=dst):
            pltpu.make_async_copy(stage.at[slot, _rows(src + offset, size), :],
                                  xs_ref.at[_rows(dst + offset, size), :], sem.at[slot]).start()
        _for_each_piece(runs_ref[j * 8 + RUN_LEN, e], TT_LOG2, put)

    @pl.when(j == pl.num_programs(0) - 1)
    def _():
        wait_tile(1 - slot)
        wait_tile(slot)


def _dispatch(runs, plan, lp, h2, p_rows):
    t = h2.shape[0]
    return pl.pallas_call(
        _dispatch_kernel,
        grid_spec=pltpu.PrefetchScalarGridSpec(
            num_scalar_prefetch=2,
            grid=(t // TT,),
            in_specs=[
                pl.BlockSpec((8, TT), lambda j, *_: (0, j)),
                pl.BlockSpec((TT, D_MODEL), lambda j, *_: (j, 0)),
            ],
            out_specs=pl.BlockSpec(memory_space=pl.ANY),
            scratch_shapes=[
                pltpu.VMEM((2, STAGE_ROWS * ROW_TILE, LANES), F32),
                pltpu.VMEM((TM // 2 * ROW_TILE, LANES), F32),
                pltpu.SemaphoreType.DMA((2,)),
                pltpu.SemaphoreType.DMA((ZERO_DEPTH,)),
            ],
        ),
        out_shape=jax.ShapeDtypeStruct((p_rows * ROW_TILE, LANES), F32),
        compiler_params=pltpu.CompilerParams(
            dimension_semantics=("arbitrary",), vmem_limit_bytes=VMEM_LIMIT),
        name="dispatch",
    )(runs, plan, lp, h2)


def _experts_kernel(plan_ref, x_ref, wg_ref, bg_ref, wu_ref, bu_ref, wd_ref, bd_ref,
                    o_ref, act, wbuf, wg_b, wu_b, wd_b, wsem):
    i = pl.program_id(0)
    prev = jnp.maximum(i - 1, 0)

    def fetch(expert, s):
        return [pltpu.make_async_copy(w_ref.at[expert], wbuf.at[s, m], wsem.at[s])
                for m, w_ref in enumerate((wg_ref, wu_ref, wd_ref))]

    n_valid = plan_ref[PLAN_NVALID, 0]

    @pl.when(i >= n_valid)
    def _():
        o_ref[...] = jnp.zeros_like(o_ref)

    @pl.when(i < n_valid)
    def _():
        expert = plan_ref[PLAN_EXPERT, i]
        s = plan_ref[PLAN_PARITY, i]
        nxt = plan_ref[PLAN_NEXT, i]

        @pl.when(i == 0)
        def _():
            for cp in fetch(expert, s):
                cp.start()

        @pl.when((i == 0) | (expert != plan_ref[PLAN_EXPERT, prev]))
        def _():
            for cp in fetch(expert, s):
                cp.wait()
            for c in range(N_CHUNKS):
                cs = slice(c * N_COLS, (c + 1) * N_COLS)
                wg_b[c] = wbuf[s, 0, :, cs].astype(BF16)
                wu_b[c] = wbuf[s, 1, :, cs].astype(BF16)
                wd_b[c] = wbuf[s, 2, :, cs].astype(BF16)

            @pl.when(nxt != expert)
            def _():
                for cp in fetch(nxt, 1 - s):
                    cp.start()

        xb = _load_row_tiles(x_ref, TM).astype(BF16)
        for n in range(N_CHUNKS):
            gate = _dot(xb, wg_b[n]) + bg_ref[n:n + 1, :]
            up = _dot(xb, wu_b[n]) + bu_ref[n:n + 1, :]
            gate = jnp.minimum(gate, SWIGLU_LIMIT)
            up = jnp.clip(up, -SWIGLU_LIMIT, SWIGLU_LIMIT)
            act[n] = ((up + 1.0) * (gate * jax.nn.sigmoid(SWIGLU_ALPHA * gate))).astype(BF16)
        a = jnp.concatenate([act[c] for c in range(N_CHUNKS)], axis=1)
        for n in range(N_CHUNKS):
            out = _dot(a, wd_b[n]) + bd_ref[n:n + 1, :]
            for q in range(N_COLS // LANES):
                o_ref[pl.ds(n * (N_COLS // LANES) + q, TM, stride=ROW_TILE), :] = (
                    out[:, q * LANES:(q + 1) * LANES])


def _experts(plan, xs, wg, bg, wu, bu, wd, bd):
    n_tiles = xs.shape[0] // (TM * ROW_TILE)

    def x_map(i, plan_ref):
        return (jnp.minimum(i, plan_ref[PLAN_NVALID, 0] - 1), 0)

    def b_map(i, plan_ref):
        return (plan_ref[PLAN_EXPERT, i], 0, 0)

    w_spec = pl.BlockSpec(memory_space=pl.ANY)
    b_spec = pl.BlockSpec((None, N_CHUNKS, N_COLS), b_map)
    return pl.pallas_call(
        _experts_kernel,
        grid_spec=pltpu.PrefetchScalarGridSpec(
            num_scalar_prefetch=1,
            grid=(n_tiles,),
            in_specs=[pl.BlockSpec((TM * ROW_TILE, LANES), x_map),
                      w_spec, b_spec, w_spec, b_spec, w_spec, b_spec],
            out_specs=pl.BlockSpec((TM * ROW_TILE, LANES), lambda i, plan_ref: (i, 0)),
            scratch_shapes=[
                pltpu.VMEM((N_CHUNKS, TM, N_COLS), BF16),
                pltpu.VMEM((2, 3, D_MODEL, D_MODEL), F32),
                pltpu.VMEM((N_CHUNKS, D_MODEL, N_COLS), BF16),
                pltpu.VMEM((N_CHUNKS, D_MODEL, N_COLS), BF16),
                pltpu.VMEM((N_CHUNKS, D_MODEL, N_COLS), BF16),
                pltpu.SemaphoreType.DMA((2,)),
            ],
        ),
        out_shape=jax.ShapeDtypeStruct(xs.shape, F32),
        compiler_params=pltpu.CompilerParams(
            dimension_semantics=("arbitrary",), vmem_limit_bytes=VMEM_LIMIT),
        name="experts",
    )(plan, xs, wg, bg, wu, bu, wd, bd)


def _combine_kernel(runs_ref, info_ref, x1_ref, fw_ref, eo_ref, out_ref, stage, sem):
    j = pl.program_id(0)
    slot = lax.rem(j, 3)
    last = pl.num_programs(0) - 1

    def fetch(tile, s):
        for e in range(N_EXPERTS):
            src = runs_ref[tile * 8 + RUN_START, e]
            dst = runs_ref[tile * 8 + RUN_LOFF, e]

            def get(offset, size, src=src, dst=dst):
                pltpu.make_async_copy(eo_ref.at[_rows(src + offset, size), :],
                                      stage.at[s, _rows(dst + offset, size), :], sem.at[s]).start()
            _for_each_piece(runs_ref[tile * 8 + RUN_LEN, e], TT_LOG2, get)

    @pl.when(j == 0)
    def _():
        fetch(0, 0)
        fetch(jnp.minimum(1, last), 1)

    _wait_rows(stage.at[slot], eo_ref, sem.at[slot], to_hbm=False)

    info = info_ref[...]
    y = x1_ref[...]
    for c in range(STAGE_ROWS // TT):
        r = (_iota((TT, TT), 1) + c * TT).astype(F32)
        g = jnp.zeros((TT, TT), F32)
        for k in range(TOP_K):
            g = g + jnp.where(r == info[:, TOP_K + k:TOP_K + k + 1], info[:, k:k + 1], 0.0)
        g_hi = g.astype(BF16)
        g_lo = (g - g_hi.astype(F32)).astype(BF16)
        rows = jnp.concatenate(
            [stage[slot, pl.ds(c * TT * ROW_TILE + q, TT, stride=ROW_TILE), :] for q in range(ROW_TILE)],
            axis=1).astype(BF16)
        y = y + _dot(g_hi, rows) + _dot(g_lo, rows)
    ms = jnp.mean(y * y, axis=-1, keepdims=True)
    out_ref[...] = y * lax.rsqrt(ms + EPS) * fw_ref[...]

    ahead = lax.rem(j + 2, 3)
    fetch(jnp.minimum(j + 2, last), ahead)

    @pl.when(j == last)
    def _():
        _wait_rows(stage.at[lax.rem(j + 1, 3)], eo_ref, sem.at[lax.rem(j + 1, 3)], to_hbm=False)
        _wait_rows(stage.at[ahead], eo_ref, sem.at[ahead], to_hbm=False)


def _combine(runs, info, x1, final_w, eo):
    t = x1.shape[0]
    return pl.pallas_call(
        _combine_kernel,
        grid_spec=pltpu.PrefetchScalarGridSpec(
            num_scalar_prefetch=1,
            grid=(t // TT,),
            in_specs=[
                pl.BlockSpec((TT, LANES), lambda j, *_: (j, 0)),
                pl.BlockSpec((TT, D_MODEL), lambda j, *_: (j, 0)),
                pl.BlockSpec((1, D_MODEL), lambda j, *_: (0, 0)),
                pl.BlockSpec(memory_space=pl.ANY),
            ],
            out_specs=pl.BlockSpec((TT, D_MODEL), lambda j, *_: (j, 0)),
            scratch_shapes=[
                pltpu.VMEM((3, STAGE_ROWS * ROW_TILE, LANES), F32),
                pltpu.SemaphoreType.DMA((3,)),
            ],
        ),
        out_shape=jax.ShapeDtypeStruct((t, D_MODEL), F32),
        compiler_params=pltpu.CompilerParams(
            dimension_semantics=("arbitrary",), vmem_limit_bytes=VMEM_LIMIT),
        name="combine",
    )(runs, info, x1, final_w, eo)


def _pad_lanes(v, offset, fill=0.0):
    row = jnp.full((1, LANES), fill, F32)
    return row.at[0, offset:offset + v.shape[0]].set(v.astype(F32))


def kernel(x, norm_mix_w, w_in, gla_w_alpha_up, gla_b_alpha, gla_norm_w, ssd_conv_w, ssd_conv_b,
           ssd_dt_bias, ssd_A_log, ssd_D, ssd_norm_w, w_out, norm_ffn_w, router_w, router_b,
           moe_w_gate, moe_b_gate, moe_w_up, moe_b_up, moe_w_down, moe_b_down, final_norm_w):
    bsz, seqlen, d = x.shape
    t = bsz * seqlen
    depth = w_in.shape[0]
    assert depth == 1, "the final RMSNorm is fused into the (single) layer's combine step"
    p_rows = t * TOP_K + N_EXPERTS * TM
    n_tiles = p_rows // TM
    for l in range(depth):
        w = w_in[l]
        w_all = jnp.concatenate(
            [w[:, 0:1536], w[:, 1552:3088], w[:, 1536:1552], w[:, 3088:3096],
             jnp.zeros((d, N_SMALL - GLA_GATE_RANK - SSD_HEADS), w.dtype)], axis=1).astype(BF16)
        wup = jnp.zeros((N_SMALL, GLA_KW), F32).at[0:GLA_GATE_RANK].set(gla_w_alpha_up[l])
        dtb = _pad_lanes(ssd_dt_bias[l], DT_COL)
        aneg = _pad_lanes(-jnp.exp(ssd_A_log[l].astype(F32)), DT_COL)
        dexp = jnp.repeat(ssd_D[l].astype(F32), SSD_HEADDIM)[None, :]
        rw = jnp.zeros((LANES, d), F32).at[0:N_EXPERTS].set(router_w[l].T)
        rb = jnp.zeros((LANES, TL), F32).at[0:N_EXPERTS].set(
            jnp.broadcast_to(router_b[l].astype(F32)[:, None], (N_EXPERTS, TL)))

        x1, h2, logits, counts = _mixer(
            x, norm_mix_w[l][None, :], w_all, wup, gla_b_alpha[l][None, :], gla_norm_w[l][None, :], ssd_conv_w[l],
            ssd_conv_b[l][None, :], dtb, aneg, dexp, ssd_norm_w[l][None, :],
            w_out[l].astype(BF16), norm_ffn_w[l][None, :], rw, rb)

        info, lp, runs = _route(logits, counts)
        plan = _plan(counts, n_tiles, p_rows)
        xs = _dispatch(runs, plan, lp, h2.reshape(t, d), p_rows)
        eo = _experts(plan, xs,
                      moe_w_gate[l], moe_b_gate[l].reshape(N_EXPERTS, N_CHUNKS, N_COLS),
                      moe_w_up[l], moe_b_up[l].reshape(N_EXPERTS, N_CHUNKS, N_COLS),
                      moe_w_down[l], moe_b_down[l].reshape(N_EXPERTS, N_CHUNKS, N_COLS))
        x = _combine(runs, info, x1.reshape(t, d), final_norm_w[None, :], eo).reshape(bsz, seqlen, d)
    return x
```
